```python
import math
import jax, jax.numpy as jnp
from jax import lax
import numpy as np

D_MODEL = 1024
BATCH = 2
SEQ = 8192
DEPTH = 4
DEC_BATCH = 16
DEC_SEQ = 64
PAST_LEN = 4096

CHUNK = 64
EPS = 1e-6
SSD_HEADS = 8
SSD_HEAD_DIM = 64
SSD_INNER = SSD_HEADS * SSD_HEAD_DIM
SSD_GROUPS = 2
SSD_STATE = 64
CONV_W = 4
CONV_CH = SSD_INNER + 2 * SSD_GROUPS * SSD_STATE
FOX_HEADS = 4
FOX_HEAD_DIM = 64
FOX_INNER = FOX_HEADS * FOX_HEAD_DIM
FOX_BLOCK = 128
MLP_GROUPS = 4
MLP_GROUP_DIM = 64
MLP_INNER = MLP_GROUPS * MLP_GROUP_DIM
MLP_CHUNK = 128
D_MIX = SSD_INNER + FOX_INNER + MLP_INNER
PROJ_SIZES = (SSD_INNER, CONV_CH, SSD_HEADS, FOX_INNER, FOX_INNER, FOX_INNER, FOX_HEADS, MLP_INNER, MLP_INNER)
D_PROJ = SSD_INNER + CONV_CH + SSD_HEADS + 3 * FOX_INNER + FOX_HEADS + 2 * MLP_INNER
PEER_HEADS = 8
PEER_KEYS = 128
PEER_EXPERTS = PEER_KEYS * PEER_KEYS
PEER_D_KEY = 256
PEER_TOPK = 16
PEER_BLOCK = 256

kernel_name = "hymba_ssd_fox_chunkmlp_peer_stream_step"


def rms_norm(x, g):
    xf = x.astype(jnp.float32)
    return xf * lax.rsqrt(jnp.mean(xf * xf, axis=-1, keepdims=True) + EPS) * g.astype(jnp.float32)


def adaln(x, g, shift, scale):
    return rms_norm(x, g) * (1.0 + scale[:, None, :]) + shift[:, None, :]


def causal_conv(xbc, hist, w, b):
    T = xbc.shape[1]
    xp = jnp.concatenate([hist.astype(xbc.dtype), xbc], axis=1)
    out = b
    for tap in range(CONV_W):
        out = out + xp[:, tap:tap + T] * w[tap]
    return jax.nn.silu(out), xp[:, -(CONV_W - 1):]


def ssd_scan(x, dt, A, Bm, Cm, D, s0, L):
    Bt, T, H, P = x.shape
    G, N = Bm.shape[2], Bm.shape[3]
    J = H // G
    nc = T // L
    f32 = jnp.float32
    xr = x.astype(f32).reshape(Bt, nc, L, G, J, P)
    dtr = dt.astype(f32).reshape(Bt, nc, L, G, J)
    Br = Bm.astype(f32).reshape(Bt, nc, L, G, N)
    Cr = Cm.astype(f32).reshape(Bt, nc, L, G, N)
    acum = jnp.cumsum(dtr * A.astype(f32).reshape(G, J), axis=2)
    xdt = xr * dtr[..., None]
    seg = acum[:, :, :, None] - acum[:, :, None, :]
    causal = jnp.tril(jnp.ones((L, L), bool))[:, :, None, None]
    decay = jnp.exp(jnp.where(causal, seg, -jnp.inf))
    cb = jnp.einsum('bctgn,bcsgn->bctsg', Cr, Br)
    y_diag = jnp.einsum('bctsgj,bcsgjp->bctgjp', cb[..., None] * decay, xdt)
    decay_end = jnp.exp(acum[:, :, -1:] - acum)
    states = jnp.einsum('bcsgn,bcsgjp->bcgjpn', Br, xdt * decay_end[..., None])
    chunk_decay = jnp.exp(acum[:, :, -1])

    def step(s, inp):
        st, dec = inp
        return s * dec[..., None, None] + st, s

    final, prev = lax.scan(step, s0.astype(f32).reshape(Bt, G, J, P, N),
                           (jnp.moveaxis(states, 1, 0), jnp.moveaxis(chunk_decay, 1, 0)))
    prev = jnp.moveaxis(prev, 0, 1)
    y_off = jnp.einsum('bctgn,bcgjpn->bctgjp', Cr, prev) * jnp.exp(acum)[..., None]
    y = y_diag + y_off + D.astype(f32).reshape(G, J)[:, :, None] * xr
    return y.reshape(Bt, T, H, P), final.reshape(Bt, H, P, N)


def fox_block(q, k, v, Fq, Fk, qpos, kpos):
    s = jnp.einsum('bqhd,bkhd->bhqk', q, k).astype(jnp.float32) * (FOX_HEAD_DIM ** -0.5)
    s = s + (jnp.moveaxis(Fq, 1, 2)[..., :, None] - jnp.moveaxis(Fk, 1, 2)[..., None, :])
    s = jnp.where(kpos[None, :] <= qpos[:, None], s, -jnp.inf)
    p = jax.nn.softmax(s, axis=-1)
    return jnp.einsum('bhqk,bkhd->bqhd', p.astype(v.dtype), v)


def fox_prompt(q, k, v, F):
    B, T, H, d = q.shape
    nb = T // FOX_BLOCK
    pos = jnp.arange(T)
    qb = q.reshape(B, nb, FOX_BLOCK, H, d).swapaxes(0, 1)
    Fb = F.reshape(B, nb, FOX_BLOCK, H).swapaxes(0, 1)
    pb = pos.reshape(nb, FOX_BLOCK)
    out = lax.map(lambda a: fox_block(a[0], k, v, a[1], F, a[2], pos), (qb, Fb, pb))
    return out.swapaxes(0, 1).reshape(B, T, H, d)


def chunk_mlp(u, vm, w_s, b_s):
    B, T, _ = u.shape
    L = MLP_CHUNK if T % MLP_CHUNK == 0 else T
    nc = T // L
    vg = vm.astype(jnp.float32).reshape(B, T, MLP_GROUPS, MLP_GROUP_DIM)
    mu = jnp.mean(vg, axis=-1, keepdims=True)
    var = jnp.mean(jnp.square(vg - mu), axis=-1, keepdims=True)
    vn = (vg - mu) * lax.rsqrt(var + EPS)
    W = jnp.tril(w_s[:, :L, :L].astype(jnp.float32))
    sv = jnp.einsum('gts,bcsgd->bctgd', W, vn.reshape(B, nc, L, MLP_GROUPS, MLP_GROUP_DIM))
    sv = sv + jnp.transpose(b_s[:, :L])[None, None, :, :, None]
    return u * sv.reshape(B, T, MLP_INNER), vn.reshape(B, T, MLP_INNER)


def peer(h, wq, keys, u_tab, v_tab):
    B, T, D = h.shape
    n = B * T
    nb = -(-n // PEER_BLOCK)
    flat = jnp.pad(h.reshape(n, D), ((0, nb * PEER_BLOCK - n), (0, 0)))

    def block(hb):
        q = (hb @ wq).astype(jnp.float32).reshape(-1, PEER_HEADS, 2, PEER_D_KEY // 2)
        s = jnp.einsum('nhid,hikd->nhik', q, keys.astype(jnp.float32))
        top_s, top_i = lax.top_k(s, PEER_TOPK)
        cand = (top_s[:, :, 0, :, None] + top_s[:, :, 1, None, :]).reshape(-1, PEER_HEADS, PEER_TOPK * PEER_TOPK)
        cidx = (top_i[:, :, 0, :, None] * PEER_KEYS + top_i[:, :, 1, None, :]).reshape(-1, PEER_HEADS, PEER_TOPK * PEER_TOPK)
        best_s, best_pos = lax.top_k(cand, PEER_TOPK)
        eidx = jnp.take_along_axis(cidx, best_pos, axis=-1)
        g = jax.nn.softmax(best_s, axis=-1)
        ue = u_tab[eidx]
        ve = v_tab[eidx]
        act = jax.nn.gelu(jnp.einsum('nhkd,nd->nhk', ue, hb).astype(jnp.float32))
        return jnp.einsum('nhk,nhkd->nd', (g * act).astype(ve.dtype), ve)

    out = lax.map(block, flat.reshape(nb, PEER_BLOCK, D))
    return out.reshape(nb * PEER_BLOCK, D)[:n].reshape(B, T, D)


def token_mixers(h, p, conv_hist, ssm0, past):
    B, T, _ = h.shape
    proj = h @ p['w_in']
    idx = [int(i) for i in np.cumsum(PROJ_SIZES)[:-1]]
    z, xbc, dt_raw, q, k, v, f_raw, u, vm = jnp.split(proj, idx, axis=-1)
    xbc, new_conv = causal_conv(xbc, conv_hist, p['conv_w'], p['conv_b'])
    xs, Bm, Cm = jnp.split(xbc, [SSD_INNER, SSD_INNER + SSD_GROUPS * SSD_STATE], axis=-1)
    dt = jax.nn.softplus((dt_raw + p['dt_bias']).astype(jnp.float32))
    A = -jnp.exp(p['a_log'].astype(jnp.float32))
    L = CHUNK if T % CHUNK == 0 else T
    y_ssd, new_ssm = ssd_scan(xs.reshape(B, T, SSD_HEADS, SSD_HEAD_DIM), dt, A,
                              Bm.reshape(B, T, SSD_GROUPS, SSD_STATE), Cm.reshape(B, T, SSD_GROUPS, SSD_STATE),
                              p['d_skip'], ssm0, L)
    y_ssd = rms_norm(y_ssd.reshape(B, T, SSD_INNER) * jax.nn.silu(z), p['ssd_norm_g'])
    q = rms_norm(q.reshape(B, T, FOX_HEADS, FOX_HEAD_DIM), p['q_norm_g'])
    k = rms_norm(k.reshape(B, T, FOX_HEADS, FOX_HEAD_DIM), p['k_norm_g'])
    v = v.reshape(B, T, FOX_HEADS, FOX_HEAD_DIM)
    logf = jax.nn.log_sigmoid((f_raw + p['fgate_b']).astype(jnp.float32))
    if past is None:
        y_fox = fox_prompt(q, k, v, jnp.cumsum(logf, axis=1))
    else:
        pk, pv, plf = past
        P = pk.shape[1]
        k_all = jnp.concatenate([pk.astype(k.dtype), k], axis=1)
        v_all = jnp.concatenate([pv, v], axis=1)
        F_all = jnp.cumsum(jnp.concatenate([plf.astype(jnp.float32), logf], axis=1), axis=1)
        y_fox = fox_block(q, k_all, v_all, F_all[:, P:], F_all, P + jnp.arange(T), jnp.arange(P + T))
    y_mlp, v_rows = chunk_mlp(jax.nn.gelu(u), jax.nn.gelu(vm), p['w_s'], p['b_s'])
    mix = jnp.concatenate([y_ssd, y_fox.reshape(B, T, FOX_INNER).astype(y_ssd.dtype), y_mlp.astype(y_ssd.dtype)], axis=-1)
    return mix @ p['w_out'], (k, v, logf, new_ssm, new_conv, v_rows)


def trunk_layer(x, c, p, conv_hist, ssm0, past):
    mod = jax.nn.silu(c.astype(jnp.float32)) @ p['w_ada'] + p['b_ada']
    sh1, sc1, g1, sh2, sc2, g2 = jnp.split(mod, 6, axis=-1)
    h = adaln(x, p['norm1_g'], sh1, sc1)
    mix, st = token_mixers(h, p, conv_hist, ssm0, past)
    x = x + g1[:, None, :] * mix
    h2 = adaln(x, p['norm2_g'], sh2, sc2)
    x = x + g2[:, None, :] * peer(h2, p['peer_wq'], p['peer_keys'], p['peer_u'], p['peer_v'])
    return x, st


def setup_inputs(seed: int = 0) -> dict:
    key = jax.random.key(seed)
    ks = list(jax.random.split(key, 40))
    f32 = jnp.float32
    nrm = lambda k, shape, s: jax.random.normal(k, shape, f32) * s
    dt0 = jnp.exp(jax.random.uniform(ks[12], (DEPTH, SSD_HEADS), f32, math.log(1e-3), math.log(1e-1)))
    return {
        'x_prompt': nrm(ks[0], (BATCH, SEQ, D_MODEL), 1.0),
        'x_sample': nrm(ks[1], (DEC_BATCH, DEC_SEQ, D_MODEL), 1.0),
        'c_prompt': nrm(ks[2], (BATCH, D_MODEL), 1.0),
        'c_sample': nrm(ks[3], (DEC_BATCH, D_MODEL), 1.0),
        'cache_fox_k': nrm(ks[4], (DEPTH, DEC_BATCH, PAST_LEN, FOX_HEADS, FOX_HEAD_DIM), 1.0),
        'cache_fox_v': nrm(ks[5], (DEPTH, DEC_BATCH, PAST_LEN, FOX_HEADS, FOX_HEAD_DIM), 1.0),
        'cache_fox_logf': jax.nn.log_sigmoid(3.0 + nrm(ks[6], (DEPTH, DEC_BATCH, PAST_LEN, FOX_HEADS), 1.0)),
        'state_ssm': nrm(ks[7], (DEPTH, DEC_BATCH, SSD_HEADS, SSD_HEAD_DIM, SSD_STATE), 0.1),
        'state_conv': nrm(ks[8], (DEPTH, DEC_BATCH, CONV_W - 1, CONV_CH), 1.0),
        'norm1_g': 1.0 + nrm(ks[9], (DEPTH, D_MODEL), 0.05),
        'norm2_g': 1.0 + nrm(ks[10], (DEPTH, D_MODEL), 0.05),
        'w_ada': nrm(ks[11], (DEPTH, D_MODEL, 6 * D_MODEL), 0.5 * D_MODEL ** -0.5),
        'b_ada': nrm(ks[13], (DEPTH, 6 * D_MODEL), 0.02),
        'w_in': nrm(ks[14], (DEPTH, D_MODEL, D_PROJ), D_MODEL ** -0.5),
        'conv_w': nrm(ks[15], (DEPTH, CONV_W, CONV_CH), CONV_W ** -0.5),
        'conv_b': nrm(ks[16], (DEPTH, CONV_CH), 0.02),
        'dt_bias': dt0 + jnp.log(-jnp.expm1(-dt0)),
        'a_log': jnp.log(jax.random.uniform(ks[17], (DEPTH, SSD_HEADS), f32, 1.0, 16.0)),
        'd_skip': 1.0 + nrm(ks[18], (DEPTH, SSD_HEADS), 0.1),
        'ssd_norm_g': 1.0 + nrm(ks[19], (DEPTH, SSD_INNER), 0.05),
        'q_norm_g': 1.0 + nrm(ks[20], (DEPTH, FOX_HEAD_DIM), 0.05),
        'k_norm_g': 1.0 + nrm(ks[21], (DEPTH, FOX_HEAD_DIM), 0.05),
        'fgate_b': jax.random.uniform(ks[22], (DEPTH, FOX_HEADS), f32, 1.0, 5.0),
        'w_s': nrm(ks[23], (DEPTH, MLP_GROUPS, MLP_CHUNK, MLP_CHUNK), MLP_CHUNK ** -0.5),
        'b_s': 1.0 + nrm(ks[24], (DEPTH, MLP_GROUPS, MLP_CHUNK), 0.1),
        'w_out': nrm(ks[25], (DEPTH, D_MIX, D_MODEL), D_MIX ** -0.5),
        'peer_wq': nrm(ks[26], (DEPTH, D_MODEL, PEER_HEADS * PEER_D_KEY), D_MODEL ** -0.5),
        'peer_keys': nrm(ks[27], (DEPTH, PEER_HEADS, 2, PEER_KEYS, PEER_D_KEY // 2), (PEER_D_KEY // 2) ** -0.5),
        'peer_u': nrm(ks[28], (DEPTH, PEER_EXPERTS, D_MODEL), D_MODEL ** -0.5),
        'peer_v': nrm(ks[29], (DEPTH, PEER_EXPERTS, D_MODEL), 0.5 * PEER_HEADS ** -0.5),
    }


def reference(x_prompt, x_sample, c_prompt, c_sample, cache_fox_k, cache_fox_v, cache_fox_logf,
              state_ssm, state_conv, norm1_g, norm2_g, w_ada, b_ada, w_in, conv_w, conv_b, dt_bias,
              a_log, d_skip, ssd_norm_g, q_norm_g, k_norm_g, fgate_b, w_s, b_s, w_out,
              peer_wq, peer_keys, peer_u, peer_v):
    yp, ys = x_prompt, x_sample
    Bp = x_prompt.shape[0]
    kp, vp, lfp, ssmp, convp = [], [], [], [], []
    kss, vss, lfs, ssms, convs, mlpvs = [], [], [], [], [], []
    for l in range(DEPTH):
        p = {'norm1_g': norm1_g[l], 'norm2_g': norm2_g[l], 'w_ada': w_ada[l], 'b_ada': b_ada[l],
             'w_in': w_in[l], 'conv_w': conv_w[l], 'conv_b': conv_b[l], 'dt_bias': dt_bias[l],
             'a_log': a_log[l], 'd_skip': d_skip[l], 'ssd_norm_g': ssd_norm_g[l],
             'q_norm_g': q_norm_g[l], 'k_norm_g': k_norm_g[l], 'fgate_b': fgate_b[l],
             'w_s': w_s[l], 'b_s': b_s[l], 'w_out': w_out[l], 'peer_wq': peer_wq[l],
             'peer_keys': peer_keys[l], 'peer_u': peer_u[l], 'peer_v': peer_v[l]}
        yp, (k1, v1, lf1, s1, cv1, _) = trunk_layer(
            yp, c_prompt, p, jnp.zeros((Bp, CONV_W - 1, CONV_CH), x_prompt.dtype),
            jnp.zeros((Bp, SSD_HEADS, SSD_HEAD_DIM, SSD_STATE), jnp.float32), None)
        ys, (k2, v2, lf2, s2, cv2, mv2) = trunk_layer(
            ys, c_sample, p, state_conv[l], state_ssm[l], (cache_fox_k[l], cache_fox_v[l], cache_fox_logf[l]))
        kp.append(k1); vp.append(v1); lfp.append(lf1); ssmp.append(s1); convp.append(cv1)
        kss.append(k2); vss.append(v2); lfs.append(lf2); ssms.append(s2); convs.append(cv2); mlpvs.append(mv2)
    return (yp, ys,
            jnp.stack(kp), jnp.stack(vp), jnp.stack(lfp), jnp.stack(ssmp), jnp.stack(convp),
            jnp.stack(kss), jnp.stack(vss), jnp.stack(lfs), jnp.stack(ssms), jnp.stack(convs), jnp.stack(mlpvs))
```

```python
import functools

import jax
import jax.numpy as jnp
from jax import lax
from jax.experimental import pallas as pl
from jax.experimental.pallas import tpu as pltpu

F32 = jnp.float32
BF16 = jnp.bfloat16
EPS = 1e-6

D_MODEL = 1024
SSD_HEADS = 8
SSD_HEAD_DIM = 64
SSD_INNER = SSD_HEADS * SSD_HEAD_DIM
SSD_GROUPS = 2
SSD_STATE = 64
CONV_W = 4
CONV_CH = SSD_INNER + 2 * SSD_GROUPS * SSD_STATE
FOX_HEADS = 4
FOX_HEAD_DIM = 64
FOX_INNER = FOX_HEADS * FOX_HEAD_DIM
MLP_GROUPS = 4
MLP_GROUP_DIM = 64
MLP_INNER = MLP_GROUPS * MLP_GROUP_DIM
MLP_CHUNK = 128
SSD_CHUNK_PROMPT = 128
PEER_HEADS = 8
PEER_KEYS = 128
PEER_EXPERTS = PEER_KEYS * PEER_KEYS
PEER_HALF = 128
PEER_TOPK = 16
PEER_TOKEN_TILE = 512
LANES = 128
SMALL_COLS = LANES
SMALL_ROWS = 16
DT_ROW0, F_ROW0 = 0, 8
NEG = -1e30
VMEM_LIMIT = 56 * 1024 * 1024

_Z0, _XBC0, _Q0, _K0, _V0, _U0, _VM0, _SM0 = 0, 512, 1280, 1536, 1792, 2048, 2304, 2560
PROJ_COLS = _SM0 + SMALL_COLS


def _cparams(sem):
    return pltpu.CompilerParams(dimension_semantics=sem, vmem_limit_bytes=VMEM_LIMIT)


def _dot(a, b):
    return jnp.dot(a, b, preferred_element_type=F32)


def _dot_nt(a, b):
    return lax.dot_general(a, b, (((1,), (1,)), ((), ())), preferred_element_type=F32)


def _dot_tn(a, b):
    return lax.dot_general(a, b, (((0,), (0,)), ((), ())), preferred_element_type=F32)


def _split2(x):
    hi = x.astype(BF16)
    lo = (x - hi.astype(F32)).astype(BF16)
    return hi, lo


def _split3(x):
    hi = x.astype(BF16)
    r = x - hi.astype(F32)
    mid = r.astype(BF16)
    lo = (r - mid.astype(F32)).astype(BF16)
    return hi, mid, lo


def _dot3_l(x, w):
    hi, mid, lo = _split3(x)
    return _dot(hi, w) + _dot(mid, w) + _dot(lo, w)


def _dot3_r(w, x):
    hi, mid, lo = _split3(x)
    return _dot(w, hi) + _dot(w, mid) + _dot(w, lo)


def _sigmoid(x):
    return 1.0 / (1.0 + jnp.exp(-x))


def _silu(x):
    return x * _sigmoid(x)


def _softplus(x):
    return jnp.maximum(x, 0.0) + jnp.log1p(jnp.exp(-jnp.abs(x)))


def _gelu(x):
    return 0.5 * x * (1.0 + jnp.tanh(0.7978845608028654 * (x + 0.044715 * (x * x * x))))


def _iota(shape, dim):
    return lax.broadcasted_iota(jnp.int32, shape, dim)


def _mod_kernel(c_ref, w_ref, b_ref, o_ref):
    c = c_ref[...]
    o_ref[0] = jnp.dot(_silu(c), w_ref[0], preferred_element_type=F32,
                       precision=lax.Precision.HIGHEST) + b_ref[0]


def _modulation(c_all, w_ada, b_ada):
    depth, _, n6 = w_ada.shape
    bc = c_all.shape[0]
    tn = 1536
    return pl.pallas_call(
        _mod_kernel,
        grid=(depth, n6 // tn),
        in_specs=[pl.BlockSpec((bc, D_MODEL), lambda l, j: (0, 0)),
                  pl.BlockSpec((1, D_MODEL, tn), lambda l, j: (l, 0, j)),
                  pl.BlockSpec((1, 1, tn), lambda l, j: (l, 0, j))],
        out_specs=pl.BlockSpec((1, bc, tn), lambda l, j: (l, 0, j)),
        out_shape=jax.ShapeDtypeStruct((depth, bc, n6), F32),
        compiler_params=_cparams(("arbitrary", "arbitrary")),
        name="adaln_mod",
    )(c_all, w_ada, b_ada.reshape(depth, 1, n6))


def _in_kernel(with_peer, *refs):
    if with_peer:
        x_ref, p_ref, g2_ref = refs[:3]
        refs = refs[3:]
    else:
        x_ref = refs[0]
        refs = refs[1:]
    (sh_ref, sc_ref, g_ref, w_ref, wst_ref, gq_ref, gk_ref, gm_ref) = refs[:8]
    outs = refs[8:]
    if with_peer:
        xo_ref = outs[0]
        outs = outs[1:]
    (z_ref, xbc_ref, q_ref, k_ref, kb_ref, v_ref, vb_ref, ug_ref, vn_ref, sm_ref, smt_ref) = outs

    x = x_ref[0]
    if with_peer:
        x = x + g2_ref[0] * p_ref[0]
        xo_ref[0] = x
    ms = jnp.mean(x * x, axis=-1, keepdims=True)
    h = x * lax.rsqrt(ms + EPS) * g_ref[...]
    h = h * (1.0 + sc_ref[0]) + sh_ref[0]
    hb = h.astype(BF16)
    proj = _dot(hb, w_ref[...])
    z_ref[0] = proj[:, _Z0:_XBC0]
    xbc_ref[0] = proj[:, _XBC0:_Q0]
    q = proj[:, _Q0:_K0]
    k = proj[:, _K0:_V0]
    v = proj[:, _V0:_U0]
    u = proj[:, _U0:_VM0]
    vm = proj[:, _VM0:_SM0]
    sm_ref[0] = proj[:, _SM0:PROJ_COLS]
    smt_ref[0] = _dot_nt(wst_ref[...], hb)
    gm = gm_ref[...]

    def gmean(y):
        hi, lo = _split2(y)
        return _dot(hi, gm) + _dot(lo, gm)

    qn = q * lax.rsqrt(gmean(q * q) + EPS) * gq_ref[...]
    kn = k * lax.rsqrt(gmean(k * k) + EPS) * gk_ref[...]
    q_ref[0] = qn.astype(BF16)
    k_ref[0] = kn
    kb_ref[0] = kn.astype(BF16)
    v_ref[0] = v
    vb_ref[0] = v.astype(BF16)
    ug_ref[0] = _gelu(u)
    gv = _gelu(vm)
    mu = gmean(gv)
    cen = gv - mu
    var = gmean(cen * cen)
    vn_ref[0] = cen * lax.rsqrt(var + EPS)


def _in_proj(x, peer, g2, sh, sc, g, w_r, ws_t, gq, gk, gm):
    b, t, _ = x.shape
    tm = min(t, 512)
    with_peer = peer is not None
    tok = lambda c: pl.BlockSpec((1, tm, c), lambda i, j: (i, j, 0))
    per_b = pl.BlockSpec((1, 1, D_MODEL), lambda i, j: (i, 0, 0))
    full = lambda a: pl.BlockSpec(a.shape, lambda i, j: (0,) * a.ndim)
    in_specs = [tok(D_MODEL)]
    args = [x]
    if with_peer:
        in_specs += [tok(D_MODEL), per_b]
        args += [peer, g2]
    in_specs += [per_b, per_b, full(g), full(w_r), full(ws_t), full(gq), full(gk), full(gm)]
    args += [sh, sc, g, w_r, ws_t, gq, gk, gm]
    out_cols = [(SSD_INNER, F32), (CONV_CH, F32), (FOX_INNER, BF16), (FOX_INNER, F32), (FOX_INNER, BF16),
                (FOX_INNER, F32), (FOX_INNER, BF16), (MLP_INNER, F32), (MLP_INNER, F32), (SMALL_COLS, F32)]
    out_specs = [tok(c) for c, _ in out_cols]
    out_shape = [jax.ShapeDtypeStruct((b, t, c), dt) for c, dt in out_cols]
    out_specs.append(pl.BlockSpec((1, SMALL_ROWS, tm), lambda i, j: (i, 0, j)))
    out_shape.append(jax.ShapeDtypeStruct((b, SMALL_ROWS, t), F32))
    if with_peer:
        out_specs = [tok(D_MODEL)] + out_specs
        out_shape = [jax.ShapeDtypeStruct((b, t, D_MODEL), F32)] + out_shape
    res = pl.pallas_call(
        functools.partial(_in_kernel, with_peer),
        grid=(b, t // tm),
        in_specs=in_specs, out_specs=out_specs, out_shape=out_shape,
        compiler_params=_cparams(("arbitrary", "arbitrary")),
        name="in_proj",
    )(*args)
    if with_peer:
        return res[0], res[1:]
    return x, res


def _ssd_kernel(L, xbc_ref, z_ref, sm_ref, smt_ref, hist_ref, s0_ref, cw_ref, cb_ref, dtbc_ref, dtbr_ref,
                ac_ref, ar_ref, dsk_ref, ng_ref, e_ref, y_ref, sfin_ref, cfin_ref, xpad, st, ybuf):
    c = pl.program_id(1)
    nc = pl.num_programs(1)

    @pl.when(c == 0)
    def _():
        xpad[0:8, :] = hist_ref[0]
        st[...] = s0_ref[0]

    xpad[8:8 + L, :] = xbc_ref[0]
    conv = cb_ref[...]
    for tap in range(CONV_W):
        conv = conv + xpad[5 + tap:5 + tap + L, :] * cw_ref[tap:tap + 1, :]
    tail = xpad[L:L + 8, :]
    xpad[0:8, :] = tail
    xc = _silu(conv)
    xs = xc[:, 0:SSD_INNER]

    r_i = _iota((L, L), 0)
    c_i = _iota((L, L), 1)
    causal = r_i >= c_i
    tri = jnp.where(causal, 1.0, 0.0).astype(BF16)
    triu = jnp.where(r_i <= c_i, 1.0, 0.0).astype(BF16)

    dtc = _softplus(sm_ref[0] + dtbc_ref[...])
    acum_c = _dot3_r(tri, dtc * ac_ref[...])
    e = e_ref[...]
    acum_x = _dot3_l(acum_c, e)
    dt_x = _dot3_l(dtc, e)
    dtr = _softplus(smt_ref[0][DT_ROW0:DT_ROW0 + 8, :] + dtbr_ref[...])
    acum_r = _dot3_l(dtr * ar_ref[...], triu)

    bmat = [xc[:, SSD_INNER + SSD_STATE * g:SSD_INNER + SSD_STATE * (g + 1)].astype(BF16) for g in range(SSD_GROUPS)]
    c0 = SSD_INNER + SSD_GROUPS * SSD_STATE
    cmat = [xc[:, c0 + SSD_STATE * g:c0 + SSD_STATE * (g + 1)].astype(BF16) for g in range(SSD_GROUPS)]
    cb = [_dot_nt(cmat[g], bmat[g]) for g in range(SSD_GROUPS)]
    xsb = xs.astype(BF16)
    hpg = SSD_HEADS // SSD_GROUPS
    for h in range(SSD_HEADS):
        g = h // hpg
        seg = acum_c[:, h:h + 1] - acum_r[h:h + 1, :]
        dec = jnp.where(causal, jnp.exp(jnp.minimum(seg, 0.0)), 0.0)
        m = cb[g] * dec * dtr[h:h + 1, :]
        ybuf[:, SSD_HEAD_DIM * h:SSD_HEAD_DIM * (h + 1)] = _dot(
            m.astype(BF16), xsb[:, SSD_HEAD_DIM * h:SSD_HEAD_DIM * (h + 1)])

    gw = SSD_INNER // SSD_GROUPS
    stb = st[...].astype(BF16)
    y_off = jnp.concatenate([_dot(cmat[g], stb[:, gw * g:gw * (g + 1)]) for g in range(SSD_GROUPS)], axis=1)
    y = ybuf[...] + y_off * jnp.exp(acum_x) + dsk_ref[...] * xs
    a_end = acum_x[L - 1:L, :]
    xw = (xs * dt_x * jnp.exp(a_end - acum_x)).astype(BF16)
    new_states = jnp.concatenate([_dot_tn(bmat[g], xw[:, gw * g:gw * (g + 1)]) for g in range(SSD_GROUPS)], axis=1)
    st[...] = st[...] * jnp.exp(a_end) + new_states

    yg = y * _silu(z_ref[0])
    ms = jnp.mean(yg * yg, axis=-1, keepdims=True)
    y_ref[0] = yg * lax.rsqrt(ms + EPS) * ng_ref[...]

    @pl.when(c == nc - 1)
    def _():
        sfin_ref[0] = st[...]
        cfin_ref[0] = tail


def _ssd(xbc, z, sm, smt, hist8, s0t, cw, cb, dtb_c, dtb_r, a_c, a_r, dsk_x, ng, e_mat, L):
    b, t, _ = xbc.shape
    tok = lambda c: pl.BlockSpec((1, L, c), lambda i, j: (i, j, 0))
    full = lambda a: pl.BlockSpec(a.shape, lambda i, j: (0,) * a.ndim)
    per_b = lambda a: pl.BlockSpec((1,) + a.shape[1:], lambda i, j: (i,) + (0,) * (a.ndim - 1))
    return pl.pallas_call(
        functools.partial(_ssd_kernel, L),
        grid=(b, t // L),
        in_specs=[tok(CONV_CH), tok(SSD_INNER), tok(SMALL_COLS),
                  pl.BlockSpec((1, SMALL_ROWS, L), lambda i, j: (i, 0, j)),
                  per_b(hist8), per_b(s0t), full(cw), full(cb), full(dtb_c), full(dtb_r), full(a_c), full(a_r),
                  full(dsk_x), full(ng), full(e_mat)],
        out_specs=[tok(SSD_INNER),
                   pl.BlockSpec((1, SSD_STATE, SSD_INNER), lambda i, j: (i, 0, 0)),
                   pl.BlockSpec((1, 8, CONV_CH), lambda i, j: (i, 0, 0))],
        out_shape=[jax.ShapeDtypeStruct((b, t, SSD_INNER), F32),
                   jax.ShapeDtypeStruct((b, SSD_STATE, SSD_INNER), F32),
                   jax.ShapeDtypeStruct((b, 8, CONV_CH), F32)],
        scratch_shapes=[pltpu.VMEM((L + 8, CONV_CH), F32), pltpu.VMEM((SSD_STATE, SSD_INNER), F32),
                        pltpu.VMEM((L, SSD_INNER), F32)],
        compiler_params=_cparams(("arbitrary", "arbitrary")),
        name="ssd_scan",
    )(xbc, z, sm, smt, hist8, s0t, cw, cb, dtb_c, dtb_r, a_c, a_r, dsk_x, ng, e_mat)


def _fcum_kernel(tf, activate, colsrc_ref, rowsrc_ref, fbc_ref, fbr_ref, initc_ref, initr_ref,
                 lfc_ref, fc_ref, fr_ref, endc_ref, endr_ref, carc, carr):
    j = pl.program_id(1)

    @pl.when(j == 0)
    def _():
        carc[...] = initc_ref[0]
        carr[...] = initr_ref[0]

    xc = colsrc_ref[0]
    xr = rowsrc_ref[0]
    if activate:
        xc = -_softplus(-(xc + fbc_ref[...]))
        xr = -_softplus(-(xr + fbr_ref[...]))
    r_i = _iota((tf, tf), 0)
    c_i = _iota((tf, tf), 1)
    tri = jnp.where(r_i >= c_i, 1.0, 0.0).astype(BF16)
    triu = jnp.where(r_i <= c_i, 1.0, 0.0).astype(BF16)
    fcol = carc[0:1, :] + _dot3_r(tri, xc)
    frow = carr[:, 0:1] + _dot3_l(xr, triu)
    lfc_ref[0] = xc
    fc_ref[0] = fcol
    fr_ref[0] = frow
    carc[...] = jnp.broadcast_to(fcol[tf - 1:tf, :], carc.shape)
    carr[...] = jnp.broadcast_to(frow[:, tf - 1:tf], carr.shape)
    endc_ref[0] = carc[...]
    endr_ref[0] = carr[...]


def _fcum(colsrc, rowsrc, row_block, fb_c, fb_r, init_c, init_r, activate):
    b, t, _ = colsrc.shape
    tf = min(t, 512)
    full = lambda a: pl.BlockSpec(a.shape, lambda i, j: (0,) * a.ndim)
    per_b = lambda a: pl.BlockSpec((1,) + a.shape[1:], lambda i, j: (i,) + (0,) * (a.ndim - 1))
    return pl.pallas_call(
        functools.partial(_fcum_kernel, tf, activate),
        grid=(b, t // tf),
        in_specs=[pl.BlockSpec((1, tf, LANES), lambda i, j: (i, j, 0)),
                  pl.BlockSpec((1, 8, tf), lambda i, j: (i, row_block, j)),
                  full(fb_c), full(fb_r), per_b(init_c), per_b(init_r)],
        out_specs=[pl.BlockSpec((1, tf, LANES), lambda i, j: (i, j, 0)),
                   pl.BlockSpec((1, tf, LANES), lambda i, j: (i, j, 0)),
                   pl.BlockSpec((1, 8, tf), lambda i, j: (i, 0, j)),
                   pl.BlockSpec((1, 8, LANES), lambda i, j: (i, 0, 0)),
                   pl.BlockSpec((1, 8, LANES), lambda i, j: (i, 0, 0))],
        out_shape=[jax.ShapeDtypeStruct((b, t, LANES), F32), jax.ShapeDtypeStruct((b, t, LANES), F32),
                   jax.ShapeDtypeStruct((b, 8, t), F32), jax.ShapeDtypeStruct((b, 8, LANES), F32),
                   jax.ShapeDtypeStruct((b, 8, LANES), F32)],
        scratch_shapes=[pltpu.VMEM((8, LANES), F32), pltpu.VMEM((8, LANES), F32)],
        compiler_params=_cparams(("arbitrary", "arbitrary")),
        name="forget_cumsum",
    )(colsrc, rowsrc, fb_c, fb_r, init_c, init_r)


def _softmax_step(s, vb, carry):
    m, l, acc = carry
    m_new = jnp.maximum(m, jnp.max(s, axis=-1, keepdims=True))
    p = jnp.exp(s - m_new)
    alpha = jnp.exp(m - m_new)
    l = alpha * l + jnp.sum(p, axis=-1, keepdims=True)
    acc = alpha * acc + _dot(p.astype(BF16), vb)
    return m_new, l, acc


def _fox_prompt_kernel(tq, q_ref, fq_ref, k_ref, v_ref, fk_ref, o_ref):
    qi = pl.program_id(1)
    q_all = q_ref[0]
    fq_all = fq_ref[0]
    r_i = _iota((tq, tq), 0)
    c_i = _iota((tq, tq), 1)
    causal = r_i >= c_i
    for h in range(FOX_HEADS):
        lo, hi = FOX_HEAD_DIM * h, FOX_HEAD_DIM * (h + 1)
        qh = q_all[:, lo:hi]
        fq = fq_all[:, F_ROW0 + h:F_ROW0 + h + 1]

        def scores(j):
            start = pl.multiple_of(j * tq, tq)
            kb = k_ref[0, pl.ds(start, tq), lo:hi]
            vb = v_ref[0, pl.ds(start, tq), lo:hi]
            fk = fk_ref[0, j][h:h + 1, :]
            return _dot_nt(qh, kb) + fq - fk, vb

        def body(j, carry):
            s, vb = scores(j)
            return _softmax_step(s, vb, carry)

        init = (jnp.full((tq, 1), NEG, F32), jnp.zeros((tq, 1), F32), jnp.zeros((tq, FOX_HEAD_DIM), F32))
        carry = lax.fori_loop(0, qi, body, init)
        s, vb = scores(qi)
        m, l, acc = _softmax_step(jnp.where(causal, s, NEG), vb, carry)
        o_ref[0, :, lo:hi] = acc / l


def _fox_prompt(qb, fcol, kb, vb, frow):
    b, t, _ = qb.shape
    tq = min(t, 256)
    nk = t // tq
    fk = frow.reshape(b, 8, nk, tq).transpose(0, 2, 1, 3)
    return pl.pallas_call(
        functools.partial(_fox_prompt_kernel, tq),
        grid=(b, t // tq),
        in_specs=[pl.BlockSpec((1, tq, FOX_INNER), lambda i, j: (i, j, 0)),
                  pl.BlockSpec((1, tq, LANES), lambda i, j: (i, j, 0)),
                  pl.BlockSpec((1, t, FOX_INNER), lambda i, j: (i, 0, 0)),
                  pl.BlockSpec((1, t, FOX_INNER), lambda i, j: (i, 0, 0)),
                  pl.BlockSpec((1, nk, 8, tq), lambda i, j: (i, 0, 0, 0))],
        out_specs=pl.BlockSpec((1, tq, FOX_INNER), lambda i, j: (i, j, 0)),
        out_shape=jax.ShapeDtypeStruct((b, t, FOX_INNER), F32),
        compiler_params=_cparams(("arbitrary", "arbitrary")),
        name="fox_prompt",
    )(qb, fcol, kb, vb, fk)


def _fox_sample_kernel(tq, tk, npast, q_ref, fq_ref, pk_ref, pv_ref, fpk_ref, k_ref, v_ref, fk_ref, o_ref):
    q_all = q_ref[0]
    fq_all = fq_ref[0]
    k_new = k_ref[0]
    v_new = v_ref[0]
    r_i = _iota((tq, tq), 0)
    c_i = _iota((tq, tq), 1)
    causal = r_i >= c_i
    for h in range(FOX_HEADS):
        lo, hi = FOX_HEAD_DIM * h, FOX_HEAD_DIM * (h + 1)
        qh = q_all[:, lo:hi]
        fq = fq_all[:, F_ROW0 + h:F_ROW0 + h + 1]

        def body(j, carry):
            start = pl.multiple_of(j * tk, tk)
            kb = pk_ref[0, pl.ds(start, tk), lo:hi].astype(BF16)
            vb = pv_ref[0, pl.ds(start, tk), lo:hi].astype(BF16)
            fk = fpk_ref[0, j][h:h + 1, :]
            return _softmax_step(_dot_nt(qh, kb) + fq - fk, vb, carry)

        init = (jnp.full((tq, 1), NEG, F32), jnp.zeros((tq, 1), F32), jnp.zeros((tq, FOX_HEAD_DIM), F32))
        carry = lax.fori_loop(0, npast, body, init)
        s = _dot_nt(qh, k_new[:, lo:hi]) + fq - fk_ref[0][h:h + 1, :]
        m, l, acc = _softmax_step(jnp.where(causal, s, NEG), v_new[:, lo:hi], carry)
        o_ref[0, :, lo:hi] = acc / l


def _fox_sample(qb, fcol, past_k, past_v, fpast_row, kb, vb, frow):
    b, t, _ = qb.shape
    p = past_k.shape[1]
    tk = min(p, 512)
    npast = p // tk
    fpk = fpast_row.reshape(b, 8, npast, tk).transpose(0, 2, 1, 3)
    bspec = lambda a: pl.BlockSpec((1,) + a.shape[1:], lambda i: (i,) + (0,) * (a.ndim - 1))
    args = (qb, fcol, past_k, past_v, fpk, kb, vb, frow)
    return pl.pallas_call(
        functools.partial(_fox_sample_kernel, t, tk, npast),
        grid=(b,),
        in_specs=[bspec(a) for a in args],
        out_specs=pl.BlockSpec((1, t, FOX_INNER), lambda i: (i, 0, 0)),
        out_shape=jax.ShapeDtypeStruct((b, t, FOX_INNER), F32),
        compiler_params=_cparams(("arbitrary",)),
        name="fox_sample",
    )(*args)


def _mlp_kernel(lm, nchunk, ug_ref, vn_ref, ws_ref, bst_ref, y_ref):
    r_i = _iota((lm, lm), 0)
    c_i = _iota((lm, lm), 1)
    tril = r_i >= c_i
    vn = vn_ref[0].astype(BF16)
    ug = ug_ref[0]
    for g in range(MLP_GROUPS):
        lo, hi = MLP_GROUP_DIM * g, MLP_GROUP_DIM * (g + 1)
        w = jnp.where(tril, ws_ref[g], 0.0).astype(BF16)
        bias = bst_ref[:, g:g + 1]
        for c in range(nchunk):
            r0, r1 = c * lm, (c + 1) * lm
            sv = _dot(w, vn[r0:r1, lo:hi]) + bias
            y_ref[0, r0:r1, lo:hi] = ug[r0:r1, lo:hi] * sv


def _chunk_mlp(ug, vn, ws, bst):
    b, t, _ = ug.shape
    lm = ws.shape[1]
    tm = min(t, 4 * lm)
    tok = pl.BlockSpec((1, tm, MLP_INNER), lambda i, j: (i, j, 0))
    return pl.pallas_call(
        functools.partial(_mlp_kernel, lm, tm // lm),
        grid=(b, t // tm),
        in_specs=[tok, tok, pl.BlockSpec(ws.shape, lambda i, j: (0, 0, 0)),
                  pl.BlockSpec(bst.shape, lambda i, j: (0, 0))],
        out_specs=tok,
        out_shape=jax.ShapeDtypeStruct((b, t, MLP_INNER), F32),
        compiler_params=_cparams(("arbitrary", "arbitrary")),
        name="chunk_mlp",
    )(ug, vn, ws, bst)


def _out_kernel(x_ref, ys_ref, yf_ref, ym_ref, g1_ref, sh_ref, sc_ref, g_ref, wo_ref, x1_ref, h2_ref):
    mix = (_dot(ys_ref[0].astype(BF16), wo_ref[0:SSD_INNER, :])
           + _dot(yf_ref[0].astype(BF16), wo_ref[SSD_INNER:SSD_INNER + FOX_INNER, :])
           + _dot(ym_ref[0].astype(BF16), wo_ref[SSD_INNER + FOX_INNER:D_MODEL, :]))
    x1 = x_ref[0] + g1_ref[0] * mix
    x1_ref[0] = x1
    ms = jnp.mean(x1 * x1, axis=-1, keepdims=True)
    h = x1 * lax.rsqrt(ms + EPS) * g_ref[...]
    h2_ref[0] = (h * (1.0 + sc_ref[0]) + sh_ref[0]).astype(BF16)


def _out_proj(x, ys, yf, ym, g1, sh, sc, g, wo):
    b, t, _ = x.shape
    tm = min(t, 512)
    tok = lambda c: pl.BlockSpec((1, tm, c), lambda i, j: (i, j, 0))
    per_b = pl.BlockSpec((1, 1, D_MODEL), lambda i, j: (i, 0, 0))
    full = lambda a: pl.BlockSpec(a.shape, lambda i, j: (0,) * a.ndim)
    return pl.pallas_call(
        _out_kernel,
        grid=(b, t // tm),
        in_specs=[tok(D_MODEL), tok(SSD_INNER), tok(FOX_INNER), tok(MLP_INNER), per_b, per_b, per_b, full(g), full(wo)],
        out_specs=[tok(D_MODEL), tok(D_MODEL)],
        out_shape=[jax.ShapeDtypeStruct((b, t, D_MODEL), F32), jax.ShapeDtypeStruct((b, t, D_MODEL), BF16)],
        compiler_params=_cparams(("arbitrary", "arbitrary")),
        name="out_proj",
    )(x, ys, yf, ym, g1, sh, sc, g, wo)


def _top16(s):
    nk, tn = s.shape
    ridx = _iota((nk, tn), 0).astype(F32)
    r16 = _iota((PEER_TOPK, tn), 0)

    def body(it, carry):
        s, pos, tv = carry
        m = jnp.max(s, axis=0, keepdims=True)
        first = jnp.min(jnp.where(s == m, ridx, float(nk)), axis=0, keepdims=True)
        sel = ridx == first
        pos = jnp.where(sel, it.astype(F32), pos)
        tv = jnp.where(r16 == it, m, tv)
        return jnp.where(sel, -jnp.inf, s), pos, tv

    init = (s, jnp.full((nk, tn), float(PEER_TOPK), F32), jnp.zeros((PEER_TOPK, tn), F32))
    _, pos, tv = lax.fori_loop(0, PEER_TOPK, body, init)
    return pos, tv


_CAND_GROUPS = [(0, 16)] + [(ka, 8) for ka in range(1, 8)]
_CAND_ROWS = 16 + 7 * 8 + 8


def _pair_select(ta, tb):
    tn = ta.shape[1]
    pieces, flats, valids = [], [], []
    for ka, rows in _CAND_GROUPS:
        pieces.append(ta[ka:ka + 1, :] + tb[0:rows, :])
        kb = _iota((rows, 1), 0)
        flats.append((ka * PEER_TOPK + kb).astype(F32))
        valids.append((ka + 1) * (kb + 1) <= PEER_TOPK)
    pieces.append(ta[8:16, :] + tb[0:1, :])
    flats.append(((8 + _iota((8, 1), 0)) * PEER_TOPK).astype(F32))
    valids.append(_iota((8, 1), 0) >= 0)
    cand0 = jnp.concatenate(pieces, axis=0)
    flat = jnp.concatenate(flats, axis=0)
    valid = jnp.concatenate(valids, axis=0)
    cand0 = jnp.where(valid, cand0, -jnp.inf)
    best = ta[0:1, :] + tb[0:1, :]

    def body(it, carry):
        cand, selm = carry
        m = jnp.max(cand, axis=0, keepdims=True)
        first = jnp.min(jnp.where(cand == m, flat, 4096.0), axis=0, keepdims=True)
        sel = flat == first
        return jnp.where(sel, -jnp.inf, cand), jnp.where(sel, 1.0, selm)

    _, selm = lax.fori_loop(0, PEER_TOPK, body, (cand0, jnp.zeros((_CAND_ROWS, tn), F32)))
    z = jnp.sum(jnp.where(selm > 0.0, jnp.exp(cand0 - best), 0.0), axis=0, keepdims=True)
    cnts = [jnp.sum(selm[0:16, :], axis=0, keepdims=True)]
    for i in range(1, 8):
        cnts.append(jnp.sum(selm[8 + 8 * i:16 + 8 * i, :], axis=0, keepdims=True))
    cnts.append(selm[_CAND_ROWS - 8:_CAND_ROWS, :])
    return jnp.concatenate(cnts, axis=0), z


def _peer_sel_kernel(h_ref, wqt_ref, keys_ref, ea_ref, la_ref, eb_ref, pb_ref):
    qt = _dot_nt(wqt_ref[...], h_ref[...])
    for h in range(PEER_HEADS):
        sa = _dot(keys_ref[2 * h], qt[PEER_HALF * (2 * h):PEER_HALF * (2 * h + 1), :].astype(BF16))
        sb = _dot(keys_ref[2 * h + 1], qt[PEER_HALF * (2 * h + 1):PEER_HALF * (2 * h + 2), :].astype(BF16))
        pos_a, ta = _top16(sa)
        pos_b, tb = _top16(sb)
        cnt, z = _pair_select(ta, tb)
        la = jnp.zeros_like(pos_a)
        for ka in range(PEER_TOPK):
            la = jnp.where(pos_a == float(ka), cnt[ka:ka + 1, :], la)
        ea_ref[h] = jnp.where(pos_a < float(PEER_TOPK), jnp.exp(sa - ta[0:1, :]), 0.0) / z
        la_ref[h] = la
        eb_ref[h] = jnp.where(pos_b < float(PEER_TOPK), jnp.exp(sb - tb[0:1, :]), 0.0)
        pb_ref[h] = pos_b


def _peer_select(h2, wqt, keys):
    n = h2.shape[0]
    tn = 256
    per_tok = pl.BlockSpec((PEER_HEADS, PEER_KEYS, tn), lambda i: (0, 0, i))
    shp = jax.ShapeDtypeStruct((PEER_HEADS, PEER_KEYS, n), F32)
    return pl.pallas_call(
        _peer_sel_kernel,
        grid=(n // tn,),
        in_specs=[pl.BlockSpec((tn, D_MODEL), lambda i: (i, 0)),
                  pl.BlockSpec(wqt.shape, lambda i: (0, 0)),
                  pl.BlockSpec(keys.shape, lambda i: (0, 0, 0))],
        out_specs=[per_tok] * 4,
        out_shape=[shp] * 4,
        compiler_params=_cparams(("arbitrary",)),
        name="peer_select",
    )(h2, wqt, keys)


def _peer_dense_kernel(tn, ec, h_ref, u_ref, vt_ref, ea_ref, la_ref, eb_ref, pb_ref, o_ref, at, ga, acc):
    e = pl.program_id(1)
    ne = pl.num_programs(1)

    @pl.when(e == 0)
    def _():
        acc[...] = jnp.zeros_like(acc)

    at[...] = _dot_nt(u_ref[...], h_ref[...])
    rows_per = ec // PEER_KEYS
    ia0 = pl.multiple_of(e * rows_per, rows_per)
    for r in range(rows_per):
        for c in range(tn // LANES):
            cs = slice(c * LANES, (c + 1) * LANES)
            gate = jnp.zeros((PEER_KEYS, LANES), F32)
            for h in range(PEER_HEADS):
                la = la_ref[h, pl.ds(ia0, rows_per), cs][r:r + 1, :]
                ea = ea_ref[h, pl.ds(ia0, rows_per), cs][r:r + 1, :]
                gate = gate + jnp.where(pb_ref[h, :, cs] < la, eb_ref[h, :, cs], 0.0) * ea
            a = at[r * PEER_KEYS:(r + 1) * PEER_KEYS, cs]
            ga[r * PEER_KEYS:(r + 1) * PEER_KEYS, cs] = (gate * _gelu(a)).astype(BF16)
    acc[...] += _dot(vt_ref[...], ga[...])

    @pl.when(e == ne - 1)
    def _():
        o_ref[...] = acc[...].T


def _peer_dense(h2, u_b, vt_b, ea, la, eb, pb):
    n = h2.shape[0]
    tn, ec = PEER_TOKEN_TILE, 1024
    per_tok = pl.BlockSpec((PEER_HEADS, PEER_KEYS, tn), lambda i, e: (0, 0, i))
    return pl.pallas_call(
        functools.partial(_peer_dense_kernel, tn, ec),
        grid=(n // tn, PEER_EXPERTS // ec),
        in_specs=[pl.BlockSpec((tn, D_MODEL), lambda i, e: (i, 0)),
                  pl.BlockSpec((ec, D_MODEL), lambda i, e: (e, 0)),
                  pl.BlockSpec((D_MODEL, ec), lambda i, e: (0, e)),
                  per_tok, per_tok, per_tok, per_tok],
        out_specs=pl.BlockSpec((tn, D_MODEL), lambda i, e: (i, 0)),
        out_shape=jax.ShapeDtypeStruct((n, D_MODEL), F32),
        scratch_shapes=[pltpu.VMEM((ec, tn), F32), pltpu.VMEM((ec, tn), BF16), pltpu.VMEM((D_MODEL, tn), F32)],
        compiler_params=_cparams(("arbitrary", "arbitrary")),
        name="peer_dense",
    )(h2, u_b, vt_b, ea, la, eb, pb)


def _resid_kernel(x_ref, p_ref, g_ref, o_ref):
    o_ref[0] = x_ref[0] + g_ref[0] * p_ref[0]


def _residual(x, peer, g2):
    b, t, _ = x.shape
    tm = min(t, 512)
    tok = pl.BlockSpec((1, tm, D_MODEL), lambda i, j: (i, j, 0))
    return pl.pallas_call(
        _resid_kernel,
        grid=(b, t // tm),
        in_specs=[tok, tok, pl.BlockSpec((1, 1, D_MODEL), lambda i, j: (i, 0, 0))],
        out_specs=tok,
        out_shape=jax.ShapeDtypeStruct((b, t, D_MODEL), F32),
        compiler_params=_cparams(("arbitrary", "arbitrary")),
        name="peer_residual",
    )(x, peer, g2)


def _lane_pad(vec, offset, width=LANES):
    out = jnp.zeros((width,), F32)
    return out.at[offset:offset + vec.shape[0]].set(vec.astype(F32))


def _layer_params(l, norm1_g, norm2_g, w_in, conv_w, conv_b, dt_bias, a_log, d_skip, ssd_norm_g, q_norm_g,
                  k_norm_g, fgate_b, w_s, b_s, w_out, peer_wq, peer_keys, peer_u, peer_v, mlp_len_sample):
    w = w_in[l]
    o = [0]
    for sz in (SSD_INNER, CONV_CH, SSD_HEADS, FOX_INNER, FOX_INNER, FOX_INNER, FOX_HEADS, MLP_INNER, MLP_INNER):
        o.append(o[-1] + sz)
    wz, wxbc, wdt, wq, wk, wv, wf, wu, wvm = [w[:, o[i]:o[i + 1]] for i in range(9)]
    w_small = jnp.zeros((D_MODEL, SMALL_COLS), F32)
    w_small = w_small.at[:, DT_ROW0:DT_ROW0 + SSD_HEADS].set(wdt).at[:, F_ROW0:F_ROW0 + FOX_HEADS].set(wf)
    w_r = jnp.concatenate([wz, wxbc, wq, wk, wv, wu, wvm, w_small], axis=1).astype(BF16)
    ws_t = w_small[:, :SMALL_ROWS].T.astype(BF16)
    a_neg = -jnp.exp(a_log[l].astype(F32))
    grp = jnp.arange(FOX_INNER) // FOX_HEAD_DIM
    gm = jnp.where(grp[:, None] == grp[None, :], 1.0 / FOX_HEAD_DIM, 0.0).astype(BF16)
    heads = jnp.arange(SSD_INNER) // SSD_HEAD_DIM
    e_mat = (jnp.arange(LANES)[:, None] == heads[None, :]).astype(BF16)
    return dict(
        norm1_g=norm1_g[l][None], norm2_g=norm2_g[l][None], w_r=w_r, ws_t=ws_t,
        gq=(jnp.tile(q_norm_g[l], FOX_HEADS) * (FOX_HEAD_DIM ** -0.5))[None],
        gk=jnp.tile(k_norm_g[l], FOX_HEADS)[None], gm=gm,
        conv_w=conv_w[l], conv_b=conv_b[l][None],
        dtb_c=_lane_pad(dt_bias[l], DT_ROW0)[None], dtb_r=dt_bias[l].astype(F32)[:, None],
        a_c=_lane_pad(a_neg, DT_ROW0)[None], a_r=a_neg[:, None],
        dsk_x=jnp.repeat(d_skip[l].astype(F32), SSD_HEAD_DIM)[None], ssd_g=ssd_norm_g[l][None], e_mat=e_mat,
        fb_c=_lane_pad(fgate_b[l], F_ROW0)[None], fb_r=_lane_pad(fgate_b[l], 0, 8)[:, None],
        w_s=w_s[l], bs_t=jnp.zeros((MLP_CHUNK, LANES), F32).at[:, :MLP_GROUPS].set(b_s[l].T),
        w_out=w_out[l].astype(BF16),
        wq_t=peer_wq[l].T.astype(BF16),
        keys=peer_keys[l].reshape(PEER_HEADS * 2, PEER_KEYS, PEER_HALF).astype(BF16),
        u_b=peer_u[l].astype(BF16), vt_b=peer_v[l].T.astype(BF16),
    )


def _stream_mixers(x, peer, g2_prev, mod, p, hist8, s0t, past):
    sh1, sc1, g1, sh2, sc2, _ = mod
    b, t, _ = x.shape
    x, (z, xbc, qb, kn, kb, v, vb, ug, vn, sm, smt) = _in_proj(
        x, peer, g2_prev, sh1, sc1, p['norm1_g'], p['w_r'], p['ws_t'], p['gq'], p['gk'], p['gm'])
    L = SSD_CHUNK_PROMPT if t % SSD_CHUNK_PROMPT == 0 else t
    y_ssd, s_fin, c_fin = _ssd(xbc, z, sm, smt, hist8, s0t, p['conv_w'], p['conv_b'], p['dtb_c'], p['dtb_r'],
                               p['a_c'], p['a_r'], p['dsk_x'], p['ssd_g'], p['e_mat'], L)
    zeros8 = jnp.zeros((b, 8, LANES), F32)
    if past is None:
        logf_c, f_col, f_row, _, _ = _fcum(sm, smt, F_ROW0 // 8, p['fb_c'], p['fb_r'], zeros8, zeros8, True)
        y_fox = _fox_prompt(qb, f_col, kb, vb, f_row)
    else:
        pk, pv, plf_col, plf_row = past
        _, _, fp_row, end_c, end_r = _fcum(plf_col, plf_row, 0, p['fb_c'], p['fb_r'], zeros8, zeros8, False)
        logf_c, f_col, f_row, _, _ = _fcum(sm, smt, F_ROW0 // 8, p['fb_c'], p['fb_r'], end_c, end_r, True)
        y_fox = _fox_sample(qb, f_col, pk, pv, fp_row, kb, vb, f_row)
    lm = MLP_CHUNK if t % MLP_CHUNK == 0 else t
    y_mlp = _chunk_mlp(ug, vn, p['w_s'][:, :lm, :lm], p['bs_t'][:lm])
    x1, h2 = _out_proj(x, y_ssd, y_fox, y_mlp, g1, sh2, sc2, p['norm2_g'], p['w_out'])
    logf = logf_c[:, :, F_ROW0:F_ROW0 + FOX_HEADS]
    new_ssm = s_fin.reshape(b, SSD_STATE, SSD_HEADS, SSD_HEAD_DIM).transpose(0, 2, 3, 1)
    new_conv = c_fin[:, 8 - (CONV_W - 1):, :]
    kc = kn.reshape(b, t, FOX_HEADS, FOX_HEAD_DIM)
    vc = v.reshape(b, t, FOX_HEADS, FOX_HEAD_DIM)
    return x1, h2, (kc, vc, logf, new_ssm, new_conv, vn)


def kernel(x_prompt, x_sample, c_prompt, c_sample, cache_fox_k, cache_fox_v, cache_fox_logf, state_ssm, state_conv, norm1_g, norm2_g, w_ada, b_ada, w_in, conv_w, conv_b, dt_bias, a_log, d_skip, ssd_norm_g, q_norm_g, k_norm_g, fgate_b, w_s, b_s, w_out, peer_wq, peer_keys, peer_u, peer_v):
    depth = w_ada.shape[0]
    bp, tp, _ = x_prompt.shape
    bs, ts, _ = x_sample.shape
    past_len = cache_fox_k.shape[2]
    mod_all = _modulation(jnp.concatenate([c_prompt, c_sample], axis=0).astype(F32), w_ada, b_ada)

    xp, xs = x_prompt, x_sample
    peer_p = peer_s = g2p = g2s = None
    outs = [[] for _ in range(11)]
    for l in range(depth):
        p = _layer_params(l, norm1_g, norm2_g, w_in, conv_w, conv_b, dt_bias, a_log, d_skip, ssd_norm_g, q_norm_g,
                          k_norm_g, fgate_b, w_s, b_s, w_out, peer_wq, peer_keys, peer_u, peer_v, ts)
        mods = [m[:, None, :] for m in jnp.split(mod_all[l], 6, axis=-1)]
        mod_p = [m[:bp] for m in mods]
        mod_s = [m[bp:] for m in mods]
        hist_p = jnp.zeros((bp, 8, CONV_CH), F32)
        s0_p = jnp.zeros((bp, SSD_STATE, SSD_INNER), F32)
        hist_s = jnp.concatenate([jnp.zeros((bs, 8 - (CONV_W - 1), CONV_CH), F32), state_conv[l].astype(F32)], axis=1)
        s0_s = state_ssm[l].astype(F32).transpose(0, 3, 1, 2).reshape(bs, SSD_STATE, SSD_INNER)
        plf = cache_fox_logf[l].astype(F32)
        plf_col = jnp.pad(plf, ((0, 0), (0, 0), (F_ROW0, LANES - F_ROW0 - FOX_HEADS)))
        plf_row = jnp.pad(plf.transpose(0, 2, 1), ((0, 0), (0, 8 - FOX_HEADS), (0, 0)))
        past = (cache_fox_k[l].reshape(bs, past_len, FOX_INNER), cache_fox_v[l].reshape(bs, past_len, FOX_INNER),
                plf_col, plf_row)

        x1p, h2p, st_p = _stream_mixers(xp, peer_p, g2p, mod_p, p, hist_p, s0_p, None)
        x1s, h2s, st_s = _stream_mixers(xs, peer_s, g2s, mod_s, p, hist_s, s0_s, past)

        n_tok = bp * tp + bs * ts
        n_pad = -n_tok % PEER_TOKEN_TILE
        h2 = jnp.concatenate([h2p.reshape(bp * tp, D_MODEL), h2s.reshape(bs * ts, D_MODEL),
                              jnp.zeros((n_pad, D_MODEL), BF16)], axis=0)
        ea, la, eb, pb = _peer_select(h2, p['wq_t'], p['keys'])
        peer = _peer_dense(h2, p['u_b'], p['vt_b'], ea, la, eb, pb)
        peer_p = peer[:bp * tp].reshape(bp, tp, D_MODEL)
        peer_s = peer[bp * tp:n_tok].reshape(bs, ts, D_MODEL)
        xp, xs, g2p, g2s = x1p, x1s, mod_p[5], mod_s[5]
        for i in range(5):
            outs[i].append(st_p[i])
        for i in range(6):
            outs[5 + i].append(st_s[i])

    yp = _residual(xp, peer_p, g2p)
    ys = _residual(xs, peer_s, g2s)
    return (yp, ys) + tuple(jnp.stack(o) for o in outs)
```

```python
import functools

import jax
import jax.numpy as jnp
from jax import lax
from jax.experimental import pallas as pl
from jax.experimental.pallas import tpu as pltpu

F32 = jnp.float32
BF16 = jnp.bfloat16
EPS = 1e-6

D_MODEL = 1024
SSD_HEADS = 8
SSD_HEAD_DIM = 64
SSD_INNER = SSD_HEADS * SSD_HEAD_DIM
SSD_GROUPS = 2
SSD_STATE = 64
CONV_W = 4
CONV_CH = SSD_INNER + 2 * SSD_GROUPS * SSD_STATE
FOX_HEADS = 4
FOX_HEAD_DIM = 64
FOX_INNER = FOX_HEADS * FOX_HEAD_DIM
MLP_GROUPS = 4
MLP_GROUP_DIM = 64
MLP_INNER = MLP_GROUPS * MLP_GROUP_DIM
MLP_CHUNK = 128
FOX_PROMPT_TILE = 512
SSD_CHUNK_PROMPT = 128
PEER_HEADS = 8
PEER_KEYS = 128
PEER_EXPERTS = PEER_KEYS * PEER_KEYS
PEER_HALF = 128
PEER_TOPK = 16
PEER_TOKEN_TILE = 512
LANES = 128
SMALL_COLS = LANES
SMALL_ROWS = 16
DT_ROW0, F_ROW0 = 0, 8
NEG = -1e30
VMEM_LIMIT = 56 * 1024 * 1024

_Z0, _XBC0, _Q0, _K0, _V0, _U0, _VM0, _SM0 = 0, 512, 1280, 1536, 1792, 2048, 2304, 2560
PROJ_COLS = _SM0 + SMALL_COLS


def _cparams(sem):
    return pltpu.CompilerParams(dimension_semantics=sem, vmem_limit_bytes=VMEM_LIMIT)


def _dot(a, b):
    return jnp.dot(a, b, preferred_element_type=F32)


def _dot_nt(a, b):
    return lax.dot_general(a, b, (((1,), (1,)), ((), ())), preferred_element_type=F32)


def _dot_tn(a, b):
    return lax.dot_general(a, b, (((0,), (0,)), ((), ())), preferred_element_type=F32)


def _split2(x):
    hi = x.astype(BF16)
    lo = (x - hi.astype(F32)).astype(BF16)
    return hi, lo


def _split3(x):
    hi = x.astype(BF16)
    r = x - hi.astype(F32)
    mid = r.astype(BF16)
    lo = (r - mid.astype(F32)).astype(BF16)
    return hi, mid, lo


def _dot3_l(x, w):
    hi, mid, lo = _split3(x)
    return _dot(hi, w) + _dot(mid, w) + _dot(lo, w)


def _dot3_r(w, x):
    hi, mid, lo = _split3(x)
    return _dot(w, hi) + _dot(w, mid) + _dot(w, lo)


def _sigmoid(x):
    return 1.0 / (1.0 + jnp.exp(-x))


def _silu(x):
    return x * _sigmoid(x)


def _softplus(x):
    return jnp.maximum(x, 0.0) + jnp.log1p(jnp.exp(-jnp.abs(x)))


def _gelu(x):
    return 0.5 * x * (1.0 + jnp.tanh(0.7978845608028654 * (x + 0.044715 * (x * x * x))))


def _iota(shape, dim):
    return lax.broadcasted_iota(jnp.int32, shape, dim)


def _mod_kernel(c_ref, w_ref, b_ref, o_ref):
    c = c_ref[...]
    o_ref[0] = jnp.dot(_silu(c), w_ref[0], preferred_element_type=F32,
                       precision=lax.Precision.HIGHEST) + b_ref[0]


def _modulation(c_all, w_ada, b_ada):
    depth, _, n6 = w_ada.shape
    bc = c_all.shape[0]
    tn = 1536
    return pl.pallas_call(
        _mod_kernel,
        grid=(depth, n6 // tn),
        in_specs=[pl.BlockSpec((bc, D_MODEL), lambda l, j: (0, 0)),
                  pl.BlockSpec((1, D_MODEL, tn), lambda l, j: (l, 0, j)),
                  pl.BlockSpec((1, 1, tn), lambda l, j: (l, 0, j))],
        out_specs=pl.BlockSpec((1, bc, tn), lambda l, j: (l, 0, j)),
        out_shape=jax.ShapeDtypeStruct((depth, bc, n6), F32),
        compiler_params=_cparams(("arbitrary", "arbitrary")),
        name="adaln_mod",
    )(c_all, w_ada, b_ada.reshape(depth, 1, n6))


def _in_kernel(with_peer, *refs):
    if with_peer:
        x_ref, p_ref, g2_ref = refs[:3]
        refs = refs[3:]
    else:
        x_ref = refs[0]
        refs = refs[1:]
    (sh_ref, sc_ref, g_ref, w_ref, wst_ref, gq_ref, gk_ref, gm_ref) = refs[:8]
    outs = refs[8:]
    if with_peer:
        xo_ref = outs[0]
        outs = outs[1:]
    (z_ref, xbc_ref, q_ref, k_ref, kb_ref, v_ref, vb_ref, ug_ref, vn_ref, sm_ref, smt_ref) = outs

    x = x_ref[0]
    if with_peer:
        x = x + g2_ref[0] * p_ref[0]
        xo_ref[0] = x
    ms = jnp.mean(x * x, axis=-1, keepdims=True)
    h = x * lax.rsqrt(ms + EPS) * g_ref[...]
    h = h * (1.0 + sc_ref[0]) + sh_ref[0]
    hb = h.astype(BF16)
    proj = _dot(hb, w_ref[...])
    z_ref[0] = proj[:, _Z0:_XBC0]
    xbc_ref[0] = proj[:, _XBC0:_Q0]
    q = proj[:, _Q0:_K0]
    k = proj[:, _K0:_V0]
    v = proj[:, _V0:_U0]
    u = proj[:, _U0:_VM0]
    vm = proj[:, _VM0:_SM0]
    sm_ref[0] = proj[:, _SM0:PROJ_COLS]
    smt_ref[0] = _dot_nt(wst_ref[...], hb)
    gm = gm_ref[...]

    def gmean(y):
        hi, lo = _split2(y)
        return _dot(hi, gm) + _dot(lo, gm)

    qn = q * lax.rsqrt(gmean(q * q) + EPS) * gq_ref[...]
    kn = k * lax.rsqrt(gmean(k * k) + EPS) * gk_ref[...]
    q_ref[0] = qn.astype(BF16)
    k_ref[0] = kn
    kb_ref[0] = kn.astype(BF16)
    v_ref[0] = v
    vb_ref[0] = v.astype(BF16)
    ug_ref[0] = _gelu(u)
    gv = _gelu(vm)
    mu = gmean(gv)
    cen = gv - mu
    var = gmean(cen * cen)
    vn_ref[0] = cen * lax.rsqrt(var + EPS)


def _in_proj(x, peer, g2, sh, sc, g, w_r, ws_t, gq, gk, gm):
    b, t, _ = x.shape
    tm = min(t, 512)
    with_peer = peer is not None
    tok = lambda c: pl.BlockSpec((1, tm, c), lambda i, j: (i, j, 0))
    per_b = pl.BlockSpec((1, 1, D_MODEL), lambda i, j: (i, 0, 0))
    full = lambda a: pl.BlockSpec(a.shape, lambda i, j: (0,) * a.ndim)
    in_specs = [tok(D_MODEL)]
    args = [x]
    if with_peer:
        in_specs += [tok(D_MODEL), per_b]
        args += [peer, g2]
    in_specs += [per_b, per_b, full(g), full(w_r), full(ws_t), full(gq), full(gk), full(gm)]
    args += [sh, sc, g, w_r, ws_t, gq, gk, gm]
    out_cols = [(SSD_INNER, F32), (CONV_CH, F32), (FOX_INNER, BF16), (FOX_INNER, F32), (FOX_INNER, BF16),
                (FOX_INNER, F32), (FOX_INNER, BF16), (MLP_INNER, F32), (MLP_INNER, F32), (SMALL_COLS, F32)]
    out_specs = [tok(c) for c, _ in out_cols]
    out_shape = [jax.ShapeDtypeStruct((b, t, c), dt) for c, dt in out_cols]
    out_specs.append(pl.BlockSpec((1, SMALL_ROWS, tm), lambda i, j: (i, 0, j)))
    out_shape.append(jax.ShapeDtypeStruct((b, SMALL_ROWS, t), F32))
    if with_peer:
        out_specs = [tok(D_MODEL)] + out_specs
        out_shape = [jax.ShapeDtypeStruct((b, t, D_MODEL), F32)] + out_shape
    res = pl.pallas_call(
        functools.partial(_in_kernel, with_peer),
        grid=(b, t // tm),
        in_specs=in_specs, out_specs=out_specs, out_shape=out_shape,
        compiler_params=_cparams(("arbitrary", "arbitrary")),
        name="in_proj",
    )(*args)
    if with_peer:
        return res[0], res[1:]
    return x, res


def _ssd_kernel(L, xbc_ref, z_ref, sm_ref, smt_ref, hist_ref, s0_ref, cw_ref, cb_ref, dtbc_ref, dtbr_ref,
                ac_ref, ar_ref, dsk_ref, ng_ref, e_ref, y_ref, sfin_ref, cfin_ref, xpad, st, ybuf):
    c = pl.program_id(1)
    nc = pl.num_programs(1)

    @pl.when(c == 0)
    def _():
        xpad[0:8, :] = hist_ref[0]
        st[...] = s0_ref[0]

    xpad[8:8 + L, :] = xbc_ref[0]
    conv = cb_ref[...]
    for tap in range(CONV_W):
        conv = conv + xpad[5 + tap:5 + tap + L, :] * cw_ref[tap:tap + 1, :]
    tail = xpad[L:L + 8, :]
    xpad[0:8, :] = tail
    xc = _silu(conv)
    xs = xc[:, 0:SSD_INNER]

    r_i = _iota((L, L), 0)
    c_i = _iota((L, L), 1)
    causal = r_i >= c_i
    tri = jnp.where(causal, 1.0, 0.0).astype(BF16)
    triu = jnp.where(r_i <= c_i, 1.0, 0.0).astype(BF16)

    dtc = _softplus(sm_ref[0] + dtbc_ref[...])
    acum_c = _dot3_r(tri, dtc * ac_ref[...])
    e = e_ref[...]
    acum_x = _dot3_l(acum_c, e)
    dt_x = _dot3_l(dtc, e)
    dtr = _softplus(smt_ref[0][DT_ROW0:DT_ROW0 + 8, :] + dtbr_ref[...])
    acum_r = _dot3_l(dtr * ar_ref[...], triu)

    bmat = [xc[:, SSD_INNER + SSD_STATE * g:SSD_INNER + SSD_STATE * (g + 1)].astype(BF16) for g in range(SSD_GROUPS)]
    c0 = SSD_INNER + SSD_GROUPS * SSD_STATE
    cmat = [xc[:, c0 + SSD_STATE * g:c0 + SSD_STATE * (g + 1)].astype(BF16) for g in range(SSD_GROUPS)]
    cb = [_dot_nt(cmat[g], bmat[g]) for g in range(SSD_GROUPS)]
    xsb = xs.astype(BF16)
    hpg = SSD_HEADS // SSD_GROUPS
    for h in range(SSD_HEADS):
        g = h // hpg
        seg = acum_c[:, h:h + 1] - acum_r[h:h + 1, :]
        dec = jnp.where(causal, jnp.exp(jnp.minimum(seg, 0.0)), 0.0)
        m = cb[g] * dec * dtr[h:h + 1, :]
        ybuf[:, SSD_HEAD_DIM * h:SSD_HEAD_DIM * (h + 1)] = _dot(
            m.astype(BF16), xsb[:, SSD_HEAD_DIM * h:SSD_HEAD_DIM * (h + 1)])

    gw = SSD_INNER // SSD_GROUPS
    stb = st[...].astype(BF16)
    y_off = jnp.concatenate([_dot(cmat[g], stb[:, gw * g:gw * (g + 1)]) for g in range(SSD_GROUPS)], axis=1)
    y = ybuf[...] + y_off * jnp.exp(acum_x) + dsk_ref[...] * xs
    a_end = acum_x[L - 1:L, :]
    xw = (xs * dt_x * jnp.exp(a_end - acum_x)).astype(BF16)
    new_states = jnp.concatenate([_dot_tn(bmat[g], xw[:, gw * g:gw * (g + 1)]) for g in range(SSD_GROUPS)], axis=1)
    st[...] = st[...] * jnp.exp(a_end) + new_states

    yg = y * _silu(z_ref[0])
    ms = jnp.mean(yg * yg, axis=-1, keepdims=True)
    y_ref[0] = yg * lax.rsqrt(ms + EPS) * ng_ref[...]

    @pl.when(c == nc - 1)
    def _():
        sfin_ref[0] = st[...]
        cfin_ref[0] = tail


def _ssd(xbc, z, sm, smt, hist8, s0t, cw, cb, dtb_c, dtb_r, a_c, a_r, dsk_x, ng, e_mat, L):
    b, t, _ = xbc.shape
    tok = lambda c: pl.BlockSpec((1, L, c), lambda i, j: (i, j, 0))
    full = lambda a: pl.BlockSpec(a.shape, lambda i, j: (0,) * a.ndim)
    per_b = lambda a: pl.BlockSpec((1,) + a.shape[1:], lambda i, j: (i,) + (0,) * (a.ndim - 1))
    return pl.pallas_call(
        functools.partial(_ssd_kernel, L),
        grid=(b, t // L),
        in_specs=[tok(CONV_CH), tok(SSD_INNER), tok(SMALL_COLS),
                  pl.BlockSpec((1, SMALL_ROWS, L), lambda i, j: (i, 0, j)),
                  per_b(hist8), per_b(s0t), full(cw), full(cb), full(dtb_c), full(dtb_r), full(a_c), full(a_r),
                  full(dsk_x), full(ng), full(e_mat)],
        out_specs=[tok(SSD_INNER),
                   pl.BlockSpec((1, SSD_STATE, SSD_INNER), lambda i, j: (i, 0, 0)),
                   pl.BlockSpec((1, 8, CONV_CH), lambda i, j: (i, 0, 0))],
        out_shape=[jax.ShapeDtypeStruct((b, t, SSD_INNER), F32),
                   jax.ShapeDtypeStruct((b, SSD_STATE, SSD_INNER), F32),
                   jax.ShapeDtypeStruct((b, 8, CONV_CH), F32)],
        scratch_shapes=[pltpu.VMEM((L + 8, CONV_CH), F32), pltpu.VMEM((SSD_STATE, SSD_INNER), F32),
                        pltpu.VMEM((L, SSD_INNER), F32)],
        compiler_params=_cparams(("arbitrary", "arbitrary")),
        name="ssd_scan",
    )(xbc, z, sm, smt, hist8, s0t, cw, cb, dtb_c, dtb_r, a_c, a_r, dsk_x, ng, e_mat)


def _fcum_kernel(tf, activate, colsrc_ref, rowsrc_ref, fbc_ref, fbr_ref, initc_ref, initr_ref,
                 lfc_ref, fc_ref, fr_ref, endc_ref, endr_ref, carc, carr):
    j = pl.program_id(1)

    @pl.when(j == 0)
    def _():
        carc[...] = initc_ref[0]
        carr[...] = initr_ref[0]

    xc = colsrc_ref[0]
    xr = rowsrc_ref[0]
    if activate:
        xc = -_softplus(-(xc + fbc_ref[...]))
        xr = -_softplus(-(xr + fbr_ref[...]))
    r_i = _iota((tf, tf), 0)
    c_i = _iota((tf, tf), 1)
    tri = jnp.where(r_i >= c_i, 1.0, 0.0).astype(BF16)
    triu = jnp.where(r_i <= c_i, 1.0, 0.0).astype(BF16)
    fcol = carc[0:1, :] + _dot3_r(tri, xc)
    frow = carr[:, 0:1] + _dot3_l(xr, triu)
    lfc_ref[0] = xc
    fc_ref[0] = fcol
    fr_ref[0] = frow
    carc[...] = jnp.broadcast_to(fcol[tf - 1:tf, :], carc.shape)
    carr[...] = jnp.broadcast_to(frow[:, tf - 1:tf], carr.shape)
    endc_ref[0] = carc[...]
    endr_ref[0] = carr[...]


def _fcum(colsrc, rowsrc, row_block, fb_c, fb_r, init_c, init_r, activate):
    b, t, _ = colsrc.shape
    tf = min(t, 512)
    full = lambda a: pl.BlockSpec(a.shape, lambda i, j: (0,) * a.ndim)
    per_b = lambda a: pl.BlockSpec((1,) + a.shape[1:], lambda i, j: (i,) + (0,) * (a.ndim - 1))
    return pl.pallas_call(
        functools.partial(_fcum_kernel, tf, activate),
        grid=(b, t // tf),
        in_specs=[pl.BlockSpec((1, tf, LANES), lambda i, j: (i, j, 0)),
                  pl.BlockSpec((1, 8, tf), lambda i, j: (i, row_block, j)),
                  full(fb_c), full(fb_r), per_b(init_c), per_b(init_r)],
        out_specs=[pl.BlockSpec((1, tf, LANES), lambda i, j: (i, j, 0)),
                   pl.BlockSpec((1, tf, LANES), lambda i, j: (i, j, 0)),
                   pl.BlockSpec((1, 8, tf), lambda i, j: (i, 0, j)),
                   pl.BlockSpec((1, 8, LANES), lambda i, j: (i, 0, 0)),
                   pl.BlockSpec((1, 8, LANES), lambda i, j: (i, 0, 0))],
        out_shape=[jax.ShapeDtypeStruct((b, t, LANES), F32), jax.ShapeDtypeStruct((b, t, LANES), F32),
                   jax.ShapeDtypeStruct((b, 8, t), F32), jax.ShapeDtypeStruct((b, 8, LANES), F32),
                   jax.ShapeDtypeStruct((b, 8, LANES), F32)],
        scratch_shapes=[pltpu.VMEM((8, LANES), F32), pltpu.VMEM((8, LANES), F32)],
        compiler_params=_cparams(("arbitrary", "arbitrary")),
        name="forget_cumsum",
    )(colsrc, rowsrc, fb_c, fb_r, init_c, init_r)


def _softmax_step(s, vb, carry):
    m, l, acc = carry
    tq, tk = s.shape
    m_new = jnp.maximum(m, jnp.max(s, axis=-1, keepdims=True))
    p = jnp.exp(s - m_new)
    alpha = jnp.exp(m - m_new)
    if tk % LANES == 0:
        psum = p[:, 0:LANES]
        for c in range(1, tk // LANES):
            psum = psum + p[:, c * LANES:(c + 1) * LANES]
    else:
        psum = jnp.where(_iota((tq, LANES), 1) == 0, jnp.sum(p, axis=-1, keepdims=True), 0.0)
    l = alpha * l + psum
    acc = alpha * acc + _dot(p.astype(BF16), vb)
    return m_new, l, acc


def _softmax_init(tq):
    return tuple((jnp.full((tq, 1), NEG, F32), jnp.zeros((tq, LANES), F32), jnp.zeros((tq, FOX_HEAD_DIM), F32))
                 for _ in range(FOX_HEADS))


def _head_slice(h):
    return slice(FOX_HEAD_DIM * h, FOX_HEAD_DIM * (h + 1))


def _attend_block(qs, fqs, kb, vb, fk, mask, carries):
    out = []
    for h in range(FOX_HEADS):
        hs = _head_slice(h)
        s = _dot_nt(qs[h], kb[:, hs]) + fqs[h] - fk[h:h + 1, :]
        if mask is not None:
            s = jnp.where(mask, s, NEG)
        out.append(_softmax_step(s, vb[:, hs], carries[h]))
    return tuple(out)


def _attend_finish(carries, o_ref):
    for h in range(FOX_HEADS):
        _, l, acc = carries[h]
        o_ref[0, :, _head_slice(h)] = acc / jnp.sum(l, axis=-1, keepdims=True)


def _fox_prompt_kernel(tq, q_ref, fq_ref, k_ref, v_ref, fk_ref, o_ref):
    qi = pl.program_id(1)
    q_all = q_ref[0]
    fq_all = fq_ref[0]
    qs = [q_all[:, _head_slice(h)] for h in range(FOX_HEADS)]
    fqs = [fq_all[:, F_ROW0 + h:F_ROW0 + h + 1] for h in range(FOX_HEADS)]
    causal = _iota((tq, tq), 0) >= _iota((tq, tq), 1)

    def block(j, carries, mask):
        start = pl.multiple_of(j * tq, tq)
        return _attend_block(qs, fqs, k_ref[0, pl.ds(start, tq), :], v_ref[0, pl.ds(start, tq), :],
                             fk_ref[0, j], mask, carries)

    carries = lax.fori_loop(0, qi, lambda j, c: block(j, c, None), _softmax_init(tq))
    _attend_finish(block(qi, carries, causal), o_ref)


def _fox_prompt(qb, fcol, kb, vb, frow):
    b, t, _ = qb.shape
    tq = min(t, FOX_PROMPT_TILE)
    nk = t // tq
    fk = frow.reshape(b, 8, nk, tq).transpose(0, 2, 1, 3)
    return pl.pallas_call(
        functools.partial(_fox_prompt_kernel, tq),
        grid=(b, t // tq),
        in_specs=[pl.BlockSpec((1, tq, FOX_INNER), lambda i, j: (i, j, 0)),
                  pl.BlockSpec((1, tq, LANES), lambda i, j: (i, j, 0)),
                  pl.BlockSpec((1, t, FOX_INNER), lambda i, j: (i, 0, 0)),
                  pl.BlockSpec((1, t, FOX_INNER), lambda i, j: (i, 0, 0)),
                  pl.BlockSpec((1, nk, 8, tq), lambda i, j: (i, 0, 0, 0))],
        out_specs=pl.BlockSpec((1, tq, FOX_INNER), lambda i, j: (i, j, 0)),
        out_shape=jax.ShapeDtypeStruct((b, t, FOX_INNER), F32),
        compiler_params=_cparams(("arbitrary", "arbitrary")),
        name="fox_prompt",
    )(qb, fcol, kb, vb, fk)


def _fox_sample_kernel(tq, tk, npast, q_ref, fq_ref, pk_ref, pv_ref, fpk_ref, k_ref, v_ref, fk_ref, o_ref):
    q_all = q_ref[0]
    fq_all = fq_ref[0]
    qs = [q_all[:, _head_slice(h)] for h in range(FOX_HEADS)]
    fqs = [fq_all[:, F_ROW0 + h:F_ROW0 + h + 1] for h in range(FOX_HEADS)]
    causal = _iota((tq, tq), 0) >= _iota((tq, tq), 1)

    def past_block(j, carries):
        start = pl.multiple_of(j * tk, tk)
        return _attend_block(qs, fqs, pk_ref[0, pl.ds(start, tk), :].astype(BF16),
                             pv_ref[0, pl.ds(start, tk), :].astype(BF16), fpk_ref[0, j], None, carries)

    carries = lax.fori_loop(0, npast, past_block, _softmax_init(tq))
    _attend_finish(_attend_block(qs, fqs, k_ref[0], v_ref[0], fk_ref[0], causal, carries), o_ref)


def _fox_sample(qb, fcol, past_k, past_v, fpast_row, kb, vb, frow):
    b, t, _ = qb.shape
    p = past_k.shape[1]
    tk = min(p, 512)
    npast = p // tk
    fpk = fpast_row.reshape(b, 8, npast, tk).transpose(0, 2, 1, 3)
    bspec = lambda a: pl.BlockSpec((1,) + a.shape[1:], lambda i: (i,) + (0,) * (a.ndim - 1))
    args = (qb, fcol, past_k, past_v, fpk, kb, vb, frow)
    return pl.pallas_call(
        functools.partial(_fox_sample_kernel, t, tk, npast),
        grid=(b,),
        in_specs=[bspec(a) for a in args],
        out_specs=pl.BlockSpec((1, t, FOX_INNER), lambda i: (i, 0, 0)),
        out_shape=jax.ShapeDtypeStruct((b, t, FOX_INNER), F32),
        compiler_params=_cparams(("arbitrary",)),
        name="fox_sample",
    )(*args)


def _mlp_kernel(lm, nchunk, ug_ref, vn_ref, ws_ref, bst_ref, y_ref):
    r_i = _iota((lm, lm), 0)
    c_i = _iota((lm, lm), 1)
    tril = r_i >= c_i
    vn = vn_ref[0].astype(BF16)
    ug = ug_ref[0]
    for g in range(MLP_GROUPS):
        lo, hi = MLP_GROUP_DIM * g, MLP_GROUP_DIM * (g + 1)
        w = jnp.where(tril, ws_ref[g], 0.0).astype(BF16)
        bias = bst_ref[:, g:g + 1]
        for c in range(nchunk):
            r0, r1 = c * lm, (c + 1) * lm
            sv = _dot(w, vn[r0:r1, lo:hi]) + bias
            y_ref[0, r0:r1, lo:hi] = ug[r0:r1, lo:hi] * sv


def _chunk_mlp(ug, vn, ws, bst):
    b, t, _ = ug.shape
    lm = ws.shape[1]
    tm = min(t, 4 * lm)
    tok = pl.BlockSpec((1, tm, MLP_INNER), lambda i, j: (i, j, 0))
    return pl.pallas_call(
        functools.partial(_mlp_kernel, lm, tm // lm),
        grid=(b, t // tm),
        in_specs=[tok, tok, pl.BlockSpec(ws.shape, lambda i, j: (0, 0, 0)),
                  pl.BlockSpec(bst.shape, lambda i, j: (0, 0))],
        out_specs=tok,
        out_shape=jax.ShapeDtypeStruct((b, t, MLP_INNER), F32),
        compiler_params=_cparams(("arbitrary", "arbitrary")),
        name="chunk_mlp",
    )(ug, vn, ws, bst)


def _out_kernel(x_ref, ys_ref, yf_ref, ym_ref, g1_ref, sh_ref, sc_ref, g_ref, wo_ref, x1_ref, h2_ref):
    mix = (_dot(ys_ref[0].astype(BF16), wo_ref[0:SSD_INNER, :])
           + _dot(yf_ref[0].astype(BF16), wo_ref[SSD_INNER:SSD_INNER + FOX_INNER, :])
           + _dot(ym_ref[0].astype(BF16), wo_ref[SSD_INNER + FOX_INNER:D_MODEL, :]))
    x1 = x_ref[0] + g1_ref[0] * mix
    x1_ref[0] = x1
    ms = jnp.mean(x1 * x1, axis=-1, keepdims=True)
    h = x1 * lax.rsqrt(ms + EPS) * g_ref[...]
    h2_ref[0] = (h * (1.0 + sc_ref[0]) + sh_ref[0]).astype(BF16)


def _out_proj(x, ys, yf, ym, g1, sh, sc, g, wo):
    b, t, _ = x.shape
    tm = min(t, 512)
    tok = lambda c: pl.BlockSpec((1, tm, c), lambda i, j: (i, j, 0))
    per_b = pl.BlockSpec((1, 1, D_MODEL), lambda i, j: (i, 0, 0))
    full = lambda a: pl.BlockSpec(a.shape, lambda i, j: (0,) * a.ndim)
    return pl.pallas_call(
        _out_kernel,
        grid=(b, t // tm),
        in_specs=[tok(D_MODEL), tok(SSD_INNER), tok(FOX_INNER), tok(MLP_INNER), per_b, per_b, per_b, full(g), full(wo)],
        out_specs=[tok(D_MODEL), tok(D_MODEL)],
        out_shape=[jax.ShapeDtypeStruct((b, t, D_MODEL), F32), jax.ShapeDtypeStruct((b, t, D_MODEL), BF16)],
        compiler_params=_cparams(("arbitrary", "arbitrary")),
        name="out_proj",
    )(x, ys, yf, ym, g1, sh, sc, g, wo)


def _top16(s):
    nk, tn = s.shape
    ridx = _iota((nk, tn), 0).astype(F32)
    r16 = _iota((PEER_TOPK, tn), 0)
    pos = jnp.full((nk, tn), float(PEER_TOPK), F32)
    tv = jnp.zeros((PEER_TOPK, tn), F32)
    for it in range(PEER_TOPK):
        m = jnp.max(s, axis=0, keepdims=True)
        first = jnp.min(jnp.where(s == m, ridx, float(nk)), axis=0, keepdims=True)
        sel = ridx == first
        pos = jnp.where(sel, float(it), pos)
        tv = jnp.where(r16 == it, m, tv)
        s = jnp.where(sel, -jnp.inf, s)
    return pos, tv


_CAND_GROUPS = [(0, 16)] + [(ka, 8) for ka in range(1, 8)]
_CAND_ROWS = 16 + 7 * 8 + 8


def _pair_select(ta, tb):
    tn = ta.shape[1]
    pieces, flats, valids = [], [], []
    for ka, rows in _CAND_GROUPS:
        pieces.append(ta[ka:ka + 1, :] + tb[0:rows, :])
        kb = _iota((rows, 1), 0)
        flats.append((ka * PEER_TOPK + kb).astype(F32))
        valids.append((ka + 1) * (kb + 1) <= PEER_TOPK)
    pieces.append(ta[8:16, :] + tb[0:1, :])
    flats.append(((8 + _iota((8, 1), 0)) * PEER_TOPK).astype(F32))
    valids.append(_iota((8, 1), 0) >= 0)
    cand0 = jnp.concatenate(pieces, axis=0)
    flat = jnp.concatenate(flats, axis=0)
    valid = jnp.concatenate(valids, axis=0)
    cand0 = jnp.where(valid, cand0, -jnp.inf)
    best = ta[0:1, :] + tb[0:1, :]

    cand = cand0
    selm = jnp.zeros((_CAND_ROWS, tn), F32)
    for _ in range(PEER_TOPK):
        m = jnp.max(cand, axis=0, keepdims=True)
        first = jnp.min(jnp.where(cand == m, flat, 4096.0), axis=0, keepdims=True)
        sel = flat == first
        cand = jnp.where(sel, -jnp.inf, cand)
        selm = jnp.where(sel, 1.0, selm)
    z = jnp.sum(jnp.where(selm > 0.0, jnp.exp(cand0 - best), 0.0), axis=0, keepdims=True)
    cnts = [jnp.sum(selm[0:16, :], axis=0, keepdims=True)]
    for i in range(1, 8):
        cnts.append(jnp.sum(selm[8 + 8 * i:16 + 8 * i, :], axis=0, keepdims=True))
    cnts.append(selm[_CAND_ROWS - 8:_CAND_ROWS, :])
    return jnp.concatenate(cnts, axis=0), z


def _peer_sel_kernel(tn, h_ref, wqt_ref, keys_ref, ea_ref, la_ref, eb_ref, pb_ref, qt_ref):
    qt_ref[...] = _dot_nt(wqt_ref[...], h_ref[...]).astype(BF16)

    def head(h, carry):
        row = pl.multiple_of(h * (2 * PEER_HALF), 2 * PEER_HALF)
        sa_all = _dot(keys_ref[2 * h], qt_ref[pl.ds(row, PEER_HALF), :])
        sb_all = _dot(keys_ref[2 * h + 1], qt_ref[pl.ds(row + PEER_HALF, PEER_HALF), :])
        for c in range(tn // LANES):
            cs = slice(c * LANES, (c + 1) * LANES)
            sa, sb = sa_all[:, cs], sb_all[:, cs]
            pos_a, ta = _top16(sa)
            pos_b, tb = _top16(sb)
            cnt, z = _pair_select(ta, tb)
            la = jnp.zeros_like(pos_a)
            for ka in range(PEER_TOPK):
                la = jnp.where(pos_a == float(ka), cnt[ka:ka + 1, :], la)
            ea_ref[h, :, cs] = jnp.where(pos_a < float(PEER_TOPK), jnp.exp(sa - ta[0:1, :]), 0.0) / z
            la_ref[h, :, cs] = la
            eb = jnp.where(pos_b < float(PEER_TOPK), jnp.exp(sb - tb[0:1, :]), 0.0)
            eb_ref[h, :, cs] = pltpu.bitcast(eb.astype(BF16), jnp.uint32)
            pb_ref[h, :, cs] = pltpu.bitcast(pos_b.astype(BF16), jnp.uint32)
        return carry

    lax.fori_loop(0, PEER_HEADS, head, 0)


def _peer_select(h2, wqt, keys):
    n = h2.shape[0]
    tn = 256
    per_tok = pl.BlockSpec((PEER_HEADS, PEER_KEYS, tn), lambda i: (0, 0, i))
    per_tok_pk = pl.BlockSpec((PEER_HEADS, PEER_KEYS // 2, tn), lambda i: (0, 0, i))
    shp = lambda dt: jax.ShapeDtypeStruct((PEER_HEADS, PEER_KEYS, n), dt)
    shp_pk = jax.ShapeDtypeStruct((PEER_HEADS, PEER_KEYS // 2, n), jnp.uint32)
    return pl.pallas_call(
        functools.partial(_peer_sel_kernel, tn),
        grid=(n // tn,),
        in_specs=[pl.BlockSpec((tn, D_MODEL), lambda i: (i, 0)),
                  pl.BlockSpec(wqt.shape, lambda i: (0, 0)),
                  pl.BlockSpec(keys.shape, lambda i: (0, 0, 0))],
        out_specs=[per_tok, per_tok, per_tok_pk, per_tok_pk],
        out_shape=[shp(F32), shp(F32), shp_pk, shp_pk],
        scratch_shapes=[pltpu.VMEM((PEER_HEADS * 2 * PEER_HALF, tn), BF16)],
        compiler_params=_cparams(("arbitrary",)),
        name="peer_select",
    )(h2, wqt, keys)


def _peer_dense_kernel(tn, ec, h_ref, u_ref, vt_ref, ea_ref, la_ref, eb_ref, pb_ref, o_ref, at, ga, acc):
    e = pl.program_id(1)
    ne = pl.num_programs(1)

    @pl.when(e == 0)
    def _():
        acc[...] = jnp.zeros_like(acc)

    at[...] = _dot_nt(u_ref[...], h_ref[...])
    rows_per = ec // PEER_KEYS
    ia0 = pl.multiple_of(e * rows_per, rows_per)
    for r in range(rows_per):
        for c in range(tn // LANES):
            cs = slice(c * LANES, (c + 1) * LANES)
            gate = None
            for h in range(PEER_HEADS):
                la = la_ref[h, pl.ds(ia0, rows_per), cs][r:r + 1, :]
                ea = ea_ref[h, pl.ds(ia0, rows_per), cs][r:r + 1, :]
                la = jnp.broadcast_to(la, (PEER_KEYS, LANES)).astype(BF16)
                ea = jnp.broadcast_to(ea, (PEER_KEYS, LANES)).astype(BF16)
                pb = pltpu.bitcast(pb_ref[h, :, cs], BF16)
                eb = pltpu.bitcast(eb_ref[h, :, cs], BF16)
                term = jnp.where(pb < la, eb, 0.0) * ea
                gate = term if gate is None else gate + term
            a = at[r * PEER_KEYS:(r + 1) * PEER_KEYS, cs]
            ga[r * PEER_KEYS:(r + 1) * PEER_KEYS, cs] = gate * _gelu(a).astype(BF16)
    acc[...] += _dot(vt_ref[...], ga[...])

    @pl.when(e == ne - 1)
    def _():
        o_ref[...] = acc[...].T


def _peer_dense(h2, u_b, vt_b, ea, la, eb, pb):
    n = h2.shape[0]
    tn, ec = PEER_TOKEN_TILE, 1024
    per_tok = pl.BlockSpec((PEER_HEADS, PEER_KEYS, tn), lambda i, e: (0, 0, i))
    per_tok_pk = pl.BlockSpec((PEER_HEADS, PEER_KEYS // 2, tn), lambda i, e: (0, 0, i))
    return pl.pallas_call(
        functools.partial(_peer_dense_kernel, tn, ec),
        grid=(n // tn, PEER_EXPERTS // ec),
        in_specs=[pl.BlockSpec((tn, D_MODEL), lambda i, e: (i, 0)),
                  pl.BlockSpec((ec, D_MODEL), lambda i, e: (e, 0)),
                  pl.BlockSpec((D_MODEL, ec), lambda i, e: (0, e)),
                  per_tok, per_tok, per_tok_pk, per_tok_pk],
        out_specs=pl.BlockSpec((tn, D_MODEL), lambda i, e: (i, 0)),
        out_shape=jax.ShapeDtypeStruct((n, D_MODEL), F32),
        scratch_shapes=[pltpu.VMEM((ec, tn), F32), pltpu.VMEM((ec, tn), BF16), pltpu.VMEM((D_MODEL, tn), F32)],
        compiler_params=_cparams(("arbitrary", "arbitrary")),
        name="peer_dense",
    )(h2, u_b, vt_b, ea, la, eb, pb)


def _resid_kernel(x_ref, p_ref, g_ref, o_ref):
    o_ref[0] = x_ref[0] + g_ref[0] * p_ref[0]


def _residual(x, peer, g2):
    b, t, _ = x.shape
    tm = min(t, 512)
    tok = pl.BlockSpec((1, tm, D_MODEL), lambda i, j: (i, j, 0))
    return pl.pallas_call(
        _resid_kernel,
        grid=(b, t // tm),
        in_specs=[tok, tok, pl.BlockSpec((1, 1, D_MODEL), lambda i, j: (i, 0, 0))],
        out_specs=tok,
        out_shape=jax.ShapeDtypeStruct((b, t, D_MODEL), F32),
        compiler_params=_cparams(("arbitrary", "arbitrary")),
        name="peer_residual",
    )(x, peer, g2)


def _lane_pad(vec, offset, width=LANES):
    out = jnp.zeros((width,), F32)
    return out.at[offset:offset + vec.shape[0]].set(vec.astype(F32))


def _layer_params(l, norm1_g, norm2_g, w_in, conv_w, conv_b, dt_bias, a_log, d_skip, ssd_norm_g, q_norm_g,
                  k_norm_g, fgate_b, w_s, b_s, w_out, peer_wq, peer_keys, peer_u, peer_v):
    w = w_in[l]
    o = [0]
    for sz in (SSD_INNER, CONV_CH, SSD_HEADS, FOX_INNER, FOX_INNER, FOX_INNER, FOX_HEADS, MLP_INNER, MLP_INNER):
        o.append(o[-1] + sz)
    wz, wxbc, wdt, wq, wk, wv, wf, wu, wvm = [w[:, o[i]:o[i + 1]] for i in range(9)]
    w_small = jnp.zeros((D_MODEL, SMALL_COLS), F32)
    w_small = w_small.at[:, DT_ROW0:DT_ROW0 + SSD_HEADS].set(wdt).at[:, F_ROW0:F_ROW0 + FOX_HEADS].set(wf)
    w_r = jnp.concatenate([wz, wxbc, wq, wk, wv, wu, wvm, w_small], axis=1).astype(BF16)
    ws_t = w_small[:, :SMALL_ROWS].T.astype(BF16)
    a_neg = -jnp.exp(a_log[l].astype(F32))
    grp = jnp.arange(FOX_INNER) // FOX_HEAD_DIM
    gm = jnp.where(grp[:, None] == grp[None, :], 1.0 / FOX_HEAD_DIM, 0.0).astype(BF16)
    heads = jnp.arange(SSD_INNER) // SSD_HEAD_DIM
    e_mat = (jnp.arange(LANES)[:, None] == heads[None, :]).astype(BF16)
    return dict(
        norm1_g=norm1_g[l][None], norm2_g=norm2_g[l][None], w_r=w_r, ws_t=ws_t,
        gq=(jnp.tile(q_norm_g[l], FOX_HEADS) * (FOX_HEAD_DIM ** -0.5))[None],
        gk=jnp.tile(k_norm_g[l], FOX_HEADS)[None], gm=gm,
        conv_w=conv_w[l], conv_b=conv_b[l][None],
        dtb_c=_lane_pad(dt_bias[l], DT_ROW0)[None], dtb_r=dt_bias[l].astype(F32)[:, None],
        a_c=_lane_pad(a_neg, DT_ROW0)[None], a_r=a_neg[:, None],
        dsk_x=jnp.repeat(d_skip[l].astype(F32), SSD_HEAD_DIM)[None], ssd_g=ssd_norm_g[l][None], e_mat=e_mat,
        fb_c=_lane_pad(fgate_b[l], F_ROW0)[None], fb_r=_lane_pad(fgate_b[l], 0, 8)[:, None],
        w_s=w_s[l], bs_t=jnp.zeros((MLP_CHUNK, LANES), F32).at[:, :MLP_GROUPS].set(b_s[l].T),
        w_out=w_out[l].astype(BF16),
        wq_t=peer_wq[l].T.astype(BF16),
        keys=peer_keys[l].reshape(PEER_HEADS * 2, PEER_KEYS, PEER_HALF).astype(BF16),
        u_b=peer_u[l].astype(BF16), vt_b=peer_v[l].T.astype(BF16),
    )


def _stream_mixers(x, peer, g2_prev, mod, p, hist8, s0t, past):
    sh1, sc1, g1, sh2, sc2, _ = mod
    b, t, _ = x.shape
    x, (z, xbc, qb, kn, kb, v, vb, ug, vn, sm, smt) = _in_proj(
        x, peer, g2_prev, sh1, sc1, p['norm1_g'], p['w_r'], p['ws_t'], p['gq'], p['gk'], p['gm'])
    L = SSD_CHUNK_PROMPT if t % SSD_CHUNK_PROMPT == 0 else t
    y_ssd, s_fin, c_fin = _ssd(xbc, z, sm, smt, hist8, s0t, p['conv_w'], p['conv_b'], p['dtb_c'], p['dtb_r'],
                               p['a_c'], p['a_r'], p['dsk_x'], p['ssd_g'], p['e_mat'], L)
    zeros8 = jnp.zeros((b, 8, LANES), F32)
    if past is None:
        logf_c, f_col, f_row, _, _ = _fcum(sm, smt, F_ROW0 // 8, p['fb_c'], p['fb_r'], zeros8, zeros8, True)
        y_fox = _fox_prompt(qb, f_col, kb, vb, f_row)
    else:
        pk, pv, plf_col, plf_row = past
        _, _, fp_row, end_c, end_r = _fcum(plf_col, plf_row, 0, p['fb_c'], p['fb_r'], zeros8, zeros8, False)
        logf_c, f_col, f_row, _, _ = _fcum(sm, smt, F_ROW0 // 8, p['fb_c'], p['fb_r'], end_c, end_r, True)
        y_fox = _fox_sample(qb, f_col, pk, pv, fp_row, kb, vb, f_row)
    lm = MLP_CHUNK if t % MLP_CHUNK == 0 else t
    y_mlp = _chunk_mlp(ug, vn, p['w_s'][:, :lm, :lm], p['bs_t'][:lm])
    x1, h2 = _out_proj(x, y_ssd, y_fox, y_mlp, g1, sh2, sc2, p['norm2_g'], p['w_out'])
    logf = logf_c[:, :, F_ROW0:F_ROW0 + FOX_HEADS]
    new_ssm = s_fin.reshape(b, SSD_STATE, SSD_HEADS, SSD_HEAD_DIM).transpose(0, 2, 3, 1)
    new_conv = c_fin[:, 8 - (CONV_W - 1):, :]
    kc = kn.reshape(b, t, FOX_HEADS, FOX_HEAD_DIM)
    vc = v.reshape(b, t, FOX_HEADS, FOX_HEAD_DIM)
    return x1, h2, (kc, vc, logf, new_ssm, new_conv, vn)


def kernel(x_prompt, x_sample, c_prompt, c_sample, cache_fox_k, cache_fox_v, cache_fox_logf, state_ssm, state_conv, norm1_g, norm2_g, w_ada, b_ada, w_in, conv_w, conv_b, dt_bias, a_log, d_skip, ssd_norm_g, q_norm_g, k_norm_g, fgate_b, w_s, b_s, w_out, peer_wq, peer_keys, peer_u, peer_v):
    depth = w_ada.shape[0]
    bp, tp, _ = x_prompt.shape
    bs, ts, _ = x_sample.shape
    past_len = cache_fox_k.shape[2]
    mod_all = _modulation(jnp.concatenate([c_prompt, c_sample], axis=0).astype(F32), w_ada, b_ada)

    xp, xs = x_prompt, x_sample
    peer_p = peer_s = g2p = g2s = None
    outs = [[] for _ in range(11)]
    for l in range(depth):
        p = _layer_params(l, norm1_g, norm2_g, w_in, conv_w, conv_b, dt_bias, a_log, d_skip, ssd_norm_g, q_norm_g,
                          k_norm_g, fgate_b, w_s, b_s, w_out, peer_wq, peer_keys, peer_u, peer_v)
        mods = [m[:, None, :] for m in jnp.split(mod_all[l], 6, axis=-1)]
        mod_p = [m[:bp] for m in mods]
        mod_s = [m[bp:] for m in mods]
        hist_p = jnp.zeros((bp, 8, CONV_CH), F32)
        s0_p = jnp.zeros((bp, SSD_STATE, SSD_INNER), F32)
        hist_s = jnp.concatenate([jnp.zeros((bs, 8 - (CONV_W - 1), CONV_CH), F32), state_conv[l].astype(F32)], axis=1)
        s0_s = state_ssm[l].astype(F32).transpose(0, 3, 1, 2).reshape(bs, SSD_STATE, SSD_INNER)
        plf = cache_fox_logf[l].astype(F32)
        plf_col = jnp.pad(plf, ((0, 0), (0, 0), (F_ROW0, LANES - F_ROW0 - FOX_HEADS)))
        plf_row = jnp.pad(plf.transpose(0, 2, 1), ((0, 0), (0, 8 - FOX_HEADS), (0, 0)))
        past = (cache_fox_k[l].reshape(bs, past_len, FOX_INNER), cache_fox_v[l].reshape(bs, past_len, FOX_INNER),
                plf_col, plf_row)

        x1p, h2p, st_p = _stream_mixers(xp, peer_p, g2p, mod_p, p, hist_p, s0_p, None)
        x1s, h2s, st_s = _stream_mixers(xs, peer_s, g2s, mod_s, p, hist_s, s0_s, past)

        n_tok = bp * tp + bs * ts
        n_pad = -n_tok % PEER_TOKEN_TILE
        h2 = jnp.concatenate([h2p.reshape(bp * tp, D_MODEL), h2s.reshape(bs * ts, D_MODEL),
                              jnp.zeros((n_pad, D_MODEL), BF16)], axis=0)
        ea, la, eb, pb = _peer_select(h2, p['wq_t'], p['keys'])
        peer = _peer_dense(h2, p['u_b'], p['vt_b'], ea, la, eb, pb)
        peer_p = peer[:bp * tp].reshape(bp, tp, D_MODEL)
        peer_s = peer[bp * tp:n_tok].reshape(bs, ts, D_MODEL)
        xp, xs, g2p, g2s = x1p, x1s, mod_p[5], mod_s[5]
        for i in range(5):
            outs[i].append(st_p[i])
        for i in range(6):
            outs[5 + i].append(st_s[i])

    yp = _residual(xp, peer_p, g2p)
    ys = _residual(xs, peer_s, g2s)
    return (yp, ys) + tuple(jnp.stack(o) for o in outs)
```

```python
import functools

import jax
import jax.numpy as jnp
from jax import lax
from jax.experimental import pallas as pl
from jax.experimental.pallas import tpu as pltpu

F32 = jnp.float32
BF16 = jnp.bfloat16
EPS = 1e-6

D_MODEL = 1024
SSD_HEADS = 8
SSD_HEAD_DIM = 64
SSD_INNER = SSD_HEADS * SSD_HEAD_DIM
SSD_GROUPS = 2
SSD_STATE = 64
CONV_W = 4
CONV_CH = SSD_INNER + 2 * SSD_GROUPS * SSD_STATE
FOX_HEADS = 4
FOX_HEAD_DIM = 64
FOX_INNER = FOX_HEADS * FOX_HEAD_DIM
MLP_GROUPS = 4
MLP_GROUP_DIM = 64
MLP_INNER = MLP_GROUPS * MLP_GROUP_DIM
MLP_CHUNK = 128
FOX_PROMPT_TILE = 512
SSD_CHUNK_PROMPT = 128
PEER_HEADS = 8
PEER_KEYS = 128
PEER_EXPERTS = PEER_KEYS * PEER_KEYS
PEER_HALF = 128
PEER_TOPK = 16
PEER_TOKEN_TILE = 512
PEER_EXPERT_CHUNK = 1024
LANES = 128
MXU_DIM = 256
PEER_SUB = 128
SMALL_COLS = LANES
SMALL_ROWS = 16
DT_ROW0, F_ROW0 = 0, 8
NEG = -1e30
EXP_UNDERFLOW = 104.0
VMEM_LIMIT = 56 * 1024 * 1024

_Z0, _XBC0, _Q0, _K0, _V0, _U0, _VM0, _SM0 = 0, 512, 1280, 1536, 1792, 2048, 2304, 2560
PROJ_COLS = _SM0 + SMALL_COLS


def _cparams(sem, flags=None):
    return pltpu.CompilerParams(dimension_semantics=sem, vmem_limit_bytes=VMEM_LIMIT, flags=flags)


def _dot(a, b):
    return jnp.dot(a, b, preferred_element_type=F32)


def _dot_nt(a, b):
    return lax.dot_general(a, b, (((1,), (1,)), ((), ())), preferred_element_type=F32)


def _dot_tn(a, b):
    return lax.dot_general(a, b, (((0,), (0,)), ((), ())), preferred_element_type=F32)


def _split2(x):
    hi = x.astype(BF16)
    lo = (x - hi.astype(F32)).astype(BF16)
    return hi, lo


def _split3(x):
    hi = x.astype(BF16)
    r = x - hi.astype(F32)
    mid = r.astype(BF16)
    lo = (r - mid.astype(F32)).astype(BF16)
    return hi, mid, lo


def _dot3_l(x, w):
    hi, mid, lo = _split3(x)
    return _dot(hi, w) + _dot(mid, w) + _dot(lo, w)


def _dot3_r(w, x):
    hi, mid, lo = _split3(x)
    return _dot(w, hi) + _dot(w, mid) + _dot(w, lo)


def _sigmoid(x):
    return 1.0 / (1.0 + jnp.exp(-x))


def _silu(x):
    return x * _sigmoid(x)


def _softplus(x):
    return jnp.maximum(x, 0.0) + jnp.log1p(jnp.exp(-jnp.abs(x)))


_GELU_A = 2.0 * 0.7978845608028654
_GELU_B = _GELU_A * 0.044715


def _gelu(x):
    z2 = x * (_GELU_A + _GELU_B * (x * x))
    return x / (1.0 + jnp.exp(-z2))


def _iota(shape, dim):
    return lax.broadcasted_iota(jnp.int32, shape, dim)


def _mod_kernel(c_ref, w_ref, b_ref, o_ref):
    c = c_ref[...]
    o_ref[0] = jnp.dot(_silu(c), w_ref[0], preferred_element_type=F32,
                       precision=lax.Precision.HIGHEST) + b_ref[0]


def _modulation(c_all, w_ada, b_ada):
    depth, _, n6 = w_ada.shape
    bc = c_all.shape[0]
    tn = 1536
    return pl.pallas_call(
        _mod_kernel,
        grid=(depth, n6 // tn),
        in_specs=[pl.BlockSpec((bc, D_MODEL), lambda l, j: (0, 0)),
                  pl.BlockSpec((1, D_MODEL, tn), lambda l, j: (l, 0, j)),
                  pl.BlockSpec((1, 1, tn), lambda l, j: (l, 0, j))],
        out_specs=pl.BlockSpec((1, bc, tn), lambda l, j: (l, 0, j)),
        out_shape=jax.ShapeDtypeStruct((depth, bc, n6), F32),
        compiler_params=_cparams(("arbitrary", "arbitrary")),
        name="adaln_mod",
    )(c_all, w_ada, b_ada.reshape(depth, 1, n6))


def _in_kernel(with_peer, *refs):
    if with_peer:
        x_ref, p_ref, g2_ref = refs[:3]
        refs = refs[3:]
    else:
        x_ref = refs[0]
        refs = refs[1:]
    (sh_ref, sc_ref, g_ref, w_ref, wst_ref, gq_ref, gk_ref, gm_ref) = refs[:8]
    outs = refs[8:]
    if with_peer:
        xo_ref = outs[0]
        outs = outs[1:]
    (z_ref, xbc_ref, q_ref, k_ref, kb_ref, v_ref, vb_ref, ug_ref, vn_ref, sm_ref, smt_ref) = outs

    x = x_ref[0]
    if with_peer:
        x = x + g2_ref[0] * p_ref[0]
        xo_ref[0] = x
    ms = jnp.mean(x * x, axis=-1, keepdims=True)
    h = x * lax.rsqrt(ms + EPS) * g_ref[...]
    h = h * (1.0 + sc_ref[0]) + sh_ref[0]
    hb = h.astype(BF16)
    proj = _dot(hb, w_ref[...])
    z_ref[0] = proj[:, _Z0:_XBC0]
    xbc_ref[0] = proj[:, _XBC0:_Q0]
    q = proj[:, _Q0:_K0]
    k = proj[:, _K0:_V0]
    v = proj[:, _V0:_U0]
    u = proj[:, _U0:_VM0]
    vm = proj[:, _VM0:_SM0]
    sm_ref[0] = proj[:, _SM0:PROJ_COLS]
    smt_ref[0] = _dot_nt(wst_ref[...], hb)
    gm = gm_ref[...]

    def gmean(y):
        hi, lo = _split2(y)
        return _dot(hi, gm) + _dot(lo, gm)

    qn = q * lax.rsqrt(gmean(q * q) + EPS) * gq_ref[...]
    kn = k * lax.rsqrt(gmean(k * k) + EPS) * gk_ref[...]
    q_ref[0] = qn.astype(BF16)
    k_ref[0] = kn
    kb_ref[0] = kn.astype(BF16)
    v_ref[0] = v
    vb_ref[0] = v.astype(BF16)
    ug_ref[0] = _gelu(u)
    gv = _gelu(vm)
    mu = gmean(gv)
    cen = gv - mu
    var = gmean(cen * cen)
    vn_ref[0] = cen * lax.rsqrt(var + EPS)


def _in_proj(x, peer, g2, sh, sc, g, w_r, ws_t, gq, gk, gm):
    b, t, _ = x.shape
    tm = min(t, 512)
    with_peer = peer is not None
    tok = lambda c: pl.BlockSpec((1, tm, c), lambda i, j: (i, j, 0))
    per_b = pl.BlockSpec((1, 1, D_MODEL), lambda i, j: (i, 0, 0))
    full = lambda a: pl.BlockSpec(a.shape, lambda i, j: (0,) * a.ndim)
    in_specs = [tok(D_MODEL)]
    args = [x]
    if with_peer:
        in_specs += [tok(D_MODEL), per_b]
        args += [peer, g2]
    in_specs += [per_b, per_b, full(g), full(w_r), full(ws_t), full(gq), full(gk), full(gm)]
    args += [sh, sc, g, w_r, ws_t, gq, gk, gm]
    out_cols = [(SSD_INNER, F32), (CONV_CH, F32), (FOX_INNER, BF16), (FOX_INNER, F32), (FOX_INNER, BF16),
                (FOX_INNER, F32), (FOX_INNER, BF16), (MLP_INNER, F32), (MLP_INNER, F32), (SMALL_COLS, F32)]
    out_specs = [tok(c) for c, _ in out_cols]
    out_shape = [jax.ShapeDtypeStruct((b, t, c), dt) for c, dt in out_cols]
    out_specs.append(pl.BlockSpec((1, SMALL_ROWS, tm), lambda i, j: (i, 0, j)))
    out_shape.append(jax.ShapeDtypeStruct((b, SMALL_ROWS, t), F32))
    if with_peer:
        out_specs = [tok(D_MODEL)] + out_specs
        out_shape = [jax.ShapeDtypeStruct((b, t, D_MODEL), F32)] + out_shape
    res = pl.pallas_call(
        functools.partial(_in_kernel, with_peer),
        grid=(b, t // tm),
        in_specs=in_specs, out_specs=out_specs, out_shape=out_shape,
        compiler_params=_cparams(("arbitrary", "arbitrary")),
        name="in_proj",
    )(*args)
    if with_peer:
        return res[0], res[1:]
    return x, res


def _ssd_kernel(L, xbc_ref, z_ref, sm_ref, smt_ref, hist_ref, s0_ref, cw_ref, cb_ref, dtbc_ref, dtbr_ref,
                ac_ref, ar_ref, dsk_ref, ng_ref, e_ref, y_ref, sfin_ref, cfin_ref, xpad, st, ybuf):
    c = pl.program_id(1)
    nc = pl.num_programs(1)

    @pl.when(c == 0)
    def _():
        xpad[0:8, :] = hist_ref[0]
        st[...] = s0_ref[0]

    xpad[8:8 + L, :] = xbc_ref[0]
    conv = cb_ref[...]
    for tap in range(CONV_W):
        conv = conv + xpad[5 + tap:5 + tap + L, :] * cw_ref[tap:tap + 1, :]
    tail = xpad[L:L + 8, :]
    xpad[0:8, :] = tail
    xc = _silu(conv)
    xs = xc[:, 0:SSD_INNER]

    r_i = _iota((L, L), 0)
    c_i = _iota((L, L), 1)
    causal = r_i >= c_i
    tri = jnp.where(causal, 1.0, 0.0).astype(BF16)
    triu = jnp.where(r_i <= c_i, 1.0, 0.0).astype(BF16)

    dtc = _softplus(sm_ref[0] + dtbc_ref[...])
    acum_c = _dot3_r(tri, dtc * ac_ref[...])
    e = e_ref[...]
    acum_x = _dot3_l(acum_c, e)
    dt_x = _dot3_l(dtc, e)
    dtr = _softplus(smt_ref[0][DT_ROW0:DT_ROW0 + 8, :] + dtbr_ref[...])
    acum_r = _dot3_l(dtr * ar_ref[...], triu)

    bmat = [xc[:, SSD_INNER + SSD_STATE * g:SSD_INNER + SSD_STATE * (g + 1)].astype(BF16) for g in range(SSD_GROUPS)]
    c0 = SSD_INNER + SSD_GROUPS * SSD_STATE
    cmat = [xc[:, c0 + SSD_STATE * g:c0 + SSD_STATE * (g + 1)].astype(BF16) for g in range(SSD_GROUPS)]
    cb = [_dot_nt(cmat[g], bmat[g]) for g in range(SSD_GROUPS)]
    xsb = xs.astype(BF16)
    hpg = SSD_HEADS // SSD_GROUPS
    for h in range(SSD_HEADS):
        g = h // hpg
        seg = acum_c[:, h:h + 1] - acum_r[h:h + 1, :]
        dec = jnp.where(causal, jnp.exp(jnp.minimum(seg, 0.0)), 0.0)
        m = cb[g] * dec * dtr[h:h + 1, :]
        ybuf[:, SSD_HEAD_DIM * h:SSD_HEAD_DIM * (h + 1)] = _dot(
            m.astype(BF16), xsb[:, SSD_HEAD_DIM * h:SSD_HEAD_DIM * (h + 1)])

    gw = SSD_INNER // SSD_GROUPS
    stb = st[...].astype(BF16)
    y_off = jnp.concatenate([_dot(cmat[g], stb[:, gw * g:gw * (g + 1)]) for g in range(SSD_GROUPS)], axis=1)
    y = ybuf[...] + y_off * jnp.exp(acum_x) + dsk_ref[...] * xs
    a_end = acum_x[L - 1:L, :]
    xw = (xs * dt_x * jnp.exp(a_end - acum_x)).astype(BF16)
    new_states = jnp.concatenate([_dot_tn(bmat[g], xw[:, gw * g:gw * (g + 1)]) for g in range(SSD_GROUPS)], axis=1)
    st[...] = st[...] * jnp.exp(a_end) + new_states

    yg = y * _silu(z_ref[0])
    ms = jnp.mean(yg * yg, axis=-1, keepdims=True)
    y_ref[0] = yg * lax.rsqrt(ms + EPS) * ng_ref[...]

    @pl.when(c == nc - 1)
    def _():
        sfin_ref[0] = st[...]
        cfin_ref[0] = tail


def _ssd(xbc, z, sm, smt, hist8, s0t, cw, cb, dtb_c, dtb_r, a_c, a_r, dsk_x, ng, e_mat, L):
    b, t, _ = xbc.shape
    tok = lambda c: pl.BlockSpec((1, L, c), lambda i, j: (i, j, 0))
    full = lambda a: pl.BlockSpec(a.shape, lambda i, j: (0,) * a.ndim)
    per_b = lambda a: pl.BlockSpec((1,) + a.shape[1:], lambda i, j: (i,) + (0,) * (a.ndim - 1))
    return pl.pallas_call(
        functools.partial(_ssd_kernel, L),
        grid=(b, t // L),
        in_specs=[tok(CONV_CH), tok(SSD_INNER), tok(SMALL_COLS),
                  pl.BlockSpec((1, SMALL_ROWS, L), lambda i, j: (i, 0, j)),
                  per_b(hist8), per_b(s0t), full(cw), full(cb), full(dtb_c), full(dtb_r), full(a_c), full(a_r),
                  full(dsk_x), full(ng), full(e_mat)],
        out_specs=[tok(SSD_INNER),
                   pl.BlockSpec((1, SSD_STATE, SSD_INNER), lambda i, j: (i, 0, 0)),
                   pl.BlockSpec((1, 8, CONV_CH), lambda i, j: (i, 0, 0))],
        out_shape=[jax.ShapeDtypeStruct((b, t, SSD_INNER), F32),
                   jax.ShapeDtypeStruct((b, SSD_STATE, SSD_INNER), F32),
                   jax.ShapeDtypeStruct((b, 8, CONV_CH), F32)],
        scratch_shapes=[pltpu.VMEM((L + 8, CONV_CH), F32), pltpu.VMEM((SSD_STATE, SSD_INNER), F32),
                        pltpu.VMEM((L, SSD_INNER), F32)],
        compiler_params=_cparams(("arbitrary", "arbitrary")),
        name="ssd_scan",
    )(xbc, z, sm, smt, hist8, s0t, cw, cb, dtb_c, dtb_r, a_c, a_r, dsk_x, ng, e_mat)


def _fcum_kernel(tf, activate, colsrc_ref, rowsrc_ref, fbc_ref, fbr_ref, initc_ref, initr_ref,
                 lfc_ref, fc_ref, fr_ref, endc_ref, endr_ref, carc, carr):
    j = pl.program_id(1)

    @pl.when(j == 0)
    def _():
        carc[...] = initc_ref[0]
        carr[...] = initr_ref[0]

    xc = colsrc_ref[0]
    xr = rowsrc_ref[0]
    if activate:
        xc = -_softplus(-(xc + fbc_ref[...]))
        xr = -_softplus(-(xr + fbr_ref[...]))
    r_i = _iota((tf, tf), 0)
    c_i = _iota((tf, tf), 1)
    tri = jnp.where(r_i >= c_i, 1.0, 0.0).astype(BF16)
    triu = jnp.where(r_i <= c_i, 1.0, 0.0).astype(BF16)
    fcol = carc[0:1, :] + _dot3_r(tri, xc)
    frow = carr[:, 0:1] + _dot3_l(xr, triu)
    lfc_ref[0] = xc
    fc_ref[0] = fcol
    fr_ref[0] = frow
    carc[...] = jnp.broadcast_to(fcol[tf - 1:tf, :], carc.shape)
    carr[...] = jnp.broadcast_to(frow[:, tf - 1:tf], carr.shape)
    endc_ref[0] = carc[...]
    endr_ref[0] = carr[...]


def _fcum(colsrc, rowsrc, row_block, fb_c, fb_r, init_c, init_r, activate):
    b, t, _ = colsrc.shape
    tf = min(t, 512)
    full = lambda a: pl.BlockSpec(a.shape, lambda i, j: (0,) * a.ndim)
    per_b = lambda a: pl.BlockSpec((1,) + a.shape[1:], lambda i, j: (i,) + (0,) * (a.ndim - 1))
    return pl.pallas_call(
        functools.partial(_fcum_kernel, tf, activate),
        grid=(b, t // tf),
        in_specs=[pl.BlockSpec((1, tf, LANES), lambda i, j: (i, j, 0)),
                  pl.BlockSpec((1, 8, tf), lambda i, j: (i, row_block, j)),
                  full(fb_c), full(fb_r), per_b(init_c), per_b(init_r)],
        out_specs=[pl.BlockSpec((1, tf, LANES), lambda i, j: (i, j, 0)),
                   pl.BlockSpec((1, tf, LANES), lambda i, j: (i, j, 0)),
                   pl.BlockSpec((1, 8, tf), lambda i, j: (i, 0, j)),
                   pl.BlockSpec((1, 8, LANES), lambda i, j: (i, 0, 0)),
                   pl.BlockSpec((1, 8, LANES), lambda i, j: (i, 0, 0))],
        out_shape=[jax.ShapeDtypeStruct((b, t, LANES), F32), jax.ShapeDtypeStruct((b, t, LANES), F32),
                   jax.ShapeDtypeStruct((b, 8, t), F32), jax.ShapeDtypeStruct((b, 8, LANES), F32),
                   jax.ShapeDtypeStruct((b, 8, LANES), F32)],
        scratch_shapes=[pltpu.VMEM((8, LANES), F32), pltpu.VMEM((8, LANES), F32)],
        compiler_params=_cparams(("arbitrary", "arbitrary")),
        name="forget_cumsum",
    )(colsrc, rowsrc, fb_c, fb_r, init_c, init_r)


def _softmax_step(s, vb, carry):
    m, l, acc = carry
    tq, tk = s.shape
    m_new = jnp.maximum(m, jnp.max(s, axis=-1, keepdims=True))
    p = jnp.exp(s - m_new)
    alpha = jnp.exp(m - m_new)
    if tk % LANES == 0:
        psum = p[:, 0:LANES]
        for c in range(1, tk // LANES):
            psum = psum + p[:, c * LANES:(c + 1) * LANES]
    else:
        psum = jnp.where(_iota((tq, LANES), 1) == 0, jnp.sum(p, axis=-1, keepdims=True), 0.0)
    l = alpha * l + psum
    acc = alpha * acc + _dot(p.astype(BF16), vb)
    return m_new, l, acc


def _softmax_init(tq):
    return tuple((jnp.full((tq, 1), NEG, F32), jnp.zeros((tq, LANES), F32), jnp.zeros((tq, FOX_HEAD_DIM), F32))
                 for _ in range(FOX_HEADS))


def _head_slice(h):
    return slice(FOX_HEAD_DIM * h, FOX_HEAD_DIM * (h + 1))


def _attend_block(qs, fqs, kb, vb, fk, mask, carries):
    out = []
    for h in range(FOX_HEADS):
        hs = _head_slice(h)
        s = _dot_nt(qs[h], kb[:, hs]) + fqs[h] - fk[h:h + 1, :]
        if mask is not None:
            s = jnp.where(mask, s, NEG)
        out.append(_softmax_step(s, vb[:, hs], carries[h]))
    return tuple(out)


def _attend_finish(carries, o_ref):
    for h in range(FOX_HEADS):
        _, l, acc = carries[h]
        o_ref[0, :, _head_slice(h)] = acc / jnp.sum(l, axis=-1, keepdims=True)


def _fox_prompt_kernel(tq, nk, q_ref, fq_ref, k_ref, v_ref, fk_ref, fend_ref, ffirst_ref, thr_ref, o_ref):
    b = pl.program_id(0)
    qi = pl.program_id(1)
    half = tq // 2
    k_diag = k_ref[0, pl.ds(pl.multiple_of(qi * tq, tq), tq), :]
    v_diag = v_ref[0, pl.ds(pl.multiple_of(qi * tq, tq), tq), :]
    fk_diag = fk_ref[0, qi]
    init = (jnp.full((half, 1), NEG, F32), jnp.zeros((half, LANES), F32), jnp.zeros((half, FOX_HEAD_DIM), F32))
    for h in range(FOX_HEADS):
        hs = _head_slice(h)
        base = (b * FOX_HEADS + h) * nk
        slack = thr_ref[0] + ffirst_ref[base + qi]
        n_live = lax.fori_loop(0, qi, lambda j, c: c + (slack - fend_ref[base + j] >= 0.0).astype(jnp.int32), 0)
        chains = []
        carries = []
        for r2 in range(2):
            rs = slice(r2 * half, (r2 + 1) * half)
            qh = q_ref[0, rs, hs]
            fq = fq_ref[0, rs, F_ROW0 + h:F_ROW0 + h + 1]
            chains.append((qh, fq))
            s = _dot_nt(qh, k_diag[:, hs]) + fq - fk_diag[h:h + 1, :]
            causal = (r2 * half + _iota((half, tq), 0)) >= _iota((half, tq), 1)
            carries.append(_softmax_step(jnp.where(causal, s, NEG), v_diag[:, hs], init))

        def body(t, carries, chains=chains, hs=hs, h=h):
            j = qi - 1 - t
            start = pl.multiple_of(j * tq, tq)
            kb = k_ref[0, pl.ds(start, tq), hs]
            vb = v_ref[0, pl.ds(start, tq), hs]
            fk = fk_ref[0, j][h:h + 1, :]
            return tuple(_softmax_step(_dot_nt(qh, kb) + fq - fk, vb, c) for (qh, fq), c in zip(chains, carries))

        carries = lax.fori_loop(0, n_live, body, tuple(carries))
        for r2 in range(2):
            _, l, acc = carries[r2]
            o_ref[0, r2 * half:(r2 + 1) * half, hs] = acc / jnp.sum(l, axis=-1, keepdims=True)


def _fox_prompt(qb, fcol, kb, vb, frow, score_bound):
    b, t, _ = qb.shape
    tq = min(t, FOX_PROMPT_TILE)
    nk = t // tq
    fk = frow.reshape(b, 8, nk, tq).transpose(0, 2, 1, 3)
    f_heads = frow[:, :FOX_HEADS, :].reshape(b, FOX_HEADS, nk, tq)
    f_end = f_heads[:, :, :, tq - 1].reshape(-1)
    f_first = f_heads[:, :, :, 0].reshape(-1)
    thr = (2.0 * score_bound + EXP_UNDERFLOW).reshape(1).astype(F32)
    smem = pl.BlockSpec(memory_space=pltpu.SMEM)
    return pl.pallas_call(
        functools.partial(_fox_prompt_kernel, tq, nk),
        grid=(b, t // tq),
        in_specs=[pl.BlockSpec((1, tq, FOX_INNER), lambda i, j: (i, j, 0)),
                  pl.BlockSpec((1, tq, LANES), lambda i, j: (i, j, 0)),
                  pl.BlockSpec((1, t, FOX_INNER), lambda i, j: (i, 0, 0)),
                  pl.BlockSpec((1, t, FOX_INNER), lambda i, j: (i, 0, 0)),
                  pl.BlockSpec((1, nk, 8, tq), lambda i, j: (i, 0, 0, 0)),
                  smem, smem, smem],
        out_specs=pl.BlockSpec((1, tq, FOX_INNER), lambda i, j: (i, j, 0)),
        out_shape=jax.ShapeDtypeStruct((b, t, FOX_INNER), F32),
        compiler_params=_cparams(("arbitrary", "arbitrary")),
        name="fox_prompt",
    )(qb, fcol, kb, vb, fk, f_end, f_first, thr)


def _fox_sample_kernel(tq, tk, npast, q_ref, fq_ref, pk_ref, pv_ref, fpk_ref, k_ref, v_ref, fk_ref, o_ref):
    q_all = q_ref[0]
    fq_all = fq_ref[0]
    qs = [q_all[:, _head_slice(h)] for h in range(FOX_HEADS)]
    fqs = [fq_all[:, F_ROW0 + h:F_ROW0 + h + 1] for h in range(FOX_HEADS)]
    causal = _iota((tq, tq), 0) >= _iota((tq, tq), 1)

    def past_block(j, carries):
        start = pl.multiple_of(j * tk, tk)
        return _attend_block(qs, fqs, pk_ref[0, pl.ds(start, tk), :].astype(BF16),
                             pv_ref[0, pl.ds(start, tk), :].astype(BF16), fpk_ref[0, j], None, carries)

    carries = lax.fori_loop(0, npast, past_block, _softmax_init(tq))
    _attend_finish(_attend_block(qs, fqs, k_ref[0], v_ref[0], fk_ref[0], causal, carries), o_ref)


def _fox_sample(qb, fcol, past_k, past_v, fpast_row, kb, vb, frow):
    b, t, _ = qb.shape
    p = past_k.shape[1]
    tk = min(p, 512)
    npast = p // tk
    fpk = fpast_row.reshape(b, 8, npast, tk).transpose(0, 2, 1, 3)
    bspec = lambda a: pl.BlockSpec((1,) + a.shape[1:], lambda i: (i,) + (0,) * (a.ndim - 1))
    args = (qb, fcol, past_k, past_v, fpk, kb, vb, frow)
    return pl.pallas_call(
        functools.partial(_fox_sample_kernel, t, tk, npast),
        grid=(b,),
        in_specs=[bspec(a) for a in args],
        out_specs=pl.BlockSpec((1, t, FOX_INNER), lambda i: (i, 0, 0)),
        out_shape=jax.ShapeDtypeStruct((b, t, FOX_INNER), F32),
        compiler_params=_cparams(("arbitrary",)),
        name="fox_sample",
    )(*args)


def _mlp_kernel(lm, nchunk, ug_ref, vn_ref, ws_ref, bst_ref, y_ref):
    r_i = _iota((lm, lm), 0)
    c_i = _iota((lm, lm), 1)
    tril = r_i >= c_i
    vn = vn_ref[0].astype(BF16)
    ug = ug_ref[0]
    for g in range(MLP_GROUPS):
        lo, hi = MLP_GROUP_DIM * g, MLP_GROUP_DIM * (g + 1)
        w = jnp.where(tril, ws_ref[g], 0.0).astype(BF16)
        bias = bst_ref[:, g:g + 1]
        for c in range(nchunk):
            r0, r1 = c * lm, (c + 1) * lm
            sv = _dot(w, vn[r0:r1, lo:hi]) + bias
            y_ref[0, r0:r1, lo:hi] = ug[r0:r1, lo:hi] * sv


def _chunk_mlp(ug, vn, ws, bst):
    b, t, _ = ug.shape
    lm = ws.shape[1]
    tm = min(t, 4 * lm)
    tok = pl.BlockSpec((1, tm, MLP_INNER), lambda i, j: (i, j, 0))
    return pl.pallas_call(
        functools.partial(_mlp_kernel, lm, tm // lm),
        grid=(b, t // tm),
        in_specs=[tok, tok, pl.BlockSpec(ws.shape, lambda i, j: (0, 0, 0)),
                  pl.BlockSpec(bst.shape, lambda i, j: (0, 0))],
        out_specs=tok,
        out_shape=jax.ShapeDtypeStruct((b, t, MLP_INNER), F32),
        compiler_params=_cparams(("arbitrary", "arbitrary")),
        name="chunk_mlp",
    )(ug, vn, ws, bst)


def _out_kernel(x_ref, ys_ref, yf_ref, ym_ref, g1_ref, sh_ref, sc_ref, g_ref, wo_ref, x1_ref, h2_ref):
    mix = (_dot(ys_ref[0].astype(BF16), wo_ref[0:SSD_INNER, :])
           + _dot(yf_ref[0].astype(BF16), wo_ref[SSD_INNER:SSD_INNER + FOX_INNER, :])
           + _dot(ym_ref[0].astype(BF16), wo_ref[SSD_INNER + FOX_INNER:D_MODEL, :]))
    x1 = x_ref[0] + g1_ref[0] * mix
    x1_ref[0] = x1
    ms = jnp.mean(x1 * x1, axis=-1, keepdims=True)
    h = x1 * lax.rsqrt(ms + EPS) * g_ref[...]
    h2_ref[0] = (h * (1.0 + sc_ref[0]) + sh_ref[0]).astype(BF16)


def _out_proj(x, ys, yf, ym, g1, sh, sc, g, wo):
    b, t, _ = x.shape
    tm = min(t, 512)
    tok = lambda c: pl.BlockSpec((1, tm, c), lambda i, j: (i, j, 0))
    per_b = pl.BlockSpec((1, 1, D_MODEL), lambda i, j: (i, 0, 0))
    full = lambda a: pl.BlockSpec(a.shape, lambda i, j: (0,) * a.ndim)
    return pl.pallas_call(
        _out_kernel,
        grid=(b, t // tm),
        in_specs=[tok(D_MODEL), tok(SSD_INNER), tok(FOX_INNER), tok(MLP_INNER), per_b, per_b, per_b, full(g), full(wo)],
        out_specs=[tok(D_MODEL), tok(D_MODEL)],
        out_shape=[jax.ShapeDtypeStruct((b, t, D_MODEL), F32), jax.ShapeDtypeStruct((b, t, D_MODEL), BF16)],
        compiler_params=_cparams(("arbitrary", "arbitrary")),
        name="out_proj",
    )(x, ys, yf, ym, g1, sh, sc, g, wo)


def _top16(s):
    nk, tn = s.shape
    ridx = _iota((nk, tn), 0).astype(F32)
    r16 = _iota((PEER_TOPK, tn), 0)
    pos = jnp.full((nk, tn), float(PEER_TOPK), F32)
    tv = jnp.zeros((PEER_TOPK, tn), F32)
    for it in range(PEER_TOPK):
        level = [(s[g:g + 8, :], ridx[g:g + 8, :]) for g in range(0, nk, 8)]
        while len(level) > 1:
            nxt = []
            for a in range(0, len(level), 2):
                (va, ia), (vb, ib) = level[a], level[a + 1]
                nxt.append((jnp.maximum(va, vb), jnp.where(va >= vb, ia, ib)))
            level = nxt
        v8, i8 = level[0]
        m = jnp.max(v8, axis=0, keepdims=True)
        first = jnp.min(jnp.where(v8 == m, i8, float(nk)), axis=0, keepdims=True)
        sel = ridx == first
        pos = jnp.where(sel, float(it), pos)
        tv = jnp.where(r16 == it, m, tv)
        s = jnp.where(sel, -jnp.inf, s)
    return pos, tv


_CAND_GROUPS = [(0, 16)] + [(ka, 8) for ka in range(1, 8)]
_CAND_ROWS = 16 + 7 * 8 + 8


def _pair_select(ta, tb):
    tn = ta.shape[1]
    pieces, flats, valids = [], [], []
    for ka, rows in _CAND_GROUPS:
        pieces.append(ta[ka:ka + 1, :] + tb[0:rows, :])
        kb = _iota((rows, 1), 0)
        flats.append((ka * PEER_TOPK + kb).astype(F32))
        valids.append((ka + 1) * (kb + 1) <= PEER_TOPK)
    pieces.append(ta[8:16, :] + tb[0:1, :])
    flats.append(((8 + _iota((8, 1), 0)) * PEER_TOPK).astype(F32))
    valids.append(_iota((8, 1), 0) >= 0)
    cand0 = jnp.concatenate(pieces, axis=0)
    flat = jnp.concatenate(flats, axis=0)
    valid = jnp.concatenate(valids, axis=0)
    cand0 = jnp.where(valid, cand0, -jnp.inf)
    best = ta[0:1, :] + tb[0:1, :]

    cand = cand0
    selm = jnp.zeros((_CAND_ROWS, tn), F32)
    for _ in range(PEER_TOPK):
        m = jnp.max(cand, axis=0, keepdims=True)
        first = jnp.min(jnp.where(cand == m, flat, 4096.0), axis=0, keepdims=True)
        sel = flat == first
        cand = jnp.where(sel, -jnp.inf, cand)
        selm = jnp.where(sel, 1.0, selm)
    z = jnp.sum(jnp.where(selm > 0.0, jnp.exp(cand0 - best), 0.0), axis=0, keepdims=True)
    cnts = [jnp.sum(selm[0:16, :], axis=0, keepdims=True)]
    for i in range(1, 8):
        cnts.append(jnp.sum(selm[8 + 8 * i:16 + 8 * i, :], axis=0, keepdims=True))
    cnts.append(selm[_CAND_ROWS - 8:_CAND_ROWS, :])
    return jnp.concatenate(cnts, axis=0), z


def _peer_sel_kernel(tn, h_ref, wqt_ref, keys_ref, ea_ref, la_ref, eb_ref, pb_ref, qt_ref):
    qt_ref[...] = _dot_nt(wqt_ref[...], h_ref[...]).astype(BF16)

    def head(h, carry):
        row = pl.multiple_of(h * (2 * PEER_HALF), 2 * PEER_HALF)
        sa_all = _dot(keys_ref[2 * h], qt_ref[pl.ds(row, PEER_HALF), :])
        sb_all = _dot(keys_ref[2 * h + 1], qt_ref[pl.ds(row + PEER_HALF, PEER_HALF), :])
        for c in range(tn // LANES):
            cs = slice(c * LANES, (c + 1) * LANES)
            sa, sb = sa_all[:, cs], sb_all[:, cs]
            pos_a, ta = _top16(sa)
            pos_b, tb = _top16(sb)
            cnt, z = _pair_select(ta, tb)
            la = jnp.zeros_like(pos_a)
            for ka in range(PEER_TOPK):
                la = jnp.where(pos_a == float(ka), cnt[ka:ka + 1, :], la)
            ea_ref[h, :, cs] = jnp.where(pos_a < float(PEER_TOPK), jnp.exp(sa - ta[0:1, :]), 0.0) / z
            la_ref[h, :, cs] = la
            eb = jnp.where(pos_b < float(PEER_TOPK), jnp.exp(sb - tb[0:1, :]), 0.0)
            for sub in range(PEER_KEYS // PEER_SUB):
                src = slice(sub * PEER_SUB, (sub + 1) * PEER_SUB)
                dst = slice(sub * PEER_SUB // 2, (sub + 1) * PEER_SUB // 2)
                eb_ref[h, dst, cs] = pltpu.bitcast(eb[src].astype(BF16), jnp.uint32)
                pb_ref[h, dst, cs] = pltpu.bitcast(pos_b[src].astype(BF16), jnp.uint32)
        return carry

    lax.fori_loop(0, PEER_HEADS, head, 0)


def _peer_select(h2, wqt, keys):
    n = h2.shape[0]
    tn = 256
    per_tok = pl.BlockSpec((PEER_HEADS, PEER_KEYS, tn), lambda i: (0, 0, i))
    per_tok_pk = pl.BlockSpec((PEER_HEADS, PEER_KEYS // 2, tn), lambda i: (0, 0, i))
    shp = lambda dt: jax.ShapeDtypeStruct((PEER_HEADS, PEER_KEYS, n), dt)
    shp_pk = jax.ShapeDtypeStruct((PEER_HEADS, PEER_KEYS // 2, n), jnp.uint32)
    return pl.pallas_call(
        functools.partial(_peer_sel_kernel, tn),
        grid=(n // tn,),
        in_specs=[pl.BlockSpec((tn, D_MODEL), lambda i: (i, 0)),
                  pl.BlockSpec(wqt.shape, lambda i: (0, 0)),
                  pl.BlockSpec(keys.shape, lambda i: (0, 0, 0))],
        out_specs=[per_tok, per_tok, per_tok_pk, per_tok_pk],
        out_shape=[shp(F32), shp(F32), shp_pk, shp_pk],
        scratch_shapes=[pltpu.VMEM((PEER_HEADS * 2 * PEER_HALF, tn), BF16)],
        compiler_params=_cparams(("arbitrary",)),
        name="peer_select",
    )(h2, wqt, keys)


def _peer_gate_stage(ec, ia0, rows, chunks, at_r, ga_w, ea_ref, la_ref, eb_ref, pb_ref):
    rows_per = ec // PEER_KEYS
    for r in rows:
        for c in chunks:
            cs = slice(c * LANES, (c + 1) * LANES)
            for sub in range(PEER_KEYS // PEER_SUB):
                pk = slice(sub * PEER_SUB // 2, (sub + 1) * PEER_SUB // 2)
                gate = None
                for h in range(PEER_HEADS):
                    la = la_ref[h, pl.ds(ia0, rows_per), cs][r:r + 1, :]
                    ea = ea_ref[h, pl.ds(ia0, rows_per), cs][r:r + 1, :]
                    la = jnp.broadcast_to(la, (PEER_SUB, LANES)).astype(BF16)
                    ea = jnp.broadcast_to(ea, (PEER_SUB, LANES)).astype(BF16)
                    pb = pltpu.bitcast(pb_ref[h, pk, cs], BF16)
                    eb = pltpu.bitcast(eb_ref[h, pk, cs], BF16)
                    term = jnp.where(pb < la, eb, 0.0) * ea
                    gate = term if gate is None else gate + term
                ex = slice(r * PEER_KEYS + sub * PEER_SUB, r * PEER_KEYS + (sub + 1) * PEER_SUB)
                ga_w[ex, cs] = gate * _gelu(at_r[ex, cs]).astype(BF16)


def _peer_dense_kernel(tn, ec, h_ref, u_ref, vt_ref, ea_ref, la_ref, eb_ref, pb_ref, o_ref, at, ga, acc):
    e = pl.program_id(1)
    ne = pl.num_programs(1)

    @pl.when(e == 0)
    def _():
        acc[...] = jnp.zeros_like(acc)

    at[...] = _dot_nt(u_ref[...], h_ref[...])
    rows_per = ec // PEER_KEYS
    ia0 = pl.multiple_of(e * rows_per, rows_per)
    _peer_gate_stage(ec, ia0, range(rows_per), range(tn // LANES), at, ga, ea_ref, la_ref, eb_ref, pb_ref)
    acc[...] += _dot(vt_ref[...], ga[...])

    @pl.when(e == ne - 1)
    def _():
        o_ref[...] = acc[...].T


def _peer_dense(h2, u_b, vt_b, ea, la, eb, pb):
    n = h2.shape[0]
    tn, ec = PEER_TOKEN_TILE, PEER_EXPERT_CHUNK
    per_tok = pl.BlockSpec((PEER_HEADS, PEER_KEYS, tn), lambda i, e: (0, 0, i))
    per_tok_pk = pl.BlockSpec((PEER_HEADS, PEER_KEYS // 2, tn), lambda i, e: (0, 0, i))
    return pl.pallas_call(
        functools.partial(_peer_dense_kernel, tn, ec),
        grid=(n // tn, PEER_EXPERTS // ec),
        in_specs=[pl.BlockSpec((tn, D_MODEL), lambda i, e: (i, 0)),
                  pl.BlockSpec((ec, D_MODEL), lambda i, e: (e, 0)),
                  pl.BlockSpec((D_MODEL, ec), lambda i, e: (0, e)),
                  per_tok, per_tok, per_tok_pk, per_tok_pk],
        out_specs=pl.BlockSpec((tn, D_MODEL), lambda i, e: (i, 0)),
        out_shape=jax.ShapeDtypeStruct((n, D_MODEL), F32),
        scratch_shapes=[pltpu.VMEM((ec, tn), F32), pltpu.VMEM((ec, tn), BF16), pltpu.VMEM((D_MODEL, tn), F32)],
        compiler_params=_cparams(("arbitrary", "arbitrary")),
        name="peer_dense",
    )(h2, u_b, vt_b, ea, la, eb, pb)


def _resid_kernel(x_ref, p_ref, g_ref, o_ref):
    o_ref[0] = x_ref[0] + g_ref[0] * p_ref[0]


def _residual(x, peer, g2):
    b, t, _ = x.shape
    tm = min(t, 512)
    tok = pl.BlockSpec((1, tm, D_MODEL), lambda i, j: (i, j, 0))
    return pl.pallas_call(
        _resid_kernel,
        grid=(b, t // tm),
        in_specs=[tok, tok, pl.BlockSpec((1, 1, D_MODEL), lambda i, j: (i, 0, 0))],
        out_specs=tok,
        out_shape=jax.ShapeDtypeStruct((b, t, D_MODEL), F32),
        compiler_params=_cparams(("arbitrary", "arbitrary")),
        name="peer_residual",
    )(x, peer, g2)


def _lane_pad(vec, offset, width=LANES):
    out = jnp.zeros((width,), F32)
    return out.at[offset:offset + vec.shape[0]].set(vec.astype(F32))


def _layer_params(l, norm1_g, norm2_g, w_in, conv_w, conv_b, dt_bias, a_log, d_skip, ssd_norm_g, q_norm_g,
                  k_norm_g, fgate_b, w_s, b_s, w_out, peer_wq, peer_keys, peer_u, peer_v):
    w = w_in[l]
    o = [0]
    for sz in (SSD_INNER, CONV_CH, SSD_HEADS, FOX_INNER, FOX_INNER, FOX_INNER, FOX_HEADS, MLP_INNER, MLP_INNER):
        o.append(o[-1] + sz)
    wz, wxbc, wdt, wq, wk, wv, wf, wu, wvm = [w[:, o[i]:o[i + 1]] for i in range(9)]
    w_small = jnp.zeros((D_MODEL, SMALL_COLS), F32)
    w_small = w_small.at[:, DT_ROW0:DT_ROW0 + SSD_HEADS].set(wdt).at[:, F_ROW0:F_ROW0 + FOX_HEADS].set(wf)
    w_r = jnp.concatenate([wz, wxbc, wq, wk, wv, wu, wvm, w_small], axis=1).astype(BF16)
    ws_t = w_small[:, :SMALL_ROWS].T.astype(BF16)
    a_neg = -jnp.exp(a_log[l].astype(F32))
    grp = jnp.arange(FOX_INNER) // FOX_HEAD_DIM
    gm = jnp.where(grp[:, None] == grp[None, :], 1.0 / FOX_HEAD_DIM, 0.0).astype(BF16)
    heads = jnp.arange(SSD_INNER) // SSD_HEAD_DIM
    e_mat = (jnp.arange(LANES)[:, None] == heads[None, :]).astype(BF16)
    return dict(
        norm1_g=norm1_g[l][None], norm2_g=norm2_g[l][None], w_r=w_r, ws_t=ws_t,
        gq=(jnp.tile(q_norm_g[l], FOX_HEADS) * (FOX_HEAD_DIM ** -0.5))[None],
        gk=jnp.tile(k_norm_g[l], FOX_HEADS)[None], gm=gm,
        fox_bound=1.02 * FOX_HEAD_DIM ** 0.5 * jnp.max(jnp.abs(q_norm_g[l])) * jnp.max(jnp.abs(k_norm_g[l])),
        conv_w=conv_w[l], conv_b=conv_b[l][None],
        dtb_c=_lane_pad(dt_bias[l], DT_ROW0)[None], dtb_r=dt_bias[l].astype(F32)[:, None],
        a_c=_lane_pad(a_neg, DT_ROW0)[None], a_r=a_neg[:, None],
        dsk_x=jnp.repeat(d_skip[l].astype(F32), SSD_HEAD_DIM)[None], ssd_g=ssd_norm_g[l][None], e_mat=e_mat,
        fb_c=_lane_pad(fgate_b[l], F_ROW0)[None], fb_r=_lane_pad(fgate_b[l], 0, 8)[:, None],
        w_s=w_s[l], bs_t=jnp.zeros((MLP_CHUNK, LANES), F32).at[:, :MLP_GROUPS].set(b_s[l].T),
        w_out=w_out[l].astype(BF16),
        wq_t=peer_wq[l].T.astype(BF16),
        keys=peer_keys[l].reshape(PEER_HEADS * 2, PEER_KEYS, PEER_HALF).astype(BF16),
        u_b=peer_u[l].astype(BF16), vt_b=peer_v[l].T.astype(BF16),
    )


def _stream_mixers(x, peer, g2_prev, mod, p, hist8, s0t, past):
    sh1, sc1, g1, sh2, sc2, _ = mod
    b, t, _ = x.shape
    x, (z, xbc, qb, kn, kb, v, vb, ug, vn, sm, smt) = _in_proj(
        x, peer, g2_prev, sh1, sc1, p['norm1_g'], p['w_r'], p['ws_t'], p['gq'], p['gk'], p['gm'])
    L = SSD_CHUNK_PROMPT if t % SSD_CHUNK_PROMPT == 0 else t
    y_ssd, s_fin, c_fin = _ssd(xbc, z, sm, smt, hist8, s0t, p['conv_w'], p['conv_b'], p['dtb_c'], p['dtb_r'],
                               p['a_c'], p['a_r'], p['dsk_x'], p['ssd_g'], p['e_mat'], L)
    zeros8 = jnp.zeros((b, 8, LANES), F32)
    if past is None:
        logf_c, f_col, f_row, _, _ = _fcum(sm, smt, F_ROW0 // 8, p['fb_c'], p['fb_r'], zeros8, zeros8, True)
        y_fox = _fox_prompt(qb, f_col, kb, vb, f_row, p['fox_bound'])
    else:
        pk, pv, plf_col, plf_row = past
        _, _, fp_row, end_c, end_r = _fcum(plf_col, plf_row, 0, p['fb_c'], p['fb_r'], zeros8, zeros8, False)
        logf_c, f_col, f_row, _, _ = _fcum(sm, smt, F_ROW0 // 8, p['fb_c'], p['fb_r'], end_c, end_r, True)
        y_fox = _fox_sample(qb, f_col, pk, pv, fp_row, kb, vb, f_row)
    lm = MLP_CHUNK if t % MLP_CHUNK == 0 else t
    y_mlp = _chunk_mlp(ug, vn, p['w_s'][:, :lm, :lm], p['bs_t'][:lm])
    x1, h2 = _out_proj(x, y_ssd, y_fox, y_mlp, g1, sh2, sc2, p['norm2_g'], p['w_out'])
    logf = logf_c[:, :, F_ROW0:F_ROW0 + FOX_HEADS]
    new_ssm = s_fin.reshape(b, SSD_STATE, SSD_HEADS, SSD_HEAD_DIM).transpose(0, 2, 3, 1)
    new_conv = c_fin[:, 8 - (CONV_W - 1):, :]
    kc = kn.reshape(b, t, FOX_HEADS, FOX_HEAD_DIM)
    vc = v.reshape(b, t, FOX_HEADS, FOX_HEAD_DIM)
    return x1, h2, (kc, vc, logf, new_ssm, new_conv, vn)


def _peer(h2, p):
    b, t, _ = h2.shape
    n_tok = b * t
    n_pad = -n_tok % PEER_TOKEN_TILE
    flat = h2.reshape(n_tok, D_MODEL)
    if n_pad:
        flat = jnp.concatenate([flat, jnp.zeros((n_pad, D_MODEL), BF16)], axis=0)
    ea, la, eb, pb = _peer_select(flat, p['wq_t'], p['keys'])
    out = _peer_dense(flat, p['u_b'], p['vt_b'], ea, la, eb, pb)
    return out[:n_tok].reshape(b, t, D_MODEL)


def kernel(x_prompt, x_sample, c_prompt, c_sample, cache_fox_k, cache_fox_v, cache_fox_logf, state_ssm, state_conv, norm1_g, norm2_g, w_ada, b_ada, w_in, conv_w, conv_b, dt_bias, a_log, d_skip, ssd_norm_g, q_norm_g, k_norm_g, fgate_b, w_s, b_s, w_out, peer_wq, peer_keys, peer_u, peer_v):
    depth = w_ada.shape[0]
    bp, tp, _ = x_prompt.shape
    bs, ts, _ = x_sample.shape
    past_len = cache_fox_k.shape[2]
    mod_all = _modulation(jnp.concatenate([c_prompt, c_sample], axis=0).astype(F32), w_ada, b_ada)

    xp, xs = x_prompt, x_sample
    peer_p = peer_s = g2p = g2s = None
    outs = [[] for _ in range(11)]
    for l in range(depth):
        p = _layer_params(l, norm1_g, norm2_g, w_in, conv_w, conv_b, dt_bias, a_log, d_skip, ssd_norm_g, q_norm_g,
                          k_norm_g, fgate_b, w_s, b_s, w_out, peer_wq, peer_keys, peer_u, peer_v)
        mods = [m[:, None, :] for m in jnp.split(mod_all[l], 6, axis=-1)]
        mod_p = [m[:bp] for m in mods]
        mod_s = [m[bp:] for m in mods]
        hist_p = jnp.zeros((bp, 8, CONV_CH), F32)
        s0_p = jnp.zeros((bp, SSD_STATE, SSD_INNER), F32)
        hist_s = jnp.concatenate([jnp.zeros((bs, 8 - (CONV_W - 1), CONV_CH), F32), state_conv[l].astype(F32)], axis=1)
        s0_s = state_ssm[l].astype(F32).transpose(0, 3, 1, 2).reshape(bs, SSD_STATE, SSD_INNER)
        plf = cache_fox_logf[l].astype(F32)
        plf_col = jnp.pad(plf, ((0, 0), (0, 0), (F_ROW0, LANES - F_ROW0 - FOX_HEADS)))
        plf_row = jnp.pad(plf.transpose(0, 2, 1), ((0, 0), (0, 8 - FOX_HEADS), (0, 0)))
        past = (cache_fox_k[l].reshape(bs, past_len, FOX_INNER), cache_fox_v[l].reshape(bs, past_len, FOX_INNER),
                plf_col, plf_row)

        x1p, h2p, st_p = _stream_mixers(xp, peer_p, g2p, mod_p, p, hist_p, s0_p, None)
        x1s, h2s, st_s = _stream_mixers(xs, peer_s, g2s, mod_s, p, hist_s, s0_s, past)

        peer_p = _peer(h2p, p)
        peer_s = _peer(h2s, p)
        xp, xs, g2p, g2s = x1p, x1s, mod_p[5], mod_s[5]
        for i in range(5):
            outs[i].append(st_p[i])
        for i in range(6):
            outs[5 + i].append(st_s[i])

    yp = _residual(xp, peer_p, g2p)
    ys = _residual(xs, peer_s, g2s)
    return (yp, ys) + tuple(jnp.stack(o) for o in outs)
```

```python
import functools

import jax
import jax.numpy as jnp
from jax import lax
from jax.experimental import pallas as pl
from jax.experimental.pallas import tpu as pltpu

F32 = jnp.float32
BF16 = jnp.bfloat16
EPS = 1e-6

D_MODEL = 1024
SSD_HEADS = 8
SSD_HEAD_DIM = 64
SSD_INNER = SSD_HEADS * SSD_HEAD_DIM
SSD_GROUPS = 2
SSD_STATE = 64
CONV_W = 4
CONV_CH = SSD_INNER + 2 * SSD_GROUPS * SSD_STATE
FOX_HEADS = 4
FOX_HEAD_DIM = 64
FOX_INNER = FOX_HEADS * FOX_HEAD_DIM
MLP_GROUPS = 4
MLP_GROUP_DIM = 64
MLP_INNER = MLP_GROUPS * MLP_GROUP_DIM
MLP_CHUNK = 128
FOX_PROMPT_TILE = 512
SSD_CHUNK_PROMPT = 128
PEER_HEADS = 8
PEER_KEYS = 128
PEER_EXPERTS = PEER_KEYS * PEER_KEYS
PEER_HALF = 128
PEER_TOPK = 16
PEER_TOKEN_TILE = 512
PEER_EXPERT_CHUNK = 1024
LANES = 128
MXU_DIM = 256
PEER_SUB = 128
SMALL_COLS = LANES
SMALL_ROWS = 16
DT_ROW0, F_ROW0 = 0, 8
NEG = -1e30
EXP_UNDERFLOW = 104.0
VMEM_LIMIT = 56 * 1024 * 1024

_Z0, _XBC0, _Q0, _K0, _V0, _U0, _VM0, _SM0 = 0, 512, 1280, 1536, 1792, 2048, 2304, 2560
PROJ_COLS = _SM0 + SMALL_COLS


def _cparams(sem, flags=None):
    return pltpu.CompilerParams(dimension_semantics=sem, vmem_limit_bytes=VMEM_LIMIT, flags=flags)


def _dot(a, b):
    return jnp.dot(a, b, preferred_element_type=F32)


def _dot_nt(a, b):
    return lax.dot_general(a, b, (((1,), (1,)), ((), ())), preferred_element_type=F32)


def _dot_tn(a, b):
    return lax.dot_general(a, b, (((0,), (0,)), ((), ())), preferred_element_type=F32)


def _split2(x):
    hi = x.astype(BF16)
    lo = (x - hi.astype(F32)).astype(BF16)
    return hi, lo


def _split3(x):
    hi = x.astype(BF16)
    r = x - hi.astype(F32)
    mid = r.astype(BF16)
    lo = (r - mid.astype(F32)).astype(BF16)
    return hi, mid, lo


def _dot3_l(x, w):
    hi, mid, lo = _split3(x)
    return _dot(hi, w) + _dot(mid, w) + _dot(lo, w)


def _dot3_r(w, x):
    hi, mid, lo = _split3(x)
    return _dot(w, hi) + _dot(w, mid) + _dot(w, lo)


def _sigmoid(x):
    return 1.0 / (1.0 + jnp.exp(-x))


def _silu(x):
    return x * _sigmoid(x)


def _softplus(x):
    return jnp.maximum(x, 0.0) + jnp.log1p(jnp.exp(-jnp.abs(x)))


_GELU_A = 2.0 * 0.7978845608028654
_GELU_B = _GELU_A * 0.044715


def _gelu(x):
    z2 = x * (_GELU_A + _GELU_B * (x * x))
    return x / (1.0 + jnp.exp(-z2))


def _iota(shape, dim):
    return lax.broadcasted_iota(jnp.int32, shape, dim)


def _pack_rows(x):
    m, n = x.shape[-2:]
    pairs = x.reshape(x.shape[:-2] + (m // 2, 2, n))
    return lax.bitcast_convert_type(jnp.swapaxes(pairs, -1, -2), jnp.uint32)


def _unpack_rows(ref_or_val):
    return pltpu.bitcast(ref_or_val, BF16)


def _mod_kernel(c_ref, w_ref, b_ref, o_ref):
    c = c_ref[...]
    o_ref[0] = jnp.dot(_silu(c), w_ref[0], preferred_element_type=F32,
                       precision=lax.Precision.HIGHEST) + b_ref[0]


def _modulation(c_all, w_ada, b_ada):
    depth, _, n6 = w_ada.shape
    bc = c_all.shape[0]
    tn = 1536
    return pl.pallas_call(
        _mod_kernel,
        grid=(depth, n6 // tn),
        in_specs=[pl.BlockSpec((bc, D_MODEL), lambda l, j: (0, 0)),
                  pl.BlockSpec((1, D_MODEL, tn), lambda l, j: (l, 0, j)),
                  pl.BlockSpec((1, 1, tn), lambda l, j: (l, 0, j))],
        out_specs=pl.BlockSpec((1, bc, tn), lambda l, j: (l, 0, j)),
        out_shape=jax.ShapeDtypeStruct((depth, bc, n6), F32),
        compiler_params=_cparams(("arbitrary", "arbitrary")),
        name="adaln_mod",
    )(c_all, w_ada, b_ada.reshape(depth, 1, n6))


def _in_kernel(with_peer, *refs):
    if with_peer:
        x_ref, p_ref, g2_ref = refs[:3]
        refs = refs[3:]
    else:
        x_ref = refs[0]
        refs = refs[1:]
    (sh_ref, sc_ref, g_ref, w_ref, wst_ref, gq_ref, gk_ref, gm_ref) = refs[:8]
    outs = refs[8:]
    if with_peer:
        xo_ref = outs[0]
        outs = outs[1:]
    (z_ref, xbc_ref, q_ref, k_ref, kb_ref, v_ref, vb_ref, ug_ref, vn_ref, sm_ref, smt_ref) = outs

    x = x_ref[0]
    if with_peer:
        x = x + g2_ref[0] * p_ref[0]
        xo_ref[0] = x
    ms = jnp.mean(x * x, axis=-1, keepdims=True)
    h = x * lax.rsqrt(ms + EPS) * g_ref[...]
    h = h * (1.0 + sc_ref[0]) + sh_ref[0]
    hb = h.astype(BF16)
    proj = _dot(hb, w_ref[...])
    z_ref[0] = proj[:, _Z0:_XBC0]
    xbc_ref[0] = proj[:, _XBC0:_Q0]
    q = proj[:, _Q0:_K0]
    k = proj[:, _K0:_V0]
    v = proj[:, _V0:_U0]
    u = proj[:, _U0:_VM0]
    vm = proj[:, _VM0:_SM0]
    sm_ref[0] = proj[:, _SM0:PROJ_COLS]
    smt_ref[0] = _dot_nt(wst_ref[...], hb)
    gm = gm_ref[...]

    def gmean(y):
        hi, lo = _split2(y)
        return _dot(hi, gm) + _dot(lo, gm)

    qn = q * lax.rsqrt(gmean(q * q) + EPS) * gq_ref[...]
    kn = k * lax.rsqrt(gmean(k * k) + EPS) * gk_ref[...]
    q_ref[0] = qn.astype(BF16)
    k_ref[0] = kn
    kb_ref[0] = kn.astype(BF16)
    v_ref[0] = v
    vb_ref[0] = v.astype(BF16)
    ug_ref[0] = _gelu(u)
    gv = _gelu(vm)
    mu = gmean(gv)
    cen = gv - mu
    var = gmean(cen * cen)
    vn_ref[0] = cen * lax.rsqrt(var + EPS)


def _in_proj(x, peer, g2, sh, sc, g, w_r, ws_t, gq, gk, gm):
    b, t, _ = x.shape
    tm = min(t, 512)
    with_peer = peer is not None
    tok = lambda c: pl.BlockSpec((1, tm, c), lambda i, j: (i, j, 0))
    per_b = pl.BlockSpec((1, 1, D_MODEL), lambda i, j: (i, 0, 0))
    full = lambda a: pl.BlockSpec(a.shape, lambda i, j: (0,) * a.ndim)
    in_specs = [tok(D_MODEL)]
    args = [x]
    if with_peer:
        in_specs += [tok(D_MODEL), per_b]
        args += [peer, g2]
    in_specs += [per_b, per_b, full(g), full(w_r), full(ws_t), full(gq), full(gk), full(gm)]
    args += [sh, sc, g, w_r, ws_t, gq, gk, gm]
    out_cols = [(SSD_INNER, F32), (CONV_CH, F32), (FOX_INNER, BF16), (FOX_INNER, F32), (FOX_INNER, BF16),
                (FOX_INNER, F32), (FOX_INNER, BF16), (MLP_INNER, F32), (MLP_INNER, F32), (SMALL_COLS, F32)]
    out_specs = [tok(c) for c, _ in out_cols]
    out_shape = [jax.ShapeDtypeStruct((b, t, c), dt) for c, dt in out_cols]
    out_specs.append(pl.BlockSpec((1, SMALL_ROWS, tm), lambda i, j: (i, 0, j)))
    out_shape.append(jax.ShapeDtypeStruct((b, SMALL_ROWS, t), F32))
    if with_peer:
        out_specs = [tok(D_MODEL)] + out_specs
        out_shape = [jax.ShapeDtypeStruct((b, t, D_MODEL), F32)] + out_shape
    res = pl.pallas_call(
        functools.partial(_in_kernel, with_peer),
        grid=(b, t // tm),
        in_specs=in_specs, out_specs=out_specs, out_shape=out_shape,
        compiler_params=_cparams(("arbitrary", "arbitrary")),
        name="in_proj",
    )(*args)
    if with_peer:
        return res[0], res[1:]
    return x, res


def _ssd_kernel(L, xbc_ref, z_ref, sm_ref, smt_ref, hist_ref, s0_ref, cw_ref, cb_ref, dtbc_ref, dtbr_ref,
                ac_ref, ar_ref, dsk_ref, ng_ref, e_ref, y_ref, sfin_ref, cfin_ref, xpad, st, ybuf):
    c = pl.program_id(1)
    nc = pl.num_programs(1)

    @pl.when(c == 0)
    def _():
        xpad[0:8, :] = hist_ref[0]
        st[...] = s0_ref[0]

    xpad[8:8 + L, :] = xbc_ref[0]
    conv = cb_ref[...]
    for tap in range(CONV_W):
        conv = conv + xpad[5 + tap:5 + tap + L, :] * cw_ref[tap:tap + 1, :]
    tail = xpad[L:L + 8, :]
    xpad[0:8, :] = tail
    xc = _silu(conv)
    xs = xc[:, 0:SSD_INNER]

    r_i = _iota((L, L), 0)
    c_i = _iota((L, L), 1)
    causal = r_i >= c_i
    tri = jnp.where(causal, 1.0, 0.0).astype(BF16)
    triu = jnp.where(r_i <= c_i, 1.0, 0.0).astype(BF16)

    dtc = _softplus(sm_ref[0] + dtbc_ref[...])
    acum_c = _dot3_r(tri, dtc * ac_ref[...])
    e = e_ref[...]
    acum_x = _dot3_l(acum_c, e)
    dt_x = _dot3_l(dtc, e)
    dtr = _softplus(smt_ref[0][DT_ROW0:DT_ROW0 + 8, :] + dtbr_ref[...])
    acum_r = _dot3_l(dtr * ar_ref[...], triu)

    bmat = [xc[:, SSD_INNER + SSD_STATE * g:SSD_INNER + SSD_STATE * (g + 1)].astype(BF16) for g in range(SSD_GROUPS)]
    c0 = SSD_INNER + SSD_GROUPS * SSD_STATE
    cmat = [xc[:, c0 + SSD_STATE * g:c0 + SSD_STATE * (g + 1)].astype(BF16) for g in range(SSD_GROUPS)]
    cb = [_dot_nt(cmat[g], bmat[g]) for g in range(SSD_GROUPS)]
    xsb = xs.astype(BF16)
    hpg = SSD_HEADS // SSD_GROUPS
    for h in range(SSD_HEADS):
        g = h // hpg
        seg = acum_c[:, h:h + 1] - acum_r[h:h + 1, :]
        dec = jnp.where(causal, jnp.exp(jnp.minimum(seg, 0.0)), 0.0)
        m = cb[g] * dec * dtr[h:h + 1, :]
        ybuf[:, SSD_HEAD_DIM * h:SSD_HEAD_DIM * (h + 1)] = _dot(
            m.astype(BF16), xsb[:, SSD_HEAD_DIM * h:SSD_HEAD_DIM * (h + 1)])

    gw = SSD_INNER // SSD_GROUPS
    stb = st[...].astype(BF16)
    y_off = jnp.concatenate([_dot(cmat[g], stb[:, gw * g:gw * (g + 1)]) for g in range(SSD_GROUPS)], axis=1)
    y = ybuf[...] + y_off * jnp.exp(acum_x) + dsk_ref[...] * xs
    a_end = acum_x[L - 1:L, :]
    xw = (xs * dt_x * jnp.exp(a_end - acum_x)).astype(BF16)
    new_states = jnp.concatenate([_dot_tn(bmat[g], xw[:, gw * g:gw * (g + 1)]) for g in range(SSD_GROUPS)], axis=1)
    st[...] = st[...] * jnp.exp(a_end) + new_states

    yg = y * _silu(z_ref[0])
    ms = jnp.mean(yg * yg, axis=-1, keepdims=True)
    y_ref[0] = yg * lax.rsqrt(ms + EPS) * ng_ref[...]

    @pl.when(c == nc - 1)
    def _():
        sfin_ref[0] = st[...]
        cfin_ref[0] = tail


def _ssd(xbc, z, sm, smt, hist8, s0t, cw, cb, dtb_c, dtb_r, a_c, a_r, dsk_x, ng, e_mat, L):
    b, t, _ = xbc.shape
    tok = lambda c: pl.BlockSpec((1, L, c), lambda i, j: (i, j, 0))
    full = lambda a: pl.BlockSpec(a.shape, lambda i, j: (0,) * a.ndim)
    per_b = lambda a: pl.BlockSpec((1,) + a.shape[1:], lambda i, j: (i,) + (0,) * (a.ndim - 1))
    return pl.pallas_call(
        functools.partial(_ssd_kernel, L),
        grid=(b, t // L),
        in_specs=[tok(CONV_CH), tok(SSD_INNER), tok(SMALL_COLS),
                  pl.BlockSpec((1, SMALL_ROWS, L), lambda i, j: (i, 0, j)),
                  per_b(hist8), per_b(s0t), full(cw), full(cb), full(dtb_c), full(dtb_r), full(a_c), full(a_r),
                  full(dsk_x), full(ng), full(e_mat)],
        out_specs=[tok(SSD_INNER),
                   pl.BlockSpec((1, SSD_STATE, SSD_INNER), lambda i, j: (i, 0, 0)),
                   pl.BlockSpec((1, 8, CONV_CH), lambda i, j: (i, 0, 0))],
        out_shape=[jax.ShapeDtypeStruct((b, t, SSD_INNER), F32),
                   jax.ShapeDtypeStruct((b, SSD_STATE, SSD_INNER), F32),
                   jax.ShapeDtypeStruct((b, 8, CONV_CH), F32)],
        scratch_shapes=[pltpu.VMEM((L + 8, CONV_CH), F32), pltpu.VMEM((SSD_STATE, SSD_INNER), F32),
                        pltpu.VMEM((L, SSD_INNER), F32)],
        compiler_params=_cparams(("arbitrary", "arbitrary")),
        name="ssd_scan",
    )(xbc, z, sm, smt, hist8, s0t, cw, cb, dtb_c, dtb_r, a_c, a_r, dsk_x, ng, e_mat)


def _fcum_kernel(tf, activate, colsrc_ref, rowsrc_ref, fbc_ref, fbr_ref, initc_ref, initr_ref,
                 lfc_ref, fc_ref, fr_ref, endc_ref, endr_ref, carc, carr):
    j = pl.program_id(1)

    @pl.when(j == 0)
    def _():
        carc[...] = initc_ref[0]
        carr[...] = initr_ref[0]

    xc = colsrc_ref[0]
    xr = rowsrc_ref[0]
    if activate:
        xc = -_softplus(-(xc + fbc_ref[...]))
        xr = -_softplus(-(xr + fbr_ref[...]))
    r_i = _iota((tf, tf), 0)
    c_i = _iota((tf, tf), 1)
    tri = jnp.where(r_i >= c_i, 1.0, 0.0).astype(BF16)
    triu = jnp.where(r_i <= c_i, 1.0, 0.0).astype(BF16)
    fcol = carc[0:1, :] + _dot3_r(tri, xc)
    frow = carr[:, 0:1] + _dot3_l(xr, triu)
    lfc_ref[0] = xc
    fc_ref[0] = fcol
    fr_ref[0] = frow
    carc[...] = jnp.broadcast_to(fcol[tf - 1:tf, :], carc.shape)
    carr[...] = jnp.broadcast_to(frow[:, tf - 1:tf], carr.shape)
    endc_ref[0] = carc[...]
    endr_ref[0] = carr[...]


def _fcum(colsrc, rowsrc, row_block, fb_c, fb_r, init_c, init_r, activate):
    b, t, _ = colsrc.shape
    tf = min(t, 512)
    full = lambda a: pl.BlockSpec(a.shape, lambda i, j: (0,) * a.ndim)
    per_b = lambda a: pl.BlockSpec((1,) + a.shape[1:], lambda i, j: (i,) + (0,) * (a.ndim - 1))
    return pl.pallas_call(
        functools.partial(_fcum_kernel, tf, activate),
        grid=(b, t // tf),
        in_specs=[pl.BlockSpec((1, tf, LANES), lambda i, j: (i, j, 0)),
                  pl.BlockSpec((1, 8, tf), lambda i, j: (i, row_block, j)),
                  full(fb_c), full(fb_r), per_b(init_c), per_b(init_r)],
        out_specs=[pl.BlockSpec((1, tf, LANES), lambda i, j: (i, j, 0)),
                   pl.BlockSpec((1, tf, LANES), lambda i, j: (i, j, 0)),
                   pl.BlockSpec((1, 8, tf), lambda i, j: (i, 0, j)),
                   pl.BlockSpec((1, 8, LANES), lambda i, j: (i, 0, 0)),
                   pl.BlockSpec((1, 8, LANES), lambda i, j: (i, 0, 0))],
        out_shape=[jax.ShapeDtypeStruct((b, t, LANES), F32), jax.ShapeDtypeStruct((b, t, LANES), F32),
                   jax.ShapeDtypeStruct((b, 8, t), F32), jax.ShapeDtypeStruct((b, 8, LANES), F32),
                   jax.ShapeDtypeStruct((b, 8, LANES), F32)],
        scratch_shapes=[pltpu.VMEM((8, LANES), F32), pltpu.VMEM((8, LANES), F32)],
        compiler_params=_cparams(("arbitrary", "arbitrary")),
        name="forget_cumsum",
    )(colsrc, rowsrc, fb_c, fb_r, init_c, init_r)


def _softmax_step(s, vb, carry):
    m, l, acc = carry
    tq, tk = s.shape
    m_new = jnp.maximum(m, jnp.max(s, axis=-1, keepdims=True))
    p = jnp.exp(s - m_new)
    alpha = jnp.exp(m - m_new)
    if tk % LANES == 0:
        psum = p[:, 0:LANES]
        for c in range(1, tk // LANES):
            psum = psum + p[:, c * LANES:(c + 1) * LANES]
    else:
        psum = jnp.where(_iota((tq, LANES), 1) == 0, jnp.sum(p, axis=-1, keepdims=True), 0.0)
    l = alpha * l + psum
    acc = alpha * acc + _dot(p.astype(BF16), vb)
    return m_new, l, acc


def _softmax_init(tq):
    return tuple((jnp.full((tq, 1), NEG, F32), jnp.zeros((tq, LANES), F32), jnp.zeros((tq, FOX_HEAD_DIM), F32))
                 for _ in range(FOX_HEADS))


def _head_slice(h):
    return slice(FOX_HEAD_DIM * h, FOX_HEAD_DIM * (h + 1))


def _attend_block(qs, fqs, kb, vb, fk, mask, carries):
    out = []
    for h in range(FOX_HEADS):
        hs = _head_slice(h)
        s = _dot_nt(qs[h], kb[:, hs]) + fqs[h] - fk[h:h + 1, :]
        if mask is not None:
            s = jnp.where(mask, s, NEG)
        out.append(_softmax_step(s, vb[:, hs], carries[h]))
    return tuple(out)


def _attend_finish(carries, o_ref):
    for h in range(FOX_HEADS):
        _, l, acc = carries[h]
        o_ref[0, :, _head_slice(h)] = acc / jnp.sum(l, axis=-1, keepdims=True)


def _fox_prompt_kernel(tq, nk, q_ref, fq_ref, k_ref, v_ref, fk_ref, fend_ref, ffirst_ref, thr_ref, o_ref):
    b = pl.program_id(0)
    qi = pl.program_id(1)
    half = tq // 2
    k_diag = k_ref[0, pl.ds(pl.multiple_of(qi * tq, tq), tq), :]
    v_diag = v_ref[0, pl.ds(pl.multiple_of(qi * tq, tq), tq), :]
    fk_diag = fk_ref[0, qi]
    init = (jnp.full((half, 1), NEG, F32), jnp.zeros((half, LANES), F32), jnp.zeros((half, FOX_HEAD_DIM), F32))
    for h in range(FOX_HEADS):
        hs = _head_slice(h)
        base = (b * FOX_HEADS + h) * nk
        slack = thr_ref[0] + ffirst_ref[base + qi]
        n_live = lax.fori_loop(0, qi, lambda j, c: c + (slack - fend_ref[base + j] >= 0.0).astype(jnp.int32), 0)
        chains = []
        carries = []
        for r2 in range(2):
            rs = slice(r2 * half, (r2 + 1) * half)
            qh = q_ref[0, rs, hs]
            fq = fq_ref[0, rs, F_ROW0 + h:F_ROW0 + h + 1]
            chains.append((qh, fq))
            s = _dot_nt(qh, k_diag[:, hs]) + fq - fk_diag[h:h + 1, :]
            causal = (r2 * half + _iota((half, tq), 0)) >= _iota((half, tq), 1)
            carries.append(_softmax_step(jnp.where(causal, s, NEG), v_diag[:, hs], init))

        def body(t, carries, chains=chains, hs=hs, h=h):
            j = qi - 1 - t
            start = pl.multiple_of(j * tq, tq)
            kb = k_ref[0, pl.ds(start, tq), hs]
            vb = v_ref[0, pl.ds(start, tq), hs]
            fk = fk_ref[0, j][h:h + 1, :]
            return tuple(_softmax_step(_dot_nt(qh, kb) + fq - fk, vb, c) for (qh, fq), c in zip(chains, carries))

        carries = lax.fori_loop(0, n_live, body, tuple(carries))
        for r2 in range(2):
            _, l, acc = carries[r2]
            o_ref[0, r2 * half:(r2 + 1) * half, hs] = acc / jnp.sum(l, axis=-1, keepdims=True)


def _fox_prompt(qb, fcol, kb, vb, frow, score_bound):
    b, t, _ = qb.shape
    tq = min(t, FOX_PROMPT_TILE)
    nk = t // tq
    fk = frow.reshape(b, 8, nk, tq).transpose(0, 2, 1, 3)
    f_heads = frow[:, :FOX_HEADS, :].reshape(b, FOX_HEADS, nk, tq)
    f_end = f_heads[:, :, :, tq - 1].reshape(-1)
    f_first = f_heads[:, :, :, 0].reshape(-1)
    thr = (2.0 * score_bound + EXP_UNDERFLOW).reshape(1).astype(F32)
    smem = pl.BlockSpec(memory_space=pltpu.SMEM)
    return pl.pallas_call(
        functools.partial(_fox_prompt_kernel, tq, nk),
        grid=(b, t // tq),
        in_specs=[pl.BlockSpec((1, tq, FOX_INNER), lambda i, j: (i, j, 0)),
                  pl.BlockSpec((1, tq, LANES), lambda i, j: (i, j, 0)),
                  pl.BlockSpec((1, t, FOX_INNER), lambda i, j: (i, 0, 0)),
                  pl.BlockSpec((1, t, FOX_INNER), lambda i, j: (i, 0, 0)),
                  pl.BlockSpec((1, nk, 8, tq), lambda i, j: (i, 0, 0, 0)),
                  smem, smem, smem],
        out_specs=pl.BlockSpec((1, tq, FOX_INNER), lambda i, j: (i, j, 0)),
        out_shape=jax.ShapeDtypeStruct((b, t, FOX_INNER), F32),
        compiler_params=_cparams(("arbitrary", "arbitrary")),
        name="fox_prompt",
    )(qb, fcol, kb, vb, fk, f_end, f_first, thr)


def _fox_sample_kernel(tq, tk, npast, q_ref, fq_ref, pk_ref, pv_ref, fpk_ref, k_ref, v_ref, fk_ref, o_ref):
    q_all = q_ref[0]
    fq_all = fq_ref[0]
    qs = [q_all[:, _head_slice(h)] for h in range(FOX_HEADS)]
    fqs = [fq_all[:, F_ROW0 + h:F_ROW0 + h + 1] for h in range(FOX_HEADS)]
    causal = _iota((tq, tq), 0) >= _iota((tq, tq), 1)

    def past_block(j, carries):
        start = pl.multiple_of(j * tk, tk)
        return _attend_block(qs, fqs, pk_ref[0, pl.ds(start, tk), :].astype(BF16),
                             pv_ref[0, pl.ds(start, tk), :].astype(BF16), fpk_ref[0, j], None, carries)

    carries = lax.fori_loop(0, npast, past_block, _softmax_init(tq))
    _attend_finish(_attend_block(qs, fqs, k_ref[0], v_ref[0], fk_ref[0], causal, carries), o_ref)


def _fox_sample(qb, fcol, past_k, past_v, fpast_row, kb, vb, frow):
    b, t, _ = qb.shape
    p = past_k.shape[1]
    tk = min(p, 512)
    npast = p // tk
    fpk = fpast_row.reshape(b, 8, npast, tk).transpose(0, 2, 1, 3)
    bspec = lambda a: pl.BlockSpec((1,) + a.shape[1:], lambda i: (i,) + (0,) * (a.ndim - 1))
    args = (qb, fcol, past_k, past_v, fpk, kb, vb, frow)
    return pl.pallas_call(
        functools.partial(_fox_sample_kernel, t, tk, npast),
        grid=(b,),
        in_specs=[bspec(a) for a in args],
        out_specs=pl.BlockSpec((1, t, FOX_INNER), lambda i: (i, 0, 0)),
        out_shape=jax.ShapeDtypeStruct((b, t, FOX_INNER), F32),
        compiler_params=_cparams(("arbitrary",)),
        name="fox_sample",
    )(*args)


def _mlp_kernel(lm, nchunk, ug_ref, vn_ref, ws_ref, bst_ref, y_ref):
    r_i = _iota((lm, lm), 0)
    c_i = _iota((lm, lm), 1)
    tril = r_i >= c_i
    vn = vn_ref[0].astype(BF16)
    ug = ug_ref[0]
    for g in range(MLP_GROUPS):
        lo, hi = MLP_GROUP_DIM * g, MLP_GROUP_DIM * (g + 1)
        w = jnp.where(tril, ws_ref[g], 0.0).astype(BF16)
        bias = bst_ref[:, g:g + 1]
        for c in range(nchunk):
            r0, r1 = c * lm, (c + 1) * lm
            sv = _dot(w, vn[r0:r1, lo:hi]) + bias
            y_ref[0, r0:r1, lo:hi] = ug[r0:r1, lo:hi] * sv


def _chunk_mlp(ug, vn, ws, bst):
    b, t, _ = ug.shape
    lm = ws.shape[1]
    tm = min(t, 4 * lm)
    tok = pl.BlockSpec((1, tm, MLP_INNER), lambda i, j: (i, j, 0))
    return pl.pallas_call(
        functools.partial(_mlp_kernel, lm, tm // lm),
        grid=(b, t // tm),
        in_specs=[tok, tok, pl.BlockSpec(ws.shape, lambda i, j: (0, 0, 0)),
                  pl.BlockSpec(bst.shape, lambda i, j: (0, 0))],
        out_specs=tok,
        out_shape=jax.ShapeDtypeStruct((b, t, MLP_INNER), F32),
        compiler_params=_cparams(("arbitrary", "arbitrary")),
        name="chunk_mlp",
    )(ug, vn, ws, bst)


def _out_kernel(x_ref, ys_ref, yf_ref, ym_ref, g1_ref, sh_ref, sc_ref, g_ref, wo_ref, x1_ref, h2_ref):
    mix = (_dot(ys_ref[0].astype(BF16), wo_ref[0:SSD_INNER, :])
           + _dot(yf_ref[0].astype(BF16), wo_ref[SSD_INNER:SSD_INNER + FOX_INNER, :])
           + _dot(ym_ref[0].astype(BF16), wo_ref[SSD_INNER + FOX_INNER:D_MODEL, :]))
    x1 = x_ref[0] + g1_ref[0] * mix
    x1_ref[0] = x1
    ms = jnp.mean(x1 * x1, axis=-1, keepdims=True)
    h = x1 * lax.rsqrt(ms + EPS) * g_ref[...]
    h2_ref[0] = pltpu.bitcast((h * (1.0 + sc_ref[0]) + sh_ref[0]).astype(BF16), jnp.uint32)


def _out_proj(x, ys, yf, ym, g1, sh, sc, g, wo):
    b, t, _ = x.shape
    tm = min(t, 512)
    tok = lambda c: pl.BlockSpec((1, tm, c), lambda i, j: (i, j, 0))
    per_b = pl.BlockSpec((1, 1, D_MODEL), lambda i, j: (i, 0, 0))
    full = lambda a: pl.BlockSpec(a.shape, lambda i, j: (0,) * a.ndim)
    return pl.pallas_call(
        _out_kernel,
        grid=(b, t // tm),
        in_specs=[tok(D_MODEL), tok(SSD_INNER), tok(FOX_INNER), tok(MLP_INNER), per_b, per_b, per_b, full(g), full(wo)],
        out_specs=[tok(D_MODEL), pl.BlockSpec((1, tm // 2, D_MODEL), lambda i, j: (i, j, 0))],
        out_shape=[jax.ShapeDtypeStruct((b, t, D_MODEL), F32),
                   jax.ShapeDtypeStruct((b, t // 2, D_MODEL), jnp.uint32)],
        compiler_params=_cparams(("arbitrary", "arbitrary")),
        name="out_proj",
    )(x, ys, yf, ym, g1, sh, sc, g, wo)


def _top16(s):
    nk, tn = s.shape
    ridx = _iota((nk, tn), 0).astype(F32)
    r16 = _iota((PEER_TOPK, tn), 0)
    pos = jnp.full((nk, tn), float(PEER_TOPK), F32)
    tv = jnp.zeros((PEER_TOPK, tn), F32)
    for it in range(PEER_TOPK):
        level = [(s[g:g + 8, :], ridx[g:g + 8, :]) for g in range(0, nk, 8)]
        while len(level) > 1:
            nxt = []
            for a in range(0, len(level), 2):
                (va, ia), (vb, ib) = level[a], level[a + 1]
                nxt.append((jnp.maximum(va, vb), jnp.where(va >= vb, ia, ib)))
            level = nxt
        v8, i8 = level[0]
        m = jnp.max(v8, axis=0, keepdims=True)
        first = jnp.min(jnp.where(v8 == m, i8, float(nk)), axis=0, keepdims=True)
        sel = ridx == first
        pos = jnp.where(sel, float(it), pos)
        tv = jnp.where(r16 == it, m, tv)
        s = jnp.where(sel, -jnp.inf, s)
    return pos, tv


_CAND_GROUPS = [(0, 16)] + [(ka, 8) for ka in range(1, 8)]
_CAND_ROWS = 16 + 7 * 8 + 8


def _pair_select(ta, tb):
    tn = ta.shape[1]
    pieces, flats, valids = [], [], []
    for ka, rows in _CAND_GROUPS:
        pieces.append(ta[ka:ka + 1, :] + tb[0:rows, :])
        kb = _iota((rows, 1), 0)
        flats.append((ka * PEER_TOPK + kb).astype(F32))
        valids.append((ka + 1) * (kb + 1) <= PEER_TOPK)
    pieces.append(ta[8:16, :] + tb[0:1, :])
    flats.append(((8 + _iota((8, 1), 0)) * PEER_TOPK).astype(F32))
    valids.append(_iota((8, 1), 0) >= 0)
    cand0 = jnp.concatenate(pieces, axis=0)
    flat = jnp.concatenate(flats, axis=0)
    valid = jnp.concatenate(valids, axis=0)
    cand0 = jnp.where(valid, cand0, -jnp.inf)
    best = ta[0:1, :] + tb[0:1, :]

    cand = cand0
    selm = jnp.zeros((_CAND_ROWS, tn), F32)
    for _ in range(PEER_TOPK):
        m = jnp.max(cand, axis=0, keepdims=True)
        first = jnp.min(jnp.where(cand == m, flat, 4096.0), axis=0, keepdims=True)
        sel = flat == first
        cand = jnp.where(sel, -jnp.inf, cand)
        selm = jnp.where(sel, 1.0, selm)
    z = jnp.sum(jnp.where(selm > 0.0, jnp.exp(cand0 - best), 0.0), axis=0, keepdims=True)
    cnts = [jnp.sum(selm[0:16, :], axis=0, keepdims=True)]
    for i in range(1, 8):
        cnts.append(jnp.sum(selm[8 + 8 * i:16 + 8 * i, :], axis=0, keepdims=True))
    cnts.append(selm[_CAND_ROWS - 8:_CAND_ROWS, :])
    return jnp.concatenate(cnts, axis=0), z


def _peer_sel_kernel(tn, h_ref, wqt_ref, keys_ref, ea_ref, la_ref, eb_ref, pb_ref, qt_ref):
    qt_ref[...] = _dot_nt(wqt_ref[...], _unpack_rows(h_ref[...])).astype(BF16)

    def head(h, carry):
        row = pl.multiple_of(h * (2 * PEER_HALF), 2 * PEER_HALF)
        sa_all = _dot(keys_ref[2 * h], qt_ref[pl.ds(row, PEER_HALF), :])
        sb_all = _dot(keys_ref[2 * h + 1], qt_ref[pl.ds(row + PEER_HALF, PEER_HALF), :])
        for c in range(tn // LANES):
            cs = slice(c * LANES, (c + 1) * LANES)
            sa, sb = sa_all[:, cs], sb_all[:, cs]
            pos_a, ta = _top16(sa)
            pos_b, tb = _top16(sb)
            cnt, z = _pair_select(ta, tb)
            la = jnp.zeros_like(pos_a)
            for ka in range(PEER_TOPK):
                la = jnp.where(pos_a == float(ka), cnt[ka:ka + 1, :], la)
            ea_ref[h, :, cs] = jnp.where(pos_a < float(PEER_TOPK), jnp.exp(sa - ta[0:1, :]), 0.0) / z
            la_ref[h, :, cs] = la
            eb = jnp.where(pos_b < float(PEER_TOPK), jnp.exp(sb - tb[0:1, :]), 0.0)
            for sub in range(PEER_KEYS // PEER_SUB):
                src = slice(sub * PEER_SUB, (sub + 1) * PEER_SUB)
                dst = slice(sub * PEER_SUB // 2, (sub + 1) * PEER_SUB // 2)
                eb_ref[h, dst, cs] = pltpu.bitcast(eb[src].astype(BF16), jnp.uint32)
                pb_ref[h, dst, cs] = pltpu.bitcast(pos_b[src].astype(BF16), jnp.uint32)
        return carry

    lax.fori_loop(0, PEER_HEADS, head, 0)


def _peer_select(h2, wqt, keys):
    n = 2 * h2.shape[0]
    tn = 256
    per_tok = pl.BlockSpec((PEER_HEADS, PEER_KEYS, tn), lambda i: (0, 0, i))
    per_tok_pk = pl.BlockSpec((PEER_HEADS, PEER_KEYS // 2, tn), lambda i: (0, 0, i))
    shp = lambda dt: jax.ShapeDtypeStruct((PEER_HEADS, PEER_KEYS, n), dt)
    shp_pk = jax.ShapeDtypeStruct((PEER_HEADS, PEER_KEYS // 2, n), jnp.uint32)
    return pl.pallas_call(
        functools.partial(_peer_sel_kernel, tn),
        grid=(n // tn,),
        in_specs=[pl.BlockSpec((tn // 2, D_MODEL), lambda i: (i, 0)),
                  pl.BlockSpec(wqt.shape, lambda i: (0, 0)),
                  pl.BlockSpec(keys.shape, lambda i: (0, 0, 0))],
        out_specs=[per_tok, per_tok, per_tok_pk, per_tok_pk],
        out_shape=[shp(F32), shp(F32), shp_pk, shp_pk],
        scratch_shapes=[pltpu.VMEM((PEER_HEADS * 2 * PEER_HALF, tn), BF16)],
        compiler_params=_cparams(("arbitrary",)),
        name="peer_select",
    )(h2, wqt, keys)


def _peer_gate_stage(ec, ia0, rows, chunks, at_r, ga_w, ea_ref, la_ref, eb_ref, pb_ref):
    rows_per = ec // PEER_KEYS
    for r in rows:
        for c in chunks:
            cs = slice(c * LANES, (c + 1) * LANES)
            for sub in range(PEER_KEYS // PEER_SUB):
                pk = slice(sub * PEER_SUB // 2, (sub + 1) * PEER_SUB // 2)
                gate = None
                for h in range(PEER_HEADS):
                    la = la_ref[h, pl.ds(ia0, rows_per), cs][r:r + 1, :]
                    ea = ea_ref[h, pl.ds(ia0, rows_per), cs][r:r + 1, :]
                    la = jnp.broadcast_to(la, (PEER_SUB, LANES)).astype(BF16)
                    ea = jnp.broadcast_to(ea, (PEER_SUB, LANES)).astype(BF16)
                    pb = pltpu.bitcast(pb_ref[h, pk, cs], BF16)
                    eb = pltpu.bitcast(eb_ref[h, pk, cs], BF16)
                    term = jnp.where(pb < la, eb, 0.0) * ea
                    gate = term if gate is None else gate + term
                ex = slice(r * PEER_KEYS + sub * PEER_SUB, r * PEER_KEYS + (sub + 1) * PEER_SUB)
                ga_w[ex, cs] = gate * _gelu(at_r[ex, cs]).astype(BF16)


def _peer_dense_kernel(tn, ec, ne, n_items, h_ref, u_ref, vt_ref, ea_ref, la_ref, eb_ref, pb_ref, o_ref,
                       at0, at1, ga, acc):
    s = pl.program_id(0)
    w2 = jnp.maximum(s - 1, 0)
    e2 = w2 % ne
    rows_per = ec // PEER_KEYS
    ia0 = pl.multiple_of(e2 * rows_per, rows_per)
    n_sub = ec // MXU_DIM
    ia_per = MXU_DIM // PEER_KEYS
    lanes = range(tn // LANES)

    @pl.when(s == 0)
    def _():
        at1[...] = jnp.zeros_like(at1)

    @pl.when(e2 == 0)
    def _():
        acc[...] = jnp.zeros_like(acc)

    def step(at_w, at_r):
        h_t = _unpack_rows(h_ref[...])

        def pre(k):
            rows = slice(k * MXU_DIM, (k + 1) * MXU_DIM)
            pk = slice(k * MXU_DIM // 2, (k + 1) * MXU_DIM // 2)
            at_w[rows, :] = _dot_nt(_unpack_rows(u_ref[pk, :]), h_t)

        def gate(k):
            _peer_gate_stage(ec, ia0, range(k * ia_per, (k + 1) * ia_per), lanes, at_r, ga,
                             ea_ref, la_ref, eb_ref, pb_ref)

        def post(k):
            rows = slice(k * MXU_DIM, (k + 1) * MXU_DIM)
            acc[...] += _dot(_unpack_rows(vt_ref[:, rows]), ga[rows, :])

        for k in range(n_sub):
            pre(k)
            gate(k)
            if k > 0:
                post(k - 1)
        post(n_sub - 1)

    @pl.when(s % 2 == 0)
    def _():
        step(at0, at1)

    @pl.when(s % 2 == 1)
    def _():
        step(at1, at0)

    @pl.when(jnp.logical_and(e2 == ne - 1, s >= 1))
    def _():
        o_ref[...] = acc[...].T


def _peer_dense(h2, u_b, vt_b, ea, la, eb, pb):
    n = 2 * h2.shape[0]
    tn, ec = PEER_TOKEN_TILE, PEER_EXPERT_CHUNK
    ne = PEER_EXPERTS // ec
    n_items = (n // tn) * ne
    cur = lambda s: jnp.minimum(s, n_items - 1)
    prev = lambda s: jnp.maximum(s - 1, 0)
    per_tok = pl.BlockSpec((PEER_HEADS, PEER_KEYS, tn), lambda s: (0, 0, prev(s) // ne))
    per_tok_pk = pl.BlockSpec((PEER_HEADS, PEER_KEYS // 2, tn), lambda s: (0, 0, prev(s) // ne))
    return pl.pallas_call(
        functools.partial(_peer_dense_kernel, tn, ec, ne, n_items),
        grid=(n_items + 1,),
        in_specs=[pl.BlockSpec((tn // 2, D_MODEL), lambda s: (cur(s) // ne, 0)),
                  pl.BlockSpec((ec // 2, D_MODEL), lambda s: (cur(s) % ne, 0)),
                  pl.BlockSpec((D_MODEL // 2, ec), lambda s: (0, prev(s) % ne)),
                  per_tok, per_tok, per_tok_pk, per_tok_pk],
        out_specs=pl.BlockSpec((tn, D_MODEL), lambda s: (prev(s) // ne, 0)),
        out_shape=jax.ShapeDtypeStruct((n, D_MODEL), F32),
        scratch_shapes=[pltpu.VMEM((ec, tn), F32), pltpu.VMEM((ec, tn), F32), pltpu.VMEM((ec, tn), BF16),
                        pltpu.VMEM((D_MODEL, tn), F32)],
        compiler_params=_cparams(("arbitrary",)),
        name="peer_dense",
    )(h2, u_b, vt_b, ea, la, eb, pb)


def _resid_kernel(x_ref, p_ref, g_ref, o_ref):
    o_ref[0] = x_ref[0] + g_ref[0] * p_ref[0]


def _residual(x, peer, g2):
    b, t, _ = x.shape
    tm = min(t, 512)
    tok = pl.BlockSpec((1, tm, D_MODEL), lambda i, j: (i, j, 0))
    return pl.pallas_call(
        _resid_kernel,
        grid=(b, t // tm),
        in_specs=[tok, tok, pl.BlockSpec((1, 1, D_MODEL), lambda i, j: (i, 0, 0))],
        out_specs=tok,
        out_shape=jax.ShapeDtypeStruct((b, t, D_MODEL), F32),
        compiler_params=_cparams(("arbitrary", "arbitrary")),
        name="peer_residual",
    )(x, peer, g2)


def _lane_pad(vec, offset, width=LANES):
    out = jnp.zeros((width,), F32)
    return out.at[offset:offset + vec.shape[0]].set(vec.astype(F32))


def _layer_params(l, norm1_g, norm2_g, w_in, conv_w, conv_b, dt_bias, a_log, d_skip, ssd_norm_g, q_norm_g,
                  k_norm_g, fgate_b, w_s, b_s, w_out, peer_wq, peer_keys, peer_u, peer_v):
    w = w_in[l]
    o = [0]
    for sz in (SSD_INNER, CONV_CH, SSD_HEADS, FOX_INNER, FOX_INNER, FOX_INNER, FOX_HEADS, MLP_INNER, MLP_INNER):
        o.append(o[-1] + sz)
    wz, wxbc, wdt, wq, wk, wv, wf, wu, wvm = [w[:, o[i]:o[i + 1]] for i in range(9)]
    w_small = jnp.zeros((D_MODEL, SMALL_COLS), F32)
    w_small = w_small.at[:, DT_ROW0:DT_ROW0 + SSD_HEADS].set(wdt).at[:, F_ROW0:F_ROW0 + FOX_HEADS].set(wf)
    w_r = jnp.concatenate([wz, wxbc, wq, wk, wv, wu, wvm, w_small], axis=1).astype(BF16)
    ws_t = w_small[:, :SMALL_ROWS].T.astype(BF16)
    a_neg = -jnp.exp(a_log[l].astype(F32))
    grp = jnp.arange(FOX_INNER) // FOX_HEAD_DIM
    gm = jnp.where(grp[:, None] == grp[None, :], 1.0 / FOX_HEAD_DIM, 0.0).astype(BF16)
    heads = jnp.arange(SSD_INNER) // SSD_HEAD_DIM
    e_mat = (jnp.arange(LANES)[:, None] == heads[None, :]).astype(BF16)
    return dict(
        norm1_g=norm1_g[l][None], norm2_g=norm2_g[l][None], w_r=w_r, ws_t=ws_t,
        gq=(jnp.tile(q_norm_g[l], FOX_HEADS) * (FOX_HEAD_DIM ** -0.5))[None],
        gk=jnp.tile(k_norm_g[l], FOX_HEADS)[None], gm=gm,
        fox_bound=1.02 * FOX_HEAD_DIM ** 0.5 * jnp.max(jnp.abs(q_norm_g[l])) * jnp.max(jnp.abs(k_norm_g[l])),
        conv_w=conv_w[l], conv_b=conv_b[l][None],
        dtb_c=_lane_pad(dt_bias[l], DT_ROW0)[None], dtb_r=dt_bias[l].astype(F32)[:, None],
        a_c=_lane_pad(a_neg, DT_ROW0)[None], a_r=a_neg[:, None],
        dsk_x=jnp.repeat(d_skip[l].astype(F32), SSD_HEAD_DIM)[None], ssd_g=ssd_norm_g[l][None], e_mat=e_mat,
        fb_c=_lane_pad(fgate_b[l], F_ROW0)[None], fb_r=_lane_pad(fgate_b[l], 0, 8)[:, None],
        w_s=w_s[l], bs_t=jnp.zeros((MLP_CHUNK, LANES), F32).at[:, :MLP_GROUPS].set(b_s[l].T),
        w_out=w_out[l].astype(BF16),
        wq_t=peer_wq[l].T.astype(BF16),
        keys=peer_keys[l].reshape(PEER_HEADS * 2, PEER_KEYS, PEER_HALF).astype(BF16),
        u_b=_pack_rows(peer_u[l].astype(BF16)), vt_b=_pack_rows(peer_v[l].T.astype(BF16)),
    )


def _stream_mixers(x, peer, g2_prev, mod, p, hist8, s0t, past):
    sh1, sc1, g1, sh2, sc2, _ = mod
    b, t, _ = x.shape
    x, (z, xbc, qb, kn, kb, v, vb, ug, vn, sm, smt) = _in_proj(
        x, peer, g2_prev, sh1, sc1, p['norm1_g'], p['w_r'], p['ws_t'], p['gq'], p['gk'], p['gm'])
    L = SSD_CHUNK_PROMPT if t % SSD_CHUNK_PROMPT == 0 else t
    y_ssd, s_fin, c_fin = _ssd(xbc, z, sm, smt, hist8, s0t, p['conv_w'], p['conv_b'], p['dtb_c'], p['dtb_r'],
                               p['a_c'], p['a_r'], p['dsk_x'], p['ssd_g'], p['e_mat'], L)
    zeros8 = jnp.zeros((b, 8, LANES), F32)
    if past is None:
        logf_c, f_col, f_row, _, _ = _fcum(sm, smt, F_ROW0 // 8, p['fb_c'], p['fb_r'], zeros8, zeros8, True)
        y_fox = _fox_prompt(qb, f_col, kb, vb, f_row, p['fox_bound'])
    else:
        pk, pv, plf_col, plf_row = past
        _, _, fp_row, end_c, end_r = _fcum(plf_col, plf_row, 0, p['fb_c'], p['fb_r'], zeros8, zeros8, False)
        logf_c, f_col, f_row, _, _ = _fcum(sm, smt, F_ROW0 // 8, p['fb_c'], p['fb_r'], end_c, end_r, True)
        y_fox = _fox_sample(qb, f_col, pk, pv, fp_row, kb, vb, f_row)
    lm = MLP_CHUNK if t % MLP_CHUNK == 0 else t
    y_mlp = _chunk_mlp(ug, vn, p['w_s'][:, :lm, :lm], p['bs_t'][:lm])
    x1, h2 = _out_proj(x, y_ssd, y_fox, y_mlp, g1, sh2, sc2, p['norm2_g'], p['w_out'])
    logf = logf_c[:, :, F_ROW0:F_ROW0 + FOX_HEADS]
    new_ssm = s_fin.reshape(b, SSD_STATE, SSD_HEADS, SSD_HEAD_DIM).transpose(0, 2, 3, 1)
    new_conv = c_fin[:, 8 - (CONV_W - 1):, :]
    kc = kn.reshape(b, t, FOX_HEADS, FOX_HEAD_DIM)
    vc = v.reshape(b, t, FOX_HEADS, FOX_HEAD_DIM)
    return x1, h2, (kc, vc, logf, new_ssm, new_conv, vn)


def _peer(h2, p):
    b, t2, _ = h2.shape
    n_tok = 2 * b * t2
    n_pad = -n_tok % PEER_TOKEN_TILE
    flat = h2.reshape(n_tok // 2, D_MODEL)
    if n_pad:
        flat = jnp.concatenate([flat, jnp.zeros((n_pad // 2, D_MODEL), jnp.uint32)], axis=0)
    ea, la, eb, pb = _peer_select(flat, p['wq_t'], p['keys'])
    out = _peer_dense(flat, p['u_b'], p['vt_b'], ea, la, eb, pb)
    return out[:n_tok].reshape(b, 2 * t2, D_MODEL)


def kernel(x_prompt, x_sample, c_prompt, c_sample, cache_fox_k, cache_fox_v, cache_fox_logf, state_ssm, state_conv, norm1_g, norm2_g, w_ada, b_ada, w_in, conv_w, conv_b, dt_bias, a_log, d_skip, ssd_norm_g, q_norm_g, k_norm_g, fgate_b, w_s, b_s, w_out, peer_wq, peer_keys, peer_u, peer_v):
    depth = w_ada.shape[0]
    bp, tp, _ = x_prompt.shape
    bs, ts, _ = x_sample.shape
    past_len = cache_fox_k.shape[2]
    mod_all = _modulation(jnp.concatenate([c_prompt, c_sample], axis=0).astype(F32), w_ada, b_ada)

    xp, xs = x_prompt, x_sample
    peer_p = peer_s = g2p = g2s = None
    outs = [[] for _ in range(11)]
    for l in range(depth):
        p = _layer_params(l, norm1_g, norm2_g, w_in, conv_w, conv_b, dt_bias, a_log, d_skip, ssd_norm_g, q_norm_g,
                          k_norm_g, fgate_b, w_s, b_s, w_out, peer_wq, peer_keys, peer_u, peer_v)
        mods = [m[:, None, :] for m in jnp.split(mod_all[l], 6, axis=-1)]
        mod_p = [m[:bp] for m in mods]
        mod_s = [m[bp:] for m in mods]
        hist_p = jnp.zeros((bp, 8, CONV_CH), F32)
        s0_p = jnp.zeros((bp, SSD_STATE, SSD_INNER), F32)
        hist_s = jnp.concatenate([jnp.zeros((bs, 8 - (CONV_W - 1), CONV_CH), F32), state_conv[l].astype(F32)], axis=1)
        s0_s = state_ssm[l].astype(F32).transpose(0, 3, 1, 2).reshape(bs, SSD_STATE, SSD_INNER)
        plf = cache_fox_logf[l].astype(F32)
        plf_col = jnp.pad(plf, ((0, 0), (0, 0), (F_ROW0, LANES - F_ROW0 - FOX_HEADS)))
        plf_row = jnp.pad(plf.transpose(0, 2, 1), ((0, 0), (0, 8 - FOX_HEADS), (0, 0)))
        past = (cache_fox_k[l].reshape(bs, past_len, FOX_INNER), cache_fox_v[l].reshape(bs, past_len, FOX_INNER),
                plf_col, plf_row)

        x1p, h2p, st_p = _stream_mixers(xp, peer_p, g2p, mod_p, p, hist_p, s0_p, None)
        x1s, h2s, st_s = _stream_mixers(xs, peer_s, g2s, mod_s, p, hist_s, s0_s, past)

        peer_p = _peer(h2p, p)
        peer_s = _peer(h2s, p)
        xp, xs, g2p, g2s = x1p, x1s, mod_p[5], mod_s[5]
        for i in range(5):
            outs[i].append(st_p[i])
        for i in range(6):
            outs[5 + i].append(st_s[i])

    yp = _residual(xp, peer_p, g2p)
    ys = _residual(xs, peer_s, g2s)
    return (yp, ys) + tuple(jnp.stack(o) for o in outs)
```

```python
import functools

import jax
import jax.numpy as jnp
from jax import lax
from jax.experimental import pallas as pl
from jax.experimental.pallas import tpu as pltpu

F32 = jnp.float32
BF16 = jnp.bfloat16
EPS = 1e-6

D_MODEL = 1024
SSD_HEADS = 8
SSD_HEAD_DIM = 64
SSD_INNER = SSD_HEADS * SSD_HEAD_DIM
SSD_GROUPS = 2
SSD_STATE = 64
CONV_W = 4
CONV_CH = SSD_INNER + 2 * SSD_GROUPS * SSD_STATE
FOX_HEADS = 4
FOX_HEAD_DIM = 64
FOX_INNER = FOX_HEADS * FOX_HEAD_DIM
MLP_GROUPS = 4
MLP_GROUP_DIM = 64
MLP_INNER = MLP_GROUPS * MLP_GROUP_DIM
MLP_CHUNK = 128
FOX_PROMPT_TILE = 512
FCUM_TILE = 2048
FCUM_SUB = 512
SSD_CHUNK_PROMPT = 128
PEER_HEADS = 8
PEER_KEYS = 128
PEER_EXPERTS = PEER_KEYS * PEER_KEYS
PEER_HALF = 128
PEER_TOPK = 16
PEER_TOKEN_TILE = 512
PEER_EXPERT_CHUNK = 1024
LANES = 128
MXU_DIM = 256
PEER_SUB = 128
SMALL_COLS = LANES
SMALL_ROWS = 16
DT_ROW0, F_ROW0 = 0, 8
NEG = -1e30
EXP_UNDERFLOW = 104.0
VMEM_LIMIT = 56 * 1024 * 1024

_Z0, _XBC0, _Q0, _K0, _V0, _U0, _VM0, _SM0 = 0, 512, 1280, 1536, 1792, 2048, 2304, 2560
PROJ_COLS = _SM0 + SMALL_COLS


def _cparams(sem, flags=None):
    return pltpu.CompilerParams(dimension_semantics=sem, vmem_limit_bytes=VMEM_LIMIT, flags=flags)


def _dot(a, b):
    return jnp.dot(a, b, preferred_element_type=F32)


def _dot_nt(a, b):
    return lax.dot_general(a, b, (((1,), (1,)), ((), ())), preferred_element_type=F32)


def _dot_tn(a, b):
    return lax.dot_general(a, b, (((0,), (0,)), ((), ())), preferred_element_type=F32)


def _split2(x):
    hi = x.astype(BF16)
    lo = (x - hi.astype(F32)).astype(BF16)
    return hi, lo


def _split3(x):
    hi = x.astype(BF16)
    r = x - hi.astype(F32)
    mid = r.astype(BF16)
    lo = (r - mid.astype(F32)).astype(BF16)
    return hi, mid, lo


def _dot3_l(x, w):
    hi, mid, lo = _split3(x)
    return _dot(hi, w) + _dot(mid, w) + _dot(lo, w)


def _dot3_r(w, x):
    hi, mid, lo = _split3(x)
    return _dot(w, hi) + _dot(w, mid) + _dot(w, lo)


def _sigmoid(x):
    return 1.0 / (1.0 + jnp.exp(-x))


def _silu(x):
    return x * _sigmoid(x)


def _softplus(x):
    return jnp.maximum(x, 0.0) + jnp.log1p(jnp.exp(-jnp.abs(x)))


_GELU_A = 2.0 * 0.7978845608028654
_GELU_B = _GELU_A * 0.044715


def _gelu(x):
    z2 = x * (_GELU_A + _GELU_B * (x * x))
    return x / (1.0 + jnp.exp(-z2))


def _iota(shape, dim):
    return lax.broadcasted_iota(jnp.int32, shape, dim)


def _cast_kernel(x_ref, o_ref):
    o_ref[0] = x_ref[0].astype(BF16)


def _cast_t_kernel(x_ref, o_ref):
    o_ref[0] = x_ref[0].T.astype(BF16)


def _to_bf16(x, rows):
    n, r, c = x.shape
    return pl.pallas_call(
        _cast_kernel,
        grid=(n, r // rows),
        in_specs=[pl.BlockSpec((1, rows, c), lambda l, i: (l, i, 0))],
        out_specs=pl.BlockSpec((1, rows, c), lambda l, i: (l, i, 0)),
        out_shape=jax.ShapeDtypeStruct((n, r, c), BF16),
        compiler_params=_cparams(("arbitrary", "arbitrary")),
        name="cast_bf16",
    )(x)


def _to_bf16_t(x, rows, cols):
    n, r, c = x.shape
    return pl.pallas_call(
        _cast_t_kernel,
        grid=(n, r // rows, c // cols),
        in_specs=[pl.BlockSpec((1, rows, cols), lambda l, i, j: (l, i, j))],
        out_specs=pl.BlockSpec((1, cols, rows), lambda l, i, j: (l, j, i)),
        out_shape=jax.ShapeDtypeStruct((n, c, r), BF16),
        compiler_params=_cparams(("arbitrary", "arbitrary", "arbitrary")),
        name="cast_bf16_transposed",
    )(x)


def _mod_kernel(c_ref, w_ref, b_ref, o_ref):
    c = c_ref[...]
    o_ref[0] = jnp.dot(_silu(c), w_ref[0], preferred_element_type=F32,
                       precision=lax.Precision.HIGHEST) + b_ref[0]


def _modulation(c_all, w_ada, b_ada):
    depth, _, n6 = w_ada.shape
    bc = c_all.shape[0]
    tn = 1536
    return pl.pallas_call(
        _mod_kernel,
        grid=(depth, n6 // tn),
        in_specs=[pl.BlockSpec((bc, D_MODEL), lambda l, j: (0, 0)),
                  pl.BlockSpec((1, D_MODEL, tn), lambda l, j: (l, 0, j)),
                  pl.BlockSpec((1, 1, tn), lambda l, j: (l, 0, j))],
        out_specs=pl.BlockSpec((1, bc, tn), lambda l, j: (l, 0, j)),
        out_shape=jax.ShapeDtypeStruct((depth, bc, n6), F32),
        compiler_params=_cparams(("arbitrary", "arbitrary")),
        name="adaln_mod",
    )(c_all, w_ada, b_ada.reshape(depth, 1, n6))


def _in_kernel(with_peer, *refs):
    if with_peer:
        x_ref, p_ref, g2_ref = refs[:3]
        refs = refs[3:]
    else:
        x_ref = refs[0]
        refs = refs[1:]
    (sh_ref, sc_ref, g_ref, w_ref, wst_ref, gq_ref, gk_ref, gm_ref) = refs[:8]
    outs = refs[8:]
    if with_peer:
        xo_ref = outs[0]
        outs = outs[1:]
    (z_ref, xbc_ref, q_ref, k_ref, kb_ref, v_ref, vb_ref, ug_ref, vn_ref, sm_ref, smt_ref) = outs

    x = x_ref[0]
    if with_peer:
        x = x + g2_ref[0] * p_ref[0]
        xo_ref[0] = x
    ms = jnp.mean(x * x, axis=-1, keepdims=True)
    h = x * lax.rsqrt(ms + EPS) * g_ref[...]
    h = h * (1.0 + sc_ref[0]) + sh_ref[0]
    hb = h.astype(BF16)
    proj = _dot(hb, w_ref[...])
    z_ref[0] = proj[:, _Z0:_XBC0]
    xbc_ref[0] = proj[:, _XBC0:_Q0]
    q = proj[:, _Q0:_K0]
    k = proj[:, _K0:_V0]
    v = proj[:, _V0:_U0]
    u = proj[:, _U0:_VM0]
    vm = proj[:, _VM0:_SM0]
    sm_ref[0] = proj[:, _SM0:PROJ_COLS]
    smt_ref[0] = _dot_nt(wst_ref[...], hb)
    gm = gm_ref[...]

    def gmean(y):
        hi, lo = _split2(y)
        return _dot(hi, gm) + _dot(lo, gm)

    qn = q * lax.rsqrt(gmean(q * q) + EPS) * gq_ref[...]
    kn = k * lax.rsqrt(gmean(k * k) + EPS) * gk_ref[...]
    q_ref[0] = qn.astype(BF16)
    k_ref[0] = kn
    kb_ref[0] = kn.astype(BF16)
    v_ref[0] = v
    vb_ref[0] = v.astype(BF16)
    ug_ref[0] = _gelu(u)
    gv = _gelu(vm)
    mu = gmean(gv)
    cen = gv - mu
    var = gmean(cen * cen)
    vn_ref[0] = cen * lax.rsqrt(var + EPS)


def _in_proj(x, peer, g2, sh, sc, g, w_r, ws_t, gq, gk, gm):
    b, t, _ = x.shape
    tm = min(t, 512)
    with_peer = peer is not None
    tok = lambda c: pl.BlockSpec((1, tm, c), lambda i, j: (i, j, 0))
    per_b = pl.BlockSpec((1, 1, D_MODEL), lambda i, j: (i, 0, 0))
    full = lambda a: pl.BlockSpec(a.shape, lambda i, j: (0,) * a.ndim)
    in_specs = [tok(D_MODEL)]
    args = [x]
    if with_peer:
        in_specs += [tok(D_MODEL), per_b]
        args += [peer, g2]
    in_specs += [per_b, per_b, full(g), full(w_r), full(ws_t), full(gq), full(gk), full(gm)]
    args += [sh, sc, g, w_r, ws_t, gq, gk, gm]
    out_cols = [(SSD_INNER, F32), (CONV_CH, F32), (FOX_INNER, BF16), (FOX_INNER, F32), (FOX_INNER, BF16),
                (FOX_INNER, F32), (FOX_INNER, BF16), (MLP_INNER, F32), (MLP_INNER, F32), (SMALL_COLS, F32)]
    out_specs = [tok(c) for c, _ in out_cols]
    out_shape = [jax.ShapeDtypeStruct((b, t, c), dt) for c, dt in out_cols]
    out_specs.append(pl.BlockSpec((1, SMALL_ROWS, tm), lambda i, j: (i, 0, j)))
    out_shape.append(jax.ShapeDtypeStruct((b, SMALL_ROWS, t), F32))
    if with_peer:
        out_specs = [tok(D_MODEL)] + out_specs
        out_shape = [jax.ShapeDtypeStruct((b, t, D_MODEL), F32)] + out_shape
    res = pl.pallas_call(
        functools.partial(_in_kernel, with_peer),
        grid=(b, t // tm),
        in_specs=in_specs, out_specs=out_specs, out_shape=out_shape,
        compiler_params=_cparams(("arbitrary", "arbitrary")),
        name="in_proj",
    )(*args)
    if with_peer:
        return res[0], res[1:]
    return x, res


def _ssd_kernel(L, xbc_ref, z_ref, sm_ref, smt_ref, hist_ref, s0_ref, cw_ref, cb_ref, dtbc_ref, dtbr_ref,
                ac_ref, ar_ref, dsk_ref, ng_ref, e_ref, y_ref, sfin_ref, cfin_ref, xpad, st, ybuf):
    c = pl.program_id(1)
    nc = pl.num_programs(1)

    @pl.when(c == 0)
    def _():
        xpad[0:8, :] = hist_ref[0]
        st[...] = s0_ref[0]

    xpad[8:8 + L, :] = xbc_ref[0]
    conv = cb_ref[...]
    for tap in range(CONV_W):
        conv = conv + xpad[5 + tap:5 + tap + L, :] * cw_ref[tap:tap + 1, :]
    tail = xpad[L:L + 8, :]
    xpad[0:8, :] = tail
    xc = _silu(conv)
    xs = xc[:, 0:SSD_INNER]

    r_i = _iota((L, L), 0)
    c_i = _iota((L, L), 1)
    causal = r_i >= c_i
    tri = jnp.where(causal, 1.0, 0.0).astype(BF16)
    triu = jnp.where(r_i <= c_i, 1.0, 0.0).astype(BF16)

    dtc = _softplus(sm_ref[0] + dtbc_ref[...])
    acum_c = _dot3_r(tri, dtc * ac_ref[...])
    e = e_ref[...]
    acum_x = _dot3_l(acum_c, e)
    dt_x = _dot3_l(dtc, e)
    dtr = _softplus(smt_ref[0][DT_ROW0:DT_ROW0 + 8, :] + dtbr_ref[...])
    acum_r = _dot3_l(dtr * ar_ref[...], triu)

    bmat = [xc[:, SSD_INNER + SSD_STATE * g:SSD_INNER + SSD_STATE * (g + 1)].astype(BF16) for g in range(SSD_GROUPS)]
    c0 = SSD_INNER + SSD_GROUPS * SSD_STATE
    cmat = [xc[:, c0 + SSD_STATE * g:c0 + SSD_STATE * (g + 1)].astype(BF16) for g in range(SSD_GROUPS)]
    cb = [_dot_nt(cmat[g], bmat[g]) for g in range(SSD_GROUPS)]
    xsb = xs.astype(BF16)
    hpg = SSD_HEADS // SSD_GROUPS
    for h in range(SSD_HEADS):
        g = h // hpg
        seg = acum_c[:, h:h + 1] - acum_r[h:h + 1, :]
        dec = jnp.where(causal, jnp.exp(jnp.minimum(seg, 0.0)), 0.0)
        m = cb[g] * dec * dtr[h:h + 1, :]
        ybuf[:, SSD_HEAD_DIM * h:SSD_HEAD_DIM * (h + 1)] = _dot(
            m.astype(BF16), xsb[:, SSD_HEAD_DIM * h:SSD_HEAD_DIM * (h + 1)])

    gw = SSD_INNER // SSD_GROUPS
    stb = st[...].astype(BF16)
    y_off = jnp.concatenate([_dot(cmat[g], stb[:, gw * g:gw * (g + 1)]) for g in range(SSD_GROUPS)], axis=1)
    y = ybuf[...] + y_off * jnp.exp(acum_x) + dsk_ref[...] * xs
    a_end = acum_x[L - 1:L, :]
    xw = (xs * dt_x * jnp.exp(a_end - acum_x)).astype(BF16)
    new_states = jnp.concatenate([_dot_tn(bmat[g], xw[:, gw * g:gw * (g + 1)]) for g in range(SSD_GROUPS)], axis=1)
    st[...] = st[...] * jnp.exp(a_end) + new_states

    yg = y * _silu(z_ref[0])
    ms = jnp.mean(yg * yg, axis=-1, keepdims=True)
    y_ref[0] = yg * lax.rsqrt(ms + EPS) * ng_ref[...]

    @pl.when(c == nc - 1)
    def _():
        sfin_ref[0] = st[...]
        cfin_ref[0] = tail


def _ssd(xbc, z, sm, smt, hist8, s0t, cw, cb, dtb_c, dtb_r, a_c, a_r, dsk_x, ng, e_mat, L):
    b, t, _ = xbc.shape
    tok = lambda c: pl.BlockSpec((1, L, c), lambda i, j: (i, j, 0))
    full = lambda a: pl.BlockSpec(a.shape, lambda i, j: (0,) * a.ndim)
    per_b = lambda a: pl.BlockSpec((1,) + a.shape[1:], lambda i, j: (i,) + (0,) * (a.ndim - 1))
    return pl.pallas_call(
        functools.partial(_ssd_kernel, L),
        grid=(b, t // L),
        in_specs=[tok(CONV_CH), tok(SSD_INNER), tok(SMALL_COLS),
                  pl.BlockSpec((1, SMALL_ROWS, L), lambda i, j: (i, 0, j)),
                  per_b(hist8), per_b(s0t), full(cw), full(cb), full(dtb_c), full(dtb_r), full(a_c), full(a_r),
                  full(dsk_x), full(ng), full(e_mat)],
        out_specs=[tok(SSD_INNER),
                   pl.BlockSpec((1, SSD_STATE, SSD_INNER), lambda i, j: (i, 0, 0)),
                   pl.BlockSpec((1, 8, CONV_CH), lambda i, j: (i, 0, 0))],
        out_shape=[jax.ShapeDtypeStruct((b, t, SSD_INNER), F32),
                   jax.ShapeDtypeStruct((b, SSD_STATE, SSD_INNER), F32),
                   jax.ShapeDtypeStruct((b, 8, CONV_CH), F32)],
        scratch_shapes=[pltpu.VMEM((L + 8, CONV_CH), F32), pltpu.VMEM((SSD_STATE, SSD_INNER), F32),
                        pltpu.VMEM((L, SSD_INNER), F32)],
        compiler_params=_cparams(("arbitrary", "arbitrary")),
        name="ssd_scan",
    )(xbc, z, sm, smt, hist8, s0t, cw, cb, dtb_c, dtb_r, a_c, a_r, dsk_x, ng, e_mat)


def _fcum_kernel(tf, activate, colsrc_ref, rowsrc_ref, fbc_ref, fbr_ref, initc_ref, initr_ref,
                 lfc_ref, fc_ref, fr_ref, endc_ref, endr_ref, carc, carr):
    j = pl.program_id(1)

    @pl.when(j == 0)
    def _():
        carc[...] = initc_ref[0]
        carr[...] = initr_ref[0]

    ts = min(tf, FCUM_SUB)
    r_i = _iota((ts, ts), 0)
    c_i = _iota((ts, ts), 1)
    tri = jnp.where(r_i >= c_i, 1.0, 0.0).astype(BF16)
    triu = jnp.where(r_i <= c_i, 1.0, 0.0).astype(BF16)
    car_c = carc[0:1, :]
    car_r = carr[:, 0:1]
    for k in range(tf // ts):
        xc = colsrc_ref[0, k * ts:(k + 1) * ts, :]
        xr = rowsrc_ref[0, :, k * ts:(k + 1) * ts]
        if activate:
            xc = -_softplus(-(xc + fbc_ref[...]))
            xr = -_softplus(-(xr + fbr_ref[...]))
        fcol = car_c + _dot3_r(tri, xc)
        frow = car_r + _dot3_l(xr, triu)
        lfc_ref[0, k * ts:(k + 1) * ts, :] = xc
        fc_ref[0, k * ts:(k + 1) * ts, :] = fcol
        fr_ref[0, :, k * ts:(k + 1) * ts] = frow
        car_c = fcol[ts - 1:ts, :]
        car_r = frow[:, ts - 1:ts]
    carc[...] = jnp.broadcast_to(car_c, carc.shape)
    carr[...] = jnp.broadcast_to(car_r, carr.shape)
    endc_ref[0] = carc[...]
    endr_ref[0] = carr[...]


def _fcum(colsrc, rowsrc, row_block, fb_c, fb_r, init_c, init_r, activate):
    b, t, _ = colsrc.shape
    tf = min(t, FCUM_TILE)
    full = lambda a: pl.BlockSpec(a.shape, lambda i, j: (0,) * a.ndim)
    per_b = lambda a: pl.BlockSpec((1,) + a.shape[1:], lambda i, j: (i,) + (0,) * (a.ndim - 1))
    return pl.pallas_call(
        functools.partial(_fcum_kernel, tf, activate),
        grid=(b, t // tf),
        in_specs=[pl.BlockSpec((1, tf, LANES), lambda i, j: (i, j, 0)),
                  pl.BlockSpec((1, 8, tf), lambda i, j: (i, row_block, j)),
                  full(fb_c), full(fb_r), per_b(init_c), per_b(init_r)],
        out_specs=[pl.BlockSpec((1, tf, LANES), lambda i, j: (i, j, 0)),
                   pl.BlockSpec((1, tf, LANES), lambda i, j: (i, j, 0)),
                   pl.BlockSpec((1, 8, tf), lambda i, j: (i, 0, j)),
                   pl.BlockSpec((1, 8, LANES), lambda i, j: (i, 0, 0)),
                   pl.BlockSpec((1, 8, LANES), lambda i, j: (i, 0, 0))],
        out_shape=[jax.ShapeDtypeStruct((b, t, LANES), F32), jax.ShapeDtypeStruct((b, t, LANES), F32),
                   jax.ShapeDtypeStruct((b, 8, t), F32), jax.ShapeDtypeStruct((b, 8, LANES), F32),
                   jax.ShapeDtypeStruct((b, 8, LANES), F32)],
        scratch_shapes=[pltpu.VMEM((8, LANES), F32), pltpu.VMEM((8, LANES), F32)],
        compiler_params=_cparams(("arbitrary", "arbitrary")),
        name="forget_cumsum",
    )(colsrc, rowsrc, fb_c, fb_r, init_c, init_r)


def _softmax_step(s, vb, carry):
    m, l, acc = carry
    tq, tk = s.shape
    m_new = jnp.maximum(m, jnp.max(s, axis=-1, keepdims=True))
    p = jnp.exp(s - m_new)
    alpha = jnp.exp(m - m_new)
    if tk % LANES == 0:
        psum = p[:, 0:LANES]
        for c in range(1, tk // LANES):
            psum = psum + p[:, c * LANES:(c + 1) * LANES]
    else:
        psum = jnp.where(_iota((tq, LANES), 1) == 0, jnp.sum(p, axis=-1, keepdims=True), 0.0)
    l = alpha * l + psum
    acc = alpha * acc + _dot(p.astype(BF16), vb)
    return m_new, l, acc


def _softmax_init(tq):
    return tuple((jnp.full((tq, 1), NEG, F32), jnp.zeros((tq, LANES), F32), jnp.zeros((tq, FOX_HEAD_DIM), F32))
                 for _ in range(FOX_HEADS))


def _head_slice(h):
    return slice(FOX_HEAD_DIM * h, FOX_HEAD_DIM * (h + 1))


def _attend_block(qs, fqs, kb, vb, fk, mask, carries):
    out = []
    for h in range(FOX_HEADS):
        hs = _head_slice(h)
        s = _dot_nt(qs[h], kb[:, hs]) + fqs[h] - fk[h:h + 1, :]
        if mask is not None:
            s = jnp.where(mask, s, NEG)
        out.append(_softmax_step(s, vb[:, hs], carries[h]))
    return tuple(out)


def _attend_finish(carries, o_ref):
    for h in range(FOX_HEADS):
        _, l, acc = carries[h]
        o_ref[0, :, _head_slice(h)] = acc / jnp.sum(l, axis=-1, keepdims=True)


def _fox_prompt_kernel(tq, nk, q_ref, fq_ref, k_ref, v_ref, fk_ref, fend_ref, ffirst_ref, thr_ref, o_ref):
    b = pl.program_id(0)
    qi = pl.program_id(1)
    half = tq // 2
    k_diag = k_ref[0, pl.ds(pl.multiple_of(qi * tq, tq), tq), :]
    v_diag = v_ref[0, pl.ds(pl.multiple_of(qi * tq, tq), tq), :]
    fk_diag = fk_ref[0, qi]
    init = (jnp.full((half, 1), NEG, F32), jnp.zeros((half, LANES), F32), jnp.zeros((half, FOX_HEAD_DIM), F32))
    for h in range(FOX_HEADS):
        hs = _head_slice(h)
        base = (b * FOX_HEADS + h) * nk
        slack = thr_ref[0] + ffirst_ref[base + qi]
        n_live = lax.fori_loop(0, qi, lambda j, c: c + (slack - fend_ref[base + j] >= 0.0).astype(jnp.int32), 0)
        chains = []
        carries = []
        for r2 in range(2):
            rs = slice(r2 * half, (r2 + 1) * half)
            qh = q_ref[0, rs, hs]
            fq = fq_ref[0, rs, F_ROW0 + h:F_ROW0 + h + 1]
            chains.append((qh, fq))
            s = _dot_nt(qh, k_diag[:, hs]) + fq - fk_diag[h:h + 1, :]
            causal = (r2 * half + _iota((half, tq), 0)) >= _iota((half, tq), 1)
            carries.append(_softmax_step(jnp.where(causal, s, NEG), v_diag[:, hs], init))

        def body(t, carries, chains=chains, hs=hs, h=h):
            j = qi - 1 - t
            start = pl.multiple_of(j * tq, tq)
            kb = k_ref[0, pl.ds(start, tq), hs]
            vb = v_ref[0, pl.ds(start, tq), hs]
            fk = fk_ref[0, j][h:h + 1, :]
            return tuple(_softmax_step(_dot_nt(qh, kb) + fq - fk, vb, c) for (qh, fq), c in zip(chains, carries))

        carries = lax.fori_loop(0, n_live, body, tuple(carries))
        for r2 in range(2):
            _, l, acc = carries[r2]
            o_ref[0, r2 * half:(r2 + 1) * half, hs] = acc / jnp.sum(l, axis=-1, keepdims=True)


def _fox_prompt(qb, fcol, kb, vb, frow, score_bound):
    b, t, _ = qb.shape
    tq = min(t, FOX_PROMPT_TILE)
    nk = t // tq
    fk = frow.reshape(b, 8, nk, tq).transpose(0, 2, 1, 3)
    f_heads = frow[:, :FOX_HEADS, :].reshape(b, FOX_HEADS, nk, tq)
    f_end = f_heads[:, :, :, tq - 1].reshape(-1)
    f_first = f_heads[:, :, :, 0].reshape(-1)
    thr = (2.0 * score_bound + EXP_UNDERFLOW).reshape(1).astype(F32)
    smem = pl.BlockSpec(memory_space=pltpu.SMEM)
    return pl.pallas_call(
        functools.partial(_fox_prompt_kernel, tq, nk),
        grid=(b, t // tq),
        in_specs=[pl.BlockSpec((1, tq, FOX_INNER), lambda i, j: (i, j, 0)),
                  pl.BlockSpec((1, tq, LANES), lambda i, j: (i, j, 0)),
                  pl.BlockSpec((1, t, FOX_INNER), lambda i, j: (i, 0, 0)),
                  pl.BlockSpec((1, t, FOX_INNER), lambda i, j: (i, 0, 0)),
                  pl.BlockSpec((1, nk, 8, tq), lambda i, j: (i, 0, 0, 0)),
                  smem, smem, smem],
        out_specs=pl.BlockSpec((1, tq, FOX_INNER), lambda i, j: (i, j, 0)),
        out_shape=jax.ShapeDtypeStruct((b, t, FOX_INNER), F32),
        compiler_params=_cparams(("arbitrary", "arbitrary")),
        name="fox_prompt",
    )(qb, fcol, kb, vb, fk, f_end, f_first, thr)


def _fox_sample_kernel(tq, tk, npast, q_ref, fq_ref, pk_ref, pv_ref, fpk_ref, k_ref, v_ref, fk_ref, o_ref):
    q_all = q_ref[0]
    fq_all = fq_ref[0]
    qs = [q_all[:, _head_slice(h)] for h in range(FOX_HEADS)]
    fqs = [fq_all[:, F_ROW0 + h:F_ROW0 + h + 1] for h in range(FOX_HEADS)]
    causal = _iota((tq, tq), 0) >= _iota((tq, tq), 1)

    def past_block(j, carries):
        start = pl.multiple_of(j * tk, tk)
        return _attend_block(qs, fqs, pk_ref[0, pl.ds(start, tk), :].astype(BF16),
                             pv_ref[0, pl.ds(start, tk), :].astype(BF16), fpk_ref[0, j], None, carries)

    carries = lax.fori_loop(0, npast, past_block, _softmax_init(tq))
    _attend_finish(_attend_block(qs, fqs, k_ref[0], v_ref[0], fk_ref[0], causal, carries), o_ref)


def _fox_sample(qb, fcol, past_k, past_v, fpast_row, kb, vb, frow):
    b, t, _ = qb.shape
    p = past_k.shape[1]
    tk = min(p, 512)
    npast = p // tk
    fpk = fpast_row.reshape(b, 8, npast, tk).transpose(0, 2, 1, 3)
    bspec = lambda a: pl.BlockSpec((1,) + a.shape[1:], lambda i: (i,) + (0,) * (a.ndim - 1))
    args = (qb, fcol, past_k, past_v, fpk, kb, vb, frow)
    return pl.pallas_call(
        functools.partial(_fox_sample_kernel, t, tk, npast),
        grid=(b,),
        in_specs=[bspec(a) for a in args],
        out_specs=pl.BlockSpec((1, t, FOX_INNER), lambda i: (i, 0, 0)),
        out_shape=jax.ShapeDtypeStruct((b, t, FOX_INNER), F32),
        compiler_params=_cparams(("arbitrary",)),
        name="fox_sample",
    )(*args)


def _mlp_kernel(lm, nchunk, ug_ref, vn_ref, ws_ref, bst_ref, y_ref):
    r_i = _iota((lm, lm), 0)
    c_i = _iota((lm, lm), 1)
    tril = r_i >= c_i
    vn = vn_ref[0].astype(BF16)
    ug = ug_ref[0]
    for g in range(MLP_GROUPS):
        lo, hi = MLP_GROUP_DIM * g, MLP_GROUP_DIM * (g + 1)
        w = jnp.where(tril, ws_ref[g], 0.0).astype(BF16)
        bias = bst_ref[:, g:g + 1]
        for c in range(nchunk):
            r0, r1 = c * lm, (c + 1) * lm
            sv = _dot(w, vn[r0:r1, lo:hi]) + bias
            y_ref[0, r0:r1, lo:hi] = ug[r0:r1, lo:hi] * sv


def _chunk_mlp(ug, vn, ws, bst):
    b, t, _ = ug.shape
    lm = ws.shape[1]
    tm = min(t, 4 * lm)
    tok = pl.BlockSpec((1, tm, MLP_INNER), lambda i, j: (i, j, 0))
    return pl.pallas_call(
        functools.partial(_mlp_kernel, lm, tm // lm),
        grid=(b, t // tm),
        in_specs=[tok, tok, pl.BlockSpec(ws.shape, lambda i, j: (0, 0, 0)),
                  pl.BlockSpec(bst.shape, lambda i, j: (0, 0))],
        out_specs=tok,
        out_shape=jax.ShapeDtypeStruct((b, t, MLP_INNER), F32),
        compiler_params=_cparams(("arbitrary", "arbitrary")),
        name="chunk_mlp",
    )(ug, vn, ws, bst)


def _out_kernel(x_ref, ys_ref, yf_ref, ym_ref, g1_ref, sh_ref, sc_ref, g_ref, wo_ref, x1_ref, h2_ref):
    mix = (_dot(ys_ref[0].astype(BF16), wo_ref[0:SSD_INNER, :])
           + _dot(yf_ref[0].astype(BF16), wo_ref[SSD_INNER:SSD_INNER + FOX_INNER, :])
           + _dot(ym_ref[0].astype(BF16), wo_ref[SSD_INNER + FOX_INNER:D_MODEL, :]))
    x1 = x_ref[0] + g1_ref[0] * mix
    x1_ref[0] = x1
    ms = jnp.mean(x1 * x1, axis=-1, keepdims=True)
    h = x1 * lax.rsqrt(ms + EPS) * g_ref[...]
    h2_ref[0] = (h * (1.0 + sc_ref[0]) + sh_ref[0]).astype(BF16)


def _out_proj(x, ys, yf, ym, g1, sh, sc, g, wo):
    b, t, _ = x.shape
    tm = min(t, 512)
    tok = lambda c: pl.BlockSpec((1, tm, c), lambda i, j: (i, j, 0))
    per_b = pl.BlockSpec((1, 1, D_MODEL), lambda i, j: (i, 0, 0))
    full = lambda a: pl.BlockSpec(a.shape, lambda i, j: (0,) * a.ndim)
    return pl.pallas_call(
        _out_kernel,
        grid=(b, t // tm),
        in_specs=[tok(D_MODEL), tok(SSD_INNER), tok(FOX_INNER), tok(MLP_INNER), per_b, per_b, per_b, full(g), full(wo)],
        out_specs=[tok(D_MODEL), tok(D_MODEL)],
        out_shape=[jax.ShapeDtypeStruct((b, t, D_MODEL), F32), jax.ShapeDtypeStruct((b, t, D_MODEL), BF16)],
        compiler_params=_cparams(("arbitrary", "arbitrary")),
        name="out_proj",
    )(x, ys, yf, ym, g1, sh, sc, g, wo)


def _top16(s):
    nk, tn = s.shape
    ridx = _iota((nk, tn), 0).astype(F32)
    r16 = _iota((PEER_TOPK, tn), 0)
    pos = jnp.full((nk, tn), float(PEER_TOPK), F32)
    tv = jnp.zeros((PEER_TOPK, tn), F32)
    for it in range(PEER_TOPK):
        level = [(s[g:g + 8, :], ridx[g:g + 8, :]) for g in range(0, nk, 8)]
        while len(level) > 1:
            nxt = []
            for a in range(0, len(level), 2):
                (va, ia), (vb, ib) = level[a], level[a + 1]
                nxt.append((jnp.maximum(va, vb), jnp.where(va >= vb, ia, ib)))
            level = nxt
        v8, i8 = level[0]
        m = jnp.max(v8, axis=0, keepdims=True)
        first = jnp.min(jnp.where(v8 == m, i8, float(nk)), axis=0, keepdims=True)
        sel = ridx == first
        pos = jnp.where(sel, float(it), pos)
        tv = jnp.where(r16 == it, m, tv)
        s = jnp.where(sel, -jnp.inf, s)
    return pos, tv


_CAND_GROUPS = [(0, 16)] + [(ka, 8) for ka in range(1, 8)]
_CAND_ROWS = 16 + 7 * 8 + 8


def _pair_select(ta, tb):
    tn = ta.shape[1]
    pieces, flats, valids = [], [], []
    for ka, rows in _CAND_GROUPS:
        pieces.append(ta[ka:ka + 1, :] + tb[0:rows, :])
        kb = _iota((rows, 1), 0)
        flats.append((ka * PEER_TOPK + kb).astype(F32))
        valids.append((ka + 1) * (kb + 1) <= PEER_TOPK)
    pieces.append(ta[8:16, :] + tb[0:1, :])
    flats.append(((8 + _iota((8, 1), 0)) * PEER_TOPK).astype(F32))
    valids.append(_iota((8, 1), 0) >= 0)
    cand0 = jnp.concatenate(pieces, axis=0)
    flat = jnp.concatenate(flats, axis=0)
    valid = jnp.concatenate(valids, axis=0)
    cand0 = jnp.where(valid, cand0, -jnp.inf)
    best = ta[0:1, :] + tb[0:1, :]

    cand = cand0
    selm = jnp.zeros((_CAND_ROWS, tn), F32)
    for _ in range(PEER_TOPK):
        m = jnp.max(cand, axis=0, keepdims=True)
        first = jnp.min(jnp.where(cand == m, flat, 4096.0), axis=0, keepdims=True)
        sel = flat == first
        cand = jnp.where(sel, -jnp.inf, cand)
        selm = jnp.where(sel, 1.0, selm)
    z = jnp.sum(jnp.where(selm > 0.0, jnp.exp(cand0 - best), 0.0), axis=0, keepdims=True)
    cnts = [jnp.sum(selm[0:16, :], axis=0, keepdims=True)]
    for i in range(1, 8):
        cnts.append(jnp.sum(selm[8 + 8 * i:16 + 8 * i, :], axis=0, keepdims=True))
    cnts.append(selm[_CAND_ROWS - 8:_CAND_ROWS, :])
    return jnp.concatenate(cnts, axis=0), z


def _peer_sel_kernel(tn, h_ref, wqt_ref, keys_ref, ea_ref, la_ref, eb_ref, pb_ref, qt_ref):
    qt_ref[...] = _dot_nt(wqt_ref[...], h_ref[...]).astype(BF16)

    def head(h, carry):
        row = pl.multiple_of(h * (2 * PEER_HALF), 2 * PEER_HALF)
        sa_all = _dot(keys_ref[2 * h], qt_ref[pl.ds(row, PEER_HALF), :])
        sb_all = _dot(keys_ref[2 * h + 1], qt_ref[pl.ds(row + PEER_HALF, PEER_HALF), :])
        for c in range(tn // LANES):
            cs = slice(c * LANES, (c + 1) * LANES)
            sa, sb = sa_all[:, cs], sb_all[:, cs]
            pos_a, ta = _top16(sa)
            pos_b, tb = _top16(sb)
            cnt, z = _pair_select(ta, tb)
            la = jnp.zeros_like(pos_a)
            for ka in range(PEER_TOPK):
                la = jnp.where(pos_a == float(ka), cnt[ka:ka + 1, :], la)
            ea_ref[h, :, cs] = jnp.where(pos_a < float(PEER_TOPK), jnp.exp(sa - ta[0:1, :]), 0.0) / z
            la_ref[h, :, cs] = la
            eb = jnp.where(pos_b < float(PEER_TOPK), jnp.exp(sb - tb[0:1, :]), 0.0)
            for sub in range(PEER_KEYS // PEER_SUB):
                src = slice(sub * PEER_SUB, (sub + 1) * PEER_SUB)
                dst = slice(sub * PEER_SUB // 2, (sub + 1) * PEER_SUB // 2)
                eb_ref[h, dst, cs] = pltpu.bitcast(eb[src].astype(BF16), jnp.uint32)
                pb_ref[h, dst, cs] = pltpu.bitcast(pos_b[src].astype(BF16), jnp.uint32)
        return carry

    lax.fori_loop(0, PEER_HEADS, head, 0)


def _peer_select(h2, wqt_all, layer, keys):
    n = h2.shape[0]
    tn = 256
    per_tok = pl.BlockSpec((PEER_HEADS, PEER_KEYS, tn), lambda i: (0, 0, i))
    per_tok_pk = pl.BlockSpec((PEER_HEADS, PEER_KEYS // 2, tn), lambda i: (0, 0, i))
    shp = lambda dt: jax.ShapeDtypeStruct((PEER_HEADS, PEER_KEYS, n), dt)
    shp_pk = jax.ShapeDtypeStruct((PEER_HEADS, PEER_KEYS // 2, n), jnp.uint32)
    return pl.pallas_call(
        functools.partial(_peer_sel_kernel, tn),
        grid=(n // tn,),
        in_specs=[pl.BlockSpec((tn, D_MODEL), lambda i: (i, 0)),
                  pl.BlockSpec((None,) + wqt_all.shape[1:], lambda i: (layer, 0, 0)),
                  pl.BlockSpec(keys.shape, lambda i: (0, 0, 0))],
        out_specs=[per_tok, per_tok, per_tok_pk, per_tok_pk],
        out_shape=[shp(F32), shp(F32), shp_pk, shp_pk],
        scratch_shapes=[pltpu.VMEM((PEER_HEADS * 2 * PEER_HALF, tn), BF16)],
        compiler_params=_cparams(("arbitrary",)),
        name="peer_select",
    )(h2, wqt_all, keys)


def _peer_gate_stage(ec, ia0, rows, chunks, at_r, ga_w, ea_ref, la_ref, eb_ref, pb_ref):
    rows_per = ec // PEER_KEYS
    for r in rows:
        for c in chunks:
            cs = slice(c * LANES, (c + 1) * LANES)
            for sub in range(PEER_KEYS // PEER_SUB):
                pk = slice(sub * PEER_SUB // 2, (sub + 1) * PEER_SUB // 2)
                gate = None
                for h in range(PEER_HEADS):
                    la = la_ref[h, pl.ds(ia0, rows_per), cs][r:r + 1, :]
                    ea = ea_ref[h, pl.ds(ia0, rows_per), cs][r:r + 1, :]
                    la = jnp.broadcast_to(la, (PEER_SUB, LANES)).astype(BF16)
                    ea = jnp.broadcast_to(ea, (PEER_SUB, LANES)).astype(BF16)
                    pb = pltpu.bitcast(pb_ref[h, pk, cs], BF16)
                    eb = pltpu.bitcast(eb_ref[h, pk, cs], BF16)
                    term = jnp.where(pb < la, eb, 0.0) * ea
                    gate = term if gate is None else gate + term
                ex = slice(r * PEER_KEYS + sub * PEER_SUB, r * PEER_KEYS + (sub + 1) * PEER_SUB)
                ga_w[ex, cs] = gate * _gelu(at_r[ex, cs]).astype(BF16)


def _peer_dense_kernel(tn, ec, h_ref, u_ref, vt_ref, ea_ref, la_ref, eb_ref, pb_ref, o_ref, at, ga, acc):
    e = pl.program_id(1)
    ne = pl.num_programs(1)

    @pl.when(e == 0)
    def _():
        acc[...] = jnp.zeros_like(acc)

    at[...] = _dot_nt(u_ref[...], h_ref[...])
    rows_per = ec // PEER_KEYS
    ia0 = pl.multiple_of(e * rows_per, rows_per)
    _peer_gate_stage(ec, ia0, range(rows_per), range(tn // LANES), at, ga, ea_ref, la_ref, eb_ref, pb_ref)
    acc[...] += _dot(vt_ref[...], ga[...])

    @pl.when(e == ne - 1)
    def _():
        o_ref[...] = acc[...].T


def _peer_dense(h2, u_all, vt_all, layer, ea, la, eb, pb):
    n = h2.shape[0]
    tn, ec = PEER_TOKEN_TILE, PEER_EXPERT_CHUNK
    per_tok = pl.BlockSpec((PEER_HEADS, PEER_KEYS, tn), lambda i, e: (0, 0, i))
    per_tok_pk = pl.BlockSpec((PEER_HEADS, PEER_KEYS // 2, tn), lambda i, e: (0, 0, i))
    return pl.pallas_call(
        functools.partial(_peer_dense_kernel, tn, ec),
        grid=(n // tn, PEER_EXPERTS // ec),
        in_specs=[pl.BlockSpec((tn, D_MODEL), lambda i, e: (i, 0)),
                  pl.BlockSpec((None, ec, D_MODEL), lambda i, e: (layer, e, 0)),
                  pl.BlockSpec((None, D_MODEL, ec), lambda i, e: (layer, 0, e)),
                  per_tok, per_tok, per_tok_pk, per_tok_pk],
        out_specs=pl.BlockSpec((tn, D_MODEL), lambda i, e: (i, 0)),
        out_shape=jax.ShapeDtypeStruct((n, D_MODEL), F32),
        scratch_shapes=[pltpu.VMEM((ec, tn), F32), pltpu.VMEM((ec, tn), BF16), pltpu.VMEM((D_MODEL, tn), F32)],
        compiler_params=_cparams(("arbitrary", "arbitrary")),
        name="peer_dense",
    )(h2, u_all, vt_all, ea, la, eb, pb)


def _resid_kernel(x_ref, p_ref, g_ref, o_ref):
    o_ref[0] = x_ref[0] + g_ref[0] * p_ref[0]


def _residual(x, peer, g2):
    b, t, _ = x.shape
    tm = min(t, 512)
    tok = pl.BlockSpec((1, tm, D_MODEL), lambda i, j: (i, j, 0))
    return pl.pallas_call(
        _resid_kernel,
        grid=(b, t // tm),
        in_specs=[tok, tok, pl.BlockSpec((1, 1, D_MODEL), lambda i, j: (i, 0, 0))],
        out_specs=tok,
        out_shape=jax.ShapeDtypeStruct((b, t, D_MODEL), F32),
        compiler_params=_cparams(("arbitrary", "arbitrary")),
        name="peer_residual",
    )(x, peer, g2)


def _lane_pad(vec, offset, width=LANES):
    out = jnp.zeros((width,), F32)
    return out.at[offset:offset + vec.shape[0]].set(vec.astype(F32))


def _layer_params(l, norm1_g, norm2_g, w_in, conv_w, conv_b, dt_bias, a_log, d_skip, ssd_norm_g, q_norm_g,
                  k_norm_g, fgate_b, w_s, b_s, w_out, peer_keys):
    w = w_in[l]
    o = [0]
    for sz in (SSD_INNER, CONV_CH, SSD_HEADS, FOX_INNER, FOX_INNER, FOX_INNER, FOX_HEADS, MLP_INNER, MLP_INNER):
        o.append(o[-1] + sz)
    wz, wxbc, wdt, wq, wk, wv, wf, wu, wvm = [w[:, o[i]:o[i + 1]] for i in range(9)]
    w_small = jnp.zeros((D_MODEL, SMALL_COLS), F32)
    w_small = w_small.at[:, DT_ROW0:DT_ROW0 + SSD_HEADS].set(wdt).at[:, F_ROW0:F_ROW0 + FOX_HEADS].set(wf)
    w_r = jnp.concatenate([wz, wxbc, wq, wk, wv, wu, wvm, w_small], axis=1).astype(BF16)
    ws_t = w_small[:, :SMALL_ROWS].T.astype(BF16)
    a_neg = -jnp.exp(a_log[l].astype(F32))
    grp = jnp.arange(FOX_INNER) // FOX_HEAD_DIM
    gm = jnp.where(grp[:, None] == grp[None, :], 1.0 / FOX_HEAD_DIM, 0.0).astype(BF16)
    heads = jnp.arange(SSD_INNER) // SSD_HEAD_DIM
    e_mat = (jnp.arange(LANES)[:, None] == heads[None, :]).astype(BF16)
    return dict(
        norm1_g=norm1_g[l][None], norm2_g=norm2_g[l][None], w_r=w_r, ws_t=ws_t,
        gq=(jnp.tile(q_norm_g[l], FOX_HEADS) * (FOX_HEAD_DIM ** -0.5))[None],
        gk=jnp.tile(k_norm_g[l], FOX_HEADS)[None], gm=gm,
        fox_bound=1.02 * FOX_HEAD_DIM ** 0.5 * jnp.max(jnp.abs(q_norm_g[l])) * jnp.max(jnp.abs(k_norm_g[l])),
        conv_w=conv_w[l], conv_b=conv_b[l][None],
        dtb_c=_lane_pad(dt_bias[l], DT_ROW0)[None], dtb_r=dt_bias[l].astype(F32)[:, None],
        a_c=_lane_pad(a_neg, DT_ROW0)[None], a_r=a_neg[:, None],
        dsk_x=jnp.repeat(d_skip[l].astype(F32), SSD_HEAD_DIM)[None], ssd_g=ssd_norm_g[l][None], e_mat=e_mat,
        fb_c=_lane_pad(fgate_b[l], F_ROW0)[None], fb_r=_lane_pad(fgate_b[l], 0, 8)[:, None],
        w_s=w_s[l], bs_t=jnp.zeros((MLP_CHUNK, LANES), F32).at[:, :MLP_GROUPS].set(b_s[l].T),
        w_out=w_out[l].astype(BF16),
        keys=peer_keys[l].reshape(PEER_HEADS * 2, PEER_KEYS, PEER_HALF).astype(BF16),
    )


def _stream_mixers(x, peer, g2_prev, mod, p, hist8, s0t, past):
    sh1, sc1, g1, sh2, sc2, _ = mod
    b, t, _ = x.shape
    x, (z, xbc, qb, kn, kb, v, vb, ug, vn, sm, smt) = _in_proj(
        x, peer, g2_prev, sh1, sc1, p['norm1_g'], p['w_r'], p['ws_t'], p['gq'], p['gk'], p['gm'])
    L = SSD_CHUNK_PROMPT if t % SSD_CHUNK_PROMPT == 0 else t
    y_ssd, s_fin, c_fin = _ssd(xbc, z, sm, smt, hist8, s0t, p['conv_w'], p['conv_b'], p['dtb_c'], p['dtb_r'],
                               p['a_c'], p['a_r'], p['dsk_x'], p['ssd_g'], p['e_mat'], L)
    zeros8 = jnp.zeros((b, 8, LANES), F32)
    if past is None:
        logf_c, f_col, f_row, _, _ = _fcum(sm, smt, F_ROW0 // 8, p['fb_c'], p['fb_r'], zeros8, zeros8, True)
        y_fox = _fox_prompt(qb, f_col, kb, vb, f_row, p['fox_bound'])
    else:
        pk, pv, plf_col, plf_row = past
        _, _, fp_row, end_c, end_r = _fcum(plf_col, plf_row, 0, p['fb_c'], p['fb_r'], zeros8, zeros8, False)
        logf_c, f_col, f_row, _, _ = _fcum(sm, smt, F_ROW0 // 8, p['fb_c'], p['fb_r'], end_c, end_r, True)
        y_fox = _fox_sample(qb, f_col, pk, pv, fp_row, kb, vb, f_row)
    lm = MLP_CHUNK if t % MLP_CHUNK == 0 else t
    y_mlp = _chunk_mlp(ug, vn, p['w_s'][:, :lm, :lm], p['bs_t'][:lm])
    x1, h2 = _out_proj(x, y_ssd, y_fox, y_mlp, g1, sh2, sc2, p['norm2_g'], p['w_out'])
    logf = logf_c[:, :, F_ROW0:F_ROW0 + FOX_HEADS]
    new_ssm = s_fin.reshape(b, SSD_STATE, SSD_HEADS, SSD_HEAD_DIM).transpose(0, 2, 3, 1)
    new_conv = c_fin[:, 8 - (CONV_W - 1):, :]
    kc = kn.reshape(b, t, FOX_HEADS, FOX_HEAD_DIM)
    vc = v.reshape(b, t, FOX_HEADS, FOX_HEAD_DIM)
    return x1, h2, (kc, vc, logf, new_ssm, new_conv, vn)


def _peer(h2, p, tables, layer):
    b, t, _ = h2.shape
    n_tok = b * t
    n_pad = -n_tok % PEER_TOKEN_TILE
    flat = h2.reshape(n_tok, D_MODEL)
    if n_pad:
        flat = jnp.concatenate([flat, jnp.zeros((n_pad, D_MODEL), BF16)], axis=0)
    wqt_all, u_all, vt_all = tables
    ea, la, eb, pb = _peer_select(flat, wqt_all, layer, p['keys'])
    out = _peer_dense(flat, u_all, vt_all, layer, ea, la, eb, pb)
    return out[:n_tok].reshape(b, t, D_MODEL)


def kernel(x_prompt, x_sample, c_prompt, c_sample, cache_fox_k, cache_fox_v, cache_fox_logf, state_ssm, state_conv, norm1_g, norm2_g, w_ada, b_ada, w_in, conv_w, conv_b, dt_bias, a_log, d_skip, ssd_norm_g, q_norm_g, k_norm_g, fgate_b, w_s, b_s, w_out, peer_wq, peer_keys, peer_u, peer_v):
    depth = w_ada.shape[0]
    bp, tp, _ = x_prompt.shape
    bs, ts, _ = x_sample.shape
    past_len = cache_fox_k.shape[2]
    mod_all = _modulation(jnp.concatenate([c_prompt, c_sample], axis=0).astype(F32), w_ada, b_ada)

    peer_tables = (_to_bf16_t(peer_wq, D_MODEL, 512), _to_bf16(peer_u, 1024), _to_bf16_t(peer_v, 512, D_MODEL))

    xp, xs = x_prompt, x_sample
    peer_p = peer_s = g2p = g2s = None
    outs = [[] for _ in range(11)]
    for l in range(depth):
        p = _layer_params(l, norm1_g, norm2_g, w_in, conv_w, conv_b, dt_bias, a_log, d_skip, ssd_norm_g, q_norm_g,
                          k_norm_g, fgate_b, w_s, b_s, w_out, peer_keys)
        mods = [m[:, None, :] for m in jnp.split(mod_all[l], 6, axis=-1)]
        mod_p = [m[:bp] for m in mods]
        mod_s = [m[bp:] for m in mods]
        hist_p = jnp.zeros((bp, 8, CONV_CH), F32)
        s0_p = jnp.zeros((bp, SSD_STATE, SSD_INNER), F32)
        hist_s = jnp.concatenate([jnp.zeros((bs, 8 - (CONV_W - 1), CONV_CH), F32), state_conv[l].astype(F32)], axis=1)
        s0_s = state_ssm[l].astype(F32).transpose(0, 3, 1, 2).reshape(bs, SSD_STATE, SSD_INNER)
        plf = cache_fox_logf[l].astype(F32)
        plf_col = jnp.pad(plf, ((0, 0), (0, 0), (F_ROW0, LANES - F_ROW0 - FOX_HEADS)))
        plf_row = jnp.pad(plf.transpose(0, 2, 1), ((0, 0), (0, 8 - FOX_HEADS), (0, 0)))
        past = (cache_fox_k[l].reshape(bs, past_len, FOX_INNER), cache_fox_v[l].reshape(bs, past_len, FOX_INNER),
                plf_col, plf_row)

        x1p, h2p, st_p = _stream_mixers(xp, peer_p, g2p, mod_p, p, hist_p, s0_p, None)
        x1s, h2s, st_s = _stream_mixers(xs, peer_s, g2s, mod_s, p, hist_s, s0_s, past)

        peer_p = _peer(h2p, p, peer_tables, l)
        peer_s = _peer(h2s, p, peer_tables, l)
        xp, xs, g2p, g2s = x1p, x1s, mod_p[5], mod_s[5]
        for i in range(5):
            outs[i].append(st_p[i])
        for i in range(6):
            outs[5 + i].append(st_s[i])

    yp = _residual(xp, peer_p, g2p)
    ys = _residual(xs, peer_s, g2s)
    return (yp, ys) + tuple(jnp.stack(o) for o in outs)
```

```python
import functools

import jax
import jax.numpy as jnp
from jax import lax
from jax.experimental import pallas as pl
from jax.experimental.pallas import tpu as pltpu

F32 = jnp.float32
BF16 = jnp.bfloat16
EPS = 1e-6

D_MODEL = 1024
SSD_HEADS = 8
SSD_HEAD_DIM = 64
SSD_INNER = SSD_HEADS * SSD_HEAD_DIM
SSD_GROUPS = 2
SSD_STATE = 64
CONV_W = 4
CONV_CH = SSD_INNER + 2 * SSD_GROUPS * SSD_STATE
FOX_HEADS = 4
FOX_HEAD_DIM = 64
FOX_INNER = FOX_HEADS * FOX_HEAD_DIM
MLP_GROUPS = 4
MLP_GROUP_DIM = 64
MLP_INNER = MLP_GROUPS * MLP_GROUP_DIM
MLP_CHUNK = 128
FOX_PROMPT_TILE = 512
FOX_CHAINS = 4
FCUM_TILE = 2048
FCUM_SUB = 512
SSD_CHUNK_PROMPT = 128
PEER_HEADS = 8
PEER_KEYS = 128
PEER_EXPERTS = PEER_KEYS * PEER_KEYS
PEER_HALF = 128
PEER_TOPK = 16
PEER_TOKEN_TILE = 512
PEER_EXPERT_CHUNK = 1024
LANES = 128
MXU_DIM = 256
PEER_SUB = 128
SMALL_COLS = LANES
SMALL_ROWS = 16
DT_ROW0, F_ROW0 = 0, 8
NEG = -1e30
EXP_UNDERFLOW = 104.0
VMEM_LIMIT = 56 * 1024 * 1024

_Z0, _XBC0, _Q0, _K0, _V0, _U0, _VM0, _SM0 = 0, 512, 1280, 1536, 1792, 2048, 2304, 2560
PROJ_COLS = _SM0 + SMALL_COLS


def _cparams(sem, flags=None):
    return pltpu.CompilerParams(dimension_semantics=sem, vmem_limit_bytes=VMEM_LIMIT, flags=flags)


def _dot(a, b):
    return jnp.dot(a, b, preferred_element_type=F32)


def _dot_nt(a, b):
    return lax.dot_general(a, b, (((1,), (1,)), ((), ())), preferred_element_type=F32)


def _dot_tn(a, b):
    return lax.dot_general(a, b, (((0,), (0,)), ((), ())), preferred_element_type=F32)


def _split2(x):
    hi = x.astype(BF16)
    lo = (x - hi.astype(F32)).astype(BF16)
    return hi, lo


def _split3(x):
    hi = x.astype(BF16)
    r = x - hi.astype(F32)
    mid = r.astype(BF16)
    lo = (r - mid.astype(F32)).astype(BF16)
    return hi, mid, lo


def _dot3_l(x, w):
    hi, mid, lo = _split3(x)
    return _dot(hi, w) + _dot(mid, w) + _dot(lo, w)


def _dot3_r(w, x):
    hi, mid, lo = _split3(x)
    return _dot(w, hi) + _dot(w, mid) + _dot(w, lo)


def _sigmoid(x):
    return 1.0 / (1.0 + jnp.exp(-x))


def _silu(x):
    return x * _sigmoid(x)


def _softplus(x):
    return jnp.maximum(x, 0.0) + jnp.log1p(jnp.exp(-jnp.abs(x)))


_GELU_A = 2.0 * 0.7978845608028654
_GELU_B = _GELU_A * 0.044715


def _gelu(x):
    z2 = x * (_GELU_A + _GELU_B * (x * x))
    return x / (1.0 + jnp.exp(-z2))


def _iota(shape, dim):
    return lax.broadcasted_iota(jnp.int32, shape, dim)


def _cast_kernel(x_ref, o_ref):
    o_ref[0] = x_ref[0].astype(BF16)


def _cast_t_kernel(x_ref, o_ref):
    o_ref[0] = x_ref[0].T.astype(BF16)


def _to_bf16(x, rows):
    n, r, c = x.shape
    return pl.pallas_call(
        _cast_kernel,
        grid=(n, r // rows),
        in_specs=[pl.BlockSpec((1, rows, c), lambda l, i: (l, i, 0))],
        out_specs=pl.BlockSpec((1, rows, c), lambda l, i: (l, i, 0)),
        out_shape=jax.ShapeDtypeStruct((n, r, c), BF16),
        compiler_params=_cparams(("arbitrary", "arbitrary")),
        name="cast_bf16",
    )(x)


def _to_bf16_t(x, rows, cols):
    n, r, c = x.shape
    return pl.pallas_call(
        _cast_t_kernel,
        grid=(n, r // rows, c // cols),
        in_specs=[pl.BlockSpec((1, rows, cols), lambda l, i, j: (l, i, j))],
        out_specs=pl.BlockSpec((1, cols, rows), lambda l, i, j: (l, j, i)),
        out_shape=jax.ShapeDtypeStruct((n, c, r), BF16),
        compiler_params=_cparams(("arbitrary", "arbitrary", "arbitrary")),
        name="cast_bf16_transposed",
    )(x)


def _mod_kernel(c_ref, w_ref, b_ref, o_ref):
    c = c_ref[...]
    o_ref[0] = jnp.dot(_silu(c), w_ref[0], preferred_element_type=F32,
                       precision=lax.Precision.HIGHEST) + b_ref[0]


def _modulation(c_all, w_ada, b_ada):
    depth, _, n6 = w_ada.shape
    bc = c_all.shape[0]
    tn = 1536
    return pl.pallas_call(
        _mod_kernel,
        grid=(depth, n6 // tn),
        in_specs=[pl.BlockSpec((bc, D_MODEL), lambda l, j: (0, 0)),
                  pl.BlockSpec((1, D_MODEL, tn), lambda l, j: (l, 0, j)),
                  pl.BlockSpec((1, 1, tn), lambda l, j: (l, 0, j))],
        out_specs=pl.BlockSpec((1, bc, tn), lambda l, j: (l, 0, j)),
        out_shape=jax.ShapeDtypeStruct((depth, bc, n6), F32),
        compiler_params=_cparams(("arbitrary", "arbitrary")),
        name="adaln_mod",
    )(c_all, w_ada, b_ada.reshape(depth, 1, n6))


def _in_kernel(with_peer, *refs):
    if with_peer:
        x_ref, p_ref, g2_ref = refs[:3]
        refs = refs[3:]
    else:
        x_ref = refs[0]
        refs = refs[1:]
    (sh_ref, sc_ref, g_ref, w_ref, wst_ref, gq_ref, gk_ref, gm_ref) = refs[:8]
    outs = refs[8:]
    if with_peer:
        xo_ref = outs[0]
        outs = outs[1:]
    (z_ref, xbc_ref, q_ref, k_ref, kb_ref, v_ref, vb_ref, ug_ref, vn_ref, sm_ref, smt_ref) = outs

    x = x_ref[0]
    if with_peer:
        x = x + g2_ref[0] * p_ref[0]
        xo_ref[0] = x
    ms = jnp.mean(x * x, axis=-1, keepdims=True)
    h = x * lax.rsqrt(ms + EPS) * g_ref[...]
    h = h * (1.0 + sc_ref[0]) + sh_ref[0]
    hb = h.astype(BF16)
    proj = _dot(hb, w_ref[...])
    z_ref[0] = proj[:, _Z0:_XBC0]
    xbc_ref[0] = proj[:, _XBC0:_Q0]
    q = proj[:, _Q0:_K0]
    k = proj[:, _K0:_V0]
    v = proj[:, _V0:_U0]
    u = proj[:, _U0:_VM0]
    vm = proj[:, _VM0:_SM0]
    sm_ref[0] = proj[:, _SM0:PROJ_COLS]
    smt_ref[0] = _dot_nt(wst_ref[...], hb)
    gm = gm_ref[...]

    def gmean(y):
        hi, lo = _split2(y)
        return _dot(hi, gm) + _dot(lo, gm)

    qn = q * lax.rsqrt(gmean(q * q) + EPS) * gq_ref[...]
    kn = k * lax.rsqrt(gmean(k * k) + EPS) * gk_ref[...]
    q_ref[0] = qn.astype(BF16)
    k_ref[0] = kn
    kb_ref[0] = kn.astype(BF16)
    v_ref[0] = v
    vb_ref[0] = v.astype(BF16)
    ug_ref[0] = _gelu(u)
    gv = _gelu(vm)
    mu = gmean(gv)
    cen = gv - mu
    var = gmean(cen * cen)
    vn_ref[0] = cen * lax.rsqrt(var + EPS)


def _in_proj(x, peer, g2, sh, sc, g, w_r, ws_t, gq, gk, gm):
    b, t, _ = x.shape
    tm = min(t, 512)
    with_peer = peer is not None
    tok = lambda c: pl.BlockSpec((1, tm, c), lambda i, j: (i, j, 0))
    per_b = pl.BlockSpec((1, 1, D_MODEL), lambda i, j: (i, 0, 0))
    full = lambda a: pl.BlockSpec(a.shape, lambda i, j: (0,) * a.ndim)
    in_specs = [tok(D_MODEL)]
    args = [x]
    if with_peer:
        in_specs += [tok(D_MODEL), per_b]
        args += [peer, g2]
    in_specs += [per_b, per_b, full(g), full(w_r), full(ws_t), full(gq), full(gk), full(gm)]
    args += [sh, sc, g, w_r, ws_t, gq, gk, gm]
    out_cols = [(SSD_INNER, F32), (CONV_CH, F32), (FOX_INNER, BF16), (FOX_INNER, F32), (FOX_INNER, BF16),
                (FOX_INNER, F32), (FOX_INNER, BF16), (MLP_INNER, F32), (MLP_INNER, F32), (SMALL_COLS, F32)]
    out_specs = [tok(c) for c, _ in out_cols]
    out_shape = [jax.ShapeDtypeStruct((b, t, c), dt) for c, dt in out_cols]
    out_specs.append(pl.BlockSpec((1, SMALL_ROWS, tm), lambda i, j: (i, 0, j)))
    out_shape.append(jax.ShapeDtypeStruct((b, SMALL_ROWS, t), F32))
    if with_peer:
        out_specs = [tok(D_MODEL)] + out_specs
        out_shape = [jax.ShapeDtypeStruct((b, t, D_MODEL), F32)] + out_shape
    res = pl.pallas_call(
        functools.partial(_in_kernel, with_peer),
        grid=(b, t // tm),
        in_specs=in_specs, out_specs=out_specs, out_shape=out_shape,
        compiler_params=_cparams(("arbitrary", "arbitrary")),
        name="in_proj",
    )(*args)
    if with_peer:
        return res[0], res[1:]
    return x, res


def _ssd_kernel(L, xbc_ref, z_ref, sm_ref, smt_ref, hist_ref, s0_ref, cw_ref, cb_ref, dtbc_ref, dtbr_ref,
                ac_ref, ar_ref, dsk_ref, ng_ref, e_ref, y_ref, sfin_ref, cfin_ref, xpad, st, ybuf):
    c = pl.program_id(1)
    nc = pl.num_programs(1)

    @pl.when(c == 0)
    def _():
        xpad[0:8, :] = hist_ref[0]
        st[...] = s0_ref[0]

    xpad[8:8 + L, :] = xbc_ref[0]
    conv = cb_ref[...]
    for tap in range(CONV_W):
        conv = conv + xpad[5 + tap:5 + tap + L, :] * cw_ref[tap:tap + 1, :]
    tail = xpad[L:L + 8, :]
    xpad[0:8, :] = tail
    xc = _silu(conv)
    xs = xc[:, 0:SSD_INNER]

    r_i = _iota((L, L), 0)
    c_i = _iota((L, L), 1)
    causal = r_i >= c_i
    tri = jnp.where(causal, 1.0, 0.0).astype(BF16)
    triu = jnp.where(r_i <= c_i, 1.0, 0.0).astype(BF16)

    dtc = _softplus(sm_ref[0] + dtbc_ref[...])
    acum_c = _dot3_r(tri, dtc * ac_ref[...])
    e = e_ref[...]
    acum_x = _dot3_l(acum_c, e)
    dt_x = _dot3_l(dtc, e)
    dtr = _softplus(smt_ref[0][DT_ROW0:DT_ROW0 + 8, :] + dtbr_ref[...])
    acum_r = _dot3_l(dtr * ar_ref[...], triu)

    bmat = [xc[:, SSD_INNER + SSD_STATE * g:SSD_INNER + SSD_STATE * (g + 1)].astype(BF16) for g in range(SSD_GROUPS)]
    c0 = SSD_INNER + SSD_GROUPS * SSD_STATE
    cmat = [xc[:, c0 + SSD_STATE * g:c0 + SSD_STATE * (g + 1)].astype(BF16) for g in range(SSD_GROUPS)]
    cb = [_dot_nt(cmat[g], bmat[g]) for g in range(SSD_GROUPS)]
    xsb = xs.astype(BF16)
    hpg = SSD_HEADS // SSD_GROUPS
    for h in range(SSD_HEADS):
        g = h // hpg
        seg = acum_c[:, h:h + 1] - acum_r[h:h + 1, :]
        dec = jnp.where(causal, jnp.exp(jnp.minimum(seg, 0.0)), 0.0)
        m = cb[g] * dec * dtr[h:h + 1, :]
        ybuf[:, SSD_HEAD_DIM * h:SSD_HEAD_DIM * (h + 1)] = _dot(
            m.astype(BF16), xsb[:, SSD_HEAD_DIM * h:SSD_HEAD_DIM * (h + 1)])

    gw = SSD_INNER // SSD_GROUPS
    stb = st[...].astype(BF16)
    y_off = jnp.concatenate([_dot(cmat[g], stb[:, gw * g:gw * (g + 1)]) for g in range(SSD_GROUPS)], axis=1)
    y = ybuf[...] + y_off * jnp.exp(acum_x) + dsk_ref[...] * xs
    a_end = acum_x[L - 1:L, :]
    xw = (xs * dt_x * jnp.exp(a_end - acum_x)).astype(BF16)
    new_states = jnp.concatenate([_dot_tn(bmat[g], xw[:, gw * g:gw * (g + 1)]) for g in range(SSD_GROUPS)], axis=1)
    st[...] = st[...] * jnp.exp(a_end) + new_states

    yg = y * _silu(z_ref[0])
    ms = jnp.mean(yg * yg, axis=-1, keepdims=True)
    y_ref[0] = yg * lax.rsqrt(ms + EPS) * ng_ref[...]

    @pl.when(c == nc - 1)
    def _():
        sfin_ref[0] = st[...]
        cfin_ref[0] = tail


def _ssd(xbc, z, sm, smt, hist8, s0t, cw, cb, dtb_c, dtb_r, a_c, a_r, dsk_x, ng, e_mat, L):
    b, t, _ = xbc.shape
    tok = lambda c: pl.BlockSpec((1, L, c), lambda i, j: (i, j, 0))
    full = lambda a: pl.BlockSpec(a.shape, lambda i, j: (0,) * a.ndim)
    per_b = lambda a: pl.BlockSpec((1,) + a.shape[1:], lambda i, j: (i,) + (0,) * (a.ndim - 1))
    return pl.pallas_call(
        functools.partial(_ssd_kernel, L),
        grid=(b, t // L),
        in_specs=[tok(CONV_CH), tok(SSD_INNER), tok(SMALL_COLS),
                  pl.BlockSpec((1, SMALL_ROWS, L), lambda i, j: (i, 0, j)),
                  per_b(hist8), per_b(s0t), full(cw), full(cb), full(dtb_c), full(dtb_r), full(a_c), full(a_r),
                  full(dsk_x), full(ng), full(e_mat)],
        out_specs=[tok(SSD_INNER),
                   pl.BlockSpec((1, SSD_STATE, SSD_INNER), lambda i, j: (i, 0, 0)),
                   pl.BlockSpec((1, 8, CONV_CH), lambda i, j: (i, 0, 0))],
        out_shape=[jax.ShapeDtypeStruct((b, t, SSD_INNER), F32),
                   jax.ShapeDtypeStruct((b, SSD_STATE, SSD_INNER), F32),
                   jax.ShapeDtypeStruct((b, 8, CONV_CH), F32)],
        scratch_shapes=[pltpu.VMEM((L + 8, CONV_CH), F32), pltpu.VMEM((SSD_STATE, SSD_INNER), F32),
                        pltpu.VMEM((L, SSD_INNER), F32)],
        compiler_params=_cparams(("arbitrary", "arbitrary")),
        name="ssd_scan",
    )(xbc, z, sm, smt, hist8, s0t, cw, cb, dtb_c, dtb_r, a_c, a_r, dsk_x, ng, e_mat)


def _fcum_kernel(tf, activate, colsrc_ref, rowsrc_ref, fbc_ref, fbr_ref, initc_ref, initr_ref,
                 lfc_ref, fc_ref, fr_ref, endc_ref, endr_ref, carc, carr):
    j = pl.program_id(1)

    @pl.when(j == 0)
    def _():
        carc[...] = initc_ref[0]
        carr[...] = initr_ref[0]

    ts = min(tf, FCUM_SUB)
    r_i = _iota((ts, ts), 0)
    c_i = _iota((ts, ts), 1)
    tri = jnp.where(r_i >= c_i, 1.0, 0.0).astype(BF16)
    triu = jnp.where(r_i <= c_i, 1.0, 0.0).astype(BF16)
    car_c = carc[0:1, :]
    car_r = carr[:, 0:1]
    for k in range(tf // ts):
        xc = colsrc_ref[0, k * ts:(k + 1) * ts, :]
        xr = rowsrc_ref[0, :, k * ts:(k + 1) * ts]
        if activate:
            xc = -_softplus(-(xc + fbc_ref[...]))
            xr = -_softplus(-(xr + fbr_ref[...]))
        fcol = car_c + _dot3_r(tri, xc)
        frow = car_r + _dot3_l(xr, triu)
        lfc_ref[0, k * ts:(k + 1) * ts, :] = xc
        fc_ref[0, k * ts:(k + 1) * ts, :] = fcol
        fr_ref[0, :, k * ts:(k + 1) * ts] = frow
        car_c = fcol[ts - 1:ts, :]
        car_r = frow[:, ts - 1:ts]
    carc[...] = jnp.broadcast_to(car_c, carc.shape)
    carr[...] = jnp.broadcast_to(car_r, carr.shape)
    endc_ref[0] = carc[...]
    endr_ref[0] = carr[...]


def _fcum(colsrc, rowsrc, row_block, fb_c, fb_r, init_c, init_r, activate):
    b, t, _ = colsrc.shape
    tf = min(t, FCUM_TILE)
    full = lambda a: pl.BlockSpec(a.shape, lambda i, j: (0,) * a.ndim)
    per_b = lambda a: pl.BlockSpec((1,) + a.shape[1:], lambda i, j: (i,) + (0,) * (a.ndim - 1))
    return pl.pallas_call(
        functools.partial(_fcum_kernel, tf, activate),
        grid=(b, t // tf),
        in_specs=[pl.BlockSpec((1, tf, LANES), lambda i, j: (i, j, 0)),
                  pl.BlockSpec((1, 8, tf), lambda i, j: (i, row_block, j)),
                  full(fb_c), full(fb_r), per_b(init_c), per_b(init_r)],
        out_specs=[pl.BlockSpec((1, tf, LANES), lambda i, j: (i, j, 0)),
                   pl.BlockSpec((1, tf, LANES), lambda i, j: (i, j, 0)),
                   pl.BlockSpec((1, 8, tf), lambda i, j: (i, 0, j)),
                   pl.BlockSpec((1, 8, LANES), lambda i, j: (i, 0, 0)),
                   pl.BlockSpec((1, 8, LANES), lambda i, j: (i, 0, 0))],
        out_shape=[jax.ShapeDtypeStruct((b, t, LANES), F32), jax.ShapeDtypeStruct((b, t, LANES), F32),
                   jax.ShapeDtypeStruct((b, 8, t), F32), jax.ShapeDtypeStruct((b, 8, LANES), F32),
                   jax.ShapeDtypeStruct((b, 8, LANES), F32)],
        scratch_shapes=[pltpu.VMEM((8, LANES), F32), pltpu.VMEM((8, LANES), F32)],
        compiler_params=_cparams(("arbitrary", "arbitrary")),
        name="forget_cumsum",
    )(colsrc, rowsrc, fb_c, fb_r, init_c, init_r)


def _softmax_step(s, vb, carry):
    m, l, acc = carry
    tq, tk = s.shape
    m_new = jnp.maximum(m, jnp.max(s, axis=-1, keepdims=True))
    p = jnp.exp(s - m_new)
    alpha = jnp.exp(m - m_new)
    if tk % LANES == 0:
        psum = p[:, 0:LANES]
        for c in range(1, tk // LANES):
            psum = psum + p[:, c * LANES:(c + 1) * LANES]
    else:
        psum = jnp.where(_iota((tq, LANES), 1) == 0, jnp.sum(p, axis=-1, keepdims=True), 0.0)
    l = alpha * l + psum
    acc = alpha * acc + _dot(p.astype(BF16), vb)
    return m_new, l, acc


def _softmax_init(tq):
    return tuple((jnp.full((tq, 1), NEG, F32), jnp.zeros((tq, LANES), F32), jnp.zeros((tq, FOX_HEAD_DIM), F32))
                 for _ in range(FOX_HEADS))


def _head_slice(h):
    return slice(FOX_HEAD_DIM * h, FOX_HEAD_DIM * (h + 1))


def _attend_block(qs, fqs, kb, vb, fk, mask, carries):
    out = []
    for h in range(FOX_HEADS):
        hs = _head_slice(h)
        s = _dot_nt(qs[h], kb[:, hs]) + fqs[h] - fk[h:h + 1, :]
        if mask is not None:
            s = jnp.where(mask, s, NEG)
        out.append(_softmax_step(s, vb[:, hs], carries[h]))
    return tuple(out)


def _attend_finish(carries, o_ref):
    for h in range(FOX_HEADS):
        _, l, acc = carries[h]
        o_ref[0, :, _head_slice(h)] = acc / jnp.sum(l, axis=-1, keepdims=True)


def _fox_prompt_kernel(tq, nk, q_ref, fq_ref, k_ref, v_ref, fk_ref, fend_ref, ffirst_ref, thr_ref, o_ref):
    b = pl.program_id(0)
    qi = pl.program_id(1)
    half = tq // 2
    k_diag = k_ref[0, pl.ds(pl.multiple_of(qi * tq, tq), tq), :]
    v_diag = v_ref[0, pl.ds(pl.multiple_of(qi * tq, tq), tq), :]
    fk_diag = fk_ref[0, qi]
    init = (jnp.full((half, 1), NEG, F32), jnp.zeros((half, LANES), F32), jnp.zeros((half, FOX_HEAD_DIM), F32))
    for h in range(FOX_HEADS):
        hs = _head_slice(h)
        base = (b * FOX_HEADS + h) * nk
        slack = thr_ref[0] + ffirst_ref[base + qi]
        n_live = lax.fori_loop(0, qi, lambda j, c: c + (slack - fend_ref[base + j] >= 0.0).astype(jnp.int32), 0)
        chains = []
        carries = []
        for r2 in range(2):
            rs = slice(r2 * half, (r2 + 1) * half)
            qh = q_ref[0, rs, hs]
            fq = fq_ref[0, rs, F_ROW0 + h:F_ROW0 + h + 1]
            chains.append((qh, fq))
            s = _dot_nt(qh, k_diag[:, hs]) + fq - fk_diag[h:h + 1, :]
            causal = (r2 * half + _iota((half, tq), 0)) >= _iota((half, tq), 1)
            carries.append(_softmax_step(jnp.where(causal, s, NEG), v_diag[:, hs], init))

        def body(t, carries, chains=chains, hs=hs, h=h):
            j = qi - 1 - t
            start = pl.multiple_of(j * tq, tq)
            kb = k_ref[0, pl.ds(start, tq), hs]
            vb = v_ref[0, pl.ds(start, tq), hs]
            fk = fk_ref[0, j][h:h + 1, :]
            return tuple(_softmax_step(_dot_nt(qh, kb) + fq - fk, vb, c) for (qh, fq), c in zip(chains, carries))

        carries = lax.fori_loop(0, n_live, body, tuple(carries))
        for r2 in range(2):
            _, l, acc = carries[r2]
            o_ref[0, r2 * half:(r2 + 1) * half, hs] = acc / jnp.sum(l, axis=-1, keepdims=True)


def _fox_prompt(qb, fcol, kb, vb, frow, score_bound):
    b, t, _ = qb.shape
    tq = min(t, FOX_PROMPT_TILE)
    nk = t // tq
    fk = frow.reshape(b, 8, nk, tq).transpose(0, 2, 1, 3)
    f_heads = frow[:, :FOX_HEADS, :].reshape(b, FOX_HEADS, nk, tq)
    f_end = f_heads[:, :, :, tq - 1].reshape(-1)
    f_first = f_heads[:, :, :, 0].reshape(-1)
    thr = (2.0 * score_bound + EXP_UNDERFLOW).reshape(1).astype(F32)
    smem = pl.BlockSpec(memory_space=pltpu.SMEM)
    return pl.pallas_call(
        functools.partial(_fox_prompt_kernel, tq, nk),
        grid=(b, t // tq),
        in_specs=[pl.BlockSpec((1, tq, FOX_INNER), lambda i, j: (i, j, 0)),
                  pl.BlockSpec((1, tq, LANES), lambda i, j: (i, j, 0)),
                  pl.BlockSpec((1, t, FOX_INNER), lambda i, j: (i, 0, 0)),
                  pl.BlockSpec((1, t, FOX_INNER), lambda i, j: (i, 0, 0)),
                  pl.BlockSpec((1, nk, 8, tq), lambda i, j: (i, 0, 0, 0)),
                  smem, smem, smem],
        out_specs=pl.BlockSpec((1, tq, FOX_INNER), lambda i, j: (i, j, 0)),
        out_shape=jax.ShapeDtypeStruct((b, t, FOX_INNER), F32),
        compiler_params=_cparams(("arbitrary", "arbitrary")),
        name="fox_prompt",
    )(qb, fcol, kb, vb, fk, f_end, f_first, thr)


def _bf16_terms(x):
    hi, mid, lo = _split3(x)
    return hi.astype(F32), mid.astype(F32), lo.astype(F32)


def _fox_prep_kernel(q_ref, k_ref, v_ref, fc_ref, fr_ref, ka_ref, qa_ref, vt_ref):
    tm = q_ref.shape[1]
    qt = q_ref[0].astype(F32).T
    k = k_ref[0].astype(F32)
    vt_ref[0, 0] = v_ref[0].astype(F32).T.astype(BF16)
    lane = _iota((tm, LANES), 1)
    row = _iota((FOX_HEAD_DIM, tm), 0)
    d = FOX_HEAD_DIM
    for h in range(FOX_HEADS):
        pair, odd = divmod(h, 2)
        kc = k[:, pair * LANES:(pair + 1) * LANES]
        if odd:
            kc = pltpu.roll(kc, d, axis=1)
        ka = jnp.where(lane < d, kc, 0.0)
        for i, term in enumerate(_bf16_terms(-fc_ref[0][:, F_ROW0 + h:F_ROW0 + h + 1])):
            ka = jnp.where(lane == d + i, term, ka)
        ka = jnp.where(jnp.logical_and(lane >= d + 3, lane < d + 6), 1.0, ka)
        ka_ref[0, h] = ka.astype(BF16)
        tail = jnp.where(row < 3, 1.0, 0.0)
        for i, term in enumerate(_bf16_terms(fr_ref[0][h:h + 1, :])):
            tail = jnp.where(row == 3 + i, term, tail)
        qa_ref[0, h, 0:d, :] = qt[d * h:d * (h + 1), :].astype(BF16)
        qa_ref[0, h, d:2 * d, :] = tail.astype(BF16)


def _fox_prep(qb, kb, vb, fcol, frow, tm):
    b, t, _ = qb.shape
    tok = lambda c: pl.BlockSpec((1, tm, c), lambda i, j: (i, j, 0))
    return pl.pallas_call(
        _fox_prep_kernel,
        grid=(b, t // tm),
        in_specs=[tok(FOX_INNER), tok(FOX_INNER), tok(FOX_INNER), tok(LANES),
                  pl.BlockSpec((1, 8, tm), lambda i, j: (i, 0, j))],
        out_specs=[pl.BlockSpec((1, FOX_HEADS, tm, LANES), lambda i, j: (i, 0, j, 0)),
                   pl.BlockSpec((1, FOX_HEADS, LANES, tm), lambda i, j: (i, 0, 0, j)),
                   pl.BlockSpec((1, 1, FOX_INNER, tm), lambda i, j: (i, j, 0, 0))],
        out_shape=[jax.ShapeDtypeStruct((b, FOX_HEADS, t, LANES), BF16),
                   jax.ShapeDtypeStruct((b, FOX_HEADS, LANES, t), BF16),
                   jax.ShapeDtypeStruct((b, t // tm, FOX_INNER, tm), BF16)],
        compiler_params=_cparams(("arbitrary", "arbitrary")),
        name="fox_prep",
    )(qb, kb, vb, fcol, frow)


def _fox_prompt_t_kernel(tq, nk, qa_ref, ka_ref, vt_ref, fend_ref, ffirst_ref, thr_ref, o_ref):
    b = pl.program_id(0)
    qi = pl.program_id(1)
    half = tq // FOX_CHAINS
    d = FOX_HEAD_DIM
    diag = pl.multiple_of(qi * tq, tq)
    init = (jnp.full((1, half), NEG, F32), jnp.zeros((1, half), F32), jnp.zeros((d, half), F32))

    def update(blocks, qas, masks, carries):
        scores = [[_dot(ka, qa) for qa in qas] for ka, _ in blocks]
        for (_, vt), block_scores in zip(blocks, scores):
            stats = []
            for s, mask, (m, l, acc) in zip(block_scores, masks, carries):
                if mask is not None:
                    s = jnp.where(mask, s, NEG)
                m_new = jnp.maximum(m, jnp.max(s, axis=0, keepdims=True))
                p = jnp.exp(s - m_new)
                alpha = jnp.exp(m - m_new)
                stats.append((m_new, alpha * l + jnp.sum(p, axis=0, keepdims=True), alpha * acc, p.astype(BF16)))
            carries = tuple((m, l, acc + _dot(vt, p)) for m, l, acc, p in stats)
        return carries

    def key_block(h, j):
        return (ka_ref[0, h, pl.ds(pl.multiple_of(j * tq, tq), tq), :], vt_ref[0, j][d * h:d * (h + 1), :])

    for h in range(FOX_HEADS):
        base = (b * FOX_HEADS + h) * nk
        slack = thr_ref[0] + ffirst_ref[base + qi]
        n_live = lax.fori_loop(0, qi, lambda j, c: c + (slack - fend_ref[base + j] >= 0.0).astype(jnp.int32), 0)
        qas = [qa_ref[0, h, :, r2 * half:(r2 + 1) * half] for r2 in range(FOX_CHAINS)]
        ka_d = ka_ref[0, h, pl.ds(diag, tq), :]
        vt_d = vt_ref[0, qi][d * h:d * (h + 1), :]
        causal = [_iota((tq, half), 0) <= (r2 * half + _iota((tq, half), 1)) for r2 in range(FOX_CHAINS)]
        carries = update([(ka_d, vt_d)], qas, causal, [init] * FOX_CHAINS)
        no_mask = [None] * FOX_CHAINS

        def one(t, carries, qas=qas, h=h):
            return update([key_block(h, qi - 1 - t)], qas, no_mask, carries)

        def two(t, carries, qas=qas, h=h):
            j = qi - 1 - (n_live % 2) - 2 * t
            return update([key_block(h, j), key_block(h, j - 1)], qas, no_mask, carries)

        carries = lax.fori_loop(0, n_live % 2, one, carries)
        carries = lax.fori_loop(0, n_live // 2, two, carries)
        for r2 in range(FOX_CHAINS):
            _, l, acc = carries[r2]
            o_ref[0, r2 * half:(r2 + 1) * half, d * h:d * (h + 1)] = (acc / l).T


def _fox_prompt_t(qb, kb, vb, fcol, frow, score_bound):
    b, t, _ = qb.shape
    tq = min(t, FOX_PROMPT_TILE)
    nk = t // tq
    ka, qa, vt = _fox_prep(qb, kb, vb, fcol, frow, tq)
    f_heads = frow[:, :FOX_HEADS, :].reshape(b, FOX_HEADS, nk, tq)
    f_end = f_heads[:, :, :, tq - 1].reshape(-1)
    f_first = f_heads[:, :, :, 0].reshape(-1)
    thr = (2.0 * score_bound + EXP_UNDERFLOW).reshape(1).astype(F32)
    smem = pl.BlockSpec(memory_space=pltpu.SMEM)
    return pl.pallas_call(
        functools.partial(_fox_prompt_t_kernel, tq, nk),
        grid=(b, nk),
        in_specs=[pl.BlockSpec((1, FOX_HEADS, LANES, tq), lambda i, j: (i, 0, 0, j)),
                  pl.BlockSpec((1, FOX_HEADS, t, LANES), lambda i, j: (i, 0, 0, 0)),
                  pl.BlockSpec((1, nk, FOX_INNER, tq), lambda i, j: (i, 0, 0, 0)),
                  smem, smem, smem],
        out_specs=pl.BlockSpec((1, tq, FOX_INNER), lambda i, j: (i, j, 0)),
        out_shape=jax.ShapeDtypeStruct((b, t, FOX_INNER), F32),
        compiler_params=_cparams(("arbitrary", "arbitrary")),
        name="fox_prompt",
    )(qa, ka, vt, f_end, f_first, thr)


def _fox_sample_kernel(tq, tk, npast, q_ref, fq_ref, pk_ref, pv_ref, fpk_ref, k_ref, v_ref, fk_ref, o_ref):
    q_all = q_ref[0]
    fq_all = fq_ref[0]
    qs = [q_all[:, _head_slice(h)] for h in range(FOX_HEADS)]
    fqs = [fq_all[:, F_ROW0 + h:F_ROW0 + h + 1] for h in range(FOX_HEADS)]
    causal = _iota((tq, tq), 0) >= _iota((tq, tq), 1)

    def past_block(j, carries):
        start = pl.multiple_of(j * tk, tk)
        return _attend_block(qs, fqs, pk_ref[0, pl.ds(start, tk), :].astype(BF16),
                             pv_ref[0, pl.ds(start, tk), :].astype(BF16), fpk_ref[0, j], None, carries)

    carries = lax.fori_loop(0, npast, past_block, _softmax_init(tq))
    _attend_finish(_attend_block(qs, fqs, k_ref[0], v_ref[0], fk_ref[0], causal, carries), o_ref)


def _fox_sample(qb, fcol, past_k, past_v, layer, fpast_row, kb, vb, frow):
    b, t, _ = qb.shape
    p = past_k.shape[2]
    tk = min(p, 512)
    npast = p // tk
    fpk = fpast_row.reshape(b, 8, npast, tk).transpose(0, 2, 1, 3)
    bspec = lambda a: pl.BlockSpec((1,) + a.shape[1:], lambda i: (i,) + (0,) * (a.ndim - 1))
    cache = pl.BlockSpec((None, 1, p, FOX_INNER), lambda i: (layer, i, 0, 0))
    args = (qb, fcol, past_k, past_v, fpk, kb, vb, frow)
    return pl.pallas_call(
        functools.partial(_fox_sample_kernel, t, tk, npast),
        grid=(b,),
        in_specs=[bspec(qb), bspec(fcol), cache, cache, bspec(fpk), bspec(kb), bspec(vb), bspec(frow)],
        out_specs=pl.BlockSpec((1, t, FOX_INNER), lambda i: (i, 0, 0)),
        out_shape=jax.ShapeDtypeStruct((b, t, FOX_INNER), F32),
        compiler_params=_cparams(("arbitrary",)),
        name="fox_sample",
    )(*args)


def _mlp_kernel(lm, nchunk, ug_ref, vn_ref, ws_ref, bst_ref, y_ref):
    r_i = _iota((lm, lm), 0)
    c_i = _iota((lm, lm), 1)
    tril = r_i >= c_i
    vn = vn_ref[0].astype(BF16)
    ug = ug_ref[0]
    for g in range(MLP_GROUPS):
        lo, hi = MLP_GROUP_DIM * g, MLP_GROUP_DIM * (g + 1)
        w = jnp.where(tril, ws_ref[g], 0.0).astype(BF16)
        bias = bst_ref[:, g:g + 1]
        for c in range(nchunk):
            r0, r1 = c * lm, (c + 1) * lm
            sv = _dot(w, vn[r0:r1, lo:hi]) + bias
            y_ref[0, r0:r1, lo:hi] = ug[r0:r1, lo:hi] * sv


def _chunk_mlp(ug, vn, ws, bst):
    b, t, _ = ug.shape
    lm = ws.shape[1]
    tm = min(t, 4 * lm)
    tok = pl.BlockSpec((1, tm, MLP_INNER), lambda i, j: (i, j, 0))
    return pl.pallas_call(
        functools.partial(_mlp_kernel, lm, tm // lm),
        grid=(b, t // tm),
        in_specs=[tok, tok, pl.BlockSpec(ws.shape, lambda i, j: (0, 0, 0)),
                  pl.BlockSpec(bst.shape, lambda i, j: (0, 0))],
        out_specs=tok,
        out_shape=jax.ShapeDtypeStruct((b, t, MLP_INNER), F32),
        compiler_params=_cparams(("arbitrary", "arbitrary")),
        name="chunk_mlp",
    )(ug, vn, ws, bst)


def _out_kernel(x_ref, ys_ref, yf_ref, ym_ref, g1_ref, sh_ref, sc_ref, g_ref, wo_ref, x1_ref, h2_ref):
    mix = (_dot(ys_ref[0].astype(BF16), wo_ref[0:SSD_INNER, :])
           + _dot(yf_ref[0].astype(BF16), wo_ref[SSD_INNER:SSD_INNER + FOX_INNER, :])
           + _dot(ym_ref[0].astype(BF16), wo_ref[SSD_INNER + FOX_INNER:D_MODEL, :]))
    x1 = x_ref[0] + g1_ref[0] * mix
    x1_ref[0] = x1
    ms = jnp.mean(x1 * x1, axis=-1, keepdims=True)
    h = x1 * lax.rsqrt(ms + EPS) * g_ref[...]
    h2_ref[0] = (h * (1.0 + sc_ref[0]) + sh_ref[0]).astype(BF16)


def _out_proj(x, ys, yf, ym, g1, sh, sc, g, wo):
    b, t, _ = x.shape
    tm = min(t, 512)
    tok = lambda c: pl.BlockSpec((1, tm, c), lambda i, j: (i, j, 0))
    per_b = pl.BlockSpec((1, 1, D_MODEL), lambda i, j: (i, 0, 0))
    full = lambda a: pl.BlockSpec(a.shape, lambda i, j: (0,) * a.ndim)
    return pl.pallas_call(
        _out_kernel,
        grid=(b, t // tm),
        in_specs=[tok(D_MODEL), tok(SSD_INNER), tok(FOX_INNER), tok(MLP_INNER), per_b, per_b, per_b, full(g), full(wo)],
        out_specs=[tok(D_MODEL), tok(D_MODEL)],
        out_shape=[jax.ShapeDtypeStruct((b, t, D_MODEL), F32), jax.ShapeDtypeStruct((b, t, D_MODEL), BF16)],
        compiler_params=_cparams(("arbitrary", "arbitrary")),
        name="out_proj",
    )(x, ys, yf, ym, g1, sh, sc, g, wo)


def _top16(scores):
    nk, tn = scores[0].shape
    ridx = _iota((nk, tn), 0).astype(F32)
    r16 = _iota((PEER_TOPK, tn), 0)
    state = [(s, jnp.full((nk, tn), float(PEER_TOPK), F32), jnp.zeros((PEER_TOPK, tn), F32)) for s in scores]
    for it in range(PEER_TOPK):
        nxt_state = []
        for s, pos, tv in state:
            level = [(s[g:g + 8, :], ridx[g:g + 8, :]) for g in range(0, nk, 8)]
            while len(level) > 1:
                nxt = []
                for a in range(0, len(level), 2):
                    (va, ia), (vb, ib) = level[a], level[a + 1]
                    nxt.append((jnp.maximum(va, vb), jnp.where(va >= vb, ia, ib)))
                level = nxt
            v8, i8 = level[0]
            m = jnp.max(v8, axis=0, keepdims=True)
            first = jnp.min(jnp.where(v8 == m, i8, float(nk)), axis=0, keepdims=True)
            sel = ridx == first
            nxt_state.append((jnp.where(sel, -jnp.inf, s), jnp.where(sel, float(it), pos),
                              jnp.where(r16 == it, m, tv)))
        state = nxt_state
    return [(pos, tv) for _, pos, tv in state]


_CAND_GROUPS = [(0, 16)] + [(ka, 8) for ka in range(1, 8)]
_CAND_ROWS = 16 + 7 * 8 + 8


def _pair_select(ta, tb):
    tn = ta.shape[1]
    pieces, flats, valids = [], [], []
    for ka, rows in _CAND_GROUPS:
        pieces.append(ta[ka:ka + 1, :] + tb[0:rows, :])
        kb = _iota((rows, 1), 0)
        flats.append((ka * PEER_TOPK + kb).astype(F32))
        valids.append((ka + 1) * (kb + 1) <= PEER_TOPK)
    pieces.append(ta[8:16, :] + tb[0:1, :])
    flats.append(((8 + _iota((8, 1), 0)) * PEER_TOPK).astype(F32))
    valids.append(_iota((8, 1), 0) >= 0)
    cand0 = jnp.concatenate(pieces, axis=0)
    flat = jnp.concatenate(flats, axis=0)
    valid = jnp.concatenate(valids, axis=0)
    cand0 = jnp.where(valid, cand0, -jnp.inf)
    best = ta[0:1, :] + tb[0:1, :]

    cand = cand0
    selm = jnp.zeros((_CAND_ROWS, tn), F32)
    for _ in range(PEER_TOPK):
        m = jnp.max(cand, axis=0, keepdims=True)
        first = jnp.min(jnp.where(cand == m, flat, 4096.0), axis=0, keepdims=True)
        sel = flat == first
        cand = jnp.where(sel, -jnp.inf, cand)
        selm = jnp.where(sel, 1.0, selm)
    z = jnp.sum(jnp.where(selm > 0.0, jnp.exp(cand0 - best), 0.0), axis=0, keepdims=True)
    cnts = [jnp.sum(selm[0:16, :], axis=0, keepdims=True)]
    for i in range(1, 8):
        cnts.append(jnp.sum(selm[8 + 8 * i:16 + 8 * i, :], axis=0, keepdims=True))
    cnts.append(selm[_CAND_ROWS - 8:_CAND_ROWS, :])
    return jnp.concatenate(cnts, axis=0), z


def _peer_sel_kernel(tn, h_ref, wqt_ref, keys_ref, ea_ref, la_ref, eb_ref, pb_ref, qt_ref):
    qt_ref[...] = _dot_nt(wqt_ref[...], h_ref[...]).astype(BF16)

    def head(h, carry):
        row = pl.multiple_of(h * (2 * PEER_HALF), 2 * PEER_HALF)
        sa_all = _dot(keys_ref[2 * h], qt_ref[pl.ds(row, PEER_HALF), :])
        sb_all = _dot(keys_ref[2 * h + 1], qt_ref[pl.ds(row + PEER_HALF, PEER_HALF), :])
        for c in range(tn // LANES):
            cs = slice(c * LANES, (c + 1) * LANES)
            sa, sb = sa_all[:, cs], sb_all[:, cs]
            (pos_a, ta), (pos_b, tb) = _top16([sa, sb])
            cnt, z = _pair_select(ta, tb)
            la = jnp.zeros_like(pos_a)
            for ka in range(PEER_TOPK):
                la = jnp.where(pos_a == float(ka), cnt[ka:ka + 1, :], la)
            ea_ref[h, :, cs] = jnp.where(pos_a < float(PEER_TOPK), jnp.exp(sa - ta[0:1, :]), 0.0) / z
            la_ref[h, :, cs] = la
            eb = jnp.where(pos_b < float(PEER_TOPK), jnp.exp(sb - tb[0:1, :]), 0.0)
            for sub in range(PEER_KEYS // PEER_SUB):
                src = slice(sub * PEER_SUB, (sub + 1) * PEER_SUB)
                dst = slice(sub * PEER_SUB // 2, (sub + 1) * PEER_SUB // 2)
                eb_ref[h, dst, cs] = pltpu.bitcast(eb[src].astype(BF16), jnp.uint32)
                pb_ref[h, dst, cs] = pltpu.bitcast(pos_b[src].astype(BF16), jnp.uint32)
        return carry

    lax.fori_loop(0, PEER_HEADS, head, 0)


def _peer_select(h2, wqt_all, layer, keys):
    n = h2.shape[0]
    tn = 256
    per_tok = pl.BlockSpec((PEER_HEADS, PEER_KEYS, tn), lambda i: (0, 0, i))
    per_tok_pk = pl.BlockSpec((PEER_HEADS, PEER_KEYS // 2, tn), lambda i: (0, 0, i))
    shp = lambda dt: jax.ShapeDtypeStruct((PEER_HEADS, PEER_KEYS, n), dt)
    shp_pk = jax.ShapeDtypeStruct((PEER_HEADS, PEER_KEYS // 2, n), jnp.uint32)
    return pl.pallas_call(
        functools.partial(_peer_sel_kernel, tn),
        grid=(n // tn,),
        in_specs=[pl.BlockSpec((tn, D_MODEL), lambda i: (i, 0)),
                  pl.BlockSpec((None,) + wqt_all.shape[1:], lambda i: (layer, 0, 0)),
                  pl.BlockSpec(keys.shape, lambda i: (0, 0, 0))],
        out_specs=[per_tok, per_tok, per_tok_pk, per_tok_pk],
        out_shape=[shp(F32), shp(F32), shp_pk, shp_pk],
        scratch_shapes=[pltpu.VMEM((PEER_HEADS * 2 * PEER_HALF, tn), BF16)],
        compiler_params=_cparams(("arbitrary",)),
        name="peer_select",
    )(h2, wqt_all, keys)


def _peer_gate_stage(ec, ia0, rows, chunks, at_r, ga_w, ea_ref, la_ref, eb_ref, pb_ref):
    rows_per = ec // PEER_KEYS
    for r in rows:
        for c in chunks:
            cs = slice(c * LANES, (c + 1) * LANES)
            for sub in range(PEER_KEYS // PEER_SUB):
                pk = slice(sub * PEER_SUB // 2, (sub + 1) * PEER_SUB // 2)
                gate = None
                for h in range(PEER_HEADS):
                    la = la_ref[h, pl.ds(ia0, rows_per), cs][r:r + 1, :]
                    ea = ea_ref[h, pl.ds(ia0, rows_per), cs][r:r + 1, :]
                    la = jnp.broadcast_to(la, (PEER_SUB, LANES)).astype(BF16)
                    ea = jnp.broadcast_to(ea, (PEER_SUB, LANES)).astype(BF16)
                    pb = pltpu.bitcast(pb_ref[h, pk, cs], BF16)
                    eb = pltpu.bitcast(eb_ref[h, pk, cs], BF16)
                    term = jnp.where(pb < la, eb, 0.0) * ea
                    gate = term if gate is None else gate + term
                ex = slice(r * PEER_KEYS + sub * PEER_SUB, r * PEER_KEYS + (sub + 1) * PEER_SUB)
                ga_w[ex, cs] = gate * _gelu(at_r[ex, cs]).astype(BF16)


def _peer_dense_kernel(tn, ec, h_ref, u_ref, vt_ref, ea_ref, la_ref, eb_ref, pb_ref, o_ref, at, ga, acc):
    e = pl.program_id(1)
    ne = pl.num_programs(1)

    @pl.when(e == 0)
    def _():
        acc[...] = jnp.zeros_like(acc)

    at[...] = _dot_nt(u_ref[...], h_ref[...])
    rows_per = ec // PEER_KEYS
    ia0 = pl.multiple_of(e * rows_per, rows_per)
    _peer_gate_stage(ec, ia0, range(rows_per), range(tn // LANES), at, ga, ea_ref, la_ref, eb_ref, pb_ref)
    acc[...] += _dot(vt_ref[...], ga[...])

    @pl.when(e == ne - 1)
    def _():
        o_ref[...] = acc[...].T


def _peer_dense(h2, u_all, vt_all, layer, ea, la, eb, pb):
    n = h2.shape[0]
    tn, ec = PEER_TOKEN_TILE, PEER_EXPERT_CHUNK
    per_tok = pl.BlockSpec((PEER_HEADS, PEER_KEYS, tn), lambda i, e: (0, 0, i))
    per_tok_pk = pl.BlockSpec((PEER_HEADS, PEER_KEYS // 2, tn), lambda i, e: (0, 0, i))
    return pl.pallas_call(
        functools.partial(_peer_dense_kernel, tn, ec),
        grid=(n // tn, PEER_EXPERTS // ec),
        in_specs=[pl.BlockSpec((tn, D_MODEL), lambda i, e: (i, 0)),
                  pl.BlockSpec((None, ec, D_MODEL), lambda i, e: (layer, e, 0)),
                  pl.BlockSpec((None, D_MODEL, ec), lambda i, e: (layer, 0, e)),
                  per_tok, per_tok, per_tok_pk, per_tok_pk],
        out_specs=pl.BlockSpec((tn, D_MODEL), lambda i, e: (i, 0)),
        out_shape=jax.ShapeDtypeStruct((n, D_MODEL), F32),
        scratch_shapes=[pltpu.VMEM((ec, tn), F32), pltpu.VMEM((ec, tn), BF16), pltpu.VMEM((D_MODEL, tn), F32)],
        compiler_params=_cparams(("arbitrary", "arbitrary")),
        name="peer_dense",
    )(h2, u_all, vt_all, ea, la, eb, pb)


def _resid_kernel(x_ref, p_ref, g_ref, o_ref):
    o_ref[0] = x_ref[0] + g_ref[0] * p_ref[0]


def _residual(x, peer, g2):
    b, t, _ = x.shape
    tm = min(t, 512)
    tok = pl.BlockSpec((1, tm, D_MODEL), lambda i, j: (i, j, 0))
    return pl.pallas_call(
        _resid_kernel,
        grid=(b, t // tm),
        in_specs=[tok, tok, pl.BlockSpec((1, 1, D_MODEL), lambda i, j: (i, 0, 0))],
        out_specs=tok,
        out_shape=jax.ShapeDtypeStruct((b, t, D_MODEL), F32),
        compiler_params=_cparams(("arbitrary", "arbitrary")),
        name="peer_residual",
    )(x, peer, g2)


def _lane_pad(vec, offset, width=LANES):
    out = jnp.zeros((width,), F32)
    return out.at[offset:offset + vec.shape[0]].set(vec.astype(F32))


def _layer_params(l, norm1_g, norm2_g, w_in, conv_w, conv_b, dt_bias, a_log, d_skip, ssd_norm_g, q_norm_g,
                  k_norm_g, fgate_b, w_s, b_s, w_out, peer_keys):
    w = w_in[l]
    o = [0]
    for sz in (SSD_INNER, CONV_CH, SSD_HEADS, FOX_INNER, FOX_INNER, FOX_INNER, FOX_HEADS, MLP_INNER, MLP_INNER):
        o.append(o[-1] + sz)
    wz, wxbc, wdt, wq, wk, wv, wf, wu, wvm = [w[:, o[i]:o[i + 1]] for i in range(9)]
    w_small = jnp.zeros((D_MODEL, SMALL_COLS), F32)
    w_small = w_small.at[:, DT_ROW0:DT_ROW0 + SSD_HEADS].set(wdt).at[:, F_ROW0:F_ROW0 + FOX_HEADS].set(wf)
    w_r = jnp.concatenate([wz, wxbc, wq, wk, wv, wu, wvm, w_small], axis=1).astype(BF16)
    ws_t = w_small[:, :SMALL_ROWS].T.astype(BF16)
    a_neg = -jnp.exp(a_log[l].astype(F32))
    grp = jnp.arange(FOX_INNER) // FOX_HEAD_DIM
    gm = jnp.where(grp[:, None] == grp[None, :], 1.0 / FOX_HEAD_DIM, 0.0).astype(BF16)
    heads = jnp.arange(SSD_INNER) // SSD_HEAD_DIM
    e_mat = (jnp.arange(LANES)[:, None] == heads[None, :]).astype(BF16)
    return dict(
        norm1_g=norm1_g[l][None], norm2_g=norm2_g[l][None], w_r=w_r, ws_t=ws_t,
        gq=(jnp.tile(q_norm_g[l], FOX_HEADS) * (FOX_HEAD_DIM ** -0.5))[None],
        gk=jnp.tile(k_norm_g[l], FOX_HEADS)[None], gm=gm,
        fox_bound=1.02 * FOX_HEAD_DIM ** 0.5 * jnp.max(jnp.abs(q_norm_g[l])) * jnp.max(jnp.abs(k_norm_g[l])),
        conv_w=conv_w[l], conv_b=conv_b[l][None],
        dtb_c=_lane_pad(dt_bias[l], DT_ROW0)[None], dtb_r=dt_bias[l].astype(F32)[:, None],
        a_c=_lane_pad(a_neg, DT_ROW0)[None], a_r=a_neg[:, None],
        dsk_x=jnp.repeat(d_skip[l].astype(F32), SSD_HEAD_DIM)[None], ssd_g=ssd_norm_g[l][None], e_mat=e_mat,
        fb_c=_lane_pad(fgate_b[l], F_ROW0)[None], fb_r=_lane_pad(fgate_b[l], 0, 8)[:, None],
        w_s=w_s[l], bs_t=jnp.zeros((MLP_CHUNK, LANES), F32).at[:, :MLP_GROUPS].set(b_s[l].T),
        w_out=w_out[l].astype(BF16),
        keys=peer_keys[l].reshape(PEER_HEADS * 2, PEER_KEYS, PEER_HALF).astype(BF16),
    )


def _stream_mixers(x, peer, g2_prev, mod, p, hist8, s0t, past):
    sh1, sc1, g1, sh2, sc2, _ = mod
    b, t, _ = x.shape
    x, (z, xbc, qb, kn, kb, v, vb, ug, vn, sm, smt) = _in_proj(
        x, peer, g2_prev, sh1, sc1, p['norm1_g'], p['w_r'], p['ws_t'], p['gq'], p['gk'], p['gm'])
    L = SSD_CHUNK_PROMPT if t % SSD_CHUNK_PROMPT == 0 else t
    y_ssd, s_fin, c_fin = _ssd(xbc, z, sm, smt, hist8, s0t, p['conv_w'], p['conv_b'], p['dtb_c'], p['dtb_r'],
                               p['a_c'], p['a_r'], p['dsk_x'], p['ssd_g'], p['e_mat'], L)
    zeros8 = jnp.zeros((b, 8, LANES), F32)
    if past is None:
        logf_c, f_col, f_row, _, _ = _fcum(sm, smt, F_ROW0 // 8, p['fb_c'], p['fb_r'], zeros8, zeros8, True)
        y_fox = _fox_prompt_t(qb, kb, vb, f_col, f_row, p['fox_bound'])
    else:
        pk, pv, layer, plf_col, plf_row = past
        _, _, fp_row, end_c, end_r = _fcum(plf_col, plf_row, 0, p['fb_c'], p['fb_r'], zeros8, zeros8, False)
        logf_c, f_col, f_row, _, _ = _fcum(sm, smt, F_ROW0 // 8, p['fb_c'], p['fb_r'], end_c, end_r, True)
        y_fox = _fox_sample(qb, f_col, pk, pv, layer, fp_row, kb, vb, f_row)
    lm = MLP_CHUNK if t % MLP_CHUNK == 0 else t
    y_mlp = _chunk_mlp(ug, vn, p['w_s'][:, :lm, :lm], p['bs_t'][:lm])
    x1, h2 = _out_proj(x, y_ssd, y_fox, y_mlp, g1, sh2, sc2, p['norm2_g'], p['w_out'])
    logf = logf_c[:, :, F_ROW0:F_ROW0 + FOX_HEADS]
    new_ssm = s_fin.reshape(b, SSD_STATE, SSD_HEADS, SSD_HEAD_DIM).transpose(0, 2, 3, 1)
    new_conv = c_fin[:, 8 - (CONV_W - 1):, :]
    kc = kn.reshape(b, t, FOX_HEADS, FOX_HEAD_DIM)
    vc = v.reshape(b, t, FOX_HEADS, FOX_HEAD_DIM)
    return x1, h2, (kc, vc, logf, new_ssm, new_conv, vn)


def _peer(h2, p, tables, layer):
    b, t, _ = h2.shape
    n_tok = b * t
    n_pad = -n_tok % PEER_TOKEN_TILE
    flat = h2.reshape(n_tok, D_MODEL)
    if n_pad:
        flat = jnp.concatenate([flat, jnp.zeros((n_pad, D_MODEL), BF16)], axis=0)
    wqt_all, u_all, vt_all = tables
    ea, la, eb, pb = _peer_select(flat, wqt_all, layer, p['keys'])
    out = _peer_dense(flat, u_all, vt_all, layer, ea, la, eb, pb)
    return out[:n_tok].reshape(b, t, D_MODEL)


def kernel(x_prompt, x_sample, c_prompt, c_sample, cache_fox_k, cache_fox_v, cache_fox_logf, state_ssm, state_conv, norm1_g, norm2_g, w_ada, b_ada, w_in, conv_w, conv_b, dt_bias, a_log, d_skip, ssd_norm_g, q_norm_g, k_norm_g, fgate_b, w_s, b_s, w_out, peer_wq, peer_keys, peer_u, peer_v):
    depth = w_ada.shape[0]
    bp, tp, _ = x_prompt.shape
    bs, ts, _ = x_sample.shape
    past_len = cache_fox_k.shape[2]
    mod_all = _modulation(jnp.concatenate([c_prompt, c_sample], axis=0).astype(F32), w_ada, b_ada)

    peer_tables = (_to_bf16_t(peer_wq, D_MODEL, 512), _to_bf16(peer_u, 1024), _to_bf16_t(peer_v, 512, D_MODEL))

    past_k_all = cache_fox_k.reshape(depth, bs, past_len, FOX_INNER)
    past_v_all = cache_fox_v.reshape(depth, bs, past_len, FOX_INNER)

    xp, xs = x_prompt, x_sample
    peer_p = peer_s = g2p = g2s = None
    outs = [[] for _ in range(11)]
    for l in range(depth):
        p = _layer_params(l, norm1_g, norm2_g, w_in, conv_w, conv_b, dt_bias, a_log, d_skip, ssd_norm_g, q_norm_g,
                          k_norm_g, fgate_b, w_s, b_s, w_out, peer_keys)
        mods = [m[:, None, :] for m in jnp.split(mod_all[l], 6, axis=-1)]
        mod_p = [m[:bp] for m in mods]
        mod_s = [m[bp:] for m in mods]
        hist_p = jnp.zeros((bp, 8, CONV_CH), F32)
        s0_p = jnp.zeros((bp, SSD_STATE, SSD_INNER), F32)
        hist_s = jnp.concatenate([jnp.zeros((bs, 8 - (CONV_W - 1), CONV_CH), F32), state_conv[l].astype(F32)], axis=1)
        s0_s = state_ssm[l].astype(F32).transpose(0, 3, 1, 2).reshape(bs, SSD_STATE, SSD_INNER)
        plf = cache_fox_logf[l].astype(F32)
        plf_col = jnp.pad(plf, ((0, 0), (0, 0), (F_ROW0, LANES - F_ROW0 - FOX_HEADS)))
        plf_row = jnp.pad(plf.transpose(0, 2, 1), ((0, 0), (0, 8 - FOX_HEADS), (0, 0)))
        past = (past_k_all, past_v_all, l, plf_col, plf_row)

        x1p, h2p, st_p = _stream_mixers(xp, peer_p, g2p, mod_p, p, hist_p, s0_p, None)
        x1s, h2s, st_s = _stream_mixers(xs, peer_s, g2s, mod_s, p, hist_s, s0_s, past)

        peer_p = _peer(h2p, p, peer_tables, l)
        peer_s = _peer(h2s, p, peer_tables, l)
        xp, xs, g2p, g2s = x1p, x1s, mod_p[5], mod_s[5]
        for i in range(5):
            outs[i].append(st_p[i])
        for i in range(6):
            outs[5 + i].append(st_s[i])

    yp = _residual(xp, peer_p, g2p)
    ys = _residual(xs, peer_s, g2s)
    return (yp, ys) + tuple(jnp.stack(o) for o in outs)
```

```python
import functools

import jax
import jax.numpy as jnp
from jax import lax
from jax.experimental import pallas as pl
from jax.experimental.pallas import tpu as pltpu

F32 = jnp.float32
BF16 = jnp.bfloat16
EPS = 1e-6

D_MODEL = 1024
SSD_HEADS = 8
SSD_HEAD_DIM = 64
SSD_INNER = SSD_HEADS * SSD_HEAD_DIM
SSD_GROUPS = 2
SSD_STATE = 64
CONV_W = 4
CONV_CH = SSD_INNER + 2 * SSD_GROUPS * SSD_STATE
FOX_HEADS = 4
FOX_HEAD_DIM = 64
FOX_INNER = FOX_HEADS * FOX_HEAD_DIM
MLP_GROUPS = 4
MLP_GROUP_DIM = 64
MLP_INNER = MLP_GROUPS * MLP_GROUP_DIM
MLP_CHUNK = 128
FOX_PROMPT_TILE = 512
FOX_CHAINS = 4
FCUM_TILE = 2048
FCUM_SUB = 512
SSD_CHUNK_PROMPT = 128
PEER_HEADS = 8
PEER_KEYS = 128
PEER_EXPERTS = PEER_KEYS * PEER_KEYS
PEER_HALF = 128
PEER_TOPK = 16
PEER_TOKEN_TILE = 512
PEER_EXPERT_CHUNK = 2048
LANES = 128
MXU_DIM = 256
PEER_SUB = 128
SMALL_COLS = LANES
SMALL_ROWS = 16
DT_ROW0, F_ROW0 = 0, 8
NEG = -1e30
EXP_UNDERFLOW = 104.0
VMEM_LIMIT = 56 * 1024 * 1024

_Z0, _XBC0, _Q0, _K0, _V0, _U0, _VM0, _SM0 = 0, 512, 1280, 1536, 1792, 2048, 2304, 2560
PROJ_COLS = _SM0 + SMALL_COLS


def _cparams(sem, flags=None):
    return pltpu.CompilerParams(dimension_semantics=sem, vmem_limit_bytes=VMEM_LIMIT, flags=flags)


def _dot(a, b):
    return jnp.dot(a, b, preferred_element_type=F32)


def _dot_nt(a, b):
    return lax.dot_general(a, b, (((1,), (1,)), ((), ())), preferred_element_type=F32)


def _dot_tn(a, b):
    return lax.dot_general(a, b, (((0,), (0,)), ((), ())), preferred_element_type=F32)


def _split2(x):
    hi = x.astype(BF16)
    lo = (x - hi.astype(F32)).astype(BF16)
    return hi, lo


def _split3(x):
    hi = x.astype(BF16)
    r = x - hi.astype(F32)
    mid = r.astype(BF16)
    lo = (r - mid.astype(F32)).astype(BF16)
    return hi, mid, lo


def _dot3_l(x, w):
    hi, mid, lo = _split3(x)
    return _dot(hi, w) + _dot(mid, w) + _dot(lo, w)


def _dot3_r(w, x):
    hi, mid, lo = _split3(x)
    return _dot(w, hi) + _dot(w, mid) + _dot(w, lo)


def _sigmoid(x):
    return 1.0 / (1.0 + jnp.exp(-x))


def _silu(x):
    return x * _sigmoid(x)


def _softplus(x):
    return jnp.maximum(x, 0.0) + jnp.log1p(jnp.exp(-jnp.abs(x)))


_GELU_A = 2.0 * 0.7978845608028654
_GELU_B = _GELU_A * 0.044715


def _gelu(x):
    z2 = x * (_GELU_A + _GELU_B * (x * x))
    return x / (1.0 + jnp.exp(-z2))


def _iota(shape, dim):
    return lax.broadcasted_iota(jnp.int32, shape, dim)


def _cast_kernel(x_ref, o_ref):
    o_ref[0] = x_ref[0].astype(BF16)


def _cast_t_kernel(x_ref, o_ref):
    o_ref[0] = x_ref[0].T.astype(BF16)


def _to_bf16(x, rows):
    n, r, c = x.shape
    return pl.pallas_call(
        _cast_kernel,
        grid=(n, r // rows),
        in_specs=[pl.BlockSpec((1, rows, c), lambda l, i: (l, i, 0))],
        out_specs=pl.BlockSpec((1, rows, c), lambda l, i: (l, i, 0)),
        out_shape=jax.ShapeDtypeStruct((n, r, c), BF16),
        compiler_params=_cparams(("arbitrary", "arbitrary")),
        name="cast_bf16",
    )(x)


def _to_bf16_t(x, rows, cols):
    n, r, c = x.shape
    return pl.pallas_call(
        _cast_t_kernel,
        grid=(n, r // rows, c // cols),
        in_specs=[pl.BlockSpec((1, rows, cols), lambda l, i, j: (l, i, j))],
        out_specs=pl.BlockSpec((1, cols, rows), lambda l, i, j: (l, j, i)),
        out_shape=jax.ShapeDtypeStruct((n, c, r), BF16),
        compiler_params=_cparams(("arbitrary", "arbitrary", "arbitrary")),
        name="cast_bf16_transposed",
    )(x)


def _mod_kernel(c_ref, w_ref, b_ref, o_ref):
    c = c_ref[...]
    o_ref[0] = jnp.dot(_silu(c), w_ref[0], preferred_element_type=F32,
                       precision=lax.Precision.HIGHEST) + b_ref[0]


def _modulation(c_all, w_ada, b_ada):
    depth, _, n6 = w_ada.shape
    bc = c_all.shape[0]
    tn = 1536
    return pl.pallas_call(
        _mod_kernel,
        grid=(depth, n6 // tn),
        in_specs=[pl.BlockSpec((bc, D_MODEL), lambda l, j: (0, 0)),
                  pl.BlockSpec((1, D_MODEL, tn), lambda l, j: (l, 0, j)),
                  pl.BlockSpec((1, 1, tn), lambda l, j: (l, 0, j))],
        out_specs=pl.BlockSpec((1, bc, tn), lambda l, j: (l, 0, j)),
        out_shape=jax.ShapeDtypeStruct((depth, bc, n6), F32),
        compiler_params=_cparams(("arbitrary", "arbitrary")),
        name="adaln_mod",
    )(c_all, w_ada, b_ada.reshape(depth, 1, n6))


def _in_kernel(with_peer, *refs):
    if with_peer:
        x_ref, p_ref, g2_ref = refs[:3]
        refs = refs[3:]
    else:
        x_ref = refs[0]
        refs = refs[1:]
    (sh_ref, sc_ref, g_ref, w_ref, wst_ref, gq_ref, gk_ref, gm_ref) = refs[:8]
    outs = refs[8:]
    if with_peer:
        xo_ref = outs[0]
        outs = outs[1:]
    (z_ref, xbc_ref, q_ref, k_ref, kb_ref, v_ref, vb_ref, ug_ref, vn_ref, sm_ref, smt_ref) = outs

    x = x_ref[0]
    if with_peer:
        x = x + g2_ref[0] * p_ref[0]
        xo_ref[0] = x
    ms = jnp.mean(x * x, axis=-1, keepdims=True)
    h = x * lax.rsqrt(ms + EPS) * g_ref[...]
    h = h * (1.0 + sc_ref[0]) + sh_ref[0]
    hb = h.astype(BF16)
    proj = _dot(hb, w_ref[...])
    z_ref[0] = proj[:, _Z0:_XBC0]
    xbc_ref[0] = proj[:, _XBC0:_Q0]
    q = proj[:, _Q0:_K0]
    k = proj[:, _K0:_V0]
    v = proj[:, _V0:_U0]
    u = proj[:, _U0:_VM0]
    vm = proj[:, _VM0:_SM0]
    sm_ref[0] = proj[:, _SM0:PROJ_COLS]
    smt_ref[0] = _dot_nt(wst_ref[...], hb)
    gm = gm_ref[...]

    def gmean(y):
        hi, lo = _split2(y)
        return _dot(hi, gm) + _dot(lo, gm)

    qn = q * lax.rsqrt(gmean(q * q) + EPS) * gq_ref[...]
    kn = k * lax.rsqrt(gmean(k * k) + EPS) * gk_ref[...]
    q_ref[0] = qn.astype(BF16)
    k_ref[0] = kn
    kb_ref[0] = kn.astype(BF16)
    v_ref[0] = v
    vb_ref[0] = v.astype(BF16)
    ug_ref[0] = _gelu(u)
    gv = _gelu(vm)
    mu = gmean(gv)
    cen = gv - mu
    var = gmean(cen * cen)
    vn_ref[0] = cen * lax.rsqrt(var + EPS)


def _in_proj(x, peer, g2, sh, sc, g, w_r, ws_t, gq, gk, gm):
    b, t, _ = x.shape
    tm = min(t, 512)
    with_peer = peer is not None
    tok = lambda c: pl.BlockSpec((1, tm, c), lambda i, j: (i, j, 0))
    per_b = pl.BlockSpec((1, 1, D_MODEL), lambda i, j: (i, 0, 0))
    full = lambda a: pl.BlockSpec(a.shape, lambda i, j: (0,) * a.ndim)
    in_specs = [tok(D_MODEL)]
    args = [x]
    if with_peer:
        in_specs += [tok(D_MODEL), per_b]
        args += [peer, g2]
    in_specs += [per_b, per_b, full(g), full(w_r), full(ws_t), full(gq), full(gk), full(gm)]
    args += [sh, sc, g, w_r, ws_t, gq, gk, gm]
    out_cols = [(SSD_INNER, F32), (CONV_CH, F32), (FOX_INNER, BF16), (FOX_INNER, F32), (FOX_INNER, BF16),
                (FOX_INNER, F32), (FOX_INNER, BF16), (MLP_INNER, F32), (MLP_INNER, F32), (SMALL_COLS, F32)]
    out_specs = [tok(c) for c, _ in out_cols]
    out_shape = [jax.ShapeDtypeStruct((b, t, c), dt) for c, dt in out_cols]
    out_specs.append(pl.BlockSpec((1, SMALL_ROWS, tm), lambda i, j: (i, 0, j)))
    out_shape.append(jax.ShapeDtypeStruct((b, SMALL_ROWS, t), F32))
    if with_peer:
        out_specs = [tok(D_MODEL)] + out_specs
        out_shape = [jax.ShapeDtypeStruct((b, t, D_MODEL), F32)] + out_shape
    res = pl.pallas_call(
        functools.partial(_in_kernel, with_peer),
        grid=(b, t // tm),
        in_specs=in_specs, out_specs=out_specs, out_shape=out_shape,
        compiler_params=_cparams(("arbitrary", "arbitrary")),
        name="in_proj",
    )(*args)
    if with_peer:
        return res[0], res[1:]
    return x, res


def _ssd_kernel(L, xbc_ref, z_ref, sm_ref, smt_ref, hist_ref, s0_ref, cw_ref, cb_ref, dtbc_ref, dtbr_ref,
                ac_ref, ar_ref, dsk_ref, ng_ref, e_ref, y_ref, sfin_ref, cfin_ref, xpad, st, ybuf):
    c = pl.program_id(1)
    nc = pl.num_programs(1)

    @pl.when(c == 0)
    def _():
        xpad[0:8, :] = hist_ref[0]
        st[...] = s0_ref[0]

    xpad[8:8 + L, :] = xbc_ref[0]
    conv = cb_ref[...]
    for tap in range(CONV_W):
        conv = conv + xpad[5 + tap:5 + tap + L, :] * cw_ref[tap:tap + 1, :]
    tail = xpad[L:L + 8, :]
    xpad[0:8, :] = tail
    xc = _silu(conv)
    xs = xc[:, 0:SSD_INNER]

    r_i = _iota((L, L), 0)
    c_i = _iota((L, L), 1)
    causal = r_i >= c_i
    tri = jnp.where(causal, 1.0, 0.0).astype(BF16)
    triu = jnp.where(r_i <= c_i, 1.0, 0.0).astype(BF16)

    dtc = _softplus(sm_ref[0] + dtbc_ref[...])
    acum_c = _dot3_r(tri, dtc * ac_ref[...])
    e = e_ref[...]
    acum_x = _dot3_l(acum_c, e)
    dt_x = _dot3_l(dtc, e)
    dtr = _softplus(smt_ref[0][DT_ROW0:DT_ROW0 + 8, :] + dtbr_ref[...])
    acum_r = _dot3_l(dtr * ar_ref[...], triu)

    bmat = [xc[:, SSD_INNER + SSD_STATE * g:SSD_INNER + SSD_STATE * (g + 1)].astype(BF16) for g in range(SSD_GROUPS)]
    c0 = SSD_INNER + SSD_GROUPS * SSD_STATE
    cmat = [xc[:, c0 + SSD_STATE * g:c0 + SSD_STATE * (g + 1)].astype(BF16) for g in range(SSD_GROUPS)]
    cb = [_dot_nt(cmat[g], bmat[g]) for g in range(SSD_GROUPS)]
    gw = SSD_INNER // SSD_GROUPS
    stb = st[...].astype(BF16)
    y_off = jnp.concatenate([_dot(cmat[g], stb[:, gw * g:gw * (g + 1)]) for g in range(SSD_GROUPS)], axis=1)
    xsb = xs.astype(BF16)
    hpg = SSD_HEADS // SSD_GROUPS
    mixes = []
    for h in range(SSD_HEADS):
        seg = acum_c[:, h:h + 1] - acum_r[h:h + 1, :]
        dec = jnp.where(causal, jnp.exp(jnp.minimum(seg, 0.0)), 0.0)
        mixes.append((cb[h // hpg] * dec * dtr[h:h + 1, :]).astype(BF16))
    heads = [_dot(mixes[h], xsb[:, SSD_HEAD_DIM * h:SSD_HEAD_DIM * (h + 1)]) for h in range(SSD_HEADS)]
    for h in range(SSD_HEADS):
        ybuf[:, SSD_HEAD_DIM * h:SSD_HEAD_DIM * (h + 1)] = heads[h]

    y = ybuf[...] + y_off * jnp.exp(acum_x) + dsk_ref[...] * xs
    a_end = acum_x[L - 1:L, :]
    xw = (xs * dt_x * jnp.exp(a_end - acum_x)).astype(BF16)
    new_states = jnp.concatenate([_dot_tn(bmat[g], xw[:, gw * g:gw * (g + 1)]) for g in range(SSD_GROUPS)], axis=1)
    st[...] = st[...] * jnp.exp(a_end) + new_states

    yg = y * _silu(z_ref[0])
    ms = jnp.mean(yg * yg, axis=-1, keepdims=True)
    y_ref[0] = yg * lax.rsqrt(ms + EPS) * ng_ref[...]

    @pl.when(c == nc - 1)
    def _():
        sfin_ref[0] = st[...]
        cfin_ref[0] = tail


def _ssd(xbc, z, sm, smt, hist8, s0t, cw, cb, dtb_c, dtb_r, a_c, a_r, dsk_x, ng, e_mat, L):
    b, t, _ = xbc.shape
    tok = lambda c: pl.BlockSpec((1, L, c), lambda i, j: (i, j, 0))
    full = lambda a: pl.BlockSpec(a.shape, lambda i, j: (0,) * a.ndim)
    per_b = lambda a: pl.BlockSpec((1,) + a.shape[1:], lambda i, j: (i,) + (0,) * (a.ndim - 1))
    return pl.pallas_call(
        functools.partial(_ssd_kernel, L),
        grid=(b, t // L),
        in_specs=[tok(CONV_CH), tok(SSD_INNER), tok(SMALL_COLS),
                  pl.BlockSpec((1, SMALL_ROWS, L), lambda i, j: (i, 0, j)),
                  per_b(hist8), per_b(s0t), full(cw), full(cb), full(dtb_c), full(dtb_r), full(a_c), full(a_r),
                  full(dsk_x), full(ng), full(e_mat)],
        out_specs=[tok(SSD_INNER),
                   pl.BlockSpec((1, SSD_STATE, SSD_INNER), lambda i, j: (i, 0, 0)),
                   pl.BlockSpec((1, 8, CONV_CH), lambda i, j: (i, 0, 0))],
        out_shape=[jax.ShapeDtypeStruct((b, t, SSD_INNER), F32),
                   jax.ShapeDtypeStruct((b, SSD_STATE, SSD_INNER), F32),
                   jax.ShapeDtypeStruct((b, 8, CONV_CH), F32)],
        scratch_shapes=[pltpu.VMEM((L + 8, CONV_CH), F32), pltpu.VMEM((SSD_STATE, SSD_INNER), F32),
                        pltpu.VMEM((L, SSD_INNER), F32)],
        compiler_params=_cparams(("arbitrary", "arbitrary")),
        name="ssd_scan",
    )(xbc, z, sm, smt, hist8, s0t, cw, cb, dtb_c, dtb_r, a_c, a_r, dsk_x, ng, e_mat)


def _fcum_kernel(tf, activate, colsrc_ref, rowsrc_ref, fbc_ref, fbr_ref, initc_ref, initr_ref,
                 lfc_ref, fc_ref, fr_ref, endc_ref, endr_ref, carc, carr):
    j = pl.program_id(1)

    @pl.when(j == 0)
    def _():
        carc[...] = initc_ref[0]
        carr[...] = initr_ref[0]

    ts = min(tf, FCUM_SUB)
    r_i = _iota((ts, ts), 0)
    c_i = _iota((ts, ts), 1)
    tri = jnp.where(r_i >= c_i, 1.0, 0.0).astype(BF16)
    triu = jnp.where(r_i <= c_i, 1.0, 0.0).astype(BF16)
    car_c = carc[0:1, :]
    car_r = carr[:, 0:1]
    for k in range(tf // ts):
        xc = colsrc_ref[0, k * ts:(k + 1) * ts, :]
        xr = rowsrc_ref[0, :, k * ts:(k + 1) * ts]
        if activate:
            xc = -_softplus(-(xc + fbc_ref[...]))
            xr = -_softplus(-(xr + fbr_ref[...]))
        fcol = car_c + _dot3_r(tri, xc)
        frow = car_r + _dot3_l(xr, triu)
        lfc_ref[0, k * ts:(k + 1) * ts, :] = xc
        fc_ref[0, k * ts:(k + 1) * ts, :] = fcol
        fr_ref[0, :, k * ts:(k + 1) * ts] = frow
        car_c = fcol[ts - 1:ts, :]
        car_r = frow[:, ts - 1:ts]
    carc[...] = jnp.broadcast_to(car_c, carc.shape)
    carr[...] = jnp.broadcast_to(car_r, carr.shape)
    endc_ref[0] = carc[...]
    endr_ref[0] = carr[...]


def _fcum(colsrc, rowsrc, row_block, fb_c, fb_r, init_c, init_r, activate):
    b, t, _ = colsrc.shape
    tf = min(t, FCUM_TILE)
    full = lambda a: pl.BlockSpec(a.shape, lambda i, j: (0,) * a.ndim)
    per_b = lambda a: pl.BlockSpec((1,) + a.shape[1:], lambda i, j: (i,) + (0,) * (a.ndim - 1))
    return pl.pallas_call(
        functools.partial(_fcum_kernel, tf, activate),
        grid=(b, t // tf),
        in_specs=[pl.BlockSpec((1, tf, LANES), lambda i, j: (i, j, 0)),
                  pl.BlockSpec((1, 8, tf), lambda i, j: (i, row_block, j)),
                  full(fb_c), full(fb_r), per_b(init_c), per_b(init_r)],
        out_specs=[pl.BlockSpec((1, tf, LANES), lambda i, j: (i, j, 0)),
                   pl.BlockSpec((1, tf, LANES), lambda i, j: (i, j, 0)),
                   pl.BlockSpec((1, 8, tf), lambda i, j: (i, 0, j)),
                   pl.BlockSpec((1, 8, LANES), lambda i, j: (i, 0, 0)),
                   pl.BlockSpec((1, 8, LANES), lambda i, j: (i, 0, 0))],
        out_shape=[jax.ShapeDtypeStruct((b, t, LANES), F32), jax.ShapeDtypeStruct((b, t, LANES), F32),
                   jax.ShapeDtypeStruct((b, 8, t), F32), jax.ShapeDtypeStruct((b, 8, LANES), F32),
                   jax.ShapeDtypeStruct((b, 8, LANES), F32)],
        scratch_shapes=[pltpu.VMEM((8, LANES), F32), pltpu.VMEM((8, LANES), F32)],
        compiler_params=_cparams(("arbitrary", "arbitrary")),
        name="forget_cumsum",
    )(colsrc, rowsrc, fb_c, fb_r, init_c, init_r)


def _softmax_stats(s, carry):
    m, l, acc = carry
    tq, tk = s.shape
    m_new = jnp.maximum(m, jnp.max(s, axis=-1, keepdims=True))
    p = jnp.exp(s - m_new)
    alpha = jnp.exp(m - m_new)
    if tk % LANES == 0:
        psum = p[:, 0:LANES]
        for c in range(1, tk // LANES):
            psum = psum + p[:, c * LANES:(c + 1) * LANES]
    else:
        psum = jnp.where(_iota((tq, LANES), 1) == 0, jnp.sum(p, axis=-1, keepdims=True), 0.0)
    return m_new, alpha * l + psum, alpha * acc, p.astype(BF16)


def _softmax_init(tq):
    return tuple((jnp.full((tq, 1), NEG, F32), jnp.zeros((tq, LANES), F32), jnp.zeros((tq, FOX_HEAD_DIM), F32))
                 for _ in range(FOX_HEADS))


def _head_slice(h):
    return slice(FOX_HEAD_DIM * h, FOX_HEAD_DIM * (h + 1))


def _attend_block(qs, fqs, kb, vb, fk, mask, carries):
    heads = range(FOX_HEADS)
    scores = [_dot_nt(qs[h], kb[:, _head_slice(h)]) for h in heads]
    stats = []
    for h in heads:
        s = scores[h] + fqs[h] - fk[h:h + 1, :]
        if mask is not None:
            s = jnp.where(mask, s, NEG)
        stats.append(_softmax_stats(s, carries[h]))
    return tuple((m, l, acc + _dot(p, vb[:, _head_slice(h)])) for h, (m, l, acc, p) in zip(heads, stats))


def _attend_finish(carries, o_ref):
    for h in range(FOX_HEADS):
        _, l, acc = carries[h]
        o_ref[0, :, _head_slice(h)] = acc / jnp.sum(l, axis=-1, keepdims=True)


def _bf16_terms(x):
    hi, mid, lo = _split3(x)
    return hi.astype(F32), mid.astype(F32), lo.astype(F32)


def _fox_prep_kernel(q_ref, k_ref, v_ref, fc_ref, fr_ref, ka_ref, qa_ref, vt_ref):
    tm = q_ref.shape[1]
    qt = q_ref[0].astype(F32).T
    k = k_ref[0].astype(F32)
    vt_ref[0, 0] = v_ref[0].astype(F32).T.astype(BF16)
    lane = _iota((tm, LANES), 1)
    row = _iota((FOX_HEAD_DIM, tm), 0)
    d = FOX_HEAD_DIM
    for h in range(FOX_HEADS):
        pair, odd = divmod(h, 2)
        kc = k[:, pair * LANES:(pair + 1) * LANES]
        if odd:
            kc = pltpu.roll(kc, d, axis=1)
        ka = jnp.where(lane < d, kc, 0.0)
        for i, term in enumerate(_bf16_terms(-fc_ref[0][:, F_ROW0 + h:F_ROW0 + h + 1])):
            ka = jnp.where(lane == d + i, term, ka)
        ka = jnp.where(jnp.logical_and(lane >= d + 3, lane < d + 6), 1.0, ka)
        ka_ref[0, h] = ka.astype(BF16)
        tail = jnp.where(row < 3, 1.0, 0.0)
        for i, term in enumerate(_bf16_terms(fr_ref[0][h:h + 1, :])):
            tail = jnp.where(row == 3 + i, term, tail)
        qa_ref[0, h, 0:d, :] = qt[d * h:d * (h + 1), :].astype(BF16)
        qa_ref[0, h, d:2 * d, :] = tail.astype(BF16)


def _fox_prep(qb, kb, vb, fcol, frow, tm):
    b, t, _ = qb.shape
    tok = lambda c: pl.BlockSpec((1, tm, c), lambda i, j: (i, j, 0))
    return pl.pallas_call(
        _fox_prep_kernel,
        grid=(b, t // tm),
        in_specs=[tok(FOX_INNER), tok(FOX_INNER), tok(FOX_INNER), tok(LANES),
                  pl.BlockSpec((1, 8, tm), lambda i, j: (i, 0, j))],
        out_specs=[pl.BlockSpec((1, FOX_HEADS, tm, LANES), lambda i, j: (i, 0, j, 0)),
                   pl.BlockSpec((1, FOX_HEADS, LANES, tm), lambda i, j: (i, 0, 0, j)),
                   pl.BlockSpec((1, 1, FOX_INNER, tm), lambda i, j: (i, j, 0, 0))],
        out_shape=[jax.ShapeDtypeStruct((b, FOX_HEADS, t, LANES), BF16),
                   jax.ShapeDtypeStruct((b, FOX_HEADS, LANES, t), BF16),
                   jax.ShapeDtypeStruct((b, t // tm, FOX_INNER, tm), BF16)],
        compiler_params=_cparams(("arbitrary", "arbitrary")),
        name="fox_prep",
    )(qb, kb, vb, fcol, frow)


def _fox_prompt_t_kernel(tq, nk, qa_ref, ka_ref, vt_ref, fend_ref, ffirst_ref, thr_ref, o_ref):
    b = pl.program_id(0)
    qi = pl.program_id(1)
    half = tq // FOX_CHAINS
    d = FOX_HEAD_DIM
    diag = pl.multiple_of(qi * tq, tq)
    init = (jnp.full((1, half), NEG, F32), jnp.zeros((1, half), F32), jnp.zeros((d, half), F32))

    def update(blocks, qas, masks, carries):
        scores = [[_dot(ka, qa) for qa in qas] for ka, _ in blocks]
        for (_, vt), block_scores in zip(blocks, scores):
            stats = []
            for s, mask, (m, l, acc) in zip(block_scores, masks, carries):
                if mask is not None:
                    s = jnp.where(mask, s, NEG)
                m_new = jnp.maximum(m, jnp.max(s, axis=0, keepdims=True))
                p = jnp.exp(s - m_new)
                alpha = jnp.exp(m - m_new)
                stats.append((m_new, alpha * l + jnp.sum(p, axis=0, keepdims=True), alpha * acc, p.astype(BF16)))
            carries = tuple((m, l, acc + _dot(vt, p)) for m, l, acc, p in stats)
        return carries

    def key_block(h, j):
        return (ka_ref[0, h, pl.ds(pl.multiple_of(j * tq, tq), tq), :], vt_ref[0, j][d * h:d * (h + 1), :])

    for h in range(FOX_HEADS):
        base = (b * FOX_HEADS + h) * nk
        slack = thr_ref[0] + ffirst_ref[base + qi]
        n_live = lax.fori_loop(0, qi, lambda j, c: c + (slack - fend_ref[base + j] >= 0.0).astype(jnp.int32), 0)
        qas = [qa_ref[0, h, :, r2 * half:(r2 + 1) * half] for r2 in range(FOX_CHAINS)]
        ka_d = ka_ref[0, h, pl.ds(diag, tq), :]
        vt_d = vt_ref[0, qi][d * h:d * (h + 1), :]
        causal = [_iota((tq, half), 0) <= (r2 * half + _iota((tq, half), 1)) for r2 in range(FOX_CHAINS)]
        carries = update([(ka_d, vt_d)], qas, causal, [init] * FOX_CHAINS)
        no_mask = [None] * FOX_CHAINS

        def one(t, carries, qas=qas, h=h):
            return update([key_block(h, qi - 1 - t)], qas, no_mask, carries)

        def two(t, carries, qas=qas, h=h):
            j = qi - 1 - (n_live % 2) - 2 * t
            return update([key_block(h, j), key_block(h, j - 1)], qas, no_mask, carries)

        carries = lax.fori_loop(0, n_live % 2, one, carries)
        carries = lax.fori_loop(0, n_live // 2, two, carries)
        for r2 in range(FOX_CHAINS):
            _, l, acc = carries[r2]
            o_ref[0, r2 * half:(r2 + 1) * half, d * h:d * (h + 1)] = (acc / l).T


def _fox_prompt_t(qb, kb, vb, fcol, frow, score_bound):
    b, t, _ = qb.shape
    tq = min(t, FOX_PROMPT_TILE)
    nk = t // tq
    ka, qa, vt = _fox_prep(qb, kb, vb, fcol, frow, tq)
    f_heads = frow[:, :FOX_HEADS, :].reshape(b, FOX_HEADS, nk, tq)
    f_end = f_heads[:, :, :, tq - 1].reshape(-1)
    f_first = f_heads[:, :, :, 0].reshape(-1)
    thr = (2.0 * score_bound + EXP_UNDERFLOW).reshape(1).astype(F32)
    smem = pl.BlockSpec(memory_space=pltpu.SMEM)
    return pl.pallas_call(
        functools.partial(_fox_prompt_t_kernel, tq, nk),
        grid=(b, nk),
        in_specs=[pl.BlockSpec((1, FOX_HEADS, LANES, tq), lambda i, j: (i, 0, 0, j)),
                  pl.BlockSpec((1, FOX_HEADS, t, LANES), lambda i, j: (i, 0, 0, 0)),
                  pl.BlockSpec((1, nk, FOX_INNER, tq), lambda i, j: (i, 0, 0, 0)),
                  smem, smem, smem],
        out_specs=pl.BlockSpec((1, tq, FOX_INNER), lambda i, j: (i, j, 0)),
        out_shape=jax.ShapeDtypeStruct((b, t, FOX_INNER), F32),
        compiler_params=_cparams(("arbitrary", "arbitrary")),
        name="fox_prompt",
    )(qa, ka, vt, f_end, f_first, thr)


def _fox_sample_kernel(tq, tk, npast, q_ref, fq_ref, pk_ref, pv_ref, fpk_ref, k_ref, v_ref, fk_ref, o_ref):
    q_all = q_ref[0]
    fq_all = fq_ref[0]
    qs = [q_all[:, _head_slice(h)] for h in range(FOX_HEADS)]
    fqs = [fq_all[:, F_ROW0 + h:F_ROW0 + h + 1] for h in range(FOX_HEADS)]
    causal = _iota((tq, tq), 0) >= _iota((tq, tq), 1)

    def past_block(j, carries):
        start = pl.multiple_of(j * tk, tk)
        return _attend_block(qs, fqs, pk_ref[0, pl.ds(start, tk), :].astype(BF16),
                             pv_ref[0, pl.ds(start, tk), :].astype(BF16), fpk_ref[0, j], None, carries)

    carries = lax.fori_loop(0, npast, past_block, _softmax_init(tq))
    _attend_finish(_attend_block(qs, fqs, k_ref[0], v_ref[0], fk_ref[0], causal, carries), o_ref)


def _fox_sample(qb, fcol, past_k, past_v, layer, fpast_row, kb, vb, frow):
    b, t, _ = qb.shape
    p = past_k.shape[2]
    tk = min(p, 512)
    npast = p // tk
    fpk = fpast_row.reshape(b, 8, npast, tk).transpose(0, 2, 1, 3)
    bspec = lambda a: pl.BlockSpec((1,) + a.shape[1:], lambda i: (i,) + (0,) * (a.ndim - 1))
    cache = pl.BlockSpec((None, 1, p, FOX_INNER), lambda i: (layer, i, 0, 0))
    args = (qb, fcol, past_k, past_v, fpk, kb, vb, frow)
    return pl.pallas_call(
        functools.partial(_fox_sample_kernel, t, tk, npast),
        grid=(b,),
        in_specs=[bspec(qb), bspec(fcol), cache, cache, bspec(fpk), bspec(kb), bspec(vb), bspec(frow)],
        out_specs=pl.BlockSpec((1, t, FOX_INNER), lambda i: (i, 0, 0)),
        out_shape=jax.ShapeDtypeStruct((b, t, FOX_INNER), F32),
        compiler_params=_cparams(("arbitrary",)),
        name="fox_sample",
    )(*args)


def _mlp_kernel(lm, nchunk, ug_ref, vn_ref, ws_ref, bst_ref, y_ref):
    r_i = _iota((lm, lm), 0)
    c_i = _iota((lm, lm), 1)
    tril = r_i >= c_i
    vn = vn_ref[0].astype(BF16)
    ug = ug_ref[0]
    for g in range(MLP_GROUPS):
        lo, hi = MLP_GROUP_DIM * g, MLP_GROUP_DIM * (g + 1)
        w = jnp.where(tril, ws_ref[g], 0.0).astype(BF16)
        bias = bst_ref[:, g:g + 1]
        for c in range(nchunk):
            r0, r1 = c * lm, (c + 1) * lm
            sv = _dot(w, vn[r0:r1, lo:hi]) + bias
            y_ref[0, r0:r1, lo:hi] = ug[r0:r1, lo:hi] * sv


def _chunk_mlp(ug, vn, ws, bst):
    b, t, _ = ug.shape
    lm = ws.shape[1]
    tm = min(t, 4 * lm)
    tok = pl.BlockSpec((1, tm, MLP_INNER), lambda i, j: (i, j, 0))
    return pl.pallas_call(
        functools.partial(_mlp_kernel, lm, tm // lm),
        grid=(b, t // tm),
        in_specs=[tok, tok, pl.BlockSpec(ws.shape, lambda i, j: (0, 0, 0)),
                  pl.BlockSpec(bst.shape, lambda i, j: (0, 0))],
        out_specs=tok,
        out_shape=jax.ShapeDtypeStruct((b, t, MLP_INNER), F32),
        compiler_params=_cparams(("arbitrary", "arbitrary")),
        name="chunk_mlp",
    )(ug, vn, ws, bst)


def _out_kernel(x_ref, ys_ref, yf_ref, ym_ref, g1_ref, sh_ref, sc_ref, g_ref, wo_ref, x1_ref, h2_ref):
    mix = (_dot(ys_ref[0].astype(BF16), wo_ref[0:SSD_INNER, :])
           + _dot(yf_ref[0].astype(BF16), wo_ref[SSD_INNER:SSD_INNER + FOX_INNER, :])
           + _dot(ym_ref[0].astype(BF16), wo_ref[SSD_INNER + FOX_INNER:D_MODEL, :]))
    x1 = x_ref[0] + g1_ref[0] * mix
    x1_ref[0] = x1
    ms = jnp.mean(x1 * x1, axis=-1, keepdims=True)
    h = x1 * lax.rsqrt(ms + EPS) * g_ref[...]
    h2_ref[0] = (h * (1.0 + sc_ref[0]) + sh_ref[0]).astype(BF16)


def _out_proj(x, ys, yf, ym, g1, sh, sc, g, wo):
    b, t, _ = x.shape
    tm = min(t, 512)
    tok = lambda c: pl.BlockSpec((1, tm, c), lambda i, j: (i, j, 0))
    per_b = pl.BlockSpec((1, 1, D_MODEL), lambda i, j: (i, 0, 0))
    full = lambda a: pl.BlockSpec(a.shape, lambda i, j: (0,) * a.ndim)
    return pl.pallas_call(
        _out_kernel,
        grid=(b, t // tm),
        in_specs=[tok(D_MODEL), tok(SSD_INNER), tok(FOX_INNER), tok(MLP_INNER), per_b, per_b, per_b, full(g), full(wo)],
        out_specs=[tok(D_MODEL), tok(D_MODEL)],
        out_shape=[jax.ShapeDtypeStruct((b, t, D_MODEL), F32), jax.ShapeDtypeStruct((b, t, D_MODEL), BF16)],
        compiler_params=_cparams(("arbitrary", "arbitrary")),
        name="out_proj",
    )(x, ys, yf, ym, g1, sh, sc, g, wo)


def _top16(scores):
    nk, tn = scores[0].shape
    ridx = _iota((nk, tn), 0).astype(F32)
    r16 = _iota((PEER_TOPK, tn), 0)
    state = [(s, jnp.full((nk, tn), float(PEER_TOPK), F32), jnp.zeros((PEER_TOPK, tn), F32)) for s in scores]
    for it in range(PEER_TOPK):
        nxt_state = []
        for s, pos, tv in state:
            level = [(s[g:g + 8, :], ridx[g:g + 8, :]) for g in range(0, nk, 8)]
            while len(level) > 1:
                nxt = []
                for a in range(0, len(level), 2):
                    (va, ia), (vb, ib) = level[a], level[a + 1]
                    nxt.append((jnp.maximum(va, vb), jnp.where(va >= vb, ia, ib)))
                level = nxt
            v8, i8 = level[0]
            m = jnp.max(v8, axis=0, keepdims=True)
            first = jnp.min(jnp.where(v8 == m, i8, float(nk)), axis=0, keepdims=True)
            sel = ridx == first
            nxt_state.append((jnp.where(sel, -jnp.inf, s), jnp.where(sel, float(it), pos),
                              jnp.where(r16 == it, m, tv)))
        state = nxt_state
    return [(pos, tv) for _, pos, tv in state]


_CAND_GROUPS = [(0, 16)] + [(ka, 8) for ka in range(1, 8)]
_CAND_ROWS = 16 + 7 * 8 + 8


def _pair_select(ta, tb):
    tn = ta.shape[1]
    pieces, flats, valids = [], [], []
    for ka, rows in _CAND_GROUPS:
        pieces.append(ta[ka:ka + 1, :] + tb[0:rows, :])
        kb = _iota((rows, 1), 0)
        flats.append((ka * PEER_TOPK + kb).astype(F32))
        valids.append((ka + 1) * (kb + 1) <= PEER_TOPK)
    pieces.append(ta[8:16, :] + tb[0:1, :])
    flats.append(((8 + _iota((8, 1), 0)) * PEER_TOPK).astype(F32))
    valids.append(_iota((8, 1), 0) >= 0)
    cand0 = jnp.concatenate(pieces, axis=0)
    flat = jnp.concatenate(flats, axis=0)
    valid = jnp.concatenate(valids, axis=0)
    cand0 = jnp.where(valid, cand0, -jnp.inf)
    best = ta[0:1, :] + tb[0:1, :]

    cand = cand0
    selm = jnp.zeros((_CAND_ROWS, tn), F32)
    for _ in range(PEER_TOPK):
        m = jnp.max(cand, axis=0, keepdims=True)
        first = jnp.min(jnp.where(cand == m, flat, 4096.0), axis=0, keepdims=True)
        sel = flat == first
        cand = jnp.where(sel, -jnp.inf, cand)
        selm = jnp.where(sel, 1.0, selm)
    z = jnp.sum(jnp.where(selm > 0.0, jnp.exp(cand0 - best), 0.0), axis=0, keepdims=True)
    cnts = [jnp.sum(selm[0:16, :], axis=0, keepdims=True)]
    for i in range(1, 8):
        cnts.append(jnp.sum(selm[8 + 8 * i:16 + 8 * i, :], axis=0, keepdims=True))
    cnts.append(selm[_CAND_ROWS - 8:_CAND_ROWS, :])
    return jnp.concatenate(cnts, axis=0), z


def _peer_sel_kernel(tn, h_ref, wqt_ref, keys_ref, ea_ref, la_ref, eb_ref, pb_ref, qt_ref):
    qt_ref[...] = _dot_nt(wqt_ref[...], h_ref[...]).astype(BF16)

    def head(h, carry):
        row = pl.multiple_of(h * (2 * PEER_HALF), 2 * PEER_HALF)
        sa_all = _dot(keys_ref[2 * h], qt_ref[pl.ds(row, PEER_HALF), :])
        sb_all = _dot(keys_ref[2 * h + 1], qt_ref[pl.ds(row + PEER_HALF, PEER_HALF), :])
        for c in range(tn // LANES):
            cs = slice(c * LANES, (c + 1) * LANES)
            sa, sb = sa_all[:, cs], sb_all[:, cs]
            (pos_a, ta), (pos_b, tb) = _top16([sa, sb])
            cnt, z = _pair_select(ta, tb)
            la = jnp.zeros_like(pos_a)
            for ka in range(PEER_TOPK):
                la = jnp.where(pos_a == float(ka), cnt[ka:ka + 1, :], la)
            ea_ref[h, :, cs] = jnp.where(pos_a < float(PEER_TOPK), jnp.exp(sa - ta[0:1, :]), 0.0) / z
            la_ref[h, :, cs] = la
            eb = jnp.where(pos_b < float(PEER_TOPK), jnp.exp(sb - tb[0:1, :]), 0.0)
            for sub in range(PEER_KEYS // PEER_SUB):
                src = slice(sub * PEER_SUB, (sub + 1) * PEER_SUB)
                dst = slice(sub * PEER_SUB // 2, (sub + 1) * PEER_SUB // 2)
                eb_ref[h, dst, cs] = pltpu.bitcast(eb[src].astype(BF16), jnp.uint32)
                pb_ref[h, dst, cs] = pltpu.bitcast(pos_b[src].astype(BF16), jnp.uint32)
        return carry

    lax.fori_loop(0, PEER_HEADS, head, 0)


def _peer_select(h2, wqt_all, layer, keys):
    n = h2.shape[0]
    tn = 256
    per_tok = pl.BlockSpec((PEER_HEADS, PEER_KEYS, tn), lambda i: (0, 0, i))
    per_tok_pk = pl.BlockSpec((PEER_HEADS, PEER_KEYS // 2, tn), lambda i: (0, 0, i))
    shp = lambda dt: jax.ShapeDtypeStruct((PEER_HEADS, PEER_KEYS, n), dt)
    shp_pk = jax.ShapeDtypeStruct((PEER_HEADS, PEER_KEYS // 2, n), jnp.uint32)
    return pl.pallas_call(
        functools.partial(_peer_sel_kernel, tn),
        grid=(n // tn,),
        in_specs=[pl.BlockSpec((tn, D_MODEL), lambda i: (i, 0)),
                  pl.BlockSpec((None,) + wqt_all.shape[1:], lambda i: (layer, 0, 0)),
                  pl.BlockSpec(keys.shape, lambda i: (0, 0, 0))],
        out_specs=[per_tok, per_tok, per_tok_pk, per_tok_pk],
        out_shape=[shp(F32), shp(F32), shp_pk, shp_pk],
        scratch_shapes=[pltpu.VMEM((PEER_HEADS * 2 * PEER_HALF, tn), BF16)],
        compiler_params=_cparams(("arbitrary",)),
        name="peer_select",
    )(h2, wqt_all, keys)


def _peer_gate_stage(ec, ia0, rows, chunks, at_r, ga_w, ea_ref, la_ref, eb_ref, pb_ref):
    rows_per = ec // PEER_KEYS
    for r in rows:
        for c in chunks:
            cs = slice(c * LANES, (c + 1) * LANES)
            for sub in range(PEER_KEYS // PEER_SUB):
                pk = slice(sub * PEER_SUB // 2, (sub + 1) * PEER_SUB // 2)
                gate = None
                for h in range(PEER_HEADS):
                    la = la_ref[h, pl.ds(ia0, rows_per), cs][r:r + 1, :]
                    ea = ea_ref[h, pl.ds(ia0, rows_per), cs][r:r + 1, :]
                    la = jnp.broadcast_to(la, (PEER_SUB, LANES)).astype(BF16)
                    ea = jnp.broadcast_to(ea, (PEER_SUB, LANES)).astype(BF16)
                    pb = pltpu.bitcast(pb_ref[h, pk, cs], BF16)
                    eb = pltpu.bitcast(eb_ref[h, pk, cs], BF16)
                    term = jnp.where(pb < la, eb, 0.0) * ea
                    gate = term if gate is None else gate + term
                ex = slice(r * PEER_KEYS + sub * PEER_SUB, r * PEER_KEYS + (sub + 1) * PEER_SUB)
                ga_w[ex, cs] = gate * _gelu(at_r[ex, cs]).astype(BF16)


def _peer_dense_kernel(tn, ec, h_ref, u_ref, vt_ref, ea_ref, la_ref, eb_ref, pb_ref, o_ref, at, ga, acc):
    e = pl.program_id(1)
    ne = pl.num_programs(1)

    @pl.when(e == 0)
    def _():
        acc[...] = jnp.zeros_like(acc)

    at[...] = _dot_nt(u_ref[...], h_ref[...])
    rows_per = ec // PEER_KEYS
    ia0 = pl.multiple_of(e * rows_per, rows_per)
    _peer_gate_stage(ec, ia0, range(rows_per), range(tn // LANES), at, ga, ea_ref, la_ref, eb_ref, pb_ref)
    acc[...] += _dot(vt_ref[...], ga[...])

    @pl.when(e == ne - 1)
    def _():
        o_ref[...] = acc[...].T


def _peer_dense(h2, u_all, vt_all, layer, ea, la, eb, pb):
    n = h2.shape[0]
    tn, ec = PEER_TOKEN_TILE, PEER_EXPERT_CHUNK
    per_tok = pl.BlockSpec((PEER_HEADS, PEER_KEYS, tn), lambda i, e: (0, 0, i))
    per_tok_pk = pl.BlockSpec((PEER_HEADS, PEER_KEYS // 2, tn), lambda i, e: (0, 0, i))
    return pl.pallas_call(
        functools.partial(_peer_dense_kernel, tn, ec),
        grid=(n // tn, PEER_EXPERTS // ec),
        in_specs=[pl.BlockSpec((tn, D_MODEL), lambda i, e: (i, 0)),
                  pl.BlockSpec((None, ec, D_MODEL), lambda i, e: (layer, e, 0)),
                  pl.BlockSpec((None, D_MODEL, ec), lambda i, e: (layer, 0, e)),
                  per_tok, per_tok, per_tok_pk, per_tok_pk],
        out_specs=pl.BlockSpec((tn, D_MODEL), lambda i, e: (i, 0)),
        out_shape=jax.ShapeDtypeStruct((n, D_MODEL), F32),
        scratch_shapes=[pltpu.VMEM((ec, tn), F32), pltpu.VMEM((ec, tn), BF16), pltpu.VMEM((D_MODEL, tn), F32)],
        compiler_params=_cparams(("arbitrary", "arbitrary")),
        name="peer_dense",
    )(h2, u_all, vt_all, ea, la, eb, pb)


def _resid_kernel(x_ref, p_ref, g_ref, o_ref):
    o_ref[0] = x_ref[0] + g_ref[0] * p_ref[0]


def _residual(x, peer, g2):
    b, t, _ = x.shape
    tm = min(t, 512)
    tok = pl.BlockSpec((1, tm, D_MODEL), lambda i, j: (i, j, 0))
    return pl.pallas_call(
        _resid_kernel,
        grid=(b, t // tm),
        in_specs=[tok, tok, pl.BlockSpec((1, 1, D_MODEL), lambda i, j: (i, 0, 0))],
        out_specs=tok,
        out_shape=jax.ShapeDtypeStruct((b, t, D_MODEL), F32),
        compiler_params=_cparams(("arbitrary", "arbitrary")),
        name="peer_residual",
    )(x, peer, g2)


def _lane_pad(vec, offset, width=LANES):
    out = jnp.zeros((width,), F32)
    return out.at[offset:offset + vec.shape[0]].set(vec.astype(F32))


def _layer_params(l, norm1_g, norm2_g, w_in, conv_w, conv_b, dt_bias, a_log, d_skip, ssd_norm_g, q_norm_g,
                  k_norm_g, fgate_b, w_s, b_s, w_out, peer_keys):
    w = w_in[l]
    o = [0]
    for sz in (SSD_INNER, CONV_CH, SSD_HEADS, FOX_INNER, FOX_INNER, FOX_INNER, FOX_HEADS, MLP_INNER, MLP_INNER):
        o.append(o[-1] + sz)
    wz, wxbc, wdt, wq, wk, wv, wf, wu, wvm = [w[:, o[i]:o[i + 1]] for i in range(9)]
    w_small = jnp.zeros((D_MODEL, SMALL_COLS), F32)
    w_small = w_small.at[:, DT_ROW0:DT_ROW0 + SSD_HEADS].set(wdt).at[:, F_ROW0:F_ROW0 + FOX_HEADS].set(wf)
    w_r = jnp.concatenate([wz, wxbc, wq, wk, wv, wu, wvm, w_small], axis=1).astype(BF16)
    ws_t = w_small[:, :SMALL_ROWS].T.astype(BF16)
    a_neg = -jnp.exp(a_log[l].astype(F32))
    grp = jnp.arange(FOX_INNER) // FOX_HEAD_DIM
    gm = jnp.where(grp[:, None] == grp[None, :], 1.0 / FOX_HEAD_DIM, 0.0).astype(BF16)
    heads = jnp.arange(SSD_INNER) // SSD_HEAD_DIM
    e_mat = (jnp.arange(LANES)[:, None] == heads[None, :]).astype(BF16)
    return dict(
        norm1_g=norm1_g[l][None], norm2_g=norm2_g[l][None], w_r=w_r, ws_t=ws_t,
        gq=(jnp.tile(q_norm_g[l], FOX_HEADS) * (FOX_HEAD_DIM ** -0.5))[None],
        gk=jnp.tile(k_norm_g[l], FOX_HEADS)[None], gm=gm,
        fox_bound=1.02 * FOX_HEAD_DIM ** 0.5 * jnp.max(jnp.abs(q_norm_g[l])) * jnp.max(jnp.abs(k_norm_g[l])),
        conv_w=conv_w[l], conv_b=conv_b[l][None],
        dtb_c=_lane_pad(dt_bias[l], DT_ROW0)[None], dtb_r=dt_bias[l].astype(F32)[:, None],
        a_c=_lane_pad(a_neg, DT_ROW0)[None], a_r=a_neg[:, None],
        dsk_x=jnp.repeat(d_skip[l].astype(F32), SSD_HEAD_DIM)[None], ssd_g=ssd_norm_g[l][None], e_mat=e_mat,
        fb_c=_lane_pad(fgate_b[l], F_ROW0)[None], fb_r=_lane_pad(fgate_b[l], 0, 8)[:, None],
        w_s=w_s[l], bs_t=jnp.zeros((MLP_CHUNK, LANES), F32).at[:, :MLP_GROUPS].set(b_s[l].T),
        w_out=w_out[l].astype(BF16),
        keys=peer_keys[l].reshape(PEER_HEADS * 2, PEER_KEYS, PEER_HALF).astype(BF16),
    )


def _stream_mixers(x, peer, g2_prev, mod, p, hist8, s0t, past):
    sh1, sc1, g1, sh2, sc2, _ = mod
    b, t, _ = x.shape
    x, (z, xbc, qb, kn, kb, v, vb, ug, vn, sm, smt) = _in_proj(
        x, peer, g2_prev, sh1, sc1, p['norm1_g'], p['w_r'], p['ws_t'], p['gq'], p['gk'], p['gm'])
    L = SSD_CHUNK_PROMPT if t % SSD_CHUNK_PROMPT == 0 else t
    y_ssd, s_fin, c_fin = _ssd(xbc, z, sm, smt, hist8, s0t, p['conv_w'], p['conv_b'], p['dtb_c'], p['dtb_r'],
                               p['a_c'], p['a_r'], p['dsk_x'], p['ssd_g'], p['e_mat'], L)
    zeros8 = jnp.zeros((b, 8, LANES), F32)
    if past is None:
        logf_c, f_col, f_row, _, _ = _fcum(sm, smt, F_ROW0 // 8, p['fb_c'], p['fb_r'], zeros8, zeros8, True)
        y_fox = _fox_prompt_t(qb, kb, vb, f_col, f_row, p['fox_bound'])
    else:
        pk, pv, layer, plf_col, plf_row = past
        _, _, fp_row, end_c, end_r = _fcum(plf_col, plf_row, 0, p['fb_c'], p['fb_r'], zeros8, zeros8, False)
        logf_c, f_col, f_row, _, _ = _fcum(sm, smt, F_ROW0 // 8, p['fb_c'], p['fb_r'], end_c, end_r, True)
        y_fox = _fox_sample(qb, f_col, pk, pv, layer, fp_row, kb, vb, f_row)
    lm = MLP_CHUNK if t % MLP_CHUNK == 0 else t
    y_mlp = _chunk_mlp(ug, vn, p['w_s'][:, :lm, :lm], p['bs_t'][:lm])
    x1, h2 = _out_proj(x, y_ssd, y_fox, y_mlp, g1, sh2, sc2, p['norm2_g'], p['w_out'])
    logf = logf_c[:, :, F_ROW0:F_ROW0 + FOX_HEADS]
    new_ssm = s_fin.reshape(b, SSD_STATE, SSD_HEADS, SSD_HEAD_DIM).transpose(0, 2, 3, 1)
    new_conv = c_fin[:, 8 - (CONV_W - 1):, :]
    kc = kn.reshape(b, t, FOX_HEADS, FOX_HEAD_DIM)
    vc = v.reshape(b, t, FOX_HEADS, FOX_HEAD_DIM)
    return x1, h2, (kc, vc, logf, new_ssm, new_conv, vn)


def _peer(h2, p, tables, layer):
    b, t, _ = h2.shape
    n_tok = b * t
    n_pad = -n_tok % PEER_TOKEN_TILE
    flat = h2.reshape(n_tok, D_MODEL)
    if n_pad:
        flat = jnp.concatenate([flat, jnp.zeros((n_pad, D_MODEL), BF16)], axis=0)
    wqt_all, u_all, vt_all = tables
    ea, la, eb, pb = _peer_select(flat, wqt_all, layer, p['keys'])
    out = _peer_dense(flat, u_all, vt_all, layer, ea, la, eb, pb)
    return out[:n_tok].reshape(b, t, D_MODEL)


def kernel(x_prompt, x_sample, c_prompt, c_sample, cache_fox_k, cache_fox_v, cache_fox_logf, state_ssm, state_conv, norm1_g, norm2_g, w_ada, b_ada, w_in, conv_w, conv_b, dt_bias, a_log, d_skip, ssd_norm_g, q_norm_g, k_norm_g, fgate_b, w_s, b_s, w_out, peer_wq, peer_keys, peer_u, peer_v):
    depth = w_ada.shape[0]
    bp, tp, _ = x_prompt.shape
    bs, ts, _ = x_sample.shape
    past_len = cache_fox_k.shape[2]
    mod_all = _modulation(jnp.concatenate([c_prompt, c_sample], axis=0).astype(F32), w_ada, b_ada)

    peer_tables = (_to_bf16_t(peer_wq, D_MODEL, 512), _to_bf16(peer_u, 1024), _to_bf16_t(peer_v, 512, D_MODEL))

    past_k_all = cache_fox_k.reshape(depth, bs, past_len, FOX_INNER)
    past_v_all = cache_fox_v.reshape(depth, bs, past_len, FOX_INNER)

    xp, xs = x_prompt, x_sample
    peer_p = peer_s = g2p = g2s = None
    outs = [[] for _ in range(11)]
    for l in range(depth):
        p = _layer_params(l, norm1_g, norm2_g, w_in, conv_w, conv_b, dt_bias, a_log, d_skip, ssd_norm_g, q_norm_g,
                          k_norm_g, fgate_b, w_s, b_s, w_out, peer_keys)
        mods = [m[:, None, :] for m in jnp.split(mod_all[l], 6, axis=-1)]
        mod_p = [m[:bp] for m in mods]
        mod_s = [m[bp:] for m in mods]
        hist_p = jnp.zeros((bp, 8, CONV_CH), F32)
        s0_p = jnp.zeros((bp, SSD_STATE, SSD_INNER), F32)
        hist_s = jnp.concatenate([jnp.zeros((bs, 8 - (CONV_W - 1), CONV_CH), F32), state_conv[l].astype(F32)], axis=1)
        s0_s = state_ssm[l].astype(F32).transpose(0, 3, 1, 2).reshape(bs, SSD_STATE, SSD_INNER)
        plf = cache_fox_logf[l].astype(F32)
        plf_col = jnp.pad(plf, ((0, 0), (0, 0), (F_ROW0, LANES - F_ROW0 - FOX_HEADS)))
        plf_row = jnp.pad(plf.transpose(0, 2, 1), ((0, 0), (0, 8 - FOX_HEADS), (0, 0)))
        past = (past_k_all, past_v_all, l, plf_col, plf_row)

        x1p, h2p, st_p = _stream_mixers(xp, peer_p, g2p, mod_p, p, hist_p, s0_p, None)
        x1s, h2s, st_s = _stream_mixers(xs, peer_s, g2s, mod_s, p, hist_s, s0_s, past)

        peer_p = _peer(h2p, p, peer_tables, l)
        peer_s = _peer(h2s, p, peer_tables, l)
        xp, xs, g2p, g2s = x1p, x1s, mod_p[5], mod_s[5]
        for i in range(5):
            outs[i].append(st_p[i])
        for i in range(6):
            outs[5 + i].append(st_s[i])

    yp = _residual(xp, peer_p, g2p)
    ys = _residual(xs, peer_s, g2s)
    return (yp, ys) + tuple(jnp.stack(o) for o in outs)
```

```python
import functools

import jax
import jax.numpy as jnp
from jax import lax
from jax.experimental import pallas as pl
from jax.experimental.pallas import tpu as pltpu

F32 = jnp.float32
BF16 = jnp.bfloat16
EPS = 1e-6

D_MODEL = 1024
SSD_HEADS = 8
SSD_HEAD_DIM = 64
SSD_INNER = SSD_HEADS * SSD_HEAD_DIM
SSD_GROUPS = 2
SSD_STATE = 64
CONV_W = 4
CONV_CH = SSD_INNER + 2 * SSD_GROUPS * SSD_STATE
FOX_HEADS = 4
FOX_HEAD_DIM = 64
FOX_INNER = FOX_HEADS * FOX_HEAD_DIM
MLP_GROUPS = 4
MLP_GROUP_DIM = 64
MLP_INNER = MLP_GROUPS * MLP_GROUP_DIM
MLP_CHUNK = 128
FOX_PROMPT_TILE = 512
FOX_CHAINS = 4
FCUM_TILE = 2048
FCUM_SUB = 512
SSD_CHUNK_PROMPT = 128
PEER_HEADS = 8
PEER_KEYS = 128
PEER_EXPERTS = PEER_KEYS * PEER_KEYS
PEER_HALF = 128
PEER_TOPK = 16
PEER_TOKEN_TILE = 512
PEER_EXPERT_CHUNK = 2048
LANES = 128
MXU_DIM = 256
PEER_DENSE_PARTS = 4
PEER_SUB = 128
SMALL_COLS = LANES
SMALL_ROWS = 16
DT_ROW0, F_ROW0 = 0, 8
NEG = -1e30
EXP_UNDERFLOW = 104.0
VMEM_LIMIT = 56 * 1024 * 1024

_Z0, _XBC0, _Q0, _K0, _V0, _U0, _VM0, _SM0 = 0, 512, 1280, 1536, 1792, 2048, 2304, 2560
PROJ_COLS = _SM0 + SMALL_COLS


def _cparams(sem, flags=None):
    return pltpu.CompilerParams(dimension_semantics=sem, vmem_limit_bytes=VMEM_LIMIT, flags=flags)


def _dot(a, b):
    return jnp.dot(a, b, preferred_element_type=F32)


def _dot_nt(a, b):
    return lax.dot_general(a, b, (((1,), (1,)), ((), ())), preferred_element_type=F32)


def _dot_tn(a, b):
    return lax.dot_general(a, b, (((0,), (0,)), ((), ())), preferred_element_type=F32)


def _split2(x):
    hi = x.astype(BF16)
    lo = (x - hi.astype(F32)).astype(BF16)
    return hi, lo


def _split3(x):
    hi = x.astype(BF16)
    r = x - hi.astype(F32)
    mid = r.astype(BF16)
    lo = (r - mid.astype(F32)).astype(BF16)
    return hi, mid, lo


def _dot3_l(x, w):
    hi, mid, lo = _split3(x)
    return _dot(hi, w) + _dot(mid, w) + _dot(lo, w)


def _dot3_r(w, x):
    hi, mid, lo = _split3(x)
    return _dot(w, hi) + _dot(w, mid) + _dot(w, lo)


def _sigmoid(x):
    return 1.0 / (1.0 + jnp.exp(-x))


def _silu(x):
    return x * _sigmoid(x)


def _softplus(x):
    return jnp.maximum(x, 0.0) + jnp.log1p(jnp.exp(-jnp.abs(x)))


_GELU_A = 2.0 * 0.7978845608028654
_GELU_B = _GELU_A * 0.044715


def _gelu(x):
    z2 = x * (_GELU_A + _GELU_B * (x * x))
    return x / (1.0 + jnp.exp(-z2))


def _iota(shape, dim):
    return lax.broadcasted_iota(jnp.int32, shape, dim)


def _cast_kernel(x_ref, o_ref):
    o_ref[0] = x_ref[0].astype(BF16)


def _cast_t_kernel(x_ref, o_ref):
    o_ref[0] = x_ref[0].T.astype(BF16)


def _to_bf16(x, rows):
    n, r, c = x.shape
    return pl.pallas_call(
        _cast_kernel,
        grid=(n, r // rows),
        in_specs=[pl.BlockSpec((1, rows, c), lambda l, i: (l, i, 0))],
        out_specs=pl.BlockSpec((1, rows, c), lambda l, i: (l, i, 0)),
        out_shape=jax.ShapeDtypeStruct((n, r, c), BF16),
        compiler_params=_cparams(("arbitrary", "arbitrary")),
        name="cast_bf16",
    )(x)


def _to_bf16_t(x, rows, cols):
    n, r, c = x.shape
    return pl.pallas_call(
        _cast_t_kernel,
        grid=(n, r // rows, c // cols),
        in_specs=[pl.BlockSpec((1, rows, cols), lambda l, i, j: (l, i, j))],
        out_specs=pl.BlockSpec((1, cols, rows), lambda l, i, j: (l, j, i)),
        out_shape=jax.ShapeDtypeStruct((n, c, r), BF16),
        compiler_params=_cparams(("arbitrary", "arbitrary", "arbitrary")),
        name="cast_bf16_transposed",
    )(x)


def _mod_kernel(c_ref, w_ref, b_ref, o_ref):
    c = c_ref[...]
    o_ref[0] = jnp.dot(_silu(c), w_ref[0], preferred_element_type=F32,
                       precision=lax.Precision.HIGHEST) + b_ref[0]


def _modulation(c_all, w_ada, b_ada):
    depth, _, n6 = w_ada.shape
    bc = c_all.shape[0]
    tn = 1536
    return pl.pallas_call(
        _mod_kernel,
        grid=(depth, n6 // tn),
        in_specs=[pl.BlockSpec((bc, D_MODEL), lambda l, j: (0, 0)),
                  pl.BlockSpec((1, D_MODEL, tn), lambda l, j: (l, 0, j)),
                  pl.BlockSpec((1, 1, tn), lambda l, j: (l, 0, j))],
        out_specs=pl.BlockSpec((1, bc, tn), lambda l, j: (l, 0, j)),
        out_shape=jax.ShapeDtypeStruct((depth, bc, n6), F32),
        compiler_params=_cparams(("arbitrary", "arbitrary")),
        name="adaln_mod",
    )(c_all, w_ada, b_ada.reshape(depth, 1, n6))


def _in_kernel(with_peer, *refs):
    if with_peer:
        x_ref, p_ref, g2_ref = refs[:3]
        refs = refs[3:]
    else:
        x_ref = refs[0]
        refs = refs[1:]
    (sh_ref, sc_ref, g_ref, w_ref, wst_ref, gq_ref, gk_ref, gm_ref) = refs[:8]
    outs = refs[8:]
    if with_peer:
        xo_ref = outs[0]
        outs = outs[1:]
    (z_ref, xbc_ref, q_ref, k_ref, kb_ref, v_ref, vb_ref, ug_ref, vn_ref, sm_ref, smt_ref) = outs

    x = x_ref[0]
    if with_peer:
        x = x + g2_ref[0] * p_ref[0]
        xo_ref[0] = x
    ms = jnp.mean(x * x, axis=-1, keepdims=True)
    h = x * lax.rsqrt(ms + EPS) * g_ref[...]
    h = h * (1.0 + sc_ref[0]) + sh_ref[0]
    hb = h.astype(BF16)
    proj = _dot(hb, w_ref[...])
    z_ref[0] = proj[:, _Z0:_XBC0]
    xbc_ref[0] = proj[:, _XBC0:_Q0]
    q = proj[:, _Q0:_K0]
    k = proj[:, _K0:_V0]
    v = proj[:, _V0:_U0]
    u = proj[:, _U0:_VM0]
    vm = proj[:, _VM0:_SM0]
    sm_ref[0] = proj[:, _SM0:PROJ_COLS]
    smt_ref[0] = _dot_nt(wst_ref[...], hb)
    gm = gm_ref[...]

    def gmean(y):
        hi, lo = _split2(y)
        return _dot(hi, gm) + _dot(lo, gm)

    qn = q * lax.rsqrt(gmean(q * q) + EPS) * gq_ref[...]
    kn = k * lax.rsqrt(gmean(k * k) + EPS) * gk_ref[...]
    q_ref[0] = qn.astype(BF16)
    k_ref[0] = kn
    kb_ref[0] = kn.astype(BF16)
    v_ref[0] = v
    vb_ref[0] = v.astype(BF16)
    ug_ref[0] = _gelu(u)
    gv = _gelu(vm)
    mu = gmean(gv)
    cen = gv - mu
    var = gmean(cen * cen)
    vn_ref[0] = cen * lax.rsqrt(var + EPS)


def _in_proj(x, peer, g2, sh, sc, g, w_r, ws_t, gq, gk, gm):
    b, t, _ = x.shape
    tm = min(t, 512)
    with_peer = peer is not None
    tok = lambda c: pl.BlockSpec((1, tm, c), lambda i, j: (i, j, 0))
    per_b = pl.BlockSpec((1, 1, D_MODEL), lambda i, j: (i, 0, 0))
    full = lambda a: pl.BlockSpec(a.shape, lambda i, j: (0,) * a.ndim)
    in_specs = [tok(D_MODEL)]
    args = [x]
    if with_peer:
        in_specs += [tok(D_MODEL), per_b]
        args += [peer, g2]
    in_specs += [per_b, per_b, full(g), full(w_r), full(ws_t), full(gq), full(gk), full(gm)]
    args += [sh, sc, g, w_r, ws_t, gq, gk, gm]
    out_cols = [(SSD_INNER, F32), (CONV_CH, F32), (FOX_INNER, BF16), (FOX_INNER, F32), (FOX_INNER, BF16),
                (FOX_INNER, F32), (FOX_INNER, BF16), (MLP_INNER, F32), (MLP_INNER, F32), (SMALL_COLS, F32)]
    out_specs = [tok(c) for c, _ in out_cols]
    out_shape = [jax.ShapeDtypeStruct((b, t, c), dt) for c, dt in out_cols]
    out_specs.append(pl.BlockSpec((1, SMALL_ROWS, tm), lambda i, j: (i, 0, j)))
    out_shape.append(jax.ShapeDtypeStruct((b, SMALL_ROWS, t), F32))
    if with_peer:
        out_specs = [tok(D_MODEL)] + out_specs
        out_shape = [jax.ShapeDtypeStruct((b, t, D_MODEL), F32)] + out_shape
    res = pl.pallas_call(
        functools.partial(_in_kernel, with_peer),
        grid=(b, t // tm),
        in_specs=in_specs, out_specs=out_specs, out_shape=out_shape,
        compiler_params=_cparams(("arbitrary", "arbitrary")),
        name="in_proj",
    )(*args)
    if with_peer:
        return res[0], res[1:]
    return x, res


def _ssd_kernel(L, xbc_ref, z_ref, sm_ref, smt_ref, hist_ref, s0_ref, cw_ref, cb_ref, dtbc_ref, dtbr_ref,
                ac_ref, ar_ref, dsk_ref, ng_ref, e_ref, y_ref, sfin_ref, cfin_ref, xpad, st, ybuf):
    c = pl.program_id(1)
    nc = pl.num_programs(1)

    @pl.when(c == 0)
    def _():
        xpad[0:8, :] = hist_ref[0]
        st[...] = s0_ref[0]

    xpad[8:8 + L, :] = xbc_ref[0]
    conv = cb_ref[...]
    for tap in range(CONV_W):
        conv = conv + xpad[5 + tap:5 + tap + L, :] * cw_ref[tap:tap + 1, :]
    tail = xpad[L:L + 8, :]
    xpad[0:8, :] = tail
    xc = _silu(conv)
    xs = xc[:, 0:SSD_INNER]

    r_i = _iota((L, L), 0)
    c_i = _iota((L, L), 1)
    causal = r_i >= c_i
    tri = jnp.where(causal, 1.0, 0.0).astype(BF16)
    triu = jnp.where(r_i <= c_i, 1.0, 0.0).astype(BF16)

    dtc = _softplus(sm_ref[0] + dtbc_ref[...])
    acum_c = _dot3_r(tri, dtc * ac_ref[...])
    e = e_ref[...]
    acum_x = _dot3_l(acum_c, e)
    dt_x = _dot3_l(dtc, e)
    dtr = _softplus(smt_ref[0][DT_ROW0:DT_ROW0 + 8, :] + dtbr_ref[...])
    acum_r = _dot3_l(dtr * ar_ref[...], triu)

    bmat = [xc[:, SSD_INNER + SSD_STATE * g:SSD_INNER + SSD_STATE * (g + 1)].astype(BF16) for g in range(SSD_GROUPS)]
    c0 = SSD_INNER + SSD_GROUPS * SSD_STATE
    cmat = [xc[:, c0 + SSD_STATE * g:c0 + SSD_STATE * (g + 1)].astype(BF16) for g in range(SSD_GROUPS)]
    cb = [_dot_nt(cmat[g], bmat[g]) for g in range(SSD_GROUPS)]
    gw = SSD_INNER // SSD_GROUPS
    stb = st[...].astype(BF16)
    y_off = jnp.concatenate([_dot(cmat[g], stb[:, gw * g:gw * (g + 1)]) for g in range(SSD_GROUPS)], axis=1)
    xsb = xs.astype(BF16)
    hpg = SSD_HEADS // SSD_GROUPS
    mixes = []
    for h in range(SSD_HEADS):
        seg = acum_c[:, h:h + 1] - acum_r[h:h + 1, :]
        dec = jnp.where(causal, jnp.exp(jnp.minimum(seg, 0.0)), 0.0)
        mixes.append((cb[h // hpg] * dec * dtr[h:h + 1, :]).astype(BF16))
    heads = [_dot(mixes[h], xsb[:, SSD_HEAD_DIM * h:SSD_HEAD_DIM * (h + 1)]) for h in range(SSD_HEADS)]
    for h in range(SSD_HEADS):
        ybuf[:, SSD_HEAD_DIM * h:SSD_HEAD_DIM * (h + 1)] = heads[h]

    y = ybuf[...] + y_off * jnp.exp(acum_x) + dsk_ref[...] * xs
    a_end = acum_x[L - 1:L, :]
    xw = (xs * dt_x * jnp.exp(a_end - acum_x)).astype(BF16)
    new_states = jnp.concatenate([_dot_tn(bmat[g], xw[:, gw * g:gw * (g + 1)]) for g in range(SSD_GROUPS)], axis=1)
    st[...] = st[...] * jnp.exp(a_end) + new_states

    yg = y * _silu(z_ref[0])
    ms = jnp.mean(yg * yg, axis=-1, keepdims=True)
    y_ref[0] = yg * lax.rsqrt(ms + EPS) * ng_ref[...]

    @pl.when(c == nc - 1)
    def _():
        sfin_ref[0] = st[...]
        cfin_ref[0] = tail


def _ssd(xbc, z, sm, smt, hist8, s0t, cw, cb, dtb_c, dtb_r, a_c, a_r, dsk_x, ng, e_mat, L):
    b, t, _ = xbc.shape
    tok = lambda c: pl.BlockSpec((1, L, c), lambda i, j: (i, j, 0))
    full = lambda a: pl.BlockSpec(a.shape, lambda i, j: (0,) * a.ndim)
    per_b = lambda a: pl.BlockSpec((1,) + a.shape[1:], lambda i, j: (i,) + (0,) * (a.ndim - 1))
    return pl.pallas_call(
        functools.partial(_ssd_kernel, L),
        grid=(b, t // L),
        in_specs=[tok(CONV_CH), tok(SSD_INNER), tok(SMALL_COLS),
                  pl.BlockSpec((1, SMALL_ROWS, L), lambda i, j: (i, 0, j)),
                  per_b(hist8), per_b(s0t), full(cw), full(cb), full(dtb_c), full(dtb_r), full(a_c), full(a_r),
                  full(dsk_x), full(ng), full(e_mat)],
        out_specs=[tok(SSD_INNER),
                   pl.BlockSpec((1, SSD_STATE, SSD_INNER), lambda i, j: (i, 0, 0)),
                   pl.BlockSpec((1, 8, CONV_CH), lambda i, j: (i, 0, 0))],
        out_shape=[jax.ShapeDtypeStruct((b, t, SSD_INNER), F32),
                   jax.ShapeDtypeStruct((b, SSD_STATE, SSD_INNER), F32),
                   jax.ShapeDtypeStruct((b, 8, CONV_CH), F32)],
        scratch_shapes=[pltpu.VMEM((L + 8, CONV_CH), F32), pltpu.VMEM((SSD_STATE, SSD_INNER), F32),
                        pltpu.VMEM((L, SSD_INNER), F32)],
        compiler_params=_cparams(("arbitrary", "arbitrary")),
        name="ssd_scan",
    )(xbc, z, sm, smt, hist8, s0t, cw, cb, dtb_c, dtb_r, a_c, a_r, dsk_x, ng, e_mat)


def _fcum_kernel(tf, activate, colsrc_ref, rowsrc_ref, fbc_ref, fbr_ref, initc_ref, initr_ref,
                 lfc_ref, fc_ref, fr_ref, endc_ref, endr_ref, carc, carr):
    j = pl.program_id(1)

    @pl.when(j == 0)
    def _():
        carc[...] = initc_ref[0]
        carr[...] = initr_ref[0]

    ts = min(tf, FCUM_SUB)
    r_i = _iota((ts, ts), 0)
    c_i = _iota((ts, ts), 1)
    tri = jnp.where(r_i >= c_i, 1.0, 0.0).astype(BF16)
    triu = jnp.where(r_i <= c_i, 1.0, 0.0).astype(BF16)
    car_c = carc[0:1, :]
    car_r = carr[:, 0:1]
    for k in range(tf // ts):
        xc = colsrc_ref[0, k * ts:(k + 1) * ts, :]
        xr = rowsrc_ref[0, :, k * ts:(k + 1) * ts]
        if activate:
            xc = -_softplus(-(xc + fbc_ref[...]))
            xr = -_softplus(-(xr + fbr_ref[...]))
        fcol = car_c + _dot3_r(tri, xc)
        frow = car_r + _dot3_l(xr, triu)
        lfc_ref[0, k * ts:(k + 1) * ts, :] = xc
        fc_ref[0, k * ts:(k + 1) * ts, :] = fcol
        fr_ref[0, :, k * ts:(k + 1) * ts] = frow
        car_c = fcol[ts - 1:ts, :]
        car_r = frow[:, ts - 1:ts]
    carc[...] = jnp.broadcast_to(car_c, carc.shape)
    carr[...] = jnp.broadcast_to(car_r, carr.shape)
    endc_ref[0] = carc[...]
    endr_ref[0] = carr[...]


def _fcum(colsrc, rowsrc, row_block, fb_c, fb_r, init_c, init_r, activate):
    b, t, _ = colsrc.shape
    tf = min(t, FCUM_TILE)
    full = lambda a: pl.BlockSpec(a.shape, lambda i, j: (0,) * a.ndim)
    per_b = lambda a: pl.BlockSpec((1,) + a.shape[1:], lambda i, j: (i,) + (0,) * (a.ndim - 1))
    return pl.pallas_call(
        functools.partial(_fcum_kernel, tf, activate),
        grid=(b, t // tf),
        in_specs=[pl.BlockSpec((1, tf, LANES), lambda i, j: (i, j, 0)),
                  pl.BlockSpec((1, 8, tf), lambda i, j: (i, row_block, j)),
                  full(fb_c), full(fb_r), per_b(init_c), per_b(init_r)],
        out_specs=[pl.BlockSpec((1, tf, LANES), lambda i, j: (i, j, 0)),
                   pl.BlockSpec((1, tf, LANES), lambda i, j: (i, j, 0)),
                   pl.BlockSpec((1, 8, tf), lambda i, j: (i, 0, j)),
                   pl.BlockSpec((1, 8, LANES), lambda i, j: (i, 0, 0)),
                   pl.BlockSpec((1, 8, LANES), lambda i, j: (i, 0, 0))],
        out_shape=[jax.ShapeDtypeStruct((b, t, LANES), F32), jax.ShapeDtypeStruct((b, t, LANES), F32),
                   jax.ShapeDtypeStruct((b, 8, t), F32), jax.ShapeDtypeStruct((b, 8, LANES), F32),
                   jax.ShapeDtypeStruct((b, 8, LANES), F32)],
        scratch_shapes=[pltpu.VMEM((8, LANES), F32), pltpu.VMEM((8, LANES), F32)],
        compiler_params=_cparams(("arbitrary", "arbitrary")),
        name="forget_cumsum",
    )(colsrc, rowsrc, fb_c, fb_r, init_c, init_r)


def _softmax_stats(s, carry):
    m, l, acc = carry
    tq, tk = s.shape
    m_new = jnp.maximum(m, jnp.max(s, axis=-1, keepdims=True))
    p = jnp.exp(s - m_new)
    alpha = jnp.exp(m - m_new)
    if tk % LANES == 0:
        psum = p[:, 0:LANES]
        for c in range(1, tk // LANES):
            psum = psum + p[:, c * LANES:(c + 1) * LANES]
    else:
        psum = jnp.where(_iota((tq, LANES), 1) == 0, jnp.sum(p, axis=-1, keepdims=True), 0.0)
    return m_new, alpha * l + psum, alpha * acc, p.astype(BF16)


def _softmax_init(tq):
    return tuple((jnp.full((tq, 1), NEG, F32), jnp.zeros((tq, LANES), F32), jnp.zeros((tq, FOX_HEAD_DIM), F32))
                 for _ in range(FOX_HEADS))


def _head_slice(h):
    return slice(FOX_HEAD_DIM * h, FOX_HEAD_DIM * (h + 1))


def _attend_block(qs, fqs, kb, vb, fk, mask, carries):
    heads = range(FOX_HEADS)
    scores = [_dot_nt(qs[h], kb[:, _head_slice(h)]) for h in heads]
    stats = []
    for h in heads:
        s = scores[h] + fqs[h] - fk[h:h + 1, :]
        if mask is not None:
            s = jnp.where(mask, s, NEG)
        stats.append(_softmax_stats(s, carries[h]))
    return tuple((m, l, acc + _dot(p, vb[:, _head_slice(h)])) for h, (m, l, acc, p) in zip(heads, stats))


def _attend_finish(carries, o_ref):
    for h in range(FOX_HEADS):
        _, l, acc = carries[h]
        o_ref[0, :, _head_slice(h)] = acc / jnp.sum(l, axis=-1, keepdims=True)


def _bf16_terms(x):
    hi, mid, lo = _split3(x)
    return hi.astype(F32), mid.astype(F32), lo.astype(F32)


def _fox_prep_kernel(q_ref, k_ref, v_ref, fc_ref, fr_ref, ka_ref, qa_ref, vt_ref):
    tm = q_ref.shape[1]
    qt = q_ref[0].astype(F32).T
    k = k_ref[0].astype(F32)
    vt_ref[0, 0] = v_ref[0].astype(F32).T.astype(BF16)
    lane = _iota((tm, LANES), 1)
    row = _iota((FOX_HEAD_DIM, tm), 0)
    d = FOX_HEAD_DIM
    for h in range(FOX_HEADS):
        pair, odd = divmod(h, 2)
        kc = k[:, pair * LANES:(pair + 1) * LANES]
        if odd:
            kc = pltpu.roll(kc, d, axis=1)
        ka = jnp.where(lane < d, kc, 0.0)
        for i, term in enumerate(_bf16_terms(-fc_ref[0][:, F_ROW0 + h:F_ROW0 + h + 1])):
            ka = jnp.where(lane == d + i, term, ka)
        ka = jnp.where(jnp.logical_and(lane >= d + 3, lane < d + 6), 1.0, ka)
        ka_ref[0, h] = ka.astype(BF16)
        tail = jnp.where(row < 3, 1.0, 0.0)
        for i, term in enumerate(_bf16_terms(fr_ref[0][h:h + 1, :])):
            tail = jnp.where(row == 3 + i, term, tail)
        qa_ref[0, h, 0:d, :] = qt[d * h:d * (h + 1), :].astype(BF16)
        qa_ref[0, h, d:2 * d, :] = tail.astype(BF16)


def _fox_prep(qb, kb, vb, fcol, frow, tm):
    b, t, _ = qb.shape
    tok = lambda c: pl.BlockSpec((1, tm, c), lambda i, j: (i, j, 0))
    return pl.pallas_call(
        _fox_prep_kernel,
        grid=(b, t // tm),
        in_specs=[tok(FOX_INNER), tok(FOX_INNER), tok(FOX_INNER), tok(LANES),
                  pl.BlockSpec((1, 8, tm), lambda i, j: (i, 0, j))],
        out_specs=[pl.BlockSpec((1, FOX_HEADS, tm, LANES), lambda i, j: (i, 0, j, 0)),
                   pl.BlockSpec((1, FOX_HEADS, LANES, tm), lambda i, j: (i, 0, 0, j)),
                   pl.BlockSpec((1, 1, FOX_INNER, tm), lambda i, j: (i, j, 0, 0))],
        out_shape=[jax.ShapeDtypeStruct((b, FOX_HEADS, t, LANES), BF16),
                   jax.ShapeDtypeStruct((b, FOX_HEADS, LANES, t), BF16),
                   jax.ShapeDtypeStruct((b, t // tm, FOX_INNER, tm), BF16)],
        compiler_params=_cparams(("arbitrary", "arbitrary")),
        name="fox_prep",
    )(qb, kb, vb, fcol, frow)


def _fox_prompt_t_kernel(tq, nk, qa_ref, ka_ref, vt_ref, fend_ref, ffirst_ref, thr_ref, o_ref):
    b = pl.program_id(0)
    qi = pl.program_id(1)
    half = tq // FOX_CHAINS
    d = FOX_HEAD_DIM
    diag = pl.multiple_of(qi * tq, tq)
    init = (jnp.full((1, half), NEG, F32), jnp.zeros((1, half), F32), jnp.zeros((d, half), F32))

    def update(blocks, qas, masks, carries):
        scores = [[_dot(ka, qa) for qa in qas] for ka, _ in blocks]
        for (_, vt), block_scores in zip(blocks, scores):
            stats = []
            for s, mask, (m, l, acc) in zip(block_scores, masks, carries):
                if mask is not None:
                    s = jnp.where(mask, s, NEG)
                m_new = jnp.maximum(m, jnp.max(s, axis=0, keepdims=True))
                p = jnp.exp(s - m_new)
                alpha = jnp.exp(m - m_new)
                stats.append((m_new, alpha * l + jnp.sum(p, axis=0, keepdims=True), alpha * acc, p.astype(BF16)))
            carries = tuple((m, l, acc + _dot(vt, p)) for m, l, acc, p in stats)
        return carries

    def key_block(h, j):
        return (ka_ref[0, h, pl.ds(pl.multiple_of(j * tq, tq), tq), :], vt_ref[0, j][d * h:d * (h + 1), :])

    for h in range(FOX_HEADS):
        base = (b * FOX_HEADS + h) * nk
        slack = thr_ref[0] + ffirst_ref[base + qi]
        n_live = lax.fori_loop(0, qi, lambda j, c: c + (slack - fend_ref[base + j] >= 0.0).astype(jnp.int32), 0)
        qas = [qa_ref[0, h, :, r2 * half:(r2 + 1) * half] for r2 in range(FOX_CHAINS)]
        ka_d = ka_ref[0, h, pl.ds(diag, tq), :]
        vt_d = vt_ref[0, qi][d * h:d * (h + 1), :]
        causal = [_iota((tq, half), 0) <= (r2 * half + _iota((tq, half), 1)) for r2 in range(FOX_CHAINS)]
        carries = update([(ka_d, vt_d)], qas, causal, [init] * FOX_CHAINS)
        no_mask = [None] * FOX_CHAINS

        def one(t, carries, qas=qas, h=h):
            return update([key_block(h, qi - 1 - t)], qas, no_mask, carries)

        def two(t, carries, qas=qas, h=h):
            j = qi - 1 - (n_live % 2) - 2 * t
            return update([key_block(h, j), key_block(h, j - 1)], qas, no_mask, carries)

        carries = lax.fori_loop(0, n_live % 2, one, carries)
        carries = lax.fori_loop(0, n_live // 2, two, carries)
        for r2 in range(FOX_CHAINS):
            _, l, acc = carries[r2]
            o_ref[0, r2 * half:(r2 + 1) * half, d * h:d * (h + 1)] = (acc / l).T


def _fox_prompt_t(qb, kb, vb, fcol, frow, score_bound):
    b, t, _ = qb.shape
    tq = min(t, FOX_PROMPT_TILE)
    nk = t // tq
    ka, qa, vt = _fox_prep(qb, kb, vb, fcol, frow, tq)
    f_heads = frow[:, :FOX_HEADS, :].reshape(b, FOX_HEADS, nk, tq)
    f_end = f_heads[:, :, :, tq - 1].reshape(-1)
    f_first = f_heads[:, :, :, 0].reshape(-1)
    thr = (2.0 * score_bound + EXP_UNDERFLOW).reshape(1).astype(F32)
    smem = pl.BlockSpec(memory_space=pltpu.SMEM)
    return pl.pallas_call(
        functools.partial(_fox_prompt_t_kernel, tq, nk),
        grid=(b, nk),
        in_specs=[pl.BlockSpec((1, FOX_HEADS, LANES, tq), lambda i, j: (i, 0, 0, j)),
                  pl.BlockSpec((1, FOX_HEADS, t, LANES), lambda i, j: (i, 0, 0, 0)),
                  pl.BlockSpec((1, nk, FOX_INNER, tq), lambda i, j: (i, 0, 0, 0)),
                  smem, smem, smem],
        out_specs=pl.BlockSpec((1, tq, FOX_INNER), lambda i, j: (i, j, 0)),
        out_shape=jax.ShapeDtypeStruct((b, t, FOX_INNER), F32),
        compiler_params=_cparams(("arbitrary", "arbitrary")),
        name="fox_prompt",
    )(qa, ka, vt, f_end, f_first, thr)


def _fox_sample_kernel(tq, tk, npast, q_ref, fq_ref, pk_ref, pv_ref, fpk_ref, k_ref, v_ref, fk_ref, o_ref):
    q_all = q_ref[0]
    fq_all = fq_ref[0]
    qs = [q_all[:, _head_slice(h)] for h in range(FOX_HEADS)]
    fqs = [fq_all[:, F_ROW0 + h:F_ROW0 + h + 1] for h in range(FOX_HEADS)]
    causal = _iota((tq, tq), 0) >= _iota((tq, tq), 1)

    def past_block(j, carries):
        start = pl.multiple_of(j * tk, tk)
        return _attend_block(qs, fqs, pk_ref[0, pl.ds(start, tk), :].astype(BF16),
                             pv_ref[0, pl.ds(start, tk), :].astype(BF16), fpk_ref[0, j], None, carries)

    carries = lax.fori_loop(0, npast, past_block, _softmax_init(tq))
    _attend_finish(_attend_block(qs, fqs, k_ref[0], v_ref[0], fk_ref[0], causal, carries), o_ref)


def _fox_sample(qb, fcol, past_k, past_v, layer, fpast_row, kb, vb, frow):
    b, t, _ = qb.shape
    p = past_k.shape[2]
    tk = min(p, 512)
    npast = p // tk
    fpk = fpast_row.reshape(b, 8, npast, tk).transpose(0, 2, 1, 3)
    bspec = lambda a: pl.BlockSpec((1,) + a.shape[1:], lambda i: (i,) + (0,) * (a.ndim - 1))
    cache = pl.BlockSpec((None, 1, p, FOX_INNER), lambda i: (layer, i, 0, 0))
    args = (qb, fcol, past_k, past_v, fpk, kb, vb, frow)
    return pl.pallas_call(
        functools.partial(_fox_sample_kernel, t, tk, npast),
        grid=(b,),
        in_specs=[bspec(qb), bspec(fcol), cache, cache, bspec(fpk), bspec(kb), bspec(vb), bspec(frow)],
        out_specs=pl.BlockSpec((1, t, FOX_INNER), lambda i: (i, 0, 0)),
        out_shape=jax.ShapeDtypeStruct((b, t, FOX_INNER), F32),
        compiler_params=_cparams(("arbitrary",)),
        name="fox_sample",
    )(*args)


def _mlp_kernel(lm, nchunk, ug_ref, vn_ref, ws_ref, bst_ref, y_ref):
    r_i = _iota((lm, lm), 0)
    c_i = _iota((lm, lm), 1)
    tril = r_i >= c_i
    vn = vn_ref[0].astype(BF16)
    ug = ug_ref[0]
    for g in range(MLP_GROUPS):
        lo, hi = MLP_GROUP_DIM * g, MLP_GROUP_DIM * (g + 1)
        w = jnp.where(tril, ws_ref[g], 0.0).astype(BF16)
        bias = bst_ref[:, g:g + 1]
        for c in range(nchunk):
            r0, r1 = c * lm, (c + 1) * lm
            sv = _dot(w, vn[r0:r1, lo:hi]) + bias
            y_ref[0, r0:r1, lo:hi] = ug[r0:r1, lo:hi] * sv


def _chunk_mlp(ug, vn, ws, bst):
    b, t, _ = ug.shape
    lm = ws.shape[1]
    tm = min(t, 4 * lm)
    tok = pl.BlockSpec((1, tm, MLP_INNER), lambda i, j: (i, j, 0))
    return pl.pallas_call(
        functools.partial(_mlp_kernel, lm, tm // lm),
        grid=(b, t // tm),
        in_specs=[tok, tok, pl.BlockSpec(ws.shape, lambda i, j: (0, 0, 0)),
                  pl.BlockSpec(bst.shape, lambda i, j: (0, 0))],
        out_specs=tok,
        out_shape=jax.ShapeDtypeStruct((b, t, MLP_INNER), F32),
        compiler_params=_cparams(("arbitrary", "arbitrary")),
        name="chunk_mlp",
    )(ug, vn, ws, bst)


def _out_kernel(x_ref, ys_ref, yf_ref, ym_ref, g1_ref, sh_ref, sc_ref, g_ref, wo_ref, x1_ref, h2_ref):
    mix = (_dot(ys_ref[0].astype(BF16), wo_ref[0:SSD_INNER, :])
           + _dot(yf_ref[0].astype(BF16), wo_ref[SSD_INNER:SSD_INNER + FOX_INNER, :])
           + _dot(ym_ref[0].astype(BF16), wo_ref[SSD_INNER + FOX_INNER:D_MODEL, :]))
    x1 = x_ref[0] + g1_ref[0] * mix
    x1_ref[0] = x1
    ms = jnp.mean(x1 * x1, axis=-1, keepdims=True)
    h = x1 * lax.rsqrt(ms + EPS) * g_ref[...]
    h2_ref[0] = (h * (1.0 + sc_ref[0]) + sh_ref[0]).astype(BF16)


def _out_proj(x, ys, yf, ym, g1, sh, sc, g, wo):
    b, t, _ = x.shape
    tm = min(t, 512)
    tok = lambda c: pl.BlockSpec((1, tm, c), lambda i, j: (i, j, 0))
    per_b = pl.BlockSpec((1, 1, D_MODEL), lambda i, j: (i, 0, 0))
    full = lambda a: pl.BlockSpec(a.shape, lambda i, j: (0,) * a.ndim)
    return pl.pallas_call(
        _out_kernel,
        grid=(b, t // tm),
        in_specs=[tok(D_MODEL), tok(SSD_INNER), tok(FOX_INNER), tok(MLP_INNER), per_b, per_b, per_b, full(g), full(wo)],
        out_specs=[tok(D_MODEL), tok(D_MODEL)],
        out_shape=[jax.ShapeDtypeStruct((b, t, D_MODEL), F32), jax.ShapeDtypeStruct((b, t, D_MODEL), BF16)],
        compiler_params=_cparams(("arbitrary", "arbitrary")),
        name="out_proj",
    )(x, ys, yf, ym, g1, sh, sc, g, wo)


def _top16(scores):
    nk, tn = scores[0].shape
    ridx = _iota((nk, tn), 0).astype(F32)
    r16 = _iota((PEER_TOPK, tn), 0)
    state = [(s, jnp.full((nk, tn), float(PEER_TOPK), F32), jnp.zeros((PEER_TOPK, tn), F32)) for s in scores]
    for it in range(PEER_TOPK):
        nxt_state = []
        for s, pos, tv in state:
            level = [(s[g:g + 8, :], ridx[g:g + 8, :]) for g in range(0, nk, 8)]
            while len(level) > 1:
                nxt = []
                for a in range(0, len(level), 2):
                    (va, ia), (vb, ib) = level[a], level[a + 1]
                    nxt.append((jnp.maximum(va, vb), jnp.where(va >= vb, ia, ib)))
                level = nxt
            v8, i8 = level[0]
            m = jnp.max(v8, axis=0, keepdims=True)
            first = jnp.min(jnp.where(v8 == m, i8, float(nk)), axis=0, keepdims=True)
            sel = ridx == first
            nxt_state.append((jnp.where(sel, -jnp.inf, s), jnp.where(sel, float(it), pos),
                              jnp.where(r16 == it, m, tv)))
        state = nxt_state
    return [(pos, tv) for _, pos, tv in state]


_CAND_GROUPS = [(0, 16)] + [(ka, 8) for ka in range(1, 8)]
_CAND_ROWS = 16 + 7 * 8 + 8


def _pair_select(ta, tb):
    tn = ta.shape[1]
    pieces, flats, valids = [], [], []
    for ka, rows in _CAND_GROUPS:
        pieces.append(ta[ka:ka + 1, :] + tb[0:rows, :])
        kb = _iota((rows, 1), 0)
        flats.append((ka * PEER_TOPK + kb).astype(F32))
        valids.append((ka + 1) * (kb + 1) <= PEER_TOPK)
    pieces.append(ta[8:16, :] + tb[0:1, :])
    flats.append(((8 + _iota((8, 1), 0)) * PEER_TOPK).astype(F32))
    valids.append(_iota((8, 1), 0) >= 0)
    cand0 = jnp.concatenate(pieces, axis=0)
    flat = jnp.concatenate(flats, axis=0)
    valid = jnp.concatenate(valids, axis=0)
    cand0 = jnp.where(valid, cand0, -jnp.inf)
    best = ta[0:1, :] + tb[0:1, :]

    cand = cand0
    selm = jnp.zeros((_CAND_ROWS, tn), F32)
    for _ in range(PEER_TOPK):
        m = jnp.max(cand, axis=0, keepdims=True)
        first = jnp.min(jnp.where(cand == m, flat, 4096.0), axis=0, keepdims=True)
        sel = flat == first
        cand = jnp.where(sel, -jnp.inf, cand)
        selm = jnp.where(sel, 1.0, selm)
    z = jnp.sum(jnp.where(selm > 0.0, jnp.exp(cand0 - best), 0.0), axis=0, keepdims=True)
    cnts = [jnp.sum(selm[0:16, :], axis=0, keepdims=True)]
    for i in range(1, 8):
        cnts.append(jnp.sum(selm[8 + 8 * i:16 + 8 * i, :], axis=0, keepdims=True))
    cnts.append(selm[_CAND_ROWS - 8:_CAND_ROWS, :])
    return jnp.concatenate(cnts, axis=0), z


def _peer_sel_kernel(tn, h_ref, wqt_ref, keys_ref, ea_ref, la_ref, eb_ref, pb_ref, qt_ref):
    qt_ref[...] = _dot_nt(wqt_ref[...], h_ref[...]).astype(BF16)

    def head(h, carry):
        row = pl.multiple_of(h * (2 * PEER_HALF), 2 * PEER_HALF)
        sa_all = _dot(keys_ref[2 * h], qt_ref[pl.ds(row, PEER_HALF), :])
        sb_all = _dot(keys_ref[2 * h + 1], qt_ref[pl.ds(row + PEER_HALF, PEER_HALF), :])
        for c in range(tn // LANES):
            cs = slice(c * LANES, (c + 1) * LANES)
            sa, sb = sa_all[:, cs], sb_all[:, cs]
            (pos_a, ta), (pos_b, tb) = _top16([sa, sb])
            cnt, z = _pair_select(ta, tb)
            la = jnp.zeros_like(pos_a)
            for ka in range(PEER_TOPK):
                la = jnp.where(pos_a == float(ka), cnt[ka:ka + 1, :], la)
            ea_ref[h, :, cs] = jnp.where(pos_a < float(PEER_TOPK), jnp.exp(sa - ta[0:1, :]), 0.0) / z
            la_ref[h, :, cs] = la
            eb = jnp.where(pos_b < float(PEER_TOPK), jnp.exp(sb - tb[0:1, :]), 0.0)
            for sub in range(PEER_KEYS // PEER_SUB):
                src = slice(sub * PEER_SUB, (sub + 1) * PEER_SUB)
                dst = slice(sub * PEER_SUB // 2, (sub + 1) * PEER_SUB // 2)
                eb_ref[h, dst, cs] = pltpu.bitcast(eb[src].astype(BF16), jnp.uint32)
                pb_ref[h, dst, cs] = pltpu.bitcast(pos_b[src].astype(BF16), jnp.uint32)
        return carry

    lax.fori_loop(0, PEER_HEADS, head, 0)


def _peer_select(h2, wqt_all, layer, keys):
    n = h2.shape[0]
    tn = 256
    per_tok = pl.BlockSpec((PEER_HEADS, PEER_KEYS, tn), lambda i: (0, 0, i))
    per_tok_pk = pl.BlockSpec((PEER_HEADS, PEER_KEYS // 2, tn), lambda i: (0, 0, i))
    shp = lambda dt: jax.ShapeDtypeStruct((PEER_HEADS, PEER_KEYS, n), dt)
    shp_pk = jax.ShapeDtypeStruct((PEER_HEADS, PEER_KEYS // 2, n), jnp.uint32)
    return pl.pallas_call(
        functools.partial(_peer_sel_kernel, tn),
        grid=(n // tn,),
        in_specs=[pl.BlockSpec((tn, D_MODEL), lambda i: (i, 0)),
                  pl.BlockSpec((None,) + wqt_all.shape[1:], lambda i: (layer, 0, 0)),
                  pl.BlockSpec(keys.shape, lambda i: (0, 0, 0))],
        out_specs=[per_tok, per_tok, per_tok_pk, per_tok_pk],
        out_shape=[shp(F32), shp(F32), shp_pk, shp_pk],
        scratch_shapes=[pltpu.VMEM((PEER_HEADS * 2 * PEER_HALF, tn), BF16)],
        compiler_params=_cparams(("arbitrary",)),
        name="peer_select",
    )(h2, wqt_all, keys)


def _peer_gate_stage(ec, ia0, rows, chunks, at_r, ga_w, ea_ref, la_ref, eb_ref, pb_ref):
    rows_per = ec // PEER_KEYS
    for r in rows:
        for c in chunks:
            cs = slice(c * LANES, (c + 1) * LANES)
            for sub in range(PEER_KEYS // PEER_SUB):
                pk = slice(sub * PEER_SUB // 2, (sub + 1) * PEER_SUB // 2)
                gate = None
                for h in range(PEER_HEADS):
                    la = la_ref[h, pl.ds(ia0, rows_per), cs][r:r + 1, :]
                    ea = ea_ref[h, pl.ds(ia0, rows_per), cs][r:r + 1, :]
                    la = jnp.broadcast_to(la, (PEER_SUB, LANES)).astype(BF16)
                    ea = jnp.broadcast_to(ea, (PEER_SUB, LANES)).astype(BF16)
                    pb = pltpu.bitcast(pb_ref[h, pk, cs], BF16)
                    eb = pltpu.bitcast(eb_ref[h, pk, cs], BF16)
                    term = jnp.where(pb < la, eb, 0.0) * ea
                    gate = term if gate is None else gate + term
                ex = slice(r * PEER_KEYS + sub * PEER_SUB, r * PEER_KEYS + (sub + 1) * PEER_SUB)
                ga_w[ex, cs] = gate * _gelu(at_r[ex, cs]).astype(BF16)


def _peer_dense_kernel(tn, ec, h_ref, u_ref, vt_ref, ea_ref, la_ref, eb_ref, pb_ref, o_ref, at, ga, acc):
    e = pl.program_id(1)
    ne = pl.num_programs(1)

    @pl.when(e == 0)
    def _():
        acc[...] = jnp.zeros_like(acc)

    rows_per = ec // PEER_KEYS
    ia0 = pl.multiple_of(e * rows_per, rows_per)
    parts = PEER_DENSE_PARTS
    per = ec // parts
    for g in range(parts):
        rows = slice(g * per, (g + 1) * per)
        at[rows, :] = _dot_nt(u_ref[rows, :], h_ref[...])
    for g in range(parts):
        rows = slice(g * per, (g + 1) * per)
        _peer_gate_stage(ec, ia0, range(g * rows_per // parts, (g + 1) * rows_per // parts), range(tn // LANES),
                         at, ga, ea_ref, la_ref, eb_ref, pb_ref)
        acc[...] += _dot(vt_ref[:, rows], ga[rows, :])

    @pl.when(e == ne - 1)
    def _():
        o_ref[...] = acc[...].T


def _peer_dense(h2, u_all, vt_all, layer, ea, la, eb, pb):
    n = h2.shape[0]
    tn, ec = PEER_TOKEN_TILE, PEER_EXPERT_CHUNK
    per_tok = pl.BlockSpec((PEER_HEADS, PEER_KEYS, tn), lambda i, e: (0, 0, i))
    per_tok_pk = pl.BlockSpec((PEER_HEADS, PEER_KEYS // 2, tn), lambda i, e: (0, 0, i))
    return pl.pallas_call(
        functools.partial(_peer_dense_kernel, tn, ec),
        grid=(n // tn, PEER_EXPERTS // ec),
        in_specs=[pl.BlockSpec((tn, D_MODEL), lambda i, e: (i, 0)),
                  pl.BlockSpec((None, ec, D_MODEL), lambda i, e: (layer, e, 0)),
                  pl.BlockSpec((None, D_MODEL, ec), lambda i, e: (layer, 0, e)),
                  per_tok, per_tok, per_tok_pk, per_tok_pk],
        out_specs=pl.BlockSpec((tn, D_MODEL), lambda i, e: (i, 0)),
        out_shape=jax.ShapeDtypeStruct((n, D_MODEL), F32),
        scratch_shapes=[pltpu.VMEM((ec, tn), F32), pltpu.VMEM((ec, tn), BF16), pltpu.VMEM((D_MODEL, tn), F32)],
        compiler_params=_cparams(("arbitrary", "arbitrary")),
        name="peer_dense",
    )(h2, u_all, vt_all, ea, la, eb, pb)


def _resid_kernel(x_ref, p_ref, g_ref, o_ref):
    o_ref[0] = x_ref[0] + g_ref[0] * p_ref[0]


def _residual(x, peer, g2):
    b, t, _ = x.shape
    tm = min(t, 512)
    tok = pl.BlockSpec((1, tm, D_MODEL), lambda i, j: (i, j, 0))
    return pl.pallas_call(
        _resid_kernel,
        grid=(b, t // tm),
        in_specs=[tok, tok, pl.BlockSpec((1, 1, D_MODEL), lambda i, j: (i, 0, 0))],
        out_specs=tok,
        out_shape=jax.ShapeDtypeStruct((b, t, D_MODEL), F32),
        compiler_params=_cparams(("arbitrary", "arbitrary")),
        name="peer_residual",
    )(x, peer, g2)


def _lane_pad(vec, offset, width=LANES):
    out = jnp.zeros((width,), F32)
    return out.at[offset:offset + vec.shape[0]].set(vec.astype(F32))


def _layer_params(l, norm1_g, norm2_g, w_in, conv_w, conv_b, dt_bias, a_log, d_skip, ssd_norm_g, q_norm_g,
                  k_norm_g, fgate_b, w_s, b_s, w_out, peer_keys):
    w = w_in[l]
    o = [0]
    for sz in (SSD_INNER, CONV_CH, SSD_HEADS, FOX_INNER, FOX_INNER, FOX_INNER, FOX_HEADS, MLP_INNER, MLP_INNER):
        o.append(o[-1] + sz)
    wz, wxbc, wdt, wq, wk, wv, wf, wu, wvm = [w[:, o[i]:o[i + 1]] for i in range(9)]
    w_small = jnp.zeros((D_MODEL, SMALL_COLS), F32)
    w_small = w_small.at[:, DT_ROW0:DT_ROW0 + SSD_HEADS].set(wdt).at[:, F_ROW0:F_ROW0 + FOX_HEADS].set(wf)
    w_r = jnp.concatenate([wz, wxbc, wq, wk, wv, wu, wvm, w_small], axis=1).astype(BF16)
    ws_t = w_small[:, :SMALL_ROWS].T.astype(BF16)
    a_neg = -jnp.exp(a_log[l].astype(F32))
    grp = jnp.arange(FOX_INNER) // FOX_HEAD_DIM
    gm = jnp.where(grp[:, None] == grp[None, :], 1.0 / FOX_HEAD_DIM, 0.0).astype(BF16)
    heads = jnp.arange(SSD_INNER) // SSD_HEAD_DIM
    e_mat = (jnp.arange(LANES)[:, None] == heads[None, :]).astype(BF16)
    return dict(
        norm1_g=norm1_g[l][None], norm2_g=norm2_g[l][None], w_r=w_r, ws_t=ws_t,
        gq=(jnp.tile(q_norm_g[l], FOX_HEADS) * (FOX_HEAD_DIM ** -0.5))[None],
        gk=jnp.tile(k_norm_g[l], FOX_HEADS)[None], gm=gm,
        fox_bound=1.02 * FOX_HEAD_DIM ** 0.5 * jnp.max(jnp.abs(q_norm_g[l])) * jnp.max(jnp.abs(k_norm_g[l])),
        conv_w=conv_w[l], conv_b=conv_b[l][None],
        dtb_c=_lane_pad(dt_bias[l], DT_ROW0)[None], dtb_r=dt_bias[l].astype(F32)[:, None],
        a_c=_lane_pad(a_neg, DT_ROW0)[None], a_r=a_neg[:, None],
        dsk_x=jnp.repeat(d_skip[l].astype(F32), SSD_HEAD_DIM)[None], ssd_g=ssd_norm_g[l][None], e_mat=e_mat,
        fb_c=_lane_pad(fgate_b[l], F_ROW0)[None], fb_r=_lane_pad(fgate_b[l], 0, 8)[:, None],
        w_s=w_s[l], bs_t=jnp.zeros((MLP_CHUNK, LANES), F32).at[:, :MLP_GROUPS].set(b_s[l].T),
        w_out=w_out[l].astype(BF16),
        keys=peer_keys[l].reshape(PEER_HEADS * 2, PEER_KEYS, PEER_HALF).astype(BF16),
    )


def _stream_mixers(x, peer, g2_prev, mod, p, hist8, s0t, past):
    sh1, sc1, g1, sh2, sc2, _ = mod
    b, t, _ = x.shape
    x, (z, xbc, qb, kn, kb, v, vb, ug, vn, sm, smt) = _in_proj(
        x, peer, g2_prev, sh1, sc1, p['norm1_g'], p['w_r'], p['ws_t'], p['gq'], p['gk'], p['gm'])
    L = SSD_CHUNK_PROMPT if t % SSD_CHUNK_PROMPT == 0 else t
    y_ssd, s_fin, c_fin = _ssd(xbc, z, sm, smt, hist8, s0t, p['conv_w'], p['conv_b'], p['dtb_c'], p['dtb_r'],
                               p['a_c'], p['a_r'], p['dsk_x'], p['ssd_g'], p['e_mat'], L)
    zeros8 = jnp.zeros((b, 8, LANES), F32)
    if past is None:
        logf_c, f_col, f_row, _, _ = _fcum(sm, smt, F_ROW0 // 8, p['fb_c'], p['fb_r'], zeros8, zeros8, True)
        y_fox = _fox_prompt_t(qb, kb, vb, f_col, f_row, p['fox_bound'])
    else:
        pk, pv, layer, plf_col, plf_row = past
        _, _, fp_row, end_c, end_r = _fcum(plf_col, plf_row, 0, p['fb_c'], p['fb_r'], zeros8, zeros8, False)
        logf_c, f_col, f_row, _, _ = _fcum(sm, smt, F_ROW0 // 8, p['fb_c'], p['fb_r'], end_c, end_r, True)
        y_fox = _fox_sample(qb, f_col, pk, pv, layer, fp_row, kb, vb, f_row)
    lm = MLP_CHUNK if t % MLP_CHUNK == 0 else t
    y_mlp = _chunk_mlp(ug, vn, p['w_s'][:, :lm, :lm], p['bs_t'][:lm])
    x1, h2 = _out_proj(x, y_ssd, y_fox, y_mlp, g1, sh2, sc2, p['norm2_g'], p['w_out'])
    logf = logf_c[:, :, F_ROW0:F_ROW0 + FOX_HEADS]
    new_ssm = s_fin.reshape(b, SSD_STATE, SSD_HEADS, SSD_HEAD_DIM).transpose(0, 2, 3, 1)
    new_conv = c_fin[:, 8 - (CONV_W - 1):, :]
    kc = kn.reshape(b, t, FOX_HEADS, FOX_HEAD_DIM)
    vc = v.reshape(b, t, FOX_HEADS, FOX_HEAD_DIM)
    return x1, h2, (kc, vc, logf, new_ssm, new_conv, vn)


def _peer(h2, p, tables, layer):
    b, t, _ = h2.shape
    n_tok = b * t
    n_pad = -n_tok % PEER_TOKEN_TILE
    flat = h2.reshape(n_tok, D_MODEL)
    if n_pad:
        flat = jnp.concatenate([flat, jnp.zeros((n_pad, D_MODEL), BF16)], axis=0)
    wqt_all, u_all, vt_all = tables
    ea, la, eb, pb = _peer_select(flat, wqt_all, layer, p['keys'])
    out = _peer_dense(flat, u_all, vt_all, layer, ea, la, eb, pb)
    return out[:n_tok].reshape(b, t, D_MODEL)


def kernel(x_prompt, x_sample, c_prompt, c_sample, cache_fox_k, cache_fox_v, cache_fox_logf, state_ssm, state_conv, norm1_g, norm2_g, w_ada, b_ada, w_in, conv_w, conv_b, dt_bias, a_log, d_skip, ssd_norm_g, q_norm_g, k_norm_g, fgate_b, w_s, b_s, w_out, peer_wq, peer_keys, peer_u, peer_v):
    depth = w_ada.shape[0]
    bp, tp, _ = x_prompt.shape
    bs, ts, _ = x_sample.shape
    past_len = cache_fox_k.shape[2]
    mod_all = _modulation(jnp.concatenate([c_prompt, c_sample], axis=0).astype(F32), w_ada, b_ada)

    peer_tables = (_to_bf16_t(peer_wq, D_MODEL, 512), _to_bf16(peer_u, 1024), _to_bf16_t(peer_v, 512, D_MODEL))

    past_k_all = cache_fox_k.reshape(depth, bs, past_len, FOX_INNER)
    past_v_all = cache_fox_v.reshape(depth, bs, past_len, FOX_INNER)

    xp, xs = x_prompt, x_sample
    peer_p = peer_s = g2p = g2s = None
    outs = [[] for _ in range(11)]
    for l in range(depth):
        p = _layer_params(l, norm1_g, norm2_g, w_in, conv_w, conv_b, dt_bias, a_log, d_skip, ssd_norm_g, q_norm_g,
                          k_norm_g, fgate_b, w_s, b_s, w_out, peer_keys)
        mods = [m[:, None, :] for m in jnp.split(mod_all[l], 6, axis=-1)]
        mod_p = [m[:bp] for m in mods]
        mod_s = [m[bp:] for m in mods]
        hist_p = jnp.zeros((bp, 8, CONV_CH), F32)
        s0_p = jnp.zeros((bp, SSD_STATE, SSD_INNER), F32)
        hist_s = jnp.concatenate([jnp.zeros((bs, 8 - (CONV_W - 1), CONV_CH), F32), state_conv[l].astype(F32)], axis=1)
        s0_s = state_ssm[l].astype(F32).transpose(0, 3, 1, 2).reshape(bs, SSD_STATE, SSD_INNER)
        plf = cache_fox_logf[l].astype(F32)
        plf_col = jnp.pad(plf, ((0, 0), (0, 0), (F_ROW0, LANES - F_ROW0 - FOX_HEADS)))
        plf_row = jnp.pad(plf.transpose(0, 2, 1), ((0, 0), (0, 8 - FOX_HEADS), (0, 0)))
        past = (past_k_all, past_v_all, l, plf_col, plf_row)

        x1p, h2p, st_p = _stream_mixers(xp, peer_p, g2p, mod_p, p, hist_p, s0_p, None)
        x1s, h2s, st_s = _stream_mixers(xs, peer_s, g2s, mod_s, p, hist_s, s0_s, past)

        peer_p = _peer(h2p, p, peer_tables, l)
        peer_s = _peer(h2s, p, peer_tables, l)
        xp, xs, g2p, g2s = x1p, x1s, mod_p[5], mod_s[5]
        for i in range(5):
            outs[i].append(st_p[i])
        for i in range(6):
            outs[5 + i].append(st_s[i])

    yp = _residual(xp, peer_p, g2p)
    ys = _residual(xs, peer_s, g2s)
    return (yp, ys) + tuple(jnp.stack(o) for o in outs)
```

```python
import functools

import jax
import jax.numpy as jnp
from jax import lax
from jax.experimental import pallas as pl
from jax.experimental.pallas import tpu as pltpu

F32 = jnp.float32
BF16 = jnp.bfloat16
EPS = 1e-6

D_MODEL = 1024
SSD_HEADS = 8
SSD_HEAD_DIM = 64
SSD_INNER = SSD_HEADS * SSD_HEAD_DIM
SSD_GROUPS = 2
SSD_STATE = 64
CONV_W = 4
CONV_CH = SSD_INNER + 2 * SSD_GROUPS * SSD_STATE
FOX_HEADS = 4
FOX_HEAD_DIM = 64
FOX_INNER = FOX_HEADS * FOX_HEAD_DIM
MLP_GROUPS = 4
MLP_GROUP_DIM = 64
MLP_INNER = MLP_GROUPS * MLP_GROUP_DIM
MLP_CHUNK = 128
FOX_PROMPT_TILE = 512
FOX_CHAINS = 4
FCUM_TILE = 2048
FCUM_SUB = 512
SSD_CHUNK_PROMPT = 128
PEER_HEADS = 8
PEER_KEYS = 128
PEER_EXPERTS = PEER_KEYS * PEER_KEYS
PEER_HALF = 128
PEER_TOPK = 16
PEER_TOKEN_TILE = 512
PEER_EXPERT_CHUNK = 1024
LANES = 128
MXU_DIM = 256
PEER_DENSE_PARTS = 4
PEER_SUB = 128
SMALL_COLS = LANES
SMALL_ROWS = 16
DT_ROW0, F_ROW0 = 0, 8
NEG = -1e30
EXP_UNDERFLOW = 104.0
VMEM_LIMIT = 56 * 1024 * 1024

_Z0, _XBC0, _Q0, _K0, _V0, _U0, _VM0, _SM0 = 0, 512, 1280, 1536, 1792, 2048, 2304, 2560
PROJ_COLS = _SM0 + SMALL_COLS


def _cparams(sem, flags=None):
    return pltpu.CompilerParams(dimension_semantics=sem, vmem_limit_bytes=VMEM_LIMIT, flags=flags)


def _dot(a, b):
    return jnp.dot(a, b, preferred_element_type=F32)


def _dot_nt(a, b):
    return lax.dot_general(a, b, (((1,), (1,)), ((), ())), preferred_element_type=F32)


def _dot_tn(a, b):
    return lax.dot_general(a, b, (((0,), (0,)), ((), ())), preferred_element_type=F32)


def _split2(x):
    hi = x.astype(BF16)
    lo = (x - hi.astype(F32)).astype(BF16)
    return hi, lo


def _split3(x):
    hi = x.astype(BF16)
    r = x - hi.astype(F32)
    mid = r.astype(BF16)
    lo = (r - mid.astype(F32)).astype(BF16)
    return hi, mid, lo


def _dot3_l(x, w):
    hi, mid, lo = _split3(x)
    return _dot(hi, w) + _dot(mid, w) + _dot(lo, w)


def _dot3_r(w, x):
    hi, mid, lo = _split3(x)
    return _dot(w, hi) + _dot(w, mid) + _dot(w, lo)


def _sigmoid(x):
    return 1.0 / (1.0 + jnp.exp(-x))


def _silu(x):
    return x * _sigmoid(x)


def _softplus(x):
    return jnp.maximum(x, 0.0) + jnp.log1p(jnp.exp(-jnp.abs(x)))


_GELU_A = 2.0 * 0.7978845608028654
_GELU_B = _GELU_A * 0.044715


def _gelu(x):
    z2 = x * (_GELU_A + _GELU_B * (x * x))
    return x / (1.0 + jnp.exp(-z2))


def _iota(shape, dim):
    return lax.broadcasted_iota(jnp.int32, shape, dim)


def _cast_kernel(x_ref, o_ref):
    o_ref[0] = x_ref[0].astype(BF16)


def _cast_t_kernel(x_ref, o_ref):
    o_ref[0] = x_ref[0].T.astype(BF16)


def _to_bf16(x, rows):
    n, r, c = x.shape
    return pl.pallas_call(
        _cast_kernel,
        grid=(n, r // rows),
        in_specs=[pl.BlockSpec((1, rows, c), lambda l, i: (l, i, 0))],
        out_specs=pl.BlockSpec((1, rows, c), lambda l, i: (l, i, 0)),
        out_shape=jax.ShapeDtypeStruct((n, r, c), BF16),
        compiler_params=_cparams(("arbitrary", "arbitrary")),
        name="cast_bf16",
    )(x)


def _to_bf16_t(x, rows, cols):
    n, r, c = x.shape
    return pl.pallas_call(
        _cast_t_kernel,
        grid=(n, r // rows, c // cols),
        in_specs=[pl.BlockSpec((1, rows, cols), lambda l, i, j: (l, i, j))],
        out_specs=pl.BlockSpec((1, cols, rows), lambda l, i, j: (l, j, i)),
        out_shape=jax.ShapeDtypeStruct((n, c, r), BF16),
        compiler_params=_cparams(("arbitrary", "arbitrary", "arbitrary")),
        name="cast_bf16_transposed",
    )(x)


def _mod_kernel(c_ref, w_ref, b_ref, o_ref):
    c = c_ref[...]
    o_ref[0] = jnp.dot(_silu(c), w_ref[0], preferred_element_type=F32,
                       precision=lax.Precision.HIGHEST) + b_ref[0]


def _modulation(c_all, w_ada, b_ada):
    depth, _, n6 = w_ada.shape
    bc = c_all.shape[0]
    tn = 1536
    return pl.pallas_call(
        _mod_kernel,
        grid=(depth, n6 // tn),
        in_specs=[pl.BlockSpec((bc, D_MODEL), lambda l, j: (0, 0)),
                  pl.BlockSpec((1, D_MODEL, tn), lambda l, j: (l, 0, j)),
                  pl.BlockSpec((1, 1, tn), lambda l, j: (l, 0, j))],
        out_specs=pl.BlockSpec((1, bc, tn), lambda l, j: (l, 0, j)),
        out_shape=jax.ShapeDtypeStruct((depth, bc, n6), F32),
        compiler_params=_cparams(("arbitrary", "arbitrary")),
        name="adaln_mod",
    )(c_all, w_ada, b_ada.reshape(depth, 1, n6))


def _in_kernel(with_peer, *refs):
    if with_peer:
        x_ref, p_ref, g2_ref = refs[:3]
        refs = refs[3:]
    else:
        x_ref = refs[0]
        refs = refs[1:]
    (sh_ref, sc_ref, g_ref, w_ref, wst_ref, gq_ref, gk_ref, gm_ref) = refs[:8]
    outs = refs[8:]
    if with_peer:
        xo_ref = outs[0]
        outs = outs[1:]
    (z_ref, xbc_ref, q_ref, k_ref, kb_ref, v_ref, vb_ref, ug_ref, vn_ref, sm_ref, smt_ref) = outs

    x = x_ref[0]
    if with_peer:
        x = x + g2_ref[0] * p_ref[0]
        xo_ref[0] = x
    ms = jnp.mean(x * x, axis=-1, keepdims=True)
    h = x * lax.rsqrt(ms + EPS) * g_ref[...]
    h = h * (1.0 + sc_ref[0]) + sh_ref[0]
    hb = h.astype(BF16)
    proj = _dot(hb, w_ref[...])
    z_ref[0] = proj[:, _Z0:_XBC0]
    xbc_ref[0] = proj[:, _XBC0:_Q0]
    q = proj[:, _Q0:_K0]
    k = proj[:, _K0:_V0]
    v = proj[:, _V0:_U0]
    u = proj[:, _U0:_VM0]
    vm = proj[:, _VM0:_SM0]
    sm_ref[0] = proj[:, _SM0:PROJ_COLS]
    smt_ref[0] = _dot_nt(wst_ref[...], hb)
    gm = gm_ref[...]

    def gmean(y):
        hi, lo = _split2(y)
        return _dot(hi, gm) + _dot(lo, gm)

    qn = q * lax.rsqrt(gmean(q * q) + EPS) * gq_ref[...]
    kn = k * lax.rsqrt(gmean(k * k) + EPS) * gk_ref[...]
    q_ref[0] = qn.astype(BF16)
    k_ref[0] = kn
    kb_ref[0] = kn.astype(BF16)
    v_ref[0] = v
    vb_ref[0] = v.astype(BF16)
    ug_ref[0] = _gelu(u)
    gv = _gelu(vm)
    mu = gmean(gv)
    cen = gv - mu
    var = gmean(cen * cen)
    vn_ref[0] = cen * lax.rsqrt(var + EPS)


def _in_proj(x, peer, g2, sh, sc, g, w_r, ws_t, gq, gk, gm):
    b, t, _ = x.shape
    tm = min(t, 512)
    with_peer = peer is not None
    tok = lambda c: pl.BlockSpec((1, tm, c), lambda i, j: (i, j, 0))
    per_b = pl.BlockSpec((1, 1, D_MODEL), lambda i, j: (i, 0, 0))
    full = lambda a: pl.BlockSpec(a.shape, lambda i, j: (0,) * a.ndim)
    in_specs = [tok(D_MODEL)]
    args = [x]
    if with_peer:
        in_specs += [tok(D_MODEL), per_b]
        args += [peer, g2]
    in_specs += [per_b, per_b, full(g), full(w_r), full(ws_t), full(gq), full(gk), full(gm)]
    args += [sh, sc, g, w_r, ws_t, gq, gk, gm]
    out_cols = [(SSD_INNER, F32), (CONV_CH, F32), (FOX_INNER, BF16), (FOX_INNER, F32), (FOX_INNER, BF16),
                (FOX_INNER, F32), (FOX_INNER, BF16), (MLP_INNER, F32), (MLP_INNER, F32), (SMALL_COLS, F32)]
    out_specs = [tok(c) for c, _ in out_cols]
    out_shape = [jax.ShapeDtypeStruct((b, t, c), dt) for c, dt in out_cols]
    out_specs.append(pl.BlockSpec((1, SMALL_ROWS, tm), lambda i, j: (i, 0, j)))
    out_shape.append(jax.ShapeDtypeStruct((b, SMALL_ROWS, t), F32))
    if with_peer:
        out_specs = [tok(D_MODEL)] + out_specs
        out_shape = [jax.ShapeDtypeStruct((b, t, D_MODEL), F32)] + out_shape
    res = pl.pallas_call(
        functools.partial(_in_kernel, with_peer),
        grid=(b, t // tm),
        in_specs=in_specs, out_specs=out_specs, out_shape=out_shape,
        compiler_params=_cparams(("arbitrary", "arbitrary")),
        name="in_proj",
    )(*args)
    if with_peer:
        return res[0], res[1:]
    return x, res


def _ssd_kernel(L, xbc_ref, z_ref, sm_ref, smt_ref, hist_ref, s0_ref, cw_ref, cb_ref, dtbc_ref, dtbr_ref,
                ac_ref, ar_ref, dsk_ref, ng_ref, e_ref, y_ref, sfin_ref, cfin_ref, xpad, st, ybuf):
    c = pl.program_id(1)
    nc = pl.num_programs(1)

    @pl.when(c == 0)
    def _():
        xpad[0:8, :] = hist_ref[0]
        st[...] = s0_ref[0]

    xpad[8:8 + L, :] = xbc_ref[0]
    conv = cb_ref[...]
    for tap in range(CONV_W):
        conv = conv + xpad[5 + tap:5 + tap + L, :] * cw_ref[tap:tap + 1, :]
    tail = xpad[L:L + 8, :]
    xpad[0:8, :] = tail
    xc = _silu(conv)
    xs = xc[:, 0:SSD_INNER]

    r_i = _iota((L, L), 0)
    c_i = _iota((L, L), 1)
    causal = r_i >= c_i
    tri = jnp.where(causal, 1.0, 0.0).astype(BF16)
    triu = jnp.where(r_i <= c_i, 1.0, 0.0).astype(BF16)

    dtc = _softplus(sm_ref[0] + dtbc_ref[...])
    acum_c = _dot3_r(tri, dtc * ac_ref[...])
    e = e_ref[...]
    acum_x = _dot3_l(acum_c, e)
    dt_x = _dot3_l(dtc, e)
    dtr = _softplus(smt_ref[0][DT_ROW0:DT_ROW0 + 8, :] + dtbr_ref[...])
    acum_r = _dot3_l(dtr * ar_ref[...], triu)

    bmat = [xc[:, SSD_INNER + SSD_STATE * g:SSD_INNER + SSD_STATE * (g + 1)].astype(BF16) for g in range(SSD_GROUPS)]
    c0 = SSD_INNER + SSD_GROUPS * SSD_STATE
    cmat = [xc[:, c0 + SSD_STATE * g:c0 + SSD_STATE * (g + 1)].astype(BF16) for g in range(SSD_GROUPS)]
    cb = [_dot_nt(cmat[g], bmat[g]) for g in range(SSD_GROUPS)]
    gw = SSD_INNER // SSD_GROUPS
    stb = st[...].astype(BF16)
    y_off = jnp.concatenate([_dot(cmat[g], stb[:, gw * g:gw * (g + 1)]) for g in range(SSD_GROUPS)], axis=1)
    xsb = xs.astype(BF16)
    hpg = SSD_HEADS // SSD_GROUPS
    mixes = []
    for h in range(SSD_HEADS):
        seg = acum_c[:, h:h + 1] - acum_r[h:h + 1, :]
        dec = jnp.where(causal, jnp.exp(jnp.minimum(seg, 0.0)), 0.0)
        mixes.append((cb[h // hpg] * dec * dtr[h:h + 1, :]).astype(BF16))
    heads = [_dot(mixes[h], xsb[:, SSD_HEAD_DIM * h:SSD_HEAD_DIM * (h + 1)]) for h in range(SSD_HEADS)]
    for h in range(SSD_HEADS):
        ybuf[:, SSD_HEAD_DIM * h:SSD_HEAD_DIM * (h + 1)] = heads[h]

    y = ybuf[...] + y_off * jnp.exp(acum_x) + dsk_ref[...] * xs
    a_end = acum_x[L - 1:L, :]
    xw = (xs * dt_x * jnp.exp(a_end - acum_x)).astype(BF16)
    new_states = jnp.concatenate([_dot_tn(bmat[g], xw[:, gw * g:gw * (g + 1)]) for g in range(SSD_GROUPS)], axis=1)
    st[...] = st[...] * jnp.exp(a_end) + new_states

    yg = y * _silu(z_ref[0])
    ms = jnp.mean(yg * yg, axis=-1, keepdims=True)
    y_ref[0] = yg * lax.rsqrt(ms + EPS) * ng_ref[...]

    @pl.when(c == nc - 1)
    def _():
        sfin_ref[0] = st[...]
        cfin_ref[0] = tail


def _ssd(xbc, z, sm, smt, hist8, s0t, cw, cb, dtb_c, dtb_r, a_c, a_r, dsk_x, ng, e_mat, L):
    b, t, _ = xbc.shape
    tok = lambda c: pl.BlockSpec((1, L, c), lambda i, j: (i, j, 0))
    full = lambda a: pl.BlockSpec(a.shape, lambda i, j: (0,) * a.ndim)
    per_b = lambda a: pl.BlockSpec((1,) + a.shape[1:], lambda i, j: (i,) + (0,) * (a.ndim - 1))
    return pl.pallas_call(
        functools.partial(_ssd_kernel, L),
        grid=(b, t // L),
        in_specs=[tok(CONV_CH), tok(SSD_INNER), tok(SMALL_COLS),
                  pl.BlockSpec((1, SMALL_ROWS, L), lambda i, j: (i, 0, j)),
                  per_b(hist8), per_b(s0t), full(cw), full(cb), full(dtb_c), full(dtb_r), full(a_c), full(a_r),
                  full(dsk_x), full(ng), full(e_mat)],
        out_specs=[tok(SSD_INNER),
                   pl.BlockSpec((1, SSD_STATE, SSD_INNER), lambda i, j: (i, 0, 0)),
                   pl.BlockSpec((1, 8, CONV_CH), lambda i, j: (i, 0, 0))],
        out_shape=[jax.ShapeDtypeStruct((b, t, SSD_INNER), F32),
                   jax.ShapeDtypeStruct((b, SSD_STATE, SSD_INNER), F32),
                   jax.ShapeDtypeStruct((b, 8, CONV_CH), F32)],
        scratch_shapes=[pltpu.VMEM((L + 8, CONV_CH), F32), pltpu.VMEM((SSD_STATE, SSD_INNER), F32),
                        pltpu.VMEM((L, SSD_INNER), F32)],
        compiler_params=_cparams(("arbitrary", "arbitrary")),
        name="ssd_scan",
    )(xbc, z, sm, smt, hist8, s0t, cw, cb, dtb_c, dtb_r, a_c, a_r, dsk_x, ng, e_mat)


def _fcum_kernel(tf, activate, colsrc_ref, rowsrc_ref, fbc_ref, fbr_ref, initc_ref, initr_ref,
                 lfc_ref, fc_ref, fr_ref, endc_ref, endr_ref, carc, carr):
    j = pl.program_id(1)

    @pl.when(j == 0)
    def _():
        carc[...] = initc_ref[0]
        carr[...] = initr_ref[0]

    ts = min(tf, FCUM_SUB)
    r_i = _iota((ts, ts), 0)
    c_i = _iota((ts, ts), 1)
    tri = jnp.where(r_i >= c_i, 1.0, 0.0).astype(BF16)
    triu = jnp.where(r_i <= c_i, 1.0, 0.0).astype(BF16)
    car_c = carc[0:1, :]
    car_r = carr[:, 0:1]
    for k in range(tf // ts):
        xc = colsrc_ref[0, k * ts:(k + 1) * ts, :]
        xr = rowsrc_ref[0, :, k * ts:(k + 1) * ts]
        if activate:
            xc = -_softplus(-(xc + fbc_ref[...]))
            xr = -_softplus(-(xr + fbr_ref[...]))
        fcol = car_c + _dot3_r(tri, xc)
        frow = car_r + _dot3_l(xr, triu)
        lfc_ref[0, k * ts:(k + 1) * ts, :] = xc
        fc_ref[0, k * ts:(k + 1) * ts, :] = fcol
        fr_ref[0, :, k * ts:(k + 1) * ts] = frow
        car_c = fcol[ts - 1:ts, :]
        car_r = frow[:, ts - 1:ts]
    carc[...] = jnp.broadcast_to(car_c, carc.shape)
    carr[...] = jnp.broadcast_to(car_r, carr.shape)
    endc_ref[0] = carc[...]
    endr_ref[0] = carr[...]


def _fcum(colsrc, rowsrc, row_block, fb_c, fb_r, init_c, init_r, activate):
    b, t, _ = colsrc.shape
    tf = min(t, FCUM_TILE)
    full = lambda a: pl.BlockSpec(a.shape, lambda i, j: (0,) * a.ndim)
    per_b = lambda a: pl.BlockSpec((1,) + a.shape[1:], lambda i, j: (i,) + (0,) * (a.ndim - 1))
    return pl.pallas_call(
        functools.partial(_fcum_kernel, tf, activate),
        grid=(b, t // tf),
        in_specs=[pl.BlockSpec((1, tf, LANES), lambda i, j: (i, j, 0)),
                  pl.BlockSpec((1, 8, tf), lambda i, j: (i, row_block, j)),
                  full(fb_c), full(fb_r), per_b(init_c), per_b(init_r)],
        out_specs=[pl.BlockSpec((1, tf, LANES), lambda i, j: (i, j, 0)),
                   pl.BlockSpec((1, tf, LANES), lambda i, j: (i, j, 0)),
                   pl.BlockSpec((1, 8, tf), lambda i, j: (i, 0, j)),
                   pl.BlockSpec((1, 8, LANES), lambda i, j: (i, 0, 0)),
                   pl.BlockSpec((1, 8, LANES), lambda i, j: (i, 0, 0))],
        out_shape=[jax.ShapeDtypeStruct((b, t, LANES), F32), jax.ShapeDtypeStruct((b, t, LANES), F32),
                   jax.ShapeDtypeStruct((b, 8, t), F32), jax.ShapeDtypeStruct((b, 8, LANES), F32),
                   jax.ShapeDtypeStruct((b, 8, LANES), F32)],
        scratch_shapes=[pltpu.VMEM((8, LANES), F32), pltpu.VMEM((8, LANES), F32)],
        compiler_params=_cparams(("arbitrary", "arbitrary")),
        name="forget_cumsum",
    )(colsrc, rowsrc, fb_c, fb_r, init_c, init_r)


def _softmax_stats(s, carry):
    m, l, acc = carry
    tq, tk = s.shape
    m_new = jnp.maximum(m, jnp.max(s, axis=-1, keepdims=True))
    p = jnp.exp(s - m_new)
    alpha = jnp.exp(m - m_new)
    if tk % LANES == 0:
        psum = p[:, 0:LANES]
        for c in range(1, tk // LANES):
            psum = psum + p[:, c * LANES:(c + 1) * LANES]
    else:
        psum = jnp.where(_iota((tq, LANES), 1) == 0, jnp.sum(p, axis=-1, keepdims=True), 0.0)
    return m_new, alpha * l + psum, alpha * acc, p.astype(BF16)


def _softmax_init(tq):
    return tuple((jnp.full((tq, 1), NEG, F32), jnp.zeros((tq, LANES), F32), jnp.zeros((tq, FOX_HEAD_DIM), F32))
                 for _ in range(FOX_HEADS))


def _head_slice(h):
    return slice(FOX_HEAD_DIM * h, FOX_HEAD_DIM * (h + 1))


def _attend_block(qs, fqs, kb, vb, fk, mask, carries):
    heads = range(FOX_HEADS)
    scores = [_dot_nt(qs[h], kb[:, _head_slice(h)]) for h in heads]
    stats = []
    for h in heads:
        s = scores[h] + fqs[h] - fk[h:h + 1, :]
        if mask is not None:
            s = jnp.where(mask, s, NEG)
        stats.append(_softmax_stats(s, carries[h]))
    return tuple((m, l, acc + _dot(p, vb[:, _head_slice(h)])) for h, (m, l, acc, p) in zip(heads, stats))


def _attend_finish(carries, o_ref):
    for h in range(FOX_HEADS):
        _, l, acc = carries[h]
        o_ref[0, :, _head_slice(h)] = acc / jnp.sum(l, axis=-1, keepdims=True)


def _bf16_terms(x):
    hi, mid, lo = _split3(x)
    return hi.astype(F32), mid.astype(F32), lo.astype(F32)


def _fox_prep_kernel(q_ref, k_ref, v_ref, fc_ref, fr_ref, ka_ref, qa_ref, vt_ref):
    tm = q_ref.shape[1]
    qt = q_ref[0].astype(F32).T
    k = k_ref[0].astype(F32)
    vt_ref[0, 0] = v_ref[0].astype(F32).T.astype(BF16)
    lane = _iota((tm, LANES), 1)
    row = _iota((FOX_HEAD_DIM, tm), 0)
    d = FOX_HEAD_DIM
    for h in range(FOX_HEADS):
        pair, odd = divmod(h, 2)
        kc = k[:, pair * LANES:(pair + 1) * LANES]
        if odd:
            kc = pltpu.roll(kc, d, axis=1)
        ka = jnp.where(lane < d, kc, 0.0)
        for i, term in enumerate(_bf16_terms(-fc_ref[0][:, F_ROW0 + h:F_ROW0 + h + 1])):
            ka = jnp.where(lane == d + i, term, ka)
        ka = jnp.where(jnp.logical_and(lane >= d + 3, lane < d + 6), 1.0, ka)
        ka_ref[0, h] = ka.astype(BF16)
        tail = jnp.where(row < 3, 1.0, 0.0)
        for i, term in enumerate(_bf16_terms(fr_ref[0][h:h + 1, :])):
            tail = jnp.where(row == 3 + i, term, tail)
        qa_ref[0, h, 0:d, :] = qt[d * h:d * (h + 1), :].astype(BF16)
        qa_ref[0, h, d:2 * d, :] = tail.astype(BF16)


def _fox_prep(qb, kb, vb, fcol, frow, tm):
    b, t, _ = qb.shape
    tok = lambda c: pl.BlockSpec((1, tm, c), lambda i, j: (i, j, 0))
    return pl.pallas_call(
        _fox_prep_kernel,
        grid=(b, t // tm),
        in_specs=[tok(FOX_INNER), tok(FOX_INNER), tok(FOX_INNER), tok(LANES),
                  pl.BlockSpec((1, 8, tm), lambda i, j: (i, 0, j))],
        out_specs=[pl.BlockSpec((1, FOX_HEADS, tm, LANES), lambda i, j: (i, 0, j, 0)),
                   pl.BlockSpec((1, FOX_HEADS, LANES, tm), lambda i, j: (i, 0, 0, j)),
                   pl.BlockSpec((1, 1, FOX_INNER, tm), lambda i, j: (i, j, 0, 0))],
        out_shape=[jax.ShapeDtypeStruct((b, FOX_HEADS, t, LANES), BF16),
                   jax.ShapeDtypeStruct((b, FOX_HEADS, LANES, t), BF16),
                   jax.ShapeDtypeStruct((b, t // tm, FOX_INNER, tm), BF16)],
        compiler_params=_cparams(("arbitrary", "arbitrary")),
        name="fox_prep",
    )(qb, kb, vb, fcol, frow)


def _fox_prompt_t_kernel(tq, nk, qa_ref, ka_ref, vt_ref, fend_ref, ffirst_ref, thr_ref, o_ref):
    b = pl.program_id(0)
    qi = pl.program_id(1)
    half = tq // FOX_CHAINS
    d = FOX_HEAD_DIM
    diag = pl.multiple_of(qi * tq, tq)
    init = (jnp.full((1, half), NEG, F32), jnp.zeros((1, half), F32), jnp.zeros((d, half), F32))

    def update(blocks, qas, masks, carries):
        scores = [[_dot(ka, qa) for qa in qas] for ka, _ in blocks]
        for (_, vt), block_scores in zip(blocks, scores):
            stats = []
            for s, mask, (m, l, acc) in zip(block_scores, masks, carries):
                if mask is not None:
                    s = jnp.where(mask, s, NEG)
                m_new = jnp.maximum(m, jnp.max(s, axis=0, keepdims=True))
                p = jnp.exp(s - m_new)
                alpha = jnp.exp(m - m_new)
                stats.append((m_new, alpha * l + jnp.sum(p, axis=0, keepdims=True), alpha * acc, p.astype(BF16)))
            carries = tuple((m, l, acc + _dot(vt, p)) for m, l, acc, p in stats)
        return carries

    def key_block(h, j):
        return (ka_ref[0, h, pl.ds(pl.multiple_of(j * tq, tq), tq), :], vt_ref[0, j][d * h:d * (h + 1), :])

    for h in range(FOX_HEADS):
        base = (b * FOX_HEADS + h) * nk
        slack = thr_ref[0] + ffirst_ref[base + qi]
        n_live = lax.fori_loop(0, qi, lambda j, c: c + (slack - fend_ref[base + j] >= 0.0).astype(jnp.int32), 0)
        qas = [qa_ref[0, h, :, r2 * half:(r2 + 1) * half] for r2 in range(FOX_CHAINS)]
        ka_d = ka_ref[0, h, pl.ds(diag, tq), :]
        vt_d = vt_ref[0, qi][d * h:d * (h + 1), :]
        causal = [_iota((tq, half), 0) <= (r2 * half + _iota((tq, half), 1)) for r2 in range(FOX_CHAINS)]
        carries = update([(ka_d, vt_d)], qas, causal, [init] * FOX_CHAINS)
        no_mask = [None] * FOX_CHAINS

        def one(t, carries, qas=qas, h=h):
            return update([key_block(h, qi - 1 - t)], qas, no_mask, carries)

        def two(t, carries, qas=qas, h=h):
            j = qi - 1 - (n_live % 2) - 2 * t
            return update([key_block(h, j), key_block(h, j - 1)], qas, no_mask, carries)

        carries = lax.fori_loop(0, n_live % 2, one, carries)
        carries = lax.fori_loop(0, n_live // 2, two, carries)
        for r2 in range(FOX_CHAINS):
            _, l, acc = carries[r2]
            o_ref[0, r2 * half:(r2 + 1) * half, d * h:d * (h + 1)] = (acc / l).T


def _fox_prompt_t(qb, kb, vb, fcol, frow, score_bound):
    b, t, _ = qb.shape
    tq = min(t, FOX_PROMPT_TILE)
    nk = t // tq
    ka, qa, vt = _fox_prep(qb, kb, vb, fcol, frow, tq)
    f_heads = frow[:, :FOX_HEADS, :].reshape(b, FOX_HEADS, nk, tq)
    f_end = f_heads[:, :, :, tq - 1].reshape(-1)
    f_first = f_heads[:, :, :, 0].reshape(-1)
    thr = (2.0 * score_bound + EXP_UNDERFLOW).reshape(1).astype(F32)
    smem = pl.BlockSpec(memory_space=pltpu.SMEM)
    return pl.pallas_call(
        functools.partial(_fox_prompt_t_kernel, tq, nk),
        grid=(b, nk),
        in_specs=[pl.BlockSpec((1, FOX_HEADS, LANES, tq), lambda i, j: (i, 0, 0, j)),
                  pl.BlockSpec((1, FOX_HEADS, t, LANES), lambda i, j: (i, 0, 0, 0)),
                  pl.BlockSpec((1, nk, FOX_INNER, tq), lambda i, j: (i, 0, 0, 0)),
                  smem, smem, smem],
        out_specs=pl.BlockSpec((1, tq, FOX_INNER), lambda i, j: (i, j, 0)),
        out_shape=jax.ShapeDtypeStruct((b, t, FOX_INNER), F32),
        compiler_params=_cparams(("arbitrary", "arbitrary")),
        name="fox_prompt",
    )(qa, ka, vt, f_end, f_first, thr)


def _fox_sample_kernel(tq, tk, npast, q_ref, fq_ref, pk_ref, pv_ref, fpk_ref, k_ref, v_ref, fk_ref, o_ref):
    q_all = q_ref[0]
    fq_all = fq_ref[0]
    qs = [q_all[:, _head_slice(h)] for h in range(FOX_HEADS)]
    fqs = [fq_all[:, F_ROW0 + h:F_ROW0 + h + 1] for h in range(FOX_HEADS)]
    causal = _iota((tq, tq), 0) >= _iota((tq, tq), 1)

    def past_block(j, carries):
        start = pl.multiple_of(j * tk, tk)
        return _attend_block(qs, fqs, pk_ref[0, pl.ds(start, tk), :].astype(BF16),
                             pv_ref[0, pl.ds(start, tk), :].astype(BF16), fpk_ref[0, j], None, carries)

    carries = lax.fori_loop(0, npast, past_block, _softmax_init(tq))
    _attend_finish(_attend_block(qs, fqs, k_ref[0], v_ref[0], fk_ref[0], causal, carries), o_ref)


def _fox_sample(qb, fcol, past_k, past_v, layer, fpast_row, kb, vb, frow):
    b, t, _ = qb.shape
    p = past_k.shape[2]
    tk = min(p, 512)
    npast = p // tk
    fpk = fpast_row.reshape(b, 8, npast, tk).transpose(0, 2, 1, 3)
    bspec = lambda a: pl.BlockSpec((1,) + a.shape[1:], lambda i: (i,) + (0,) * (a.ndim - 1))
    cache = pl.BlockSpec((None, 1, p, FOX_INNER), lambda i: (layer, i, 0, 0))
    args = (qb, fcol, past_k, past_v, fpk, kb, vb, frow)
    return pl.pallas_call(
        functools.partial(_fox_sample_kernel, t, tk, npast),
        grid=(b,),
        in_specs=[bspec(qb), bspec(fcol), cache, cache, bspec(fpk), bspec(kb), bspec(vb), bspec(frow)],
        out_specs=pl.BlockSpec((1, t, FOX_INNER), lambda i: (i, 0, 0)),
        out_shape=jax.ShapeDtypeStruct((b, t, FOX_INNER), F32),
        compiler_params=_cparams(("arbitrary",)),
        name="fox_sample",
    )(*args)


def _mlp_kernel(lm, nchunk, ug_ref, vn_ref, ws_ref, bst_ref, y_ref):
    r_i = _iota((lm, lm), 0)
    c_i = _iota((lm, lm), 1)
    tril = r_i >= c_i
    vn = vn_ref[0].astype(BF16)
    ug = ug_ref[0]
    for g in range(MLP_GROUPS):
        lo, hi = MLP_GROUP_DIM * g, MLP_GROUP_DIM * (g + 1)
        w = jnp.where(tril, ws_ref[g], 0.0).astype(BF16)
        bias = bst_ref[:, g:g + 1]
        for c in range(nchunk):
            r0, r1 = c * lm, (c + 1) * lm
            sv = _dot(w, vn[r0:r1, lo:hi]) + bias
            y_ref[0, r0:r1, lo:hi] = ug[r0:r1, lo:hi] * sv


def _chunk_mlp(ug, vn, ws, bst):
    b, t, _ = ug.shape
    lm = ws.shape[1]
    tm = min(t, 4 * lm)
    tok = pl.BlockSpec((1, tm, MLP_INNER), lambda i, j: (i, j, 0))
    return pl.pallas_call(
        functools.partial(_mlp_kernel, lm, tm // lm),
        grid=(b, t // tm),
        in_specs=[tok, tok, pl.BlockSpec(ws.shape, lambda i, j: (0, 0, 0)),
                  pl.BlockSpec(bst.shape, lambda i, j: (0, 0))],
        out_specs=tok,
        out_shape=jax.ShapeDtypeStruct((b, t, MLP_INNER), F32),
        compiler_params=_cparams(("arbitrary", "arbitrary")),
        name="chunk_mlp",
    )(ug, vn, ws, bst)


def _out_kernel(x_ref, ys_ref, yf_ref, ym_ref, g1_ref, sh_ref, sc_ref, g_ref, wo_ref, x1_ref, h2_ref):
    mix = (_dot(ys_ref[0].astype(BF16), wo_ref[0:SSD_INNER, :])
           + _dot(yf_ref[0].astype(BF16), wo_ref[SSD_INNER:SSD_INNER + FOX_INNER, :])
           + _dot(ym_ref[0].astype(BF16), wo_ref[SSD_INNER + FOX_INNER:D_MODEL, :]))
    x1 = x_ref[0] + g1_ref[0] * mix
    x1_ref[0] = x1
    ms = jnp.mean(x1 * x1, axis=-1, keepdims=True)
    h = x1 * lax.rsqrt(ms + EPS) * g_ref[...]
    h2_ref[0] = (h * (1.0 + sc_ref[0]) + sh_ref[0]).astype(BF16)


def _out_proj(x, ys, yf, ym, g1, sh, sc, g, wo):
    b, t, _ = x.shape
    tm = min(t, 512)
    tok = lambda c: pl.BlockSpec((1, tm, c), lambda i, j: (i, j, 0))
    per_b = pl.BlockSpec((1, 1, D_MODEL), lambda i, j: (i, 0, 0))
    full = lambda a: pl.BlockSpec(a.shape, lambda i, j: (0,) * a.ndim)
    return pl.pallas_call(
        _out_kernel,
        grid=(b, t // tm),
        in_specs=[tok(D_MODEL), tok(SSD_INNER), tok(FOX_INNER), tok(MLP_INNER), per_b, per_b, per_b, full(g), full(wo)],
        out_specs=[tok(D_MODEL), tok(D_MODEL)],
        out_shape=[jax.ShapeDtypeStruct((b, t, D_MODEL), F32), jax.ShapeDtypeStruct((b, t, D_MODEL), BF16)],
        compiler_params=_cparams(("arbitrary", "arbitrary")),
        name="out_proj",
    )(x, ys, yf, ym, g1, sh, sc, g, wo)


def _top16(scores):
    nk, tn = scores[0].shape
    ridx = _iota((nk, tn), 0).astype(F32)
    r16 = _iota((PEER_TOPK, tn), 0)
    state = [(s, jnp.full((nk, tn), float(PEER_TOPK), F32), jnp.zeros((PEER_TOPK, tn), F32)) for s in scores]
    for it in range(PEER_TOPK):
        nxt_state = []
        for s, pos, tv in state:
            level = [(s[g:g + 8, :], ridx[g:g + 8, :]) for g in range(0, nk, 8)]
            while len(level) > 1:
                nxt = []
                for a in range(0, len(level), 2):
                    (va, ia), (vb, ib) = level[a], level[a + 1]
                    nxt.append((jnp.maximum(va, vb), jnp.where(va >= vb, ia, ib)))
                level = nxt
            v8, i8 = level[0]
            m = jnp.max(v8, axis=0, keepdims=True)
            first = jnp.min(jnp.where(v8 == m, i8, float(nk)), axis=0, keepdims=True)
            sel = ridx == first
            nxt_state.append((jnp.where(sel, -jnp.inf, s), jnp.where(sel, float(it), pos),
                              jnp.where(r16 == it, m, tv)))
        state = nxt_state
    return [(pos, tv) for _, pos, tv in state]


_CAND_GROUPS = [(0, 16)] + [(ka, 8) for ka in range(1, 8)]
_CAND_ROWS = 16 + 7 * 8 + 8


def _pair_select(ta, tb):
    tn = ta.shape[1]
    pieces, flats, valids = [], [], []
    for ka, rows in _CAND_GROUPS:
        pieces.append(ta[ka:ka + 1, :] + tb[0:rows, :])
        kb = _iota((rows, 1), 0)
        flats.append((ka * PEER_TOPK + kb).astype(F32))
        valids.append((ka + 1) * (kb + 1) <= PEER_TOPK)
    pieces.append(ta[8:16, :] + tb[0:1, :])
    flats.append(((8 + _iota((8, 1), 0)) * PEER_TOPK).astype(F32))
    valids.append(_iota((8, 1), 0) >= 0)
    cand0 = jnp.concatenate(pieces, axis=0)
    flat = jnp.concatenate(flats, axis=0)
    valid = jnp.concatenate(valids, axis=0)
    cand0 = jnp.where(valid, cand0, -jnp.inf)
    best = ta[0:1, :] + tb[0:1, :]

    cand = cand0
    selm = jnp.zeros((_CAND_ROWS, tn), F32)
    for _ in range(PEER_TOPK):
        m = jnp.max(cand, axis=0, keepdims=True)
        first = jnp.min(jnp.where(cand == m, flat, 4096.0), axis=0, keepdims=True)
        sel = flat == first
        cand = jnp.where(sel, -jnp.inf, cand)
        selm = jnp.where(sel, 1.0, selm)
    z = jnp.sum(jnp.where(selm > 0.0, jnp.exp(cand0 - best), 0.0), axis=0, keepdims=True)
    cnts = [jnp.sum(selm[0:16, :], axis=0, keepdims=True)]
    for i in range(1, 8):
        cnts.append(jnp.sum(selm[8 + 8 * i:16 + 8 * i, :], axis=0, keepdims=True))
    cnts.append(selm[_CAND_ROWS - 8:_CAND_ROWS, :])
    return jnp.concatenate(cnts, axis=0), z


def _peer_sel_kernel(tn, h_ref, wqt_ref, keys_ref, ea_ref, la_ref, eb_ref, pb_ref, qt_ref):
    qt_ref[...] = _dot_nt(wqt_ref[...], h_ref[...]).astype(BF16)

    def head(h, carry):
        row = pl.multiple_of(h * (2 * PEER_HALF), 2 * PEER_HALF)
        sa_all = _dot(keys_ref[2 * h], qt_ref[pl.ds(row, PEER_HALF), :])
        sb_all = _dot(keys_ref[2 * h + 1], qt_ref[pl.ds(row + PEER_HALF, PEER_HALF), :])
        for c in range(tn // LANES):
            cs = slice(c * LANES, (c + 1) * LANES)
            sa, sb = sa_all[:, cs], sb_all[:, cs]
            (pos_a, ta), (pos_b, tb) = _top16([sa, sb])
            cnt, z = _pair_select(ta, tb)
            la = jnp.zeros_like(pos_a)
            for ka in range(PEER_TOPK):
                la = jnp.where(pos_a == float(ka), cnt[ka:ka + 1, :], la)
            ea_ref[h, :, cs] = jnp.where(pos_a < float(PEER_TOPK), jnp.exp(sa - ta[0:1, :]), 0.0) / z
            la_ref[h, :, cs] = la
            eb = jnp.where(pos_b < float(PEER_TOPK), jnp.exp(sb - tb[0:1, :]), 0.0)
            for sub in range(PEER_KEYS // PEER_SUB):
                src = slice(sub * PEER_SUB, (sub + 1) * PEER_SUB)
                dst = slice(sub * PEER_SUB // 2, (sub + 1) * PEER_SUB // 2)
                eb_ref[h, dst, cs] = pltpu.bitcast(eb[src].astype(BF16), jnp.uint32)
                pb_ref[h, dst, cs] = pltpu.bitcast(pos_b[src].astype(BF16), jnp.uint32)
        return carry

    lax.fori_loop(0, PEER_HEADS, head, 0)


def _peer_select(h2, wqt_all, layer, keys):
    n = h2.shape[0]
    tn = 256
    per_tok = pl.BlockSpec((PEER_HEADS, PEER_KEYS, tn), lambda i: (0, 0, i))
    per_tok_pk = pl.BlockSpec((PEER_HEADS, PEER_KEYS // 2, tn), lambda i: (0, 0, i))
    shp = lambda dt: jax.ShapeDtypeStruct((PEER_HEADS, PEER_KEYS, n), dt)
    shp_pk = jax.ShapeDtypeStruct((PEER_HEADS, PEER_KEYS // 2, n), jnp.uint32)
    return pl.pallas_call(
        functools.partial(_peer_sel_kernel, tn),
        grid=(n // tn,),
        in_specs=[pl.BlockSpec((tn, D_MODEL), lambda i: (i, 0)),
                  pl.BlockSpec((None,) + wqt_all.shape[1:], lambda i: (layer, 0, 0)),
                  pl.BlockSpec(keys.shape, lambda i: (0, 0, 0))],
        out_specs=[per_tok, per_tok, per_tok_pk, per_tok_pk],
        out_shape=[shp(F32), shp(F32), shp_pk, shp_pk],
        scratch_shapes=[pltpu.VMEM((PEER_HEADS * 2 * PEER_HALF, tn), BF16)],
        compiler_params=_cparams(("arbitrary",)),
        name="peer_select",
    )(h2, wqt_all, keys)


def _peer_gate_stage(ec, ia0, rows, chunks, at_r, ga_w, ea_ref, la_ref, eb_ref, pb_ref):
    rows_per = ec // PEER_KEYS
    for r in rows:
        for c in chunks:
            cs = slice(c * LANES, (c + 1) * LANES)
            for sub in range(PEER_KEYS // PEER_SUB):
                pk = slice(sub * PEER_SUB // 2, (sub + 1) * PEER_SUB // 2)
                gate = None
                for h in range(PEER_HEADS):
                    la = la_ref[h, pl.ds(ia0, rows_per), cs][r:r + 1, :]
                    ea = ea_ref[h, pl.ds(ia0, rows_per), cs][r:r + 1, :]
                    la = jnp.broadcast_to(la, (PEER_SUB, LANES)).astype(BF16)
                    ea = jnp.broadcast_to(ea, (PEER_SUB, LANES)).astype(BF16)
                    pb = pltpu.bitcast(pb_ref[h, pk, cs], BF16)
                    eb = pltpu.bitcast(eb_ref[h, pk, cs], BF16)
                    term = jnp.where(pb < la, eb, 0.0) * ea
                    gate = term if gate is None else gate + term
                ex = slice(r * PEER_KEYS + sub * PEER_SUB, r * PEER_KEYS + (sub + 1) * PEER_SUB)
                ga_w[ex, cs] = gate * _gelu(at_r[ex, cs]).astype(BF16)


def _peer_dense_kernel(tn, ec, ne, h_a, h_b, h_c, u_a, u_b, u_c, vt_a, vt_b, ea_ref, la_ref, eb_ref, pb_ref, o_ref,
                       at0, at1, ga0, ga1, acc):
    s = pl.program_id(0)
    e_a = (2 * s) % ne
    rows_per = ec // PEER_KEYS
    parts = PEER_DENSE_PARTS
    per = ec // parts

    def pre(u_ref, h_ref, at):
        for g in range(parts):
            rows = slice(g * per, (g + 1) * per)
            at[rows, :] = _dot_nt(u_ref[rows, :], h_ref[...])

    def gate_and_accumulate(e_item, at, ga, vt_ref, ahead):
        ia0 = pl.multiple_of(e_item * rows_per, rows_per)
        for g in range(parts):
            rows = slice(g * per, (g + 1) * per)
            _peer_gate_stage(ec, ia0, range(g * rows_per // parts, (g + 1) * rows_per // parts),
                             range(tn // LANES), at, ga, ea_ref, la_ref, eb_ref, pb_ref)
            if g == 0:
                ahead()
            acc[...] += _dot(vt_ref[:, rows], ga[rows, :])

    @pl.when(s == 0)
    def _():
        pre(u_a, h_a, at0)

    @pl.when(e_a == 0)
    def _():
        acc[...] = jnp.zeros_like(acc)

    gate_and_accumulate(e_a, at0, ga0, vt_a, lambda: pre(u_b, h_b, at1))
    gate_and_accumulate(e_a + 1, at1, ga1, vt_b, lambda: pre(u_c, h_c, at0))

    @pl.when(e_a + 1 == ne - 1)
    def _():
        o_ref[...] = acc[...].T


def _peer_dense(h2, u_all, vt_all, layer, ea, la, eb, pb):
    n = h2.shape[0]
    tn, ec = PEER_TOKEN_TILE, PEER_EXPERT_CHUNK
    ne = PEER_EXPERTS // ec
    n_items = (n // tn) * ne
    item_a = lambda s: 2 * s
    item_b = lambda s: 2 * s + 1
    item_c = lambda s: jnp.minimum(2 * s + 2, n_items - 1)
    h_spec = lambda item: pl.BlockSpec((tn, D_MODEL), lambda s: (item(s) // ne, 0))
    u_spec = lambda item: pl.BlockSpec((None, ec, D_MODEL), lambda s: (layer, item(s) % ne, 0))
    vt_spec = lambda item: pl.BlockSpec((None, D_MODEL, ec), lambda s: (layer, 0, item(s) % ne))
    per_tok = pl.BlockSpec((PEER_HEADS, PEER_KEYS, tn), lambda s: (0, 0, item_a(s) // ne))
    per_tok_pk = pl.BlockSpec((PEER_HEADS, PEER_KEYS // 2, tn), lambda s: (0, 0, item_a(s) // ne))
    return pl.pallas_call(
        functools.partial(_peer_dense_kernel, tn, ec, ne),
        grid=(n_items // 2,),
        in_specs=[h_spec(item_a), h_spec(item_b), h_spec(item_c),
                  u_spec(item_a), u_spec(item_b), u_spec(item_c),
                  vt_spec(item_a), vt_spec(item_b),
                  per_tok, per_tok, per_tok_pk, per_tok_pk],
        out_specs=pl.BlockSpec((tn, D_MODEL), lambda s: (item_a(s) // ne, 0)),
        out_shape=jax.ShapeDtypeStruct((n, D_MODEL), F32),
        scratch_shapes=[pltpu.VMEM((ec, tn), F32), pltpu.VMEM((ec, tn), F32),
                        pltpu.VMEM((ec, tn), BF16), pltpu.VMEM((ec, tn), BF16),
                        pltpu.VMEM((D_MODEL, tn), F32)],
        compiler_params=_cparams(("arbitrary",)),
        name="peer_dense",
    )(h2, h2, h2, u_all, u_all, u_all, vt_all, vt_all, ea, la, eb, pb)


def _resid_kernel(x_ref, p_ref, g_ref, o_ref):
    o_ref[0] = x_ref[0] + g_ref[0] * p_ref[0]


def _residual(x, peer, g2):
    b, t, _ = x.shape
    tm = min(t, 512)
    tok = pl.BlockSpec((1, tm, D_MODEL), lambda i, j: (i, j, 0))
    return pl.pallas_call(
        _resid_kernel,
        grid=(b, t // tm),
        in_specs=[tok, tok, pl.BlockSpec((1, 1, D_MODEL), lambda i, j: (i, 0, 0))],
        out_specs=tok,
        out_shape=jax.ShapeDtypeStruct((b, t, D_MODEL), F32),
        compiler_params=_cparams(("arbitrary", "arbitrary")),
        name="peer_residual",
    )(x, peer, g2)


def _lane_pad(vec, offset, width=LANES):
    out = jnp.zeros((width,), F32)
    return out.at[offset:offset + vec.shape[0]].set(vec.astype(F32))


def _layer_params(l, norm1_g, norm2_g, w_in, conv_w, conv_b, dt_bias, a_log, d_skip, ssd_norm_g, q_norm_g,
                  k_norm_g, fgate_b, w_s, b_s, w_out, peer_keys):
    w = w_in[l]
    o = [0]
    for sz in (SSD_INNER, CONV_CH, SSD_HEADS, FOX_INNER, FOX_INNER, FOX_INNER, FOX_HEADS, MLP_INNER, MLP_INNER):
        o.append(o[-1] + sz)
    wz, wxbc, wdt, wq, wk, wv, wf, wu, wvm = [w[:, o[i]:o[i + 1]] for i in range(9)]
    w_small = jnp.zeros((D_MODEL, SMALL_COLS), F32)
    w_small = w_small.at[:, DT_ROW0:DT_ROW0 + SSD_HEADS].set(wdt).at[:, F_ROW0:F_ROW0 + FOX_HEADS].set(wf)
    w_r = jnp.concatenate([wz, wxbc, wq, wk, wv, wu, wvm, w_small], axis=1).astype(BF16)
    ws_t = w_small[:, :SMALL_ROWS].T.astype(BF16)
    a_neg = -jnp.exp(a_log[l].astype(F32))
    grp = jnp.arange(FOX_INNER) // FOX_HEAD_DIM
    gm = jnp.where(grp[:, None] == grp[None, :], 1.0 / FOX_HEAD_DIM, 0.0).astype(BF16)
    heads = jnp.arange(SSD_INNER) // SSD_HEAD_DIM
    e_mat = (jnp.arange(LANES)[:, None] == heads[None, :]).astype(BF16)
    return dict(
        norm1_g=norm1_g[l][None], norm2_g=norm2_g[l][None], w_r=w_r, ws_t=ws_t,
        gq=(jnp.tile(q_norm_g[l], FOX_HEADS) * (FOX_HEAD_DIM ** -0.5))[None],
        gk=jnp.tile(k_norm_g[l], FOX_HEADS)[None], gm=gm,
        fox_bound=1.02 * FOX_HEAD_DIM ** 0.5 * jnp.max(jnp.abs(q_norm_g[l])) * jnp.max(jnp.abs(k_norm_g[l])),
        conv_w=conv_w[l], conv_b=conv_b[l][None],
        dtb_c=_lane_pad(dt_bias[l], DT_ROW0)[None], dtb_r=dt_bias[l].astype(F32)[:, None],
        a_c=_lane_pad(a_neg, DT_ROW0)[None], a_r=a_neg[:, None],
        dsk_x=jnp.repeat(d_skip[l].astype(F32), SSD_HEAD_DIM)[None], ssd_g=ssd_norm_g[l][None], e_mat=e_mat,
        fb_c=_lane_pad(fgate_b[l], F_ROW0)[None], fb_r=_lane_pad(fgate_b[l], 0, 8)[:, None],
        w_s=w_s[l], bs_t=jnp.zeros((MLP_CHUNK, LANES), F32).at[:, :MLP_GROUPS].set(b_s[l].T),
        w_out=w_out[l].astype(BF16),
        keys=peer_keys[l].reshape(PEER_HEADS * 2, PEER_KEYS, PEER_HALF).astype(BF16),
    )


def _stream_mixers(x, peer, g2_prev, mod, p, hist8, s0t, past):
    sh1, sc1, g1, sh2, sc2, _ = mod
    b, t, _ = x.shape
    x, (z, xbc, qb, kn, kb, v, vb, ug, vn, sm, smt) = _in_proj(
        x, peer, g2_prev, sh1, sc1, p['norm1_g'], p['w_r'], p['ws_t'], p['gq'], p['gk'], p['gm'])
    L = SSD_CHUNK_PROMPT if t % SSD_CHUNK_PROMPT == 0 else t
    y_ssd, s_fin, c_fin = _ssd(xbc, z, sm, smt, hist8, s0t, p['conv_w'], p['conv_b'], p['dtb_c'], p['dtb_r'],
                               p['a_c'], p['a_r'], p['dsk_x'], p['ssd_g'], p['e_mat'], L)
    zeros8 = jnp.zeros((b, 8, LANES), F32)
    if past is None:
        logf_c, f_col, f_row, _, _ = _fcum(sm, smt, F_ROW0 // 8, p['fb_c'], p['fb_r'], zeros8, zeros8, True)
        y_fox = _fox_prompt_t(qb, kb, vb, f_col, f_row, p['fox_bound'])
    else:
        pk, pv, layer, plf_col, plf_row = past
        _, _, fp_row, end_c, end_r = _fcum(plf_col, plf_row, 0, p['fb_c'], p['fb_r'], zeros8, zeros8, False)
        logf_c, f_col, f_row, _, _ = _fcum(sm, smt, F_ROW0 // 8, p['fb_c'], p['fb_r'], end_c, end_r, True)
        y_fox = _fox_sample(qb, f_col, pk, pv, layer, fp_row, kb, vb, f_row)
    lm = MLP_CHUNK if t % MLP_CHUNK == 0 else t
    y_mlp = _chunk_mlp(ug, vn, p['w_s'][:, :lm, :lm], p['bs_t'][:lm])
    x1, h2 = _out_proj(x, y_ssd, y_fox, y_mlp, g1, sh2, sc2, p['norm2_g'], p['w_out'])
    logf = logf_c[:, :, F_ROW0:F_ROW0 + FOX_HEADS]
    new_ssm = s_fin.reshape(b, SSD_STATE, SSD_HEADS, SSD_HEAD_DIM).transpose(0, 2, 3, 1)
    new_conv = c_fin[:, 8 - (CONV_W - 1):, :]
    kc = kn.reshape(b, t, FOX_HEADS, FOX_HEAD_DIM)
    vc = v.reshape(b, t, FOX_HEADS, FOX_HEAD_DIM)
    return x1, h2, (kc, vc, logf, new_ssm, new_conv, vn)


def _peer(h2, p, tables, layer):
    b, t, _ = h2.shape
    n_tok = b * t
    n_pad = -n_tok % PEER_TOKEN_TILE
    flat = h2.reshape(n_tok, D_MODEL)
    if n_pad:
        flat = jnp.concatenate([flat, jnp.zeros((n_pad, D_MODEL), BF16)], axis=0)
    wqt_all, u_all, vt_all = tables
    ea, la, eb, pb = _peer_select(flat, wqt_all, layer, p['keys'])
    out = _peer_dense(flat, u_all, vt_all, layer, ea, la, eb, pb)
    return out[:n_tok].reshape(b, t, D_MODEL)


def kernel(x_prompt, x_sample, c_prompt, c_sample, cache_fox_k, cache_fox_v, cache_fox_logf, state_ssm, state_conv, norm1_g, norm2_g, w_ada, b_ada, w_in, conv_w, conv_b, dt_bias, a_log, d_skip, ssd_norm_g, q_norm_g, k_norm_g, fgate_b, w_s, b_s, w_out, peer_wq, peer_keys, peer_u, peer_v):
    depth = w_ada.shape[0]
    bp, tp, _ = x_prompt.shape
    bs, ts, _ = x_sample.shape
    past_len = cache_fox_k.shape[2]
    mod_all = _modulation(jnp.concatenate([c_prompt, c_sample], axis=0).astype(F32), w_ada, b_ada)

    peer_tables = (_to_bf16_t(peer_wq, D_MODEL, 512), _to_bf16(peer_u, 1024), _to_bf16_t(peer_v, 512, D_MODEL))

    past_k_all = cache_fox_k.reshape(depth, bs, past_len, FOX_INNER)
    past_v_all = cache_fox_v.reshape(depth, bs, past_len, FOX_INNER)

    xp, xs = x_prompt, x_sample
    peer_p = peer_s = g2p = g2s = None
    outs = [[] for _ in range(11)]
    for l in range(depth):
        p = _layer_params(l, norm1_g, norm2_g, w_in, conv_w, conv_b, dt_bias, a_log, d_skip, ssd_norm_g, q_norm_g,
                          k_norm_g, fgate_b, w_s, b_s, w_out, peer_keys)
        mods = [m[:, None, :] for m in jnp.split(mod_all[l], 6, axis=-1)]
        mod_p = [m[:bp] for m in mods]
        mod_s = [m[bp:] for m in mods]
        hist_p = jnp.zeros((bp, 8, CONV_CH), F32)
        s0_p = jnp.zeros((bp, SSD_STATE, SSD_INNER), F32)
        hist_s = jnp.concatenate([jnp.zeros((bs, 8 - (CONV_W - 1), CONV_CH), F32), state_conv[l].astype(F32)], axis=1)
        s0_s = state_ssm[l].astype(F32).transpose(0, 3, 1, 2).reshape(bs, SSD_STATE, SSD_INNER)
        plf = cache_fox_logf[l].astype(F32)
        plf_col = jnp.pad(plf, ((0, 0), (0, 0), (F_ROW0, LANES - F_ROW0 - FOX_HEADS)))
        plf_row = jnp.pad(plf.transpose(0, 2, 1), ((0, 0), (0, 8 - FOX_HEADS), (0, 0)))
        past = (past_k_all, past_v_all, l, plf_col, plf_row)

        x1p, h2p, st_p = _stream_mixers(xp, peer_p, g2p, mod_p, p, hist_p, s0_p, None)
        x1s, h2s, st_s = _stream_mixers(xs, peer_s, g2s, mod_s, p, hist_s, s0_s, past)

        peer_p = _peer(h2p, p, peer_tables, l)
        peer_s = _peer(h2s, p, peer_tables, l)
        xp, xs, g2p, g2s = x1p, x1s, mod_p[5], mod_s[5]
        for i in range(5):
            outs[i].append(st_p[i])
        for i in range(6):
            outs[5 + i].append(st_s[i])

    yp = _residual(xp, peer_p, g2p)
    ys = _residual(xs, peer_s, g2s)
    return (yp, ys) + tuple(jnp.stack(o) for o in outs)
```

```python
import functools

import jax
import jax.numpy as jnp
from jax import lax
from jax.experimental import pallas as pl
from jax.experimental.pallas import tpu as pltpu

F32 = jnp.float32
BF16 = jnp.bfloat16
EPS = 1e-6

D_MODEL = 1024
SSD_HEADS = 8
SSD_HEAD_DIM = 64
SSD_INNER = SSD_HEADS * SSD_HEAD_DIM
SSD_GROUPS = 2
SSD_STATE = 64
CONV_W = 4
CONV_CH = SSD_INNER + 2 * SSD_GROUPS * SSD_STATE
FOX_HEADS = 4
FOX_HEAD_DIM = 64
FOX_INNER = FOX_HEADS * FOX_HEAD_DIM
MLP_GROUPS = 4
MLP_GROUP_DIM = 64
MLP_INNER = MLP_GROUPS * MLP_GROUP_DIM
MLP_CHUNK = 128
FOX_PROMPT_TILE = 512
FOX_CHAINS = 4
FCUM_TILE = 2048
FCUM_SUB = 512
SSD_CHUNK_PROMPT = 128
PEER_HEADS = 8
PEER_KEYS = 128
PEER_EXPERTS = PEER_KEYS * PEER_KEYS
PEER_HALF = 128
PEER_TOPK = 16
PEER_TOKEN_TILE = 512
PEER_EXPERT_CHUNK = 2048
LANES = 128
MXU_DIM = 256
PEER_DENSE_PARTS = 4
PEER_SUB = 128
SMALL_COLS = LANES
SMALL_ROWS = 16
DT_ROW0, F_ROW0 = 0, 8
NEG = -1e30
EXP_UNDERFLOW = 104.0
VMEM_LIMIT = 56 * 1024 * 1024

_Z0, _XBC0, _Q0, _K0, _V0, _U0, _VM0, _SM0 = 0, 512, 1280, 1536, 1792, 2048, 2304, 2560
PROJ_COLS = _SM0 + SMALL_COLS


def _cparams(sem, flags=None):
    return pltpu.CompilerParams(dimension_semantics=sem, vmem_limit_bytes=VMEM_LIMIT, flags=flags)


def _dot(a, b):
    return jnp.dot(a, b, preferred_element_type=F32)


def _dot_nt(a, b):
    return lax.dot_general(a, b, (((1,), (1,)), ((), ())), preferred_element_type=F32)


def _dot_tn(a, b):
    return lax.dot_general(a, b, (((0,), (0,)), ((), ())), preferred_element_type=F32)


def _split2(x):
    hi = x.astype(BF16)
    lo = (x - hi.astype(F32)).astype(BF16)
    return hi, lo


def _split3(x):
    hi = x.astype(BF16)
    r = x - hi.astype(F32)
    mid = r.astype(BF16)
    lo = (r - mid.astype(F32)).astype(BF16)
    return hi, mid, lo


def _dot3_l(x, w):
    hi, mid, lo = _split3(x)
    return _dot(hi, w) + _dot(mid, w) + _dot(lo, w)


def _dot3_r(w, x):
    hi, mid, lo = _split3(x)
    return _dot(w, hi) + _dot(w, mid) + _dot(w, lo)


def _sigmoid(x):
    return 1.0 / (1.0 + jnp.exp(-x))


def _silu(x):
    return x * _sigmoid(x)


def _softplus(x):
    return jnp.maximum(x, 0.0) + jnp.log1p(jnp.exp(-jnp.abs(x)))


_GELU_A = 2.0 * 0.7978845608028654
_GELU_B = _GELU_A * 0.044715


def _gelu(x):
    z2 = x * (_GELU_A + _GELU_B * (x * x))
    return x / (1.0 + jnp.exp(-z2))


def _iota(shape, dim):
    return lax.broadcasted_iota(jnp.int32, shape, dim)


def _pack_rows(x):
    return pltpu.bitcast(x, jnp.uint32)


def _unpack_rows(x):
    return pltpu.bitcast(x, BF16)


def _cast_kernel(x_ref, o_ref):
    o_ref[0] = _pack_rows(x_ref[0].astype(BF16))


def _cast_t_kernel(packed, x_ref, o_ref):
    xt = x_ref[0].T.astype(BF16)
    o_ref[0] = _pack_rows(xt) if packed else xt


def _to_bf16_packed(x, rows):
    n, r, c = x.shape
    return pl.pallas_call(
        _cast_kernel,
        grid=(n, r // rows),
        in_specs=[pl.BlockSpec((1, rows, c), lambda l, i: (l, i, 0))],
        out_specs=pl.BlockSpec((1, rows // 2, c), lambda l, i: (l, i, 0)),
        out_shape=jax.ShapeDtypeStruct((n, r // 2, c), jnp.uint32),
        compiler_params=_cparams(("arbitrary", "arbitrary")),
        name="cast_bf16",
    )(x)


def _to_bf16_t(x, rows, cols, packed):
    n, r, c = x.shape
    div = 2 if packed else 1
    return pl.pallas_call(
        functools.partial(_cast_t_kernel, packed),
        grid=(n, r // rows, c // cols),
        in_specs=[pl.BlockSpec((1, rows, cols), lambda l, i, j: (l, i, j))],
        out_specs=pl.BlockSpec((1, cols // div, rows), lambda l, i, j: (l, j, i)),
        out_shape=jax.ShapeDtypeStruct((n, c // div, r), jnp.uint32 if packed else BF16),
        compiler_params=_cparams(("arbitrary", "arbitrary", "arbitrary")),
        name="cast_bf16_transposed",
    )(x)


def _mod_kernel(c_ref, w_ref, b_ref, o_ref):
    c = c_ref[...]
    o_ref[0] = jnp.dot(_silu(c), w_ref[0], preferred_element_type=F32,
                       precision=lax.Precision.HIGHEST) + b_ref[0]


def _modulation(c_all, w_ada, b_ada):
    depth, _, n6 = w_ada.shape
    bc = c_all.shape[0]
    tn = 1536
    return pl.pallas_call(
        _mod_kernel,
        grid=(depth, n6 // tn),
        in_specs=[pl.BlockSpec((bc, D_MODEL), lambda l, j: (0, 0)),
                  pl.BlockSpec((1, D_MODEL, tn), lambda l, j: (l, 0, j)),
                  pl.BlockSpec((1, 1, tn), lambda l, j: (l, 0, j))],
        out_specs=pl.BlockSpec((1, bc, tn), lambda l, j: (l, 0, j)),
        out_shape=jax.ShapeDtypeStruct((depth, bc, n6), F32),
        compiler_params=_cparams(("arbitrary", "arbitrary")),
        name="adaln_mod",
    )(c_all, w_ada, b_ada.reshape(depth, 1, n6))


def _in_kernel(with_peer, *refs):
    if with_peer:
        x_ref, p_ref, g2_ref = refs[:3]
        refs = refs[3:]
    else:
        x_ref = refs[0]
        refs = refs[1:]
    (sh_ref, sc_ref, g_ref, w_ref, wst_ref, gq_ref, gk_ref, gm_ref) = refs[:8]
    outs = refs[8:]
    if with_peer:
        xo_ref = outs[0]
        outs = outs[1:]
    (z_ref, xbc_ref, q_ref, k_ref, kb_ref, v_ref, vb_ref, ug_ref, vn_ref, sm_ref, smt_ref) = outs

    x = x_ref[0]
    if with_peer:
        x = x + g2_ref[0] * p_ref[0]
        xo_ref[0] = x
    ms = jnp.mean(x * x, axis=-1, keepdims=True)
    h = x * lax.rsqrt(ms + EPS) * g_ref[...]
    h = h * (1.0 + sc_ref[0]) + sh_ref[0]
    hb = h.astype(BF16)
    proj = _dot(hb, w_ref[...])
    z_ref[0] = proj[:, _Z0:_XBC0]
    xbc_ref[0] = proj[:, _XBC0:_Q0]
    q = proj[:, _Q0:_K0]
    k = proj[:, _K0:_V0]
    v = proj[:, _V0:_U0]
    u = proj[:, _U0:_VM0]
    vm = proj[:, _VM0:_SM0]
    sm_ref[0] = proj[:, _SM0:PROJ_COLS]
    smt_ref[0] = _dot_nt(wst_ref[...], hb)
    gm = gm_ref[...]

    def gmean(y):
        hi, lo = _split2(y)
        return _dot(hi, gm) + _dot(lo, gm)

    qn = q * lax.rsqrt(gmean(q * q) + EPS) * gq_ref[...]
    kn = k * lax.rsqrt(gmean(k * k) + EPS) * gk_ref[...]
    q_ref[0] = qn.astype(BF16)
    k_ref[0] = kn
    kb_ref[0] = kn.astype(BF16)
    v_ref[0] = v
    vb_ref[0] = v.astype(BF16)
    ug_ref[0] = _gelu(u)
    gv = _gelu(vm)
    mu = gmean(gv)
    cen = gv - mu
    var = gmean(cen * cen)
    vn_ref[0] = cen * lax.rsqrt(var + EPS)


def _in_proj(x, peer, g2, sh, sc, g, w_r, ws_t, gq, gk, gm):
    b, t, _ = x.shape
    tm = min(t, 512)
    with_peer = peer is not None
    tok = lambda c: pl.BlockSpec((1, tm, c), lambda i, j: (i, j, 0))
    per_b = pl.BlockSpec((1, 1, D_MODEL), lambda i, j: (i, 0, 0))
    full = lambda a: pl.BlockSpec(a.shape, lambda i, j: (0,) * a.ndim)
    in_specs = [tok(D_MODEL)]
    args = [x]
    if with_peer:
        in_specs += [tok(D_MODEL), per_b]
        args += [peer, g2]
    in_specs += [per_b, per_b, full(g), full(w_r), full(ws_t), full(gq), full(gk), full(gm)]
    args += [sh, sc, g, w_r, ws_t, gq, gk, gm]
    out_cols = [(SSD_INNER, F32), (CONV_CH, F32), (FOX_INNER, BF16), (FOX_INNER, F32), (FOX_INNER, BF16),
                (FOX_INNER, F32), (FOX_INNER, BF16), (MLP_INNER, F32), (MLP_INNER, F32), (SMALL_COLS, F32)]
    out_specs = [tok(c) for c, _ in out_cols]
    out_shape = [jax.ShapeDtypeStruct((b, t, c), dt) for c, dt in out_cols]
    out_specs.append(pl.BlockSpec((1, SMALL_ROWS, tm), lambda i, j: (i, 0, j)))
    out_shape.append(jax.ShapeDtypeStruct((b, SMALL_ROWS, t), F32))
    if with_peer:
        out_specs = [tok(D_MODEL)] + out_specs
        out_shape = [jax.ShapeDtypeStruct((b, t, D_MODEL), F32)] + out_shape
    res = pl.pallas_call(
        functools.partial(_in_kernel, with_peer),
        grid=(b, t // tm),
        in_specs=in_specs, out_specs=out_specs, out_shape=out_shape,
        compiler_params=_cparams(("arbitrary", "arbitrary")),
        name="in_proj",
    )(*args)
    if with_peer:
        return res[0], res[1:]
    return x, res


def _ssd_kernel(L, xbc_ref, z_ref, sm_ref, smt_ref, hist_ref, s0_ref, cw_ref, cb_ref, dtbc_ref, dtbr_ref,
                ac_ref, ar_ref, dsk_ref, ng_ref, e_ref, y_ref, sfin_ref, cfin_ref, xpad, st, ybuf):
    c = pl.program_id(1)
    nc = pl.num_programs(1)

    @pl.when(c == 0)
    def _():
        xpad[0:8, :] = hist_ref[0]
        st[...] = s0_ref[0]

    xpad[8:8 + L, :] = xbc_ref[0]
    conv = cb_ref[...]
    for tap in range(CONV_W):
        conv = conv + xpad[5 + tap:5 + tap + L, :] * cw_ref[tap:tap + 1, :]
    tail = xpad[L:L + 8, :]
    xpad[0:8, :] = tail
    xc = _silu(conv)
    xs = xc[:, 0:SSD_INNER]

    r_i = _iota((L, L), 0)
    c_i = _iota((L, L), 1)
    causal = r_i >= c_i
    tri = jnp.where(causal, 1.0, 0.0).astype(BF16)
    triu = jnp.where(r_i <= c_i, 1.0, 0.0).astype(BF16)

    dtc = _softplus(sm_ref[0] + dtbc_ref[...])
    acum_c = _dot3_r(tri, dtc * ac_ref[...])
    e = e_ref[...]
    acum_x = _dot3_l(acum_c, e)
    dt_x = _dot3_l(dtc, e)
    dtr = _softplus(smt_ref[0][DT_ROW0:DT_ROW0 + 8, :] + dtbr_ref[...])
    acum_r = _dot3_l(dtr * ar_ref[...], triu)

    bmat = [xc[:, SSD_INNER + SSD_STATE * g:SSD_INNER + SSD_STATE * (g + 1)].astype(BF16) for g in range(SSD_GROUPS)]
    c0 = SSD_INNER + SSD_GROUPS * SSD_STATE
    cmat = [xc[:, c0 + SSD_STATE * g:c0 + SSD_STATE * (g + 1)].astype(BF16) for g in range(SSD_GROUPS)]
    cb = [_dot_nt(cmat[g], bmat[g]) for g in range(SSD_GROUPS)]
    gw = SSD_INNER // SSD_GROUPS
    stb = st[...].astype(BF16)
    y_off = jnp.concatenate([_dot(cmat[g], stb[:, gw * g:gw * (g + 1)]) for g in range(SSD_GROUPS)], axis=1)
    xsb = xs.astype(BF16)
    hpg = SSD_HEADS // SSD_GROUPS
    mixes = []
    for h in range(SSD_HEADS):
        seg = acum_c[:, h:h + 1] - acum_r[h:h + 1, :]
        dec = jnp.where(causal, jnp.exp(jnp.minimum(seg, 0.0)), 0.0)
        mixes.append((cb[h // hpg] * dec * dtr[h:h + 1, :]).astype(BF16))
    heads = [_dot(mixes[h], xsb[:, SSD_HEAD_DIM * h:SSD_HEAD_DIM * (h + 1)]) for h in range(SSD_HEADS)]
    for h in range(SSD_HEADS):
        ybuf[:, SSD_HEAD_DIM * h:SSD_HEAD_DIM * (h + 1)] = heads[h]

    y = ybuf[...] + y_off * jnp.exp(acum_x) + dsk_ref[...] * xs
    a_end = acum_x[L - 1:L, :]
    xw = (xs * dt_x * jnp.exp(a_end - acum_x)).astype(BF16)
    new_states = jnp.concatenate([_dot_tn(bmat[g], xw[:, gw * g:gw * (g + 1)]) for g in range(SSD_GROUPS)], axis=1)
    st[...] = st[...] * jnp.exp(a_end) + new_states

    yg = y * _silu(z_ref[0])
    ms = jnp.mean(yg * yg, axis=-1, keepdims=True)
    y_ref[0] = yg * lax.rsqrt(ms + EPS) * ng_ref[...]

    @pl.when(c == nc - 1)
    def _():
        sfin_ref[0] = st[...]
        cfin_ref[0] = tail


def _ssd(xbc, z, sm, smt, hist8, s0t, cw, cb, dtb_c, dtb_r, a_c, a_r, dsk_x, ng, e_mat, L):
    b, t, _ = xbc.shape
    tok = lambda c: pl.BlockSpec((1, L, c), lambda i, j: (i, j, 0))
    full = lambda a: pl.BlockSpec(a.shape, lambda i, j: (0,) * a.ndim)
    per_b = lambda a: pl.BlockSpec((1,) + a.shape[1:], lambda i, j: (i,) + (0,) * (a.ndim - 1))
    return pl.pallas_call(
        functools.partial(_ssd_kernel, L),
        grid=(b, t // L),
        in_specs=[tok(CONV_CH), tok(SSD_INNER), tok(SMALL_COLS),
                  pl.BlockSpec((1, SMALL_ROWS, L), lambda i, j: (i, 0, j)),
                  per_b(hist8), per_b(s0t), full(cw), full(cb), full(dtb_c), full(dtb_r), full(a_c), full(a_r),
                  full(dsk_x), full(ng), full(e_mat)],
        out_specs=[tok(SSD_INNER),
                   pl.BlockSpec((1, SSD_STATE, SSD_INNER), lambda i, j: (i, 0, 0)),
                   pl.BlockSpec((1, 8, CONV_CH), lambda i, j: (i, 0, 0))],
        out_shape=[jax.ShapeDtypeStruct((b, t, SSD_INNER), F32),
                   jax.ShapeDtypeStruct((b, SSD_STATE, SSD_INNER), F32),
                   jax.ShapeDtypeStruct((b, 8, CONV_CH), F32)],
        scratch_shapes=[pltpu.VMEM((L + 8, CONV_CH), F32), pltpu.VMEM((SSD_STATE, SSD_INNER), F32),
                        pltpu.VMEM((L, SSD_INNER), F32)],
        compiler_params=_cparams(("arbitrary", "arbitrary")),
        name="ssd_scan",
    )(xbc, z, sm, smt, hist8, s0t, cw, cb, dtb_c, dtb_r, a_c, a_r, dsk_x, ng, e_mat)


def _fcum_kernel(tf, activate, colsrc_ref, rowsrc_ref, fbc_ref, fbr_ref, initc_ref, initr_ref,
                 lfc_ref, fc_ref, fr_ref, endc_ref, endr_ref, carc, carr):
    j = pl.program_id(1)

    @pl.when(j == 0)
    def _():
        carc[...] = initc_ref[0]
        carr[...] = initr_ref[0]

    ts = min(tf, FCUM_SUB)
    r_i = _iota((ts, ts), 0)
    c_i = _iota((ts, ts), 1)
    tri = jnp.where(r_i >= c_i, 1.0, 0.0).astype(BF16)
    triu = jnp.where(r_i <= c_i, 1.0, 0.0).astype(BF16)
    car_c = carc[0:1, :]
    car_r = carr[:, 0:1]
    for k in range(tf // ts):
        xc = colsrc_ref[0, k * ts:(k + 1) * ts, :]
        xr = rowsrc_ref[0, :, k * ts:(k + 1) * ts]
        if activate:
            xc = -_softplus(-(xc + fbc_ref[...]))
            xr = -_softplus(-(xr + fbr_ref[...]))
        fcol = car_c + _dot3_r(tri, xc)
        frow = car_r + _dot3_l(xr, triu)
        lfc_ref[0, k * ts:(k + 1) * ts, :] = xc
        fc_ref[0, k * ts:(k + 1) * ts, :] = fcol
        fr_ref[0, :, k * ts:(k + 1) * ts] = frow
        car_c = fcol[ts - 1:ts, :]
        car_r = frow[:, ts - 1:ts]
    carc[...] = jnp.broadcast_to(car_c, carc.shape)
    carr[...] = jnp.broadcast_to(car_r, carr.shape)
    endc_ref[0] = carc[...]
    endr_ref[0] = carr[...]


def _fcum(colsrc, rowsrc, row_block, fb_c, fb_r, init_c, init_r, activate):
    b, t, _ = colsrc.shape
    tf = min(t, FCUM_TILE)
    full = lambda a: pl.BlockSpec(a.shape, lambda i, j: (0,) * a.ndim)
    per_b = lambda a: pl.BlockSpec((1,) + a.shape[1:], lambda i, j: (i,) + (0,) * (a.ndim - 1))
    return pl.pallas_call(
        functools.partial(_fcum_kernel, tf, activate),
        grid=(b, t // tf),
        in_specs=[pl.BlockSpec((1, tf, LANES), lambda i, j: (i, j, 0)),
                  pl.BlockSpec((1, 8, tf), lambda i, j: (i, row_block, j)),
                  full(fb_c), full(fb_r), per_b(init_c), per_b(init_r)],
        out_specs=[pl.BlockSpec((1, tf, LANES), lambda i, j: (i, j, 0)),
                   pl.BlockSpec((1, tf, LANES), lambda i, j: (i, j, 0)),
                   pl.BlockSpec((1, 8, tf), lambda i, j: (i, 0, j)),
                   pl.BlockSpec((1, 8, LANES), lambda i, j: (i, 0, 0)),
                   pl.BlockSpec((1, 8, LANES), lambda i, j: (i, 0, 0))],
        out_shape=[jax.ShapeDtypeStruct((b, t, LANES), F32), jax.ShapeDtypeStruct((b, t, LANES), F32),
                   jax.ShapeDtypeStruct((b, 8, t), F32), jax.ShapeDtypeStruct((b, 8, LANES), F32),
                   jax.ShapeDtypeStruct((b, 8, LANES), F32)],
        scratch_shapes=[pltpu.VMEM((8, LANES), F32), pltpu.VMEM((8, LANES), F32)],
        compiler_params=_cparams(("arbitrary", "arbitrary")),
        name="forget_cumsum",
    )(colsrc, rowsrc, fb_c, fb_r, init_c, init_r)


def _softmax_stats(s, carry):
    m, l, acc = carry
    tq, tk = s.shape
    m_new = jnp.maximum(m, jnp.max(s, axis=-1, keepdims=True))
    p = jnp.exp(s - m_new)
    alpha = jnp.exp(m - m_new)
    if tk % LANES == 0:
        psum = p[:, 0:LANES]
        for c in range(1, tk // LANES):
            psum = psum + p[:, c * LANES:(c + 1) * LANES]
    else:
        psum = jnp.where(_iota((tq, LANES), 1) == 0, jnp.sum(p, axis=-1, keepdims=True), 0.0)
    return m_new, alpha * l + psum, alpha * acc, p.astype(BF16)


def _softmax_init(tq):
    return tuple((jnp.full((tq, 1), NEG, F32), jnp.zeros((tq, LANES), F32), jnp.zeros((tq, FOX_HEAD_DIM), F32))
                 for _ in range(FOX_HEADS))


def _head_slice(h):
    return slice(FOX_HEAD_DIM * h, FOX_HEAD_DIM * (h + 1))


def _attend_block(qs, fqs, kb, vb, fk, mask, carries):
    heads = range(FOX_HEADS)
    scores = [_dot_nt(qs[h], kb[:, _head_slice(h)]) for h in heads]
    stats = []
    for h in heads:
        s = scores[h] + fqs[h] - fk[h:h + 1, :]
        if mask is not None:
            s = jnp.where(mask, s, NEG)
        stats.append(_softmax_stats(s, carries[h]))
    return tuple((m, l, acc + _dot(p, vb[:, _head_slice(h)])) for h, (m, l, acc, p) in zip(heads, stats))


def _attend_finish(carries, o_ref):
    for h in range(FOX_HEADS):
        _, l, acc = carries[h]
        o_ref[0, :, _head_slice(h)] = acc / jnp.sum(l, axis=-1, keepdims=True)


def _bf16_terms(x):
    hi, mid, lo = _split3(x)
    return hi.astype(F32), mid.astype(F32), lo.astype(F32)


def _fox_prep_kernel(q_ref, k_ref, v_ref, fc_ref, fr_ref, ka_ref, qa_ref, vt_ref):
    tm = q_ref.shape[1]
    qt = q_ref[0].astype(F32).T
    k = k_ref[0].astype(F32)
    vt_ref[0, 0] = v_ref[0].astype(F32).T.astype(BF16)
    lane = _iota((tm, LANES), 1)
    row = _iota((FOX_HEAD_DIM, tm), 0)
    d = FOX_HEAD_DIM
    for h in range(FOX_HEADS):
        pair, odd = divmod(h, 2)
        kc = k[:, pair * LANES:(pair + 1) * LANES]
        if odd:
            kc = pltpu.roll(kc, d, axis=1)
        ka = jnp.where(lane < d, kc, 0.0)
        for i, term in enumerate(_bf16_terms(-fc_ref[0][:, F_ROW0 + h:F_ROW0 + h + 1])):
            ka = jnp.where(lane == d + i, term, ka)
        ka = jnp.where(jnp.logical_and(lane >= d + 3, lane < d + 6), 1.0, ka)
        ka_ref[0, h] = ka.astype(BF16)
        tail = jnp.where(row < 3, 1.0, 0.0)
        for i, term in enumerate(_bf16_terms(fr_ref[0][h:h + 1, :])):
            tail = jnp.where(row == 3 + i, term, tail)
        qa_ref[0, h, 0:d, :] = qt[d * h:d * (h + 1), :].astype(BF16)
        qa_ref[0, h, d:2 * d, :] = tail.astype(BF16)


def _fox_prep(qb, kb, vb, fcol, frow, tm):
    b, t, _ = qb.shape
    tok = lambda c: pl.BlockSpec((1, tm, c), lambda i, j: (i, j, 0))
    return pl.pallas_call(
        _fox_prep_kernel,
        grid=(b, t // tm),
        in_specs=[tok(FOX_INNER), tok(FOX_INNER), tok(FOX_INNER), tok(LANES),
                  pl.BlockSpec((1, 8, tm), lambda i, j: (i, 0, j))],
        out_specs=[pl.BlockSpec((1, FOX_HEADS, tm, LANES), lambda i, j: (i, 0, j, 0)),
                   pl.BlockSpec((1, FOX_HEADS, LANES, tm), lambda i, j: (i, 0, 0, j)),
                   pl.BlockSpec((1, 1, FOX_INNER, tm), lambda i, j: (i, j, 0, 0))],
        out_shape=[jax.ShapeDtypeStruct((b, FOX_HEADS, t, LANES), BF16),
                   jax.ShapeDtypeStruct((b, FOX_HEADS, LANES, t), BF16),
                   jax.ShapeDtypeStruct((b, t // tm, FOX_INNER, tm), BF16)],
        compiler_params=_cparams(("arbitrary", "arbitrary")),
        name="fox_prep",
    )(qb, kb, vb, fcol, frow)


def _fox_prompt_t_kernel(tq, nk, qa_ref, ka_ref, vt_ref, fend_ref, ffirst_ref, thr_ref, o_ref):
    b = pl.program_id(0)
    qi = pl.program_id(1)
    half = tq // FOX_CHAINS
    d = FOX_HEAD_DIM
    diag = pl.multiple_of(qi * tq, tq)
    init = (jnp.full((1, half), NEG, F32), jnp.zeros((1, half), F32), jnp.zeros((d, half), F32))

    def update(blocks, qas, masks, carries):
        scores = [[_dot(ka, qa) for qa in qas] for ka, _ in blocks]
        for (_, vt), block_scores in zip(blocks, scores):
            stats = []
            for s, mask, (m, l, acc) in zip(block_scores, masks, carries):
                if mask is not None:
                    s = jnp.where(mask, s, NEG)
                m_new = jnp.maximum(m, jnp.max(s, axis=0, keepdims=True))
                p = jnp.exp(s - m_new)
                alpha = jnp.exp(m - m_new)
                stats.append((m_new, alpha * l + jnp.sum(p, axis=0, keepdims=True), alpha * acc, p.astype(BF16)))
            carries = tuple((m, l, acc + _dot(vt, p)) for m, l, acc, p in stats)
        return carries

    def key_block(h, j):
        return (ka_ref[0, h, pl.ds(pl.multiple_of(j * tq, tq), tq), :], vt_ref[0, j][d * h:d * (h + 1), :])

    for h in range(FOX_HEADS):
        base = (b * FOX_HEADS + h) * nk
        slack = thr_ref[0] + ffirst_ref[base + qi]
        n_live = lax.fori_loop(0, qi, lambda j, c: c + (slack - fend_ref[base + j] >= 0.0).astype(jnp.int32), 0)
        qas = [qa_ref[0, h, :, r2 * half:(r2 + 1) * half] for r2 in range(FOX_CHAINS)]
        ka_d = ka_ref[0, h, pl.ds(diag, tq), :]
        vt_d = vt_ref[0, qi][d * h:d * (h + 1), :]
        causal = [_iota((tq, half), 0) <= (r2 * half + _iota((tq, half), 1)) for r2 in range(FOX_CHAINS)]
        carries = update([(ka_d, vt_d)], qas, causal, [init] * FOX_CHAINS)
        no_mask = [None] * FOX_CHAINS

        def one(t, carries, qas=qas, h=h):
            return update([key_block(h, qi - 1 - t)], qas, no_mask, carries)

        def two(t, carries, qas=qas, h=h):
            j = qi - 1 - (n_live % 2) - 2 * t
            return update([key_block(h, j), key_block(h, j - 1)], qas, no_mask, carries)

        carries = lax.fori_loop(0, n_live % 2, one, carries)
        carries = lax.fori_loop(0, n_live // 2, two, carries)
        for r2 in range(FOX_CHAINS):
            _, l, acc = carries[r2]
            o_ref[0, r2 * half:(r2 + 1) * half, d * h:d * (h + 1)] = (acc / l).T


def _fox_prompt_t(qb, kb, vb, fcol, frow, score_bound):
    b, t, _ = qb.shape
    tq = min(t, FOX_PROMPT_TILE)
    nk = t // tq
    ka, qa, vt = _fox_prep(qb, kb, vb, fcol, frow, tq)
    f_heads = frow[:, :FOX_HEADS, :].reshape(b, FOX_HEADS, nk, tq)
    f_end = f_heads[:, :, :, tq - 1].reshape(-1)
    f_first = f_heads[:, :, :, 0].reshape(-1)
    thr = (2.0 * score_bound + EXP_UNDERFLOW).reshape(1).astype(F32)
    smem = pl.BlockSpec(memory_space=pltpu.SMEM)
    return pl.pallas_call(
        functools.partial(_fox_prompt_t_kernel, tq, nk),
        grid=(b, nk),
        in_specs=[pl.BlockSpec((1, FOX_HEADS, LANES, tq), lambda i, j: (i, 0, 0, j)),
                  pl.BlockSpec((1, FOX_HEADS, t, LANES), lambda i, j: (i, 0, 0, 0)),
                  pl.BlockSpec((1, nk, FOX_INNER, tq), lambda i, j: (i, 0, 0, 0)),
                  smem, smem, smem],
        out_specs=pl.BlockSpec((1, tq, FOX_INNER), lambda i, j: (i, j, 0)),
        out_shape=jax.ShapeDtypeStruct((b, t, FOX_INNER), F32),
        compiler_params=_cparams(("arbitrary", "arbitrary")),
        name="fox_prompt",
    )(qa, ka, vt, f_end, f_first, thr)


def _fox_sample_kernel(tq, tk, npast, q_ref, fq_ref, pk_ref, pv_ref, fpk_ref, k_ref, v_ref, fk_ref, o_ref):
    q_all = q_ref[0]
    fq_all = fq_ref[0]
    qs = [q_all[:, _head_slice(h)] for h in range(FOX_HEADS)]
    fqs = [fq_all[:, F_ROW0 + h:F_ROW0 + h + 1] for h in range(FOX_HEADS)]
    causal = _iota((tq, tq), 0) >= _iota((tq, tq), 1)

    def past_block(j, carries):
        start = pl.multiple_of(j * tk, tk)
        return _attend_block(qs, fqs, pk_ref[0, pl.ds(start, tk), :].astype(BF16),
                             pv_ref[0, pl.ds(start, tk), :].astype(BF16), fpk_ref[0, j], None, carries)

    carries = lax.fori_loop(0, npast, past_block, _softmax_init(tq))
    _attend_finish(_attend_block(qs, fqs, k_ref[0], v_ref[0], fk_ref[0], causal, carries), o_ref)


def _fox_sample(qb, fcol, past_k, past_v, layer, fpast_row, kb, vb, frow):
    b, t, _ = qb.shape
    p = past_k.shape[2]
    tk = min(p, 512)
    npast = p // tk
    fpk = fpast_row.reshape(b, 8, npast, tk).transpose(0, 2, 1, 3)
    bspec = lambda a: pl.BlockSpec((1,) + a.shape[1:], lambda i: (i,) + (0,) * (a.ndim - 1))
    cache = pl.BlockSpec((None, 1, p, FOX_INNER), lambda i: (layer, i, 0, 0))
    args = (qb, fcol, past_k, past_v, fpk, kb, vb, frow)
    return pl.pallas_call(
        functools.partial(_fox_sample_kernel, t, tk, npast),
        grid=(b,),
        in_specs=[bspec(qb), bspec(fcol), cache, cache, bspec(fpk), bspec(kb), bspec(vb), bspec(frow)],
        out_specs=pl.BlockSpec((1, t, FOX_INNER), lambda i: (i, 0, 0)),
        out_shape=jax.ShapeDtypeStruct((b, t, FOX_INNER), F32),
        compiler_params=_cparams(("arbitrary",)),
        name="fox_sample",
    )(*args)


def _mlp_kernel(lm, nchunk, ug_ref, vn_ref, ws_ref, bst_ref, y_ref):
    r_i = _iota((lm, lm), 0)
    c_i = _iota((lm, lm), 1)
    tril = r_i >= c_i
    vn = vn_ref[0].astype(BF16)
    ug = ug_ref[0]
    for g in range(MLP_GROUPS):
        lo, hi = MLP_GROUP_DIM * g, MLP_GROUP_DIM * (g + 1)
        w = jnp.where(tril, ws_ref[g], 0.0).astype(BF16)
        bias = bst_ref[:, g:g + 1]
        for c in range(nchunk):
            r0, r1 = c * lm, (c + 1) * lm
            sv = _dot(w, vn[r0:r1, lo:hi]) + bias
            y_ref[0, r0:r1, lo:hi] = ug[r0:r1, lo:hi] * sv


def _chunk_mlp(ug, vn, ws, bst):
    b, t, _ = ug.shape
    lm = ws.shape[1]
    tm = min(t, 4 * lm)
    tok = pl.BlockSpec((1, tm, MLP_INNER), lambda i, j: (i, j, 0))
    return pl.pallas_call(
        functools.partial(_mlp_kernel, lm, tm // lm),
        grid=(b, t // tm),
        in_specs=[tok, tok, pl.BlockSpec(ws.shape, lambda i, j: (0, 0, 0)),
                  pl.BlockSpec(bst.shape, lambda i, j: (0, 0))],
        out_specs=tok,
        out_shape=jax.ShapeDtypeStruct((b, t, MLP_INNER), F32),
        compiler_params=_cparams(("arbitrary", "arbitrary")),
        name="chunk_mlp",
    )(ug, vn, ws, bst)


def _out_kernel(x_ref, ys_ref, yf_ref, ym_ref, g1_ref, sh_ref, sc_ref, g_ref, wo_ref, x1_ref, h2_ref):
    mix = (_dot(ys_ref[0].astype(BF16), wo_ref[0:SSD_INNER, :])
           + _dot(yf_ref[0].astype(BF16), wo_ref[SSD_INNER:SSD_INNER + FOX_INNER, :])
           + _dot(ym_ref[0].astype(BF16), wo_ref[SSD_INNER + FOX_INNER:D_MODEL, :]))
    x1 = x_ref[0] + g1_ref[0] * mix
    x1_ref[0] = x1
    ms = jnp.mean(x1 * x1, axis=-1, keepdims=True)
    h = x1 * lax.rsqrt(ms + EPS) * g_ref[...]
    h2_ref[0] = _pack_rows((h * (1.0 + sc_ref[0]) + sh_ref[0]).astype(BF16))


def _out_proj(x, ys, yf, ym, g1, sh, sc, g, wo):
    b, t, _ = x.shape
    tm = min(t, 512)
    tok = lambda c: pl.BlockSpec((1, tm, c), lambda i, j: (i, j, 0))
    per_b = pl.BlockSpec((1, 1, D_MODEL), lambda i, j: (i, 0, 0))
    full = lambda a: pl.BlockSpec(a.shape, lambda i, j: (0,) * a.ndim)
    return pl.pallas_call(
        _out_kernel,
        grid=(b, t // tm),
        in_specs=[tok(D_MODEL), tok(SSD_INNER), tok(FOX_INNER), tok(MLP_INNER), per_b, per_b, per_b, full(g), full(wo)],
        out_specs=[tok(D_MODEL), pl.BlockSpec((1, tm // 2, D_MODEL), lambda i, j: (i, j, 0))],
        out_shape=[jax.ShapeDtypeStruct((b, t, D_MODEL), F32),
                   jax.ShapeDtypeStruct((b, t // 2, D_MODEL), jnp.uint32)],
        compiler_params=_cparams(("arbitrary", "arbitrary")),
        name="out_proj",
    )(x, ys, yf, ym, g1, sh, sc, g, wo)


def _top16(scores):
    nk, tn = scores[0].shape
    ridx = _iota((nk, tn), 0).astype(F32)
    r16 = _iota((PEER_TOPK, tn), 0)
    state = [(s, jnp.full((nk, tn), float(PEER_TOPK), F32), jnp.zeros((PEER_TOPK, tn), F32)) for s in scores]
    for it in range(PEER_TOPK):
        nxt_state = []
        for s, pos, tv in state:
            level = [(s[g:g + 8, :], ridx[g:g + 8, :]) for g in range(0, nk, 8)]
            while len(level) > 1:
                nxt = []
                for a in range(0, len(level), 2):
                    (va, ia), (vb, ib) = level[a], level[a + 1]
                    nxt.append((jnp.maximum(va, vb), jnp.where(va >= vb, ia, ib)))
                level = nxt
            v8, i8 = level[0]
            m = jnp.max(v8, axis=0, keepdims=True)
            first = jnp.min(jnp.where(v8 == m, i8, float(nk)), axis=0, keepdims=True)
            sel = ridx == first
            nxt_state.append((jnp.where(sel, -jnp.inf, s), jnp.where(sel, float(it), pos),
                              jnp.where(r16 == it, m, tv)))
        state = nxt_state
    return [(pos, tv) for _, pos, tv in state]


_CAND_GROUPS = [(0, 16)] + [(ka, 8) for ka in range(1, 8)]
_CAND_ROWS = 16 + 7 * 8 + 8


def _pair_select(ta, tb):
    tn = ta.shape[1]
    pieces, flats, valids = [], [], []
    for ka, rows in _CAND_GROUPS:
        pieces.append(ta[ka:ka + 1, :] + tb[0:rows, :])
        kb = _iota((rows, 1), 0)
        flats.append((ka * PEER_TOPK + kb).astype(F32))
        valids.append((ka + 1) * (kb + 1) <= PEER_TOPK)
    pieces.append(ta[8:16, :] + tb[0:1, :])
    flats.append(((8 + _iota((8, 1), 0)) * PEER_TOPK).astype(F32))
    valids.append(_iota((8, 1), 0) >= 0)
    cand0 = jnp.concatenate(pieces, axis=0)
    flat = jnp.concatenate(flats, axis=0)
    valid = jnp.concatenate(valids, axis=0)
    cand0 = jnp.where(valid, cand0, -jnp.inf)
    best = ta[0:1, :] + tb[0:1, :]

    cand = cand0
    selm = jnp.zeros((_CAND_ROWS, tn), F32)
    for _ in range(PEER_TOPK):
        m = jnp.max(cand, axis=0, keepdims=True)
        first = jnp.min(jnp.where(cand == m, flat, 4096.0), axis=0, keepdims=True)
        sel = flat == first
        cand = jnp.where(sel, -jnp.inf, cand)
        selm = jnp.where(sel, 1.0, selm)
    z = jnp.sum(jnp.where(selm > 0.0, jnp.exp(cand0 - best), 0.0), axis=0, keepdims=True)
    cnts = [jnp.sum(selm[0:16, :], axis=0, keepdims=True)]
    for i in range(1, 8):
        cnts.append(jnp.sum(selm[8 + 8 * i:16 + 8 * i, :], axis=0, keepdims=True))
    cnts.append(selm[_CAND_ROWS - 8:_CAND_ROWS, :])
    return jnp.concatenate(cnts, axis=0), z


def _peer_sel_kernel(tn, h_ref, wqt_ref, keys_ref, ea_ref, la_ref, eb_ref, pb_ref, qt_ref):
    qt_ref[...] = _dot_nt(wqt_ref[...], _unpack_rows(h_ref[...])).astype(BF16)

    def head(h, carry):
        row = pl.multiple_of(h * (2 * PEER_HALF), 2 * PEER_HALF)
        sa_all = _dot(keys_ref[2 * h], qt_ref[pl.ds(row, PEER_HALF), :])
        sb_all = _dot(keys_ref[2 * h + 1], qt_ref[pl.ds(row + PEER_HALF, PEER_HALF), :])
        for c in range(tn // LANES):
            cs = slice(c * LANES, (c + 1) * LANES)
            sa, sb = sa_all[:, cs], sb_all[:, cs]
            (pos_a, ta), (pos_b, tb) = _top16([sa, sb])
            cnt, z = _pair_select(ta, tb)
            la = jnp.zeros_like(pos_a)
            for ka in range(PEER_TOPK):
                la = jnp.where(pos_a == float(ka), cnt[ka:ka + 1, :], la)
            ea_ref[h, :, cs] = jnp.where(pos_a < float(PEER_TOPK), jnp.exp(sa - ta[0:1, :]), 0.0) / z
            la_ref[h, :, cs] = la
            eb = jnp.where(pos_b < float(PEER_TOPK), jnp.exp(sb - tb[0:1, :]), 0.0)
            for sub in range(PEER_KEYS // PEER_SUB):
                src = slice(sub * PEER_SUB, (sub + 1) * PEER_SUB)
                dst = slice(sub * PEER_SUB // 2, (sub + 1) * PEER_SUB // 2)
                eb_ref[h, dst, cs] = pltpu.bitcast(eb[src].astype(BF16), jnp.uint32)
                pb_ref[h, dst, cs] = pltpu.bitcast(pos_b[src].astype(BF16), jnp.uint32)
        return carry

    lax.fori_loop(0, PEER_HEADS, head, 0)


def _peer_select(h2, wqt_all, layer, keys):
    n = 2 * h2.shape[0]
    tn = 256
    per_tok = pl.BlockSpec((PEER_HEADS, PEER_KEYS, tn), lambda i: (0, 0, i))
    per_tok_pk = pl.BlockSpec((PEER_HEADS, PEER_KEYS // 2, tn), lambda i: (0, 0, i))
    shp = lambda dt: jax.ShapeDtypeStruct((PEER_HEADS, PEER_KEYS, n), dt)
    shp_pk = jax.ShapeDtypeStruct((PEER_HEADS, PEER_KEYS // 2, n), jnp.uint32)
    return pl.pallas_call(
        functools.partial(_peer_sel_kernel, tn),
        grid=(n // tn,),
        in_specs=[pl.BlockSpec((tn // 2, D_MODEL), lambda i: (i, 0)),
                  pl.BlockSpec((None,) + wqt_all.shape[1:], lambda i: (layer, 0, 0)),
                  pl.BlockSpec(keys.shape, lambda i: (0, 0, 0))],
        out_specs=[per_tok, per_tok, per_tok_pk, per_tok_pk],
        out_shape=[shp(F32), shp(F32), shp_pk, shp_pk],
        scratch_shapes=[pltpu.VMEM((PEER_HEADS * 2 * PEER_HALF, tn), BF16)],
        compiler_params=_cparams(("arbitrary",)),
        name="peer_select",
    )(h2, wqt_all, keys)


def _peer_gate_stage(ec, ia0, rows, chunks, at_r, ga_w, ea_ref, la_ref, eb_ref, pb_ref):
    rows_per = ec // PEER_KEYS
    for r in rows:
        for c in chunks:
            cs = slice(c * LANES, (c + 1) * LANES)
            for sub in range(PEER_KEYS // PEER_SUB):
                pk = slice(sub * PEER_SUB // 2, (sub + 1) * PEER_SUB // 2)
                gate = None
                for h in range(PEER_HEADS):
                    la = la_ref[h, pl.ds(ia0, rows_per), cs][r:r + 1, :]
                    ea = ea_ref[h, pl.ds(ia0, rows_per), cs][r:r + 1, :]
                    la = jnp.broadcast_to(la, (PEER_SUB, LANES)).astype(BF16)
                    ea = jnp.broadcast_to(ea, (PEER_SUB, LANES)).astype(BF16)
                    pb = pltpu.bitcast(pb_ref[h, pk, cs], BF16)
                    eb = pltpu.bitcast(eb_ref[h, pk, cs], BF16)
                    term = jnp.where(pb < la, eb, 0.0) * ea
                    gate = term if gate is None else gate + term
                ex = slice(r * PEER_KEYS + sub * PEER_SUB, r * PEER_KEYS + (sub + 1) * PEER_SUB)
                ga_w[ex, cs] = gate * _gelu(at_r[ex, cs]).astype(BF16)


def _peer_dense_kernel(tn, ec, h_ref, u_ref, vt_ref, ea_ref, la_ref, eb_ref, pb_ref, o_ref, at, ga, acc):
    e = pl.program_id(1)
    ne = pl.num_programs(1)

    @pl.when(e == 0)
    def _():
        acc[...] = jnp.zeros_like(acc)

    rows_per = ec // PEER_KEYS
    ia0 = pl.multiple_of(e * rows_per, rows_per)
    parts = PEER_DENSE_PARTS
    per = ec // parts
    h_t = _unpack_rows(h_ref[...])
    for g in range(parts):
        rows = slice(g * per, (g + 1) * per)
        pk = slice(g * per // 2, (g + 1) * per // 2)
        at[rows, :] = _dot_nt(_unpack_rows(u_ref[pk, :]), h_t)
    for g in range(parts):
        rows = slice(g * per, (g + 1) * per)
        _peer_gate_stage(ec, ia0, range(g * rows_per // parts, (g + 1) * rows_per // parts), range(tn // LANES),
                         at, ga, ea_ref, la_ref, eb_ref, pb_ref)
        acc[...] += _dot(_unpack_rows(vt_ref[:, rows]), ga[rows, :])

    @pl.when(e == ne - 1)
    def _():
        o_ref[...] = acc[...].T


def _peer_dense(h2, u_all, vt_all, layer, ea, la, eb, pb):
    n = 2 * h2.shape[0]
    tn, ec = PEER_TOKEN_TILE, PEER_EXPERT_CHUNK
    per_tok = pl.BlockSpec((PEER_HEADS, PEER_KEYS, tn), lambda i, e: (0, 0, i))
    per_tok_pk = pl.BlockSpec((PEER_HEADS, PEER_KEYS // 2, tn), lambda i, e: (0, 0, i))
    return pl.pallas_call(
        functools.partial(_peer_dense_kernel, tn, ec),
        grid=(n // tn, PEER_EXPERTS // ec),
        in_specs=[pl.BlockSpec((tn // 2, D_MODEL), lambda i, e: (i, 0)),
                  pl.BlockSpec((None, ec // 2, D_MODEL), lambda i, e: (layer, e, 0)),
                  pl.BlockSpec((None, D_MODEL // 2, ec), lambda i, e: (layer, 0, e)),
                  per_tok, per_tok, per_tok_pk, per_tok_pk],
        out_specs=pl.BlockSpec((tn, D_MODEL), lambda i, e: (i, 0)),
        out_shape=jax.ShapeDtypeStruct((n, D_MODEL), F32),
        scratch_shapes=[pltpu.VMEM((ec, tn), F32), pltpu.VMEM((ec, tn), BF16), pltpu.VMEM((D_MODEL, tn), F32)],
        compiler_params=_cparams(("arbitrary", "arbitrary")),
        name="peer_dense",
    )(h2, u_all, vt_all, ea, la, eb, pb)


def _resid_kernel(x_ref, p_ref, g_ref, o_ref):
    o_ref[0] = x_ref[0] + g_ref[0] * p_ref[0]


def _residual(x, peer, g2):
    b, t, _ = x.shape
    tm = min(t, 512)
    tok = pl.BlockSpec((1, tm, D_MODEL), lambda i, j: (i, j, 0))
    return pl.pallas_call(
        _resid_kernel,
        grid=(b, t // tm),
        in_specs=[tok, tok, pl.BlockSpec((1, 1, D_MODEL), lambda i, j: (i, 0, 0))],
        out_specs=tok,
        out_shape=jax.ShapeDtypeStruct((b, t, D_MODEL), F32),
        compiler_params=_cparams(("arbitrary", "arbitrary")),
        name="peer_residual",
    )(x, peer, g2)


def _lane_pad(vec, offset, width=LANES):
    out = jnp.zeros((width,), F32)
    return out.at[offset:offset + vec.shape[0]].set(vec.astype(F32))


def _layer_params(l, norm1_g, norm2_g, w_in, conv_w, conv_b, dt_bias, a_log, d_skip, ssd_norm_g, q_norm_g,
                  k_norm_g, fgate_b, w_s, b_s, w_out, peer_keys):
    w = w_in[l]
    o = [0]
    for sz in (SSD_INNER, CONV_CH, SSD_HEADS, FOX_INNER, FOX_INNER, FOX_INNER, FOX_HEADS, MLP_INNER, MLP_INNER):
        o.append(o[-1] + sz)
    wz, wxbc, wdt, wq, wk, wv, wf, wu, wvm = [w[:, o[i]:o[i + 1]] for i in range(9)]
    w_small = jnp.zeros((D_MODEL, SMALL_COLS), F32)
    w_small = w_small.at[:, DT_ROW0:DT_ROW0 + SSD_HEADS].set(wdt).at[:, F_ROW0:F_ROW0 + FOX_HEADS].set(wf)
    w_r = jnp.concatenate([wz, wxbc, wq, wk, wv, wu, wvm, w_small], axis=1).astype(BF16)
    ws_t = w_small[:, :SMALL_ROWS].T.astype(BF16)
    a_neg = -jnp.exp(a_log[l].astype(F32))
    grp = jnp.arange(FOX_INNER) // FOX_HEAD_DIM
    gm = jnp.where(grp[:, None] == grp[None, :], 1.0 / FOX_HEAD_DIM, 0.0).astype(BF16)
    heads = jnp.arange(SSD_INNER) // SSD_HEAD_DIM
    e_mat = (jnp.arange(LANES)[:, None] == heads[None, :]).astype(BF16)
    return dict(
        norm1_g=norm1_g[l][None], norm2_g=norm2_g[l][None], w_r=w_r, ws_t=ws_t,
        gq=(jnp.tile(q_norm_g[l], FOX_HEADS) * (FOX_HEAD_DIM ** -0.5))[None],
        gk=jnp.tile(k_norm_g[l], FOX_HEADS)[None], gm=gm,
        fox_bound=1.02 * FOX_HEAD_DIM ** 0.5 * jnp.max(jnp.abs(q_norm_g[l])) * jnp.max(jnp.abs(k_norm_g[l])),
        conv_w=conv_w[l], conv_b=conv_b[l][None],
        dtb_c=_lane_pad(dt_bias[l], DT_ROW0)[None], dtb_r=dt_bias[l].astype(F32)[:, None],
        a_c=_lane_pad(a_neg, DT_ROW0)[None], a_r=a_neg[:, None],
        dsk_x=jnp.repeat(d_skip[l].astype(F32), SSD_HEAD_DIM)[None], ssd_g=ssd_norm_g[l][None], e_mat=e_mat,
        fb_c=_lane_pad(fgate_b[l], F_ROW0)[None], fb_r=_lane_pad(fgate_b[l], 0, 8)[:, None],
        w_s=w_s[l], bs_t=jnp.zeros((MLP_CHUNK, LANES), F32).at[:, :MLP_GROUPS].set(b_s[l].T),
        w_out=w_out[l].astype(BF16),
        keys=peer_keys[l].reshape(PEER_HEADS * 2, PEER_KEYS, PEER_HALF).astype(BF16),
    )


def _stream_mixers(x, peer, g2_prev, mod, p, hist8, s0t, past):
    sh1, sc1, g1, sh2, sc2, _ = mod
    b, t, _ = x.shape
    x, (z, xbc, qb, kn, kb, v, vb, ug, vn, sm, smt) = _in_proj(
        x, peer, g2_prev, sh1, sc1, p['norm1_g'], p['w_r'], p['ws_t'], p['gq'], p['gk'], p['gm'])
    L = SSD_CHUNK_PROMPT if t % SSD_CHUNK_PROMPT == 0 else t
    y_ssd, s_fin, c_fin = _ssd(xbc, z, sm, smt, hist8, s0t, p['conv_w'], p['conv_b'], p['dtb_c'], p['dtb_r'],
                               p['a_c'], p['a_r'], p['dsk_x'], p['ssd_g'], p['e_mat'], L)
    zeros8 = jnp.zeros((b, 8, LANES), F32)
    if past is None:
        logf_c, f_col, f_row, _, _ = _fcum(sm, smt, F_ROW0 // 8, p['fb_c'], p['fb_r'], zeros8, zeros8, True)
        y_fox = _fox_prompt_t(qb, kb, vb, f_col, f_row, p['fox_bound'])
    else:
        pk, pv, layer, plf_col, plf_row = past
        _, _, fp_row, end_c, end_r = _fcum(plf_col, plf_row, 0, p['fb_c'], p['fb_r'], zeros8, zeros8, False)
        logf_c, f_col, f_row, _, _ = _fcum(sm, smt, F_ROW0 // 8, p['fb_c'], p['fb_r'], end_c, end_r, True)
        y_fox = _fox_sample(qb, f_col, pk, pv, layer, fp_row, kb, vb, f_row)
    lm = MLP_CHUNK if t % MLP_CHUNK == 0 else t
    y_mlp = _chunk_mlp(ug, vn, p['w_s'][:, :lm, :lm], p['bs_t'][:lm])
    x1, h2 = _out_proj(x, y_ssd, y_fox, y_mlp, g1, sh2, sc2, p['norm2_g'], p['w_out'])
    logf = logf_c[:, :, F_ROW0:F_ROW0 + FOX_HEADS]
    new_ssm = s_fin.reshape(b, SSD_STATE, SSD_HEADS, SSD_HEAD_DIM).transpose(0, 2, 3, 1)
    new_conv = c_fin[:, 8 - (CONV_W - 1):, :]
    kc = kn.reshape(b, t, FOX_HEADS, FOX_HEAD_DIM)
    vc = v.reshape(b, t, FOX_HEADS, FOX_HEAD_DIM)
    return x1, h2, (kc, vc, logf, new_ssm, new_conv, vn)


def _peer(h2, p, tables, layer):
    b, t2, _ = h2.shape
    t = 2 * t2
    n_tok = b * t
    n_pad = -n_tok % PEER_TOKEN_TILE
    flat = h2.reshape(n_tok // 2, D_MODEL)
    if n_pad:
        flat = jnp.concatenate([flat, jnp.zeros((n_pad // 2, D_MODEL), jnp.uint32)], axis=0)
    wqt_all, u_all, vt_all = tables
    ea, la, eb, pb = _peer_select(flat, wqt_all, layer, p['keys'])
    out = _peer_dense(flat, u_all, vt_all, layer, ea, la, eb, pb)
    return out[:n_tok].reshape(b, t, D_MODEL)


def kernel(x_prompt, x_sample, c_prompt, c_sample, cache_fox_k, cache_fox_v, cache_fox_logf, state_ssm, state_conv, norm1_g, norm2_g, w_ada, b_ada, w_in, conv_w, conv_b, dt_bias, a_log, d_skip, ssd_norm_g, q_norm_g, k_norm_g, fgate_b, w_s, b_s, w_out, peer_wq, peer_keys, peer_u, peer_v):
    depth = w_ada.shape[0]
    bp, tp, _ = x_prompt.shape
    bs, ts, _ = x_sample.shape
    past_len = cache_fox_k.shape[2]
    mod_all = _modulation(jnp.concatenate([c_prompt, c_sample], axis=0).astype(F32), w_ada, b_ada)

    peer_tables = (_to_bf16_t(peer_wq, D_MODEL, 512, False), _to_bf16_packed(peer_u, 1024),
                   _to_bf16_t(peer_v, 512, D_MODEL, True))

    past_k_all = cache_fox_k.reshape(depth, bs, past_len, FOX_INNER)
    past_v_all = cache_fox_v.reshape(depth, bs, past_len, FOX_INNER)

    xp, xs = x_prompt, x_sample
    peer_p = peer_s = g2p = g2s = None
    outs = [[] for _ in range(11)]
    for l in range(depth):
        p = _layer_params(l, norm1_g, norm2_g, w_in, conv_w, conv_b, dt_bias, a_log, d_skip, ssd_norm_g, q_norm_g,
                          k_norm_g, fgate_b, w_s, b_s, w_out, peer_keys)
        mods = [m[:, None, :] for m in jnp.split(mod_all[l], 6, axis=-1)]
        mod_p = [m[:bp] for m in mods]
        mod_s = [m[bp:] for m in mods]
        hist_p = jnp.zeros((bp, 8, CONV_CH), F32)
        s0_p = jnp.zeros((bp, SSD_STATE, SSD_INNER), F32)
        hist_s = jnp.concatenate([jnp.zeros((bs, 8 - (CONV_W - 1), CONV_CH), F32), state_conv[l].astype(F32)], axis=1)
        s0_s = state_ssm[l].astype(F32).transpose(0, 3, 1, 2).reshape(bs, SSD_STATE, SSD_INNER)
        plf = cache_fox_logf[l].astype(F32)
        plf_col = jnp.pad(plf, ((0, 0), (0, 0), (F_ROW0, LANES - F_ROW0 - FOX_HEADS)))
        plf_row = jnp.pad(plf.transpose(0, 2, 1), ((0, 0), (0, 8 - FOX_HEADS), (0, 0)))
        past = (past_k_all, past_v_all, l, plf_col, plf_row)

        x1p, h2p, st_p = _stream_mixers(xp, peer_p, g2p, mod_p, p, hist_p, s0_p, None)
        x1s, h2s, st_s = _stream_mixers(xs, peer_s, g2s, mod_s, p, hist_s, s0_s, past)

        peer_p = _peer(h2p, p, peer_tables, l)
        peer_s = _peer(h2s, p, peer_tables, l)
        xp, xs, g2p, g2s = x1p, x1s, mod_p[5], mod_s[5]
        for i in range(5):
            outs[i].append(st_p[i])
        for i in range(6):
            outs[5 + i].append(st_s[i])

    yp = _residual(xp, peer_p, g2p)
    ys = _residual(xs, peer_s, g2s)
    return (yp, ys) + tuple(jnp.stack(o) for o in outs)
```

```python
import functools

import jax
import jax.numpy as jnp
from jax import lax
from jax.experimental import pallas as pl
from jax.experimental.pallas import tpu as pltpu

F32 = jnp.float32
BF16 = jnp.bfloat16
EPS = 1e-6

D_MODEL = 1024
SSD_HEADS = 8
SSD_HEAD_DIM = 64
SSD_INNER = SSD_HEADS * SSD_HEAD_DIM
SSD_GROUPS = 2
SSD_STATE = 64
CONV_W = 4
CONV_CH = SSD_INNER + 2 * SSD_GROUPS * SSD_STATE
FOX_HEADS = 4
FOX_HEAD_DIM = 64
FOX_INNER = FOX_HEADS * FOX_HEAD_DIM
MLP_GROUPS = 4
MLP_GROUP_DIM = 64
MLP_INNER = MLP_GROUPS * MLP_GROUP_DIM
MLP_CHUNK = 128
FOX_PROMPT_TILE = 512
FOX_CHAINS = 4
FCUM_TILE = 2048
FCUM_SUB = 512
SSD_CHUNK_PROMPT = 128
PEER_HEADS = 8
PEER_KEYS = 128
PEER_EXPERTS = PEER_KEYS * PEER_KEYS
PEER_HALF = 128
PEER_TOPK = 16
PEER_TOKEN_TILE = 512
PEER_EXPERT_CHUNK = 2048
LANES = 128
MXU_DIM = 256
PEER_DENSE_PARTS = 4
PEER_SUB = 128
SMALL_COLS = LANES
SMALL_ROWS = 16
DT_ROW0, F_ROW0 = 0, 8
NEG = -1e30
EXP_UNDERFLOW = 104.0
VMEM_LIMIT = 56 * 1024 * 1024

_Z0, _XBC0, _Q0, _K0, _V0, _U0, _VM0, _SM0 = 0, 512, 1280, 1536, 1792, 2048, 2304, 2560
PROJ_COLS = _SM0 + SMALL_COLS


def _cparams(sem, flags=None):
    return pltpu.CompilerParams(dimension_semantics=sem, vmem_limit_bytes=VMEM_LIMIT, flags=flags)


def _dot(a, b):
    return jnp.dot(a, b, preferred_element_type=F32)


def _dot_nt(a, b):
    return lax.dot_general(a, b, (((1,), (1,)), ((), ())), preferred_element_type=F32)


def _dot_tn(a, b):
    return lax.dot_general(a, b, (((0,), (0,)), ((), ())), preferred_element_type=F32)


def _split2(x):
    hi = x.astype(BF16)
    lo = (x - hi.astype(F32)).astype(BF16)
    return hi, lo


def _split3(x):
    hi = x.astype(BF16)
    r = x - hi.astype(F32)
    mid = r.astype(BF16)
    lo = (r - mid.astype(F32)).astype(BF16)
    return hi, mid, lo


def _dot3_l(x, w):
    hi, mid, lo = _split3(x)
    return _dot(hi, w) + _dot(mid, w) + _dot(lo, w)


def _dot3_r(w, x):
    hi, mid, lo = _split3(x)
    return _dot(w, hi) + _dot(w, mid) + _dot(w, lo)


def _sigmoid(x):
    return 1.0 / (1.0 + jnp.exp(-x))


def _silu(x):
    return x * _sigmoid(x)


def _softplus(x):
    return jnp.maximum(x, 0.0) + jnp.log1p(jnp.exp(-jnp.abs(x)))


_GELU_A = 2.0 * 0.7978845608028654
_GELU_B = _GELU_A * 0.044715


def _gelu(x):
    z2 = x * (_GELU_A + _GELU_B * (x * x))
    return x / (1.0 + jnp.exp(-z2))


def _iota(shape, dim):
    return lax.broadcasted_iota(jnp.int32, shape, dim)


def _pack_rows(x):
    return pltpu.bitcast(x, jnp.uint32)


def _unpack_rows(x):
    return pltpu.bitcast(x, BF16)


def _cast_kernel(x_ref, o_ref):
    o_ref[0] = _pack_rows(x_ref[0].astype(BF16))


def _cast_t_kernel(packed, x_ref, o_ref):
    xt = x_ref[0].T.astype(BF16)
    o_ref[0] = _pack_rows(xt) if packed else xt


def _to_bf16_packed(x, rows):
    n, r, c = x.shape
    return pl.pallas_call(
        _cast_kernel,
        grid=(n, r // rows),
        in_specs=[pl.BlockSpec((1, rows, c), lambda l, i: (l, i, 0))],
        out_specs=pl.BlockSpec((1, rows // 2, c), lambda l, i: (l, i, 0)),
        out_shape=jax.ShapeDtypeStruct((n, r // 2, c), jnp.uint32),
        compiler_params=_cparams(("arbitrary", "arbitrary")),
        name="cast_bf16",
    )(x)


def _to_bf16_t(x, rows, cols, packed):
    n, r, c = x.shape
    div = 2 if packed else 1
    return pl.pallas_call(
        functools.partial(_cast_t_kernel, packed),
        grid=(n, r // rows, c // cols),
        in_specs=[pl.BlockSpec((1, rows, cols), lambda l, i, j: (l, i, j))],
        out_specs=pl.BlockSpec((1, cols // div, rows), lambda l, i, j: (l, j, i)),
        out_shape=jax.ShapeDtypeStruct((n, c // div, r), jnp.uint32 if packed else BF16),
        compiler_params=_cparams(("arbitrary", "arbitrary", "arbitrary")),
        name="cast_bf16_transposed",
    )(x)


def _mod_kernel(c_ref, w_ref, b_ref, o_ref):
    c = c_ref[...]
    o_ref[0] = jnp.dot(_silu(c), w_ref[0], preferred_element_type=F32,
                       precision=lax.Precision.HIGHEST) + b_ref[0]


def _modulation(c_all, w_ada, b_ada):
    depth, _, n6 = w_ada.shape
    bc = c_all.shape[0]
    tn = 1536
    return pl.pallas_call(
        _mod_kernel,
        grid=(depth, n6 // tn),
        in_specs=[pl.BlockSpec((bc, D_MODEL), lambda l, j: (0, 0)),
                  pl.BlockSpec((1, D_MODEL, tn), lambda l, j: (l, 0, j)),
                  pl.BlockSpec((1, 1, tn), lambda l, j: (l, 0, j))],
        out_specs=pl.BlockSpec((1, bc, tn), lambda l, j: (l, 0, j)),
        out_shape=jax.ShapeDtypeStruct((depth, bc, n6), F32),
        compiler_params=_cparams(("arbitrary", "arbitrary")),
        name="adaln_mod",
    )(c_all, w_ada, b_ada.reshape(depth, 1, n6))


def _in_kernel(with_peer, *refs):
    if with_peer:
        x_ref, p_ref, g2_ref = refs[:3]
        refs = refs[3:]
    else:
        x_ref = refs[0]
        refs = refs[1:]
    (sh_ref, sc_ref, g_ref, w_ref, wst_ref, gq_ref, gk_ref, gm_ref) = refs[:8]
    outs = refs[8:]
    if with_peer:
        xo_ref = outs[0]
        outs = outs[1:]
    (z_ref, xbc_ref, q_ref, k_ref, kb_ref, v_ref, vb_ref, ug_ref, vn_ref, sm_ref, smt_ref) = outs

    x = x_ref[0]
    if with_peer:
        x = x + g2_ref[0] * p_ref[0]
        xo_ref[0] = x
    ms = jnp.mean(x * x, axis=-1, keepdims=True)
    h = x * lax.rsqrt(ms + EPS) * g_ref[...]
    h = h * (1.0 + sc_ref[0]) + sh_ref[0]
    hb = h.astype(BF16)
    proj = _dot(hb, w_ref[...])
    z_ref[0] = proj[:, _Z0:_XBC0]
    xbc_ref[0] = proj[:, _XBC0:_Q0]
    q = proj[:, _Q0:_K0]
    k = proj[:, _K0:_V0]
    v = proj[:, _V0:_U0]
    u = proj[:, _U0:_VM0]
    vm = proj[:, _VM0:_SM0]
    sm_ref[0] = proj[:, _SM0:PROJ_COLS]
    smt_ref[0] = _dot_nt(wst_ref[...], hb)
    gm = gm_ref[...]

    def gmean(y):
        hi, lo = _split2(y)
        return _dot(hi, gm) + _dot(lo, gm)

    qn = q * lax.rsqrt(gmean(q * q) + EPS) * gq_ref[...]
    kn = k * lax.rsqrt(gmean(k * k) + EPS) * gk_ref[...]
    q_ref[0] = qn.astype(BF16)
    k_ref[0] = kn
    kb_ref[0] = kn.astype(BF16)
    v_ref[0] = v
    vb_ref[0] = v.astype(BF16)
    ug_ref[0] = _gelu(u)
    gv = _gelu(vm)
    mu = gmean(gv)
    cen = gv - mu
    var = gmean(cen * cen)
    vn_ref[0] = cen * lax.rsqrt(var + EPS)


def _in_proj(x, peer, g2, sh, sc, g, w_r, ws_t, gq, gk, gm):
    b, t, _ = x.shape
    tm = min(t, 512)
    with_peer = peer is not None
    tok = lambda c: pl.BlockSpec((1, tm, c), lambda i, j: (i, j, 0))
    per_b = pl.BlockSpec((1, 1, D_MODEL), lambda i, j: (i, 0, 0))
    full = lambda a: pl.BlockSpec(a.shape, lambda i, j: (0,) * a.ndim)
    in_specs = [tok(D_MODEL)]
    args = [x]
    if with_peer:
        in_specs += [tok(D_MODEL), per_b]
        args += [peer, g2]
    in_specs += [per_b, per_b, full(g), full(w_r), full(ws_t), full(gq), full(gk), full(gm)]
    args += [sh, sc, g, w_r, ws_t, gq, gk, gm]
    out_cols = [(SSD_INNER, F32), (CONV_CH, F32), (FOX_INNER, BF16), (FOX_INNER, F32), (FOX_INNER, BF16),
                (FOX_INNER, F32), (FOX_INNER, BF16), (MLP_INNER, F32), (MLP_INNER, F32), (SMALL_COLS, F32)]
    out_specs = [tok(c) for c, _ in out_cols]
    out_shape = [jax.ShapeDtypeStruct((b, t, c), dt) for c, dt in out_cols]
    out_specs.append(pl.BlockSpec((1, SMALL_ROWS, tm), lambda i, j: (i, 0, j)))
    out_shape.append(jax.ShapeDtypeStruct((b, SMALL_ROWS, t), F32))
    if with_peer:
        out_specs = [tok(D_MODEL)] + out_specs
        out_shape = [jax.ShapeDtypeStruct((b, t, D_MODEL), F32)] + out_shape
    res = pl.pallas_call(
        functools.partial(_in_kernel, with_peer),
        grid=(b, t // tm),
        in_specs=in_specs, out_specs=out_specs, out_shape=out_shape,
        compiler_params=_cparams(("arbitrary", "arbitrary")),
        name="in_proj",
    )(*args)
    if with_peer:
        return res[0], res[1:]
    return x, res


def _ssd_kernel(L, xbc_ref, z_ref, sm_ref, smt_ref, hist_ref, s0_ref, cw_ref, cb_ref, dtbc_ref, dtbr_ref,
                ac_ref, ar_ref, dsk_ref, ng_ref, e_ref, y_ref, sfin_ref, cfin_ref, xpad, st, ybuf):
    c = pl.program_id(1)
    nc = pl.num_programs(1)

    @pl.when(c == 0)
    def _():
        xpad[0:8, :] = hist_ref[0]
        st[...] = s0_ref[0]

    xpad[8:8 + L, :] = xbc_ref[0]
    conv = cb_ref[...]
    for tap in range(CONV_W):
        conv = conv + xpad[5 + tap:5 + tap + L, :] * cw_ref[tap:tap + 1, :]
    tail = xpad[L:L + 8, :]
    xpad[0:8, :] = tail
    xc = _silu(conv)
    xs = xc[:, 0:SSD_INNER]

    r_i = _iota((L, L), 0)
    c_i = _iota((L, L), 1)
    causal = r_i >= c_i
    tri = jnp.where(causal, 1.0, 0.0).astype(BF16)
    triu = jnp.where(r_i <= c_i, 1.0, 0.0).astype(BF16)

    dtc = _softplus(sm_ref[0] + dtbc_ref[...])
    acum_c = _dot3_r(tri, dtc * ac_ref[...])
    e = e_ref[...]
    acum_x = _dot3_l(acum_c, e)
    dt_x = _dot3_l(dtc, e)
    dtr = _softplus(smt_ref[0][DT_ROW0:DT_ROW0 + 8, :] + dtbr_ref[...])
    acum_r = _dot3_l(dtr * ar_ref[...], triu)

    bmat = [xc[:, SSD_INNER + SSD_STATE * g:SSD_INNER + SSD_STATE * (g + 1)].astype(BF16) for g in range(SSD_GROUPS)]
    c0 = SSD_INNER + SSD_GROUPS * SSD_STATE
    cmat = [xc[:, c0 + SSD_STATE * g:c0 + SSD_STATE * (g + 1)].astype(BF16) for g in range(SSD_GROUPS)]
    cb = [_dot_nt(cmat[g], bmat[g]) for g in range(SSD_GROUPS)]
    gw = SSD_INNER // SSD_GROUPS
    stb = st[...].astype(BF16)
    y_off = jnp.concatenate([_dot(cmat[g], stb[:, gw * g:gw * (g + 1)]) for g in range(SSD_GROUPS)], axis=1)
    xsb = xs.astype(BF16)
    hpg = SSD_HEADS // SSD_GROUPS
    mixes = []
    for h in range(SSD_HEADS):
        seg = acum_c[:, h:h + 1] - acum_r[h:h + 1, :]
        dec = jnp.where(causal, jnp.exp(jnp.minimum(seg, 0.0)), 0.0)
        mixes.append((cb[h // hpg] * dec * dtr[h:h + 1, :]).astype(BF16))
    heads = [_dot(mixes[h], xsb[:, SSD_HEAD_DIM * h:SSD_HEAD_DIM * (h + 1)]) for h in range(SSD_HEADS)]
    for h in range(SSD_HEADS):
        ybuf[:, SSD_HEAD_DIM * h:SSD_HEAD_DIM * (h + 1)] = heads[h]

    y = ybuf[...] + y_off * jnp.exp(acum_x) + dsk_ref[...] * xs
    a_end = acum_x[L - 1:L, :]
    xw = (xs * dt_x * jnp.exp(a_end - acum_x)).astype(BF16)
    new_states = jnp.concatenate([_dot_tn(bmat[g], xw[:, gw * g:gw * (g + 1)]) for g in range(SSD_GROUPS)], axis=1)
    st[...] = st[...] * jnp.exp(a_end) + new_states

    yg = y * _silu(z_ref[0])
    ms = jnp.mean(yg * yg, axis=-1, keepdims=True)
    y_ref[0] = yg * lax.rsqrt(ms + EPS) * ng_ref[...]

    @pl.when(c == nc - 1)
    def _():
        sfin_ref[0] = st[...]
        cfin_ref[0] = tail


def _ssd(xbc, z, sm, smt, hist8, s0t, cw, cb, dtb_c, dtb_r, a_c, a_r, dsk_x, ng, e_mat, L):
    b, t, _ = xbc.shape
    tok = lambda c: pl.BlockSpec((1, L, c), lambda i, j: (i, j, 0))
    full = lambda a: pl.BlockSpec(a.shape, lambda i, j: (0,) * a.ndim)
    per_b = lambda a: pl.BlockSpec((1,) + a.shape[1:], lambda i, j: (i,) + (0,) * (a.ndim - 1))
    return pl.pallas_call(
        functools.partial(_ssd_kernel, L),
        grid=(b, t // L),
        in_specs=[tok(CONV_CH), tok(SSD_INNER), tok(SMALL_COLS),
                  pl.BlockSpec((1, SMALL_ROWS, L), lambda i, j: (i, 0, j)),
                  per_b(hist8), per_b(s0t), full(cw), full(cb), full(dtb_c), full(dtb_r), full(a_c), full(a_r),
                  full(dsk_x), full(ng), full(e_mat)],
        out_specs=[tok(SSD_INNER),
                   pl.BlockSpec((1, SSD_STATE, SSD_INNER), lambda i, j: (i, 0, 0)),
                   pl.BlockSpec((1, 8, CONV_CH), lambda i, j: (i, 0, 0))],
        out_shape=[jax.ShapeDtypeStruct((b, t, SSD_INNER), F32),
                   jax.ShapeDtypeStruct((b, SSD_STATE, SSD_INNER), F32),
                   jax.ShapeDtypeStruct((b, 8, CONV_CH), F32)],
        scratch_shapes=[pltpu.VMEM((L + 8, CONV_CH), F32), pltpu.VMEM((SSD_STATE, SSD_INNER), F32),
                        pltpu.VMEM((L, SSD_INNER), F32)],
        compiler_params=_cparams(("arbitrary", "arbitrary")),
        name="ssd_scan",
    )(xbc, z, sm, smt, hist8, s0t, cw, cb, dtb_c, dtb_r, a_c, a_r, dsk_x, ng, e_mat)


def _fcum_kernel(tf, activate, colsrc_ref, rowsrc_ref, fbc_ref, fbr_ref, initc_ref, initr_ref,
                 lfc_ref, fc_ref, fr_ref, endc_ref, endr_ref, carc, carr):
    j = pl.program_id(1)

    @pl.when(j == 0)
    def _():
        carc[...] = initc_ref[0]
        carr[...] = initr_ref[0]

    ts = min(tf, FCUM_SUB)
    r_i = _iota((ts, ts), 0)
    c_i = _iota((ts, ts), 1)
    tri = jnp.where(r_i >= c_i, 1.0, 0.0).astype(BF16)
    triu = jnp.where(r_i <= c_i, 1.0, 0.0).astype(BF16)
    local = []
    for k in range(tf // ts):
        xc = colsrc_ref[0, k * ts:(k + 1) * ts, :]
        xr = rowsrc_ref[0, :, k * ts:(k + 1) * ts]
        if activate:
            xc = -_softplus(-(xc + fbc_ref[...]))
            xr = -_softplus(-(xr + fbr_ref[...]))
        lfc_ref[0, k * ts:(k + 1) * ts, :] = xc
        local.append((_dot3_r(tri, xc), _dot3_l(xr, triu)))
    car_c = carc[0:1, :]
    car_r = carr[:, 0:1]
    for k, (pc, pr) in enumerate(local):
        fcol = car_c + pc
        frow = car_r + pr
        fc_ref[0, k * ts:(k + 1) * ts, :] = fcol
        fr_ref[0, :, k * ts:(k + 1) * ts] = frow
        car_c = fcol[ts - 1:ts, :]
        car_r = frow[:, ts - 1:ts]
    carc[...] = jnp.broadcast_to(car_c, carc.shape)
    carr[...] = jnp.broadcast_to(car_r, carr.shape)
    endc_ref[0] = carc[...]
    endr_ref[0] = carr[...]


def _fcum(colsrc, rowsrc, row_block, fb_c, fb_r, init_c, init_r, activate):
    b, t, _ = colsrc.shape
    tf = min(t, FCUM_TILE)
    full = lambda a: pl.BlockSpec(a.shape, lambda i, j: (0,) * a.ndim)
    per_b = lambda a: pl.BlockSpec((1,) + a.shape[1:], lambda i, j: (i,) + (0,) * (a.ndim - 1))
    return pl.pallas_call(
        functools.partial(_fcum_kernel, tf, activate),
        grid=(b, t // tf),
        in_specs=[pl.BlockSpec((1, tf, LANES), lambda i, j: (i, j, 0)),
                  pl.BlockSpec((1, 8, tf), lambda i, j: (i, row_block, j)),
                  full(fb_c), full(fb_r), per_b(init_c), per_b(init_r)],
        out_specs=[pl.BlockSpec((1, tf, LANES), lambda i, j: (i, j, 0)),
                   pl.BlockSpec((1, tf, LANES), lambda i, j: (i, j, 0)),
                   pl.BlockSpec((1, 8, tf), lambda i, j: (i, 0, j)),
                   pl.BlockSpec((1, 8, LANES), lambda i, j: (i, 0, 0)),
                   pl.BlockSpec((1, 8, LANES), lambda i, j: (i, 0, 0))],
        out_shape=[jax.ShapeDtypeStruct((b, t, LANES), F32), jax.ShapeDtypeStruct((b, t, LANES), F32),
                   jax.ShapeDtypeStruct((b, 8, t), F32), jax.ShapeDtypeStruct((b, 8, LANES), F32),
                   jax.ShapeDtypeStruct((b, 8, LANES), F32)],
        scratch_shapes=[pltpu.VMEM((8, LANES), F32), pltpu.VMEM((8, LANES), F32)],
        compiler_params=_cparams(("arbitrary", "arbitrary")),
        name="forget_cumsum",
    )(colsrc, rowsrc, fb_c, fb_r, init_c, init_r)


def _softmax_stats(s, carry):
    m, l, acc = carry
    tq, tk = s.shape
    m_new = jnp.maximum(m, jnp.max(s, axis=-1, keepdims=True))
    p = jnp.exp(s - m_new)
    alpha = jnp.exp(m - m_new)
    if tk % LANES == 0:
        psum = p[:, 0:LANES]
        for c in range(1, tk // LANES):
            psum = psum + p[:, c * LANES:(c + 1) * LANES]
    else:
        psum = jnp.where(_iota((tq, LANES), 1) == 0, jnp.sum(p, axis=-1, keepdims=True), 0.0)
    return m_new, alpha * l + psum, alpha * acc, p.astype(BF16)


def _softmax_init(tq):
    return tuple((jnp.full((tq, 1), NEG, F32), jnp.zeros((tq, LANES), F32), jnp.zeros((tq, FOX_HEAD_DIM), F32))
                 for _ in range(FOX_HEADS))


def _head_slice(h):
    return slice(FOX_HEAD_DIM * h, FOX_HEAD_DIM * (h + 1))


def _attend_block(qs, fqs, kb, vb, fk, mask, carries):
    heads = range(FOX_HEADS)
    scores = [_dot_nt(qs[h], kb[:, _head_slice(h)]) for h in heads]
    stats = []
    for h in heads:
        s = scores[h] + fqs[h] - fk[h:h + 1, :]
        if mask is not None:
            s = jnp.where(mask, s, NEG)
        stats.append(_softmax_stats(s, carries[h]))
    return tuple((m, l, acc + _dot(p, vb[:, _head_slice(h)])) for h, (m, l, acc, p) in zip(heads, stats))


def _attend_finish(carries, o_ref):
    for h in range(FOX_HEADS):
        _, l, acc = carries[h]
        o_ref[0, :, _head_slice(h)] = acc / jnp.sum(l, axis=-1, keepdims=True)


def _bf16_terms(x):
    hi, mid, lo = _split3(x)
    return hi.astype(F32), mid.astype(F32), lo.astype(F32)


def _fox_prep_kernel(q_ref, k_ref, v_ref, fc_ref, fr_ref, ka_ref, qa_ref, vt_ref):
    tm = q_ref.shape[1]
    qt = q_ref[0].astype(F32).T
    k = k_ref[0].astype(F32)
    vt_ref[0, 0] = v_ref[0].astype(F32).T.astype(BF16)
    lane = _iota((tm, LANES), 1)
    row = _iota((FOX_HEAD_DIM, tm), 0)
    d = FOX_HEAD_DIM
    for h in range(FOX_HEADS):
        pair, odd = divmod(h, 2)
        kc = k[:, pair * LANES:(pair + 1) * LANES]
        if odd:
            kc = pltpu.roll(kc, d, axis=1)
        ka = jnp.where(lane < d, kc, 0.0)
        for i, term in enumerate(_bf16_terms(-fc_ref[0][:, F_ROW0 + h:F_ROW0 + h + 1])):
            ka = jnp.where(lane == d + i, term, ka)
        ka = jnp.where(jnp.logical_and(lane >= d + 3, lane < d + 6), 1.0, ka)
        ka_ref[0, h] = ka.astype(BF16)
        tail = jnp.where(row < 3, 1.0, 0.0)
        for i, term in enumerate(_bf16_terms(fr_ref[0][h:h + 1, :])):
            tail = jnp.where(row == 3 + i, term, tail)
        qa_ref[0, h, 0:d, :] = qt[d * h:d * (h + 1), :].astype(BF16)
        qa_ref[0, h, d:2 * d, :] = tail.astype(BF16)


def _fox_prep(qb, kb, vb, fcol, frow, tm):
    b, t, _ = qb.shape
    tok = lambda c: pl.BlockSpec((1, tm, c), lambda i, j: (i, j, 0))
    return pl.pallas_call(
        _fox_prep_kernel,
        grid=(b, t // tm),
        in_specs=[tok(FOX_INNER), tok(FOX_INNER), tok(FOX_INNER), tok(LANES),
                  pl.BlockSpec((1, 8, tm), lambda i, j: (i, 0, j))],
        out_specs=[pl.BlockSpec((1, FOX_HEADS, tm, LANES), lambda i, j: (i, 0, j, 0)),
                   pl.BlockSpec((1, FOX_HEADS, LANES, tm), lambda i, j: (i, 0, 0, j)),
                   pl.BlockSpec((1, 1, FOX_INNER, tm), lambda i, j: (i, j, 0, 0))],
        out_shape=[jax.ShapeDtypeStruct((b, FOX_HEADS, t, LANES), BF16),
                   jax.ShapeDtypeStruct((b, FOX_HEADS, LANES, t), BF16),
                   jax.ShapeDtypeStruct((b, t // tm, FOX_INNER, tm), BF16)],
        compiler_params=_cparams(("arbitrary", "arbitrary")),
        name="fox_prep",
    )(qb, kb, vb, fcol, frow)


def _fox_prompt_t_kernel(tq, nk, qa_ref, ka_ref, vt_ref, fend_ref, ffirst_ref, thr_ref, o_ref):
    b = pl.program_id(0)
    qi = pl.program_id(1)
    half = tq // FOX_CHAINS
    d = FOX_HEAD_DIM
    diag = pl.multiple_of(qi * tq, tq)
    init = (jnp.full((1, half), NEG, F32), jnp.zeros((1, half), F32), jnp.zeros((d, half), F32))

    def update(blocks, qas, masks, carries):
        scores = [[_dot(ka, qa) for qa in qas] for ka, _ in blocks]
        for (_, vt), block_scores in zip(blocks, scores):
            stats = []
            for s, mask, (m, l, acc) in zip(block_scores, masks, carries):
                if mask is not None:
                    s = jnp.where(mask, s, NEG)
                m_new = jnp.maximum(m, jnp.max(s, axis=0, keepdims=True))
                p = jnp.exp(s - m_new)
                alpha = jnp.exp(m - m_new)
                stats.append((m_new, alpha * l + jnp.sum(p, axis=0, keepdims=True), alpha * acc, p.astype(BF16)))
            carries = tuple((m, l, acc + _dot(vt, p)) for m, l, acc, p in stats)
        return carries

    def key_block(h, j):
        return (ka_ref[0, h, pl.ds(pl.multiple_of(j * tq, tq), tq), :], vt_ref[0, j][d * h:d * (h + 1), :])

    for h in range(FOX_HEADS):
        base = (b * FOX_HEADS + h) * nk
        slack = thr_ref[0] + ffirst_ref[base + qi]
        n_live = lax.fori_loop(0, qi, lambda j, c: c + (slack - fend_ref[base + j] >= 0.0).astype(jnp.int32), 0)
        qas = [qa_ref[0, h, :, r2 * half:(r2 + 1) * half] for r2 in range(FOX_CHAINS)]
        ka_d = ka_ref[0, h, pl.ds(diag, tq), :]
        vt_d = vt_ref[0, qi][d * h:d * (h + 1), :]
        causal = [_iota((tq, half), 0) <= (r2 * half + _iota((tq, half), 1)) for r2 in range(FOX_CHAINS)]
        carries = update([(ka_d, vt_d)], qas, causal, [init] * FOX_CHAINS)
        no_mask = [None] * FOX_CHAINS

        def one(t, carries, qas=qas, h=h):
            return update([key_block(h, qi - 1 - t)], qas, no_mask, carries)

        def two(t, carries, qas=qas, h=h):
            j = qi - 1 - (n_live % 2) - 2 * t
            return update([key_block(h, j), key_block(h, j - 1)], qas, no_mask, carries)

        carries = lax.fori_loop(0, n_live % 2, one, carries)
        carries = lax.fori_loop(0, n_live // 2, two, carries)
        for r2 in range(FOX_CHAINS):
            _, l, acc = carries[r2]
            o_ref[0, r2 * half:(r2 + 1) * half, d * h:d * (h + 1)] = (acc / l).T


def _fox_prompt_t(qb, kb, vb, fcol, frow, score_bound):
    b, t, _ = qb.shape
    tq = min(t, FOX_PROMPT_TILE)
    nk = t // tq
    ka, qa, vt = _fox_prep(qb, kb, vb, fcol, frow, tq)
    f_heads = frow[:, :FOX_HEADS, :].reshape(b, FOX_HEADS, nk, tq)
    f_end = f_heads[:, :, :, tq - 1].reshape(-1)
    f_first = f_heads[:, :, :, 0].reshape(-1)
    thr = (2.0 * score_bound + EXP_UNDERFLOW).reshape(1).astype(F32)
    smem = pl.BlockSpec(memory_space=pltpu.SMEM)
    return pl.pallas_call(
        functools.partial(_fox_prompt_t_kernel, tq, nk),
        grid=(b, nk),
        in_specs=[pl.BlockSpec((1, FOX_HEADS, LANES, tq), lambda i, j: (i, 0, 0, j)),
                  pl.BlockSpec((1, FOX_HEADS, t, LANES), lambda i, j: (i, 0, 0, 0)),
                  pl.BlockSpec((1, nk, FOX_INNER, tq), lambda i, j: (i, 0, 0, 0)),
                  smem, smem, smem],
        out_specs=pl.BlockSpec((1, tq, FOX_INNER), lambda i, j: (i, j, 0)),
        out_shape=jax.ShapeDtypeStruct((b, t, FOX_INNER), F32),
        compiler_params=_cparams(("arbitrary", "arbitrary")),
        name="fox_prompt",
    )(qa, ka, vt, f_end, f_first, thr)


def _fox_sample_kernel(tq, tk, npast, q_ref, fq_ref, pk_ref, pv_ref, fpk_ref, k_ref, v_ref, fk_ref, o_ref):
    q_all = q_ref[0]
    fq_all = fq_ref[0]
    qs = [q_all[:, _head_slice(h)] for h in range(FOX_HEADS)]
    fqs = [fq_all[:, F_ROW0 + h:F_ROW0 + h + 1] for h in range(FOX_HEADS)]
    causal = _iota((tq, tq), 0) >= _iota((tq, tq), 1)

    def past_block(j, carries):
        start = pl.multiple_of(j * tk, tk)
        return _attend_block(qs, fqs, pk_ref[0, pl.ds(start, tk), :].astype(BF16),
                             pv_ref[0, pl.ds(start, tk), :].astype(BF16), fpk_ref[0, j], None, carries)

    carries = lax.fori_loop(0, npast, past_block, _softmax_init(tq))
    _attend_finish(_attend_block(qs, fqs, k_ref[0], v_ref[0], fk_ref[0], causal, carries), o_ref)


def _fox_sample(qb, fcol, past_k, past_v, layer, fpast_row, kb, vb, frow):
    b, t, _ = qb.shape
    p = past_k.shape[2]
    tk = min(p, 512)
    npast = p // tk
    fpk = fpast_row.reshape(b, 8, npast, tk).transpose(0, 2, 1, 3)
    bspec = lambda a: pl.BlockSpec((1,) + a.shape[1:], lambda i: (i,) + (0,) * (a.ndim - 1))
    cache = pl.BlockSpec((None, 1, p, FOX_INNER), lambda i: (layer, i, 0, 0))
    args = (qb, fcol, past_k, past_v, fpk, kb, vb, frow)
    return pl.pallas_call(
        functools.partial(_fox_sample_kernel, t, tk, npast),
        grid=(b,),
        in_specs=[bspec(qb), bspec(fcol), cache, cache, bspec(fpk), bspec(kb), bspec(vb), bspec(frow)],
        out_specs=pl.BlockSpec((1, t, FOX_INNER), lambda i: (i, 0, 0)),
        out_shape=jax.ShapeDtypeStruct((b, t, FOX_INNER), F32),
        compiler_params=_cparams(("arbitrary",)),
        name="fox_sample",
    )(*args)


def _mlp_kernel(lm, nchunk, ug_ref, vn_ref, ws_ref, bst_ref, y_ref):
    r_i = _iota((lm, lm), 0)
    c_i = _iota((lm, lm), 1)
    tril = r_i >= c_i
    vn = vn_ref[0].astype(BF16)
    ug = ug_ref[0]
    for g in range(MLP_GROUPS):
        lo, hi = MLP_GROUP_DIM * g, MLP_GROUP_DIM * (g + 1)
        w = jnp.where(tril, ws_ref[g], 0.0).astype(BF16)
        bias = bst_ref[:, g:g + 1]
        for c in range(nchunk):
            r0, r1 = c * lm, (c + 1) * lm
            sv = _dot(w, vn[r0:r1, lo:hi]) + bias
            y_ref[0, r0:r1, lo:hi] = ug[r0:r1, lo:hi] * sv


def _chunk_mlp(ug, vn, ws, bst):
    b, t, _ = ug.shape
    lm = ws.shape[1]
    tm = min(t, 4 * lm)
    tok = pl.BlockSpec((1, tm, MLP_INNER), lambda i, j: (i, j, 0))
    return pl.pallas_call(
        functools.partial(_mlp_kernel, lm, tm // lm),
        grid=(b, t // tm),
        in_specs=[tok, tok, pl.BlockSpec(ws.shape, lambda i, j: (0, 0, 0)),
                  pl.BlockSpec(bst.shape, lambda i, j: (0, 0))],
        out_specs=tok,
        out_shape=jax.ShapeDtypeStruct((b, t, MLP_INNER), F32),
        compiler_params=_cparams(("arbitrary", "arbitrary")),
        name="chunk_mlp",
    )(ug, vn, ws, bst)


def _out_kernel(x_ref, ys_ref, yf_ref, ym_ref, g1_ref, sh_ref, sc_ref, g_ref, wo_ref, x1_ref, h2_ref):
    mix = (_dot(ys_ref[0].astype(BF16), wo_ref[0:SSD_INNER, :])
           + _dot(yf_ref[0].astype(BF16), wo_ref[SSD_INNER:SSD_INNER + FOX_INNER, :])
           + _dot(ym_ref[0].astype(BF16), wo_ref[SSD_INNER + FOX_INNER:D_MODEL, :]))
    x1 = x_ref[0] + g1_ref[0] * mix
    x1_ref[0] = x1
    ms = jnp.mean(x1 * x1, axis=-1, keepdims=True)
    h = x1 * lax.rsqrt(ms + EPS) * g_ref[...]
    h2_ref[0] = _pack_rows((h * (1.0 + sc_ref[0]) + sh_ref[0]).astype(BF16))


def _out_proj(x, ys, yf, ym, g1, sh, sc, g, wo):
    b, t, _ = x.shape
    tm = min(t, 512)
    tok = lambda c: pl.BlockSpec((1, tm, c), lambda i, j: (i, j, 0))
    per_b = pl.BlockSpec((1, 1, D_MODEL), lambda i, j: (i, 0, 0))
    full = lambda a: pl.BlockSpec(a.shape, lambda i, j: (0,) * a.ndim)
    return pl.pallas_call(
        _out_kernel,
        grid=(b, t // tm),
        in_specs=[tok(D_MODEL), tok(SSD_INNER), tok(FOX_INNER), tok(MLP_INNER), per_b, per_b, per_b, full(g), full(wo)],
        out_specs=[tok(D_MODEL), pl.BlockSpec((1, tm // 2, D_MODEL), lambda i, j: (i, j, 0))],
        out_shape=[jax.ShapeDtypeStruct((b, t, D_MODEL), F32),
                   jax.ShapeDtypeStruct((b, t // 2, D_MODEL), jnp.uint32)],
        compiler_params=_cparams(("arbitrary", "arbitrary")),
        name="out_proj",
    )(x, ys, yf, ym, g1, sh, sc, g, wo)


def _top16(scores):
    nk, tn = scores[0].shape
    ridx = _iota((nk, tn), 0).astype(F32)
    r16 = _iota((PEER_TOPK, tn), 0)
    state = [(s, jnp.full((nk, tn), float(PEER_TOPK), F32), jnp.zeros((PEER_TOPK, tn), F32)) for s in scores]
    for it in range(PEER_TOPK):
        nxt_state = []
        for s, pos, tv in state:
            level = [(s[g:g + 8, :], ridx[g:g + 8, :]) for g in range(0, nk, 8)]
            while len(level) > 1:
                nxt = []
                for a in range(0, len(level), 2):
                    (va, ia), (vb, ib) = level[a], level[a + 1]
                    nxt.append((jnp.maximum(va, vb), jnp.where(va >= vb, ia, ib)))
                level = nxt
            v8, i8 = level[0]
            m = jnp.max(v8, axis=0, keepdims=True)
            first = jnp.min(jnp.where(v8 == m, i8, float(nk)), axis=0, keepdims=True)
            sel = ridx == first
            nxt_state.append((jnp.where(sel, -jnp.inf, s), jnp.where(sel, float(it), pos),
                              jnp.where(r16 == it, m, tv)))
        state = nxt_state
    return [(pos, tv) for _, pos, tv in state]


_CAND_GROUPS = [(0, 16)] + [(ka, 8) for ka in range(1, 8)]
_CAND_ROWS = 16 + 7 * 8 + 8


def _pair_select(ta, tb):
    tn = ta.shape[1]
    pieces, flats, valids = [], [], []
    for ka, rows in _CAND_GROUPS:
        pieces.append(ta[ka:ka + 1, :] + tb[0:rows, :])
        kb = _iota((rows, 1), 0)
        flats.append((ka * PEER_TOPK + kb).astype(F32))
        valids.append((ka + 1) * (kb + 1) <= PEER_TOPK)
    pieces.append(ta[8:16, :] + tb[0:1, :])
    flats.append(((8 + _iota((8, 1), 0)) * PEER_TOPK).astype(F32))
    valids.append(_iota((8, 1), 0) >= 0)
    cand0 = jnp.concatenate(pieces, axis=0)
    flat = jnp.concatenate(flats, axis=0)
    valid = jnp.concatenate(valids, axis=0)
    cand0 = jnp.where(valid, cand0, -jnp.inf)
    best = ta[0:1, :] + tb[0:1, :]

    cand = cand0
    selm = jnp.zeros((_CAND_ROWS, tn), F32)
    for _ in range(PEER_TOPK):
        m = jnp.max(cand, axis=0, keepdims=True)
        first = jnp.min(jnp.where(cand == m, flat, 4096.0), axis=0, keepdims=True)
        sel = flat == first
        cand = jnp.where(sel, -jnp.inf, cand)
        selm = jnp.where(sel, 1.0, selm)
    z = jnp.sum(jnp.where(selm > 0.0, jnp.exp(cand0 - best), 0.0), axis=0, keepdims=True)
    cnts = [jnp.sum(selm[0:16, :], axis=0, keepdims=True)]
    for i in range(1, 8):
        cnts.append(jnp.sum(selm[8 + 8 * i:16 + 8 * i, :], axis=0, keepdims=True))
    cnts.append(selm[_CAND_ROWS - 8:_CAND_ROWS, :])
    return jnp.concatenate(cnts, axis=0), z


def _peer_sel_kernel(tn, h_ref, wqt_ref, keys_ref, ea_ref, la_ref, eb_ref, pb_ref, qt_ref):
    qt_ref[...] = _dot_nt(wqt_ref[...], _unpack_rows(h_ref[...])).astype(BF16)

    def head(h, carry):
        row = pl.multiple_of(h * (2 * PEER_HALF), 2 * PEER_HALF)
        sa_all = _dot(keys_ref[2 * h], qt_ref[pl.ds(row, PEER_HALF), :])
        sb_all = _dot(keys_ref[2 * h + 1], qt_ref[pl.ds(row + PEER_HALF, PEER_HALF), :])
        chunks = [slice(c * LANES, (c + 1) * LANES) for c in range(tn // LANES)]
        tops = _top16([s_all[:, cs] for cs in chunks for s_all in (sa_all, sb_all)])
        for c, cs in enumerate(chunks):
            sa, sb = sa_all[:, cs], sb_all[:, cs]
            (pos_a, ta), (pos_b, tb) = tops[2 * c], tops[2 * c + 1]
            cnt, z = _pair_select(ta, tb)
            pos_h = pos_a.astype(BF16)
            cnt_h = cnt.astype(BF16)
            la = jnp.zeros_like(pos_h)
            for ka in range(PEER_TOPK):
                la = jnp.where(pos_h == float(ka), jnp.broadcast_to(cnt_h[ka:ka + 1, :], pos_h.shape), la)
            ea_ref[h, :, cs] = jnp.where(pos_a < float(PEER_TOPK), jnp.exp(sa - ta[0:1, :]), 0.0) / z
            la_ref[h, :, cs] = la.astype(F32)
            eb = jnp.where(pos_b < float(PEER_TOPK), jnp.exp(sb - tb[0:1, :]), 0.0)
            for sub in range(PEER_KEYS // PEER_SUB):
                src = slice(sub * PEER_SUB, (sub + 1) * PEER_SUB)
                dst = slice(sub * PEER_SUB // 2, (sub + 1) * PEER_SUB // 2)
                eb_ref[h, dst, cs] = pltpu.bitcast(eb[src].astype(BF16), jnp.uint32)
                pb_ref[h, dst, cs] = pltpu.bitcast(pos_b[src].astype(BF16), jnp.uint32)
        return carry

    lax.fori_loop(0, PEER_HEADS, head, 0)


def _peer_select(h2, wqt_all, layer, keys):
    n = 2 * h2.shape[0]
    tn = 256
    per_tok = pl.BlockSpec((PEER_HEADS, PEER_KEYS, tn), lambda i: (0, 0, i))
    per_tok_pk = pl.BlockSpec((PEER_HEADS, PEER_KEYS // 2, tn), lambda i: (0, 0, i))
    shp = lambda dt: jax.ShapeDtypeStruct((PEER_HEADS, PEER_KEYS, n), dt)
    shp_pk = jax.ShapeDtypeStruct((PEER_HEADS, PEER_KEYS // 2, n), jnp.uint32)
    return pl.pallas_call(
        functools.partial(_peer_sel_kernel, tn),
        grid=(n // tn,),
        in_specs=[pl.BlockSpec((tn // 2, D_MODEL), lambda i: (i, 0)),
                  pl.BlockSpec((None,) + wqt_all.shape[1:], lambda i: (layer, 0, 0)),
                  pl.BlockSpec(keys.shape, lambda i: (0, 0, 0))],
        out_specs=[per_tok, per_tok, per_tok_pk, per_tok_pk],
        out_shape=[shp(F32), shp(F32), shp_pk, shp_pk],
        scratch_shapes=[pltpu.VMEM((PEER_HEADS * 2 * PEER_HALF, tn), BF16)],
        compiler_params=_cparams(("arbitrary",)),
        name="peer_select",
    )(h2, wqt_all, keys)


def _peer_gate_stage(ec, ia0, rows, chunks, at_r, ga_w, ea_ref, la_ref, eb_ref, pb_ref):
    rows_per = ec // PEER_KEYS
    for r in rows:
        for c in chunks:
            cs = slice(c * LANES, (c + 1) * LANES)
            for sub in range(PEER_KEYS // PEER_SUB):
                pk = slice(sub * PEER_SUB // 2, (sub + 1) * PEER_SUB // 2)
                gate = None
                for h in range(PEER_HEADS):
                    la = la_ref[h, pl.ds(ia0, rows_per), cs][r:r + 1, :]
                    ea = ea_ref[h, pl.ds(ia0, rows_per), cs][r:r + 1, :]
                    la = jnp.broadcast_to(la, (PEER_SUB, LANES)).astype(BF16)
                    ea = jnp.broadcast_to(ea, (PEER_SUB, LANES)).astype(BF16)
                    pb = pltpu.bitcast(pb_ref[h, pk, cs], BF16)
                    eb = pltpu.bitcast(eb_ref[h, pk, cs], BF16)
                    term = jnp.where(pb < la, eb, 0.0) * ea
                    gate = term if gate is None else gate + term
                ex = slice(r * PEER_KEYS + sub * PEER_SUB, r * PEER_KEYS + (sub + 1) * PEER_SUB)
                ga_w[ex, cs] = gate * _gelu(at_r[ex, cs]).astype(BF16)


def _peer_dense_kernel(tn, ec, h_ref, u_ref, vt_ref, ea_ref, la_ref, eb_ref, pb_ref, o_ref, at, ga, acc):
    e = pl.program_id(1)
    ne = pl.num_programs(1)

    @pl.when(e == 0)
    def _():
        acc[...] = jnp.zeros_like(acc)

    rows_per = ec // PEER_KEYS
    ia0 = pl.multiple_of(e * rows_per, rows_per)
    parts = PEER_DENSE_PARTS
    per = ec // parts
    h_t = _unpack_rows(h_ref[...])
    for g in range(parts):
        rows = slice(g * per, (g + 1) * per)
        pk = slice(g * per // 2, (g + 1) * per // 2)
        at[rows, :] = _dot_nt(_unpack_rows(u_ref[pk, :]), h_t)
    for g in range(parts):
        rows = slice(g * per, (g + 1) * per)
        _peer_gate_stage(ec, ia0, range(g * rows_per // parts, (g + 1) * rows_per // parts), range(tn // LANES),
                         at, ga, ea_ref, la_ref, eb_ref, pb_ref)
        acc[...] += _dot(_unpack_rows(vt_ref[:, rows]), ga[rows, :])

    @pl.when(e == ne - 1)
    def _():
        o_ref[...] = acc[...].T


def _peer_dense(h2, u_all, vt_all, layer, ea, la, eb, pb):
    n = 2 * h2.shape[0]
    tn, ec = PEER_TOKEN_TILE, PEER_EXPERT_CHUNK
    per_tok = pl.BlockSpec((PEER_HEADS, PEER_KEYS, tn), lambda i, e: (0, 0, i))
    per_tok_pk = pl.BlockSpec((PEER_HEADS, PEER_KEYS // 2, tn), lambda i, e: (0, 0, i))
    return pl.pallas_call(
        functools.partial(_peer_dense_kernel, tn, ec),
        grid=(n // tn, PEER_EXPERTS // ec),
        in_specs=[pl.BlockSpec((tn // 2, D_MODEL), lambda i, e: (i, 0)),
                  pl.BlockSpec((None, ec // 2, D_MODEL), lambda i, e: (layer, e, 0)),
                  pl.BlockSpec((None, D_MODEL // 2, ec), lambda i, e: (layer, 0, e)),
                  per_tok, per_tok, per_tok_pk, per_tok_pk],
        out_specs=pl.BlockSpec((tn, D_MODEL), lambda i, e: (i, 0)),
        out_shape=jax.ShapeDtypeStruct((n, D_MODEL), F32),
        scratch_shapes=[pltpu.VMEM((ec, tn), F32), pltpu.VMEM((ec, tn), BF16), pltpu.VMEM((D_MODEL, tn), F32)],
        compiler_params=_cparams(("arbitrary", "arbitrary")),
        name="peer_dense",
    )(h2, u_all, vt_all, ea, la, eb, pb)


def _resid_kernel(x_ref, p_ref, g_ref, o_ref):
    o_ref[0] = x_ref[0] + g_ref[0] * p_ref[0]


def _residual(x, peer, g2):
    b, t, _ = x.shape
    tm = min(t, 512)
    tok = pl.BlockSpec((1, tm, D_MODEL), lambda i, j: (i, j, 0))
    return pl.pallas_call(
        _resid_kernel,
        grid=(b, t // tm),
        in_specs=[tok, tok, pl.BlockSpec((1, 1, D_MODEL), lambda i, j: (i, 0, 0))],
        out_specs=tok,
        out_shape=jax.ShapeDtypeStruct((b, t, D_MODEL), F32),
        compiler_params=_cparams(("arbitrary", "arbitrary")),
        name="peer_residual",
    )(x, peer, g2)


def _lane_pad(vec, offset, width=LANES):
    out = jnp.zeros((width,), F32)
    return out.at[offset:offset + vec.shape[0]].set(vec.astype(F32))


def _layer_params(l, norm1_g, norm2_g, w_in, conv_w, conv_b, dt_bias, a_log, d_skip, ssd_norm_g, q_norm_g,
                  k_norm_g, fgate_b, w_s, b_s, w_out, peer_keys):
    w = w_in[l]
    o = [0]
    for sz in (SSD_INNER, CONV_CH, SSD_HEADS, FOX_INNER, FOX_INNER, FOX_INNER, FOX_HEADS, MLP_INNER, MLP_INNER):
        o.append(o[-1] + sz)
    wz, wxbc, wdt, wq, wk, wv, wf, wu, wvm = [w[:, o[i]:o[i + 1]] for i in range(9)]
    w_small = jnp.zeros((D_MODEL, SMALL_COLS), F32)
    w_small = w_small.at[:, DT_ROW0:DT_ROW0 + SSD_HEADS].set(wdt).at[:, F_ROW0:F_ROW0 + FOX_HEADS].set(wf)
    w_r = jnp.concatenate([wz, wxbc, wq, wk, wv, wu, wvm, w_small], axis=1).astype(BF16)
    ws_t = w_small[:, :SMALL_ROWS].T.astype(BF16)
    a_neg = -jnp.exp(a_log[l].astype(F32))
    grp = jnp.arange(FOX_INNER) // FOX_HEAD_DIM
    gm = jnp.where(grp[:, None] == grp[None, :], 1.0 / FOX_HEAD_DIM, 0.0).astype(BF16)
    heads = jnp.arange(SSD_INNER) // SSD_HEAD_DIM
    e_mat = (jnp.arange(LANES)[:, None] == heads[None, :]).astype(BF16)
    return dict(
        norm1_g=norm1_g[l][None], norm2_g=norm2_g[l][None], w_r=w_r, ws_t=ws_t,
        gq=(jnp.tile(q_norm_g[l], FOX_HEADS) * (FOX_HEAD_DIM ** -0.5))[None],
        gk=jnp.tile(k_norm_g[l], FOX_HEADS)[None], gm=gm,
        fox_bound=1.02 * FOX_HEAD_DIM ** 0.5 * jnp.max(jnp.abs(q_norm_g[l])) * jnp.max(jnp.abs(k_norm_g[l])),
        conv_w=conv_w[l], conv_b=conv_b[l][None],
        dtb_c=_lane_pad(dt_bias[l], DT_ROW0)[None], dtb_r=dt_bias[l].astype(F32)[:, None],
        a_c=_lane_pad(a_neg, DT_ROW0)[None], a_r=a_neg[:, None],
        dsk_x=jnp.repeat(d_skip[l].astype(F32), SSD_HEAD_DIM)[None], ssd_g=ssd_norm_g[l][None], e_mat=e_mat,
        fb_c=_lane_pad(fgate_b[l], F_ROW0)[None], fb_r=_lane_pad(fgate_b[l], 0, 8)[:, None],
        w_s=w_s[l], bs_t=jnp.zeros((MLP_CHUNK, LANES), F32).at[:, :MLP_GROUPS].set(b_s[l].T),
        w_out=w_out[l].astype(BF16),
        keys=peer_keys[l].reshape(PEER_HEADS * 2, PEER_KEYS, PEER_HALF).astype(BF16),
    )


def _stream_mixers(x, peer, g2_prev, mod, p, hist8, s0t, past):
    sh1, sc1, g1, sh2, sc2, _ = mod
    b, t, _ = x.shape
    x, (z, xbc, qb, kn, kb, v, vb, ug, vn, sm, smt) = _in_proj(
        x, peer, g2_prev, sh1, sc1, p['norm1_g'], p['w_r'], p['ws_t'], p['gq'], p['gk'], p['gm'])
    L = SSD_CHUNK_PROMPT if t % SSD_CHUNK_PROMPT == 0 else t
    y_ssd, s_fin, c_fin = _ssd(xbc, z, sm, smt, hist8, s0t, p['conv_w'], p['conv_b'], p['dtb_c'], p['dtb_r'],
                               p['a_c'], p['a_r'], p['dsk_x'], p['ssd_g'], p['e_mat'], L)
    zeros8 = jnp.zeros((b, 8, LANES), F32)
    if past is None:
        logf_c, f_col, f_row, _, _ = _fcum(sm, smt, F_ROW0 // 8, p['fb_c'], p['fb_r'], zeros8, zeros8, True)
        y_fox = _fox_prompt_t(qb, kb, vb, f_col, f_row, p['fox_bound'])
    else:
        pk, pv, layer, plf_col, plf_row = past
        _, _, fp_row, end_c, end_r = _fcum(plf_col, plf_row, 0, p['fb_c'], p['fb_r'], zeros8, zeros8, False)
        logf_c, f_col, f_row, _, _ = _fcum(sm, smt, F_ROW0 // 8, p['fb_c'], p['fb_r'], end_c, end_r, True)
        y_fox = _fox_sample(qb, f_col, pk, pv, layer, fp_row, kb, vb, f_row)
    lm = MLP_CHUNK if t % MLP_CHUNK == 0 else t
    y_mlp = _chunk_mlp(ug, vn, p['w_s'][:, :lm, :lm], p['bs_t'][:lm])
    x1, h2 = _out_proj(x, y_ssd, y_fox, y_mlp, g1, sh2, sc2, p['norm2_g'], p['w_out'])
    logf = logf_c[:, :, F_ROW0:F_ROW0 + FOX_HEADS]
    new_ssm = s_fin.reshape(b, SSD_STATE, SSD_HEADS, SSD_HEAD_DIM).transpose(0, 2, 3, 1)
    new_conv = c_fin[:, 8 - (CONV_W - 1):, :]
    kc = kn.reshape(b, t, FOX_HEADS, FOX_HEAD_DIM)
    vc = v.reshape(b, t, FOX_HEADS, FOX_HEAD_DIM)
    return x1, h2, (kc, vc, logf, new_ssm, new_conv, vn)


def _peer(h2, p, tables, layer):
    b, t2, _ = h2.shape
    t = 2 * t2
    n_tok = b * t
    n_pad = -n_tok % PEER_TOKEN_TILE
    flat = h2.reshape(n_tok // 2, D_MODEL)
    if n_pad:
        flat = jnp.concatenate([flat, jnp.zeros((n_pad // 2, D_MODEL), jnp.uint32)], axis=0)
    wqt_all, u_all, vt_all = tables
    ea, la, eb, pb = _peer_select(flat, wqt_all, layer, p['keys'])
    out = _peer_dense(flat, u_all, vt_all, layer, ea, la, eb, pb)
    return out[:n_tok].reshape(b, t, D_MODEL)


def kernel(x_prompt, x_sample, c_prompt, c_sample, cache_fox_k, cache_fox_v, cache_fox_logf, state_ssm, state_conv, norm1_g, norm2_g, w_ada, b_ada, w_in, conv_w, conv_b, dt_bias, a_log, d_skip, ssd_norm_g, q_norm_g, k_norm_g, fgate_b, w_s, b_s, w_out, peer_wq, peer_keys, peer_u, peer_v):
    depth = w_ada.shape[0]
    bp, tp, _ = x_prompt.shape
    bs, ts, _ = x_sample.shape
    past_len = cache_fox_k.shape[2]
    mod_all = _modulation(jnp.concatenate([c_prompt, c_sample], axis=0).astype(F32), w_ada, b_ada)

    peer_tables = (_to_bf16_t(peer_wq, D_MODEL, 512, False), _to_bf16_packed(peer_u, 1024),
                   _to_bf16_t(peer_v, 512, D_MODEL, True))

    past_k_all = cache_fox_k.reshape(depth, bs, past_len, FOX_INNER)
    past_v_all = cache_fox_v.reshape(depth, bs, past_len, FOX_INNER)

    xp, xs = x_prompt, x_sample
    peer_p = peer_s = g2p = g2s = None
    outs = [[] for _ in range(11)]
    for l in range(depth):
        p = _layer_params(l, norm1_g, norm2_g, w_in, conv_w, conv_b, dt_bias, a_log, d_skip, ssd_norm_g, q_norm_g,
                          k_norm_g, fgate_b, w_s, b_s, w_out, peer_keys)
        mods = [m[:, None, :] for m in jnp.split(mod_all[l], 6, axis=-1)]
        mod_p = [m[:bp] for m in mods]
        mod_s = [m[bp:] for m in mods]
        hist_p = jnp.zeros((bp, 8, CONV_CH), F32)
        s0_p = jnp.zeros((bp, SSD_STATE, SSD_INNER), F32)
        hist_s = jnp.concatenate([jnp.zeros((bs, 8 - (CONV_W - 1), CONV_CH), F32), state_conv[l].astype(F32)], axis=1)
        s0_s = state_ssm[l].astype(F32).transpose(0, 3, 1, 2).reshape(bs, SSD_STATE, SSD_INNER)
        plf = cache_fox_logf[l].astype(F32)
        plf_col = jnp.pad(plf, ((0, 0), (0, 0), (F_ROW0, LANES - F_ROW0 - FOX_HEADS)))
        plf_row = jnp.pad(plf.transpose(0, 2, 1), ((0, 0), (0, 8 - FOX_HEADS), (0, 0)))
        past = (past_k_all, past_v_all, l, plf_col, plf_row)

        x1p, h2p, st_p = _stream_mixers(xp, peer_p, g2p, mod_p, p, hist_p, s0_p, None)
        x1s, h2s, st_s = _stream_mixers(xs, peer_s, g2s, mod_s, p, hist_s, s0_s, past)

        peer_p = _peer(h2p, p, peer_tables, l)
        peer_s = _peer(h2s, p, peer_tables, l)
        xp, xs, g2p, g2s = x1p, x1s, mod_p[5], mod_s[5]
        for i in range(5):
            outs[i].append(st_p[i])
        for i in range(6):
            outs[5 + i].append(st_s[i])

    yp = _residual(xp, peer_p, g2p)
    ys = _residual(xs, peer_s, g2s)
    return (yp, ys) + tuple(jnp.stack(o) for o in outs)
```

```python
import functools

import jax
import jax.numpy as jnp
from jax import lax
from jax.experimental import pallas as pl
from jax.experimental.pallas import tpu as pltpu

F32 = jnp.float32
BF16 = jnp.bfloat16
EPS = 1e-6

D_MODEL = 1024
SSD_HEADS = 8
SSD_HEAD_DIM = 64
SSD_INNER = SSD_HEADS * SSD_HEAD_DIM
SSD_GROUPS = 2
SSD_STATE = 64
CONV_W = 4
CONV_CH = SSD_INNER + 2 * SSD_GROUPS * SSD_STATE
FOX_HEADS = 4
FOX_HEAD_DIM = 64
FOX_INNER = FOX_HEADS * FOX_HEAD_DIM
MLP_GROUPS = 4
MLP_GROUP_DIM = 64
MLP_INNER = MLP_GROUPS * MLP_GROUP_DIM
MLP_CHUNK = 128
FOX_PROMPT_TILE = 512
FOX_CHAINS = 4
FCUM_TILE = 2048
FCUM_SUB = 512
SSD_CHUNK_PROMPT = 128
PEER_HEADS = 8
PEER_KEYS = 128
PEER_EXPERTS = PEER_KEYS * PEER_KEYS
PEER_HALF = 128
PEER_TOPK = 16
PEER_TOKEN_TILE = 512
PEER_EXPERT_CHUNK = 2048
LANES = 128
MXU_DIM = 256
PEER_DENSE_PARTS = 4
PEER_SUB = 128
SMALL_COLS = LANES
SMALL_ROWS = 16
DT_ROW0, F_ROW0 = 0, 8
NEG = -1e30
EXP_UNDERFLOW = 104.0
VMEM_LIMIT = 56 * 1024 * 1024

_Z0, _XBC0, _Q0, _K0, _V0, _U0, _VM0, _SM0 = 0, 512, 1280, 1536, 1792, 2048, 2304, 2560
PROJ_COLS = _SM0 + SMALL_COLS


def _cparams(sem, flags=None):
    return pltpu.CompilerParams(dimension_semantics=sem, vmem_limit_bytes=VMEM_LIMIT, flags=flags)


def _dot(a, b):
    return jnp.dot(a, b, preferred_element_type=F32)


def _dot_nt(a, b):
    return lax.dot_general(a, b, (((1,), (1,)), ((), ())), preferred_element_type=F32)


def _dot_tn(a, b):
    return lax.dot_general(a, b, (((0,), (0,)), ((), ())), preferred_element_type=F32)


def _split2(x):
    hi = x.astype(BF16)
    lo = (x - hi.astype(F32)).astype(BF16)
    return hi, lo


def _split3(x):
    hi = x.astype(BF16)
    r = x - hi.astype(F32)
    mid = r.astype(BF16)
    lo = (r - mid.astype(F32)).astype(BF16)
    return hi, mid, lo


def _dot3_l(x, w):
    hi, mid, lo = _split3(x)
    return _dot(hi, w) + _dot(mid, w) + _dot(lo, w)


def _dot3_r(w, x):
    hi, mid, lo = _split3(x)
    return _dot(w, hi) + _dot(w, mid) + _dot(w, lo)


def _sigmoid(x):
    return 1.0 / (1.0 + jnp.exp(-x))


def _silu(x):
    return x * _sigmoid(x)


def _softplus(x):
    return jnp.maximum(x, 0.0) + jnp.log1p(jnp.exp(-jnp.abs(x)))


_GELU_A = 2.0 * 0.7978845608028654
_GELU_B = _GELU_A * 0.044715


def _gelu(x):
    z2 = x * (_GELU_A + _GELU_B * (x * x))
    return x / (1.0 + jnp.exp(-z2))


def _iota(shape, dim):
    return lax.broadcasted_iota(jnp.int32, shape, dim)


def _pack_rows(x):
    return pltpu.bitcast(x, jnp.uint32)


def _unpack_rows(x):
    return pltpu.bitcast(x, BF16)


def _twin_bf16(x):
    bits = pltpu.bitcast(x, jnp.uint32)
    return bits | lax.shift_right_logical(bits, jnp.uint32(16))


def _cast_kernel(x_ref, o_ref):
    o_ref[0] = _pack_rows(x_ref[0].astype(BF16))


def _cast_t_kernel(packed, x_ref, o_ref):
    xt = x_ref[0].T.astype(BF16)
    o_ref[0] = _pack_rows(xt) if packed else xt


def _to_bf16_packed(x, rows):
    n, r, c = x.shape
    return pl.pallas_call(
        _cast_kernel,
        grid=(n, r // rows),
        in_specs=[pl.BlockSpec((1, rows, c), lambda l, i: (l, i, 0))],
        out_specs=pl.BlockSpec((1, rows // 2, c), lambda l, i: (l, i, 0)),
        out_shape=jax.ShapeDtypeStruct((n, r // 2, c), jnp.uint32),
        compiler_params=_cparams(("arbitrary", "arbitrary")),
        name="cast_bf16",
    )(x)


def _to_bf16_t(x, rows, cols, packed):
    n, r, c = x.shape
    div = 2 if packed else 1
    return pl.pallas_call(
        functools.partial(_cast_t_kernel, packed),
        grid=(n, r // rows, c // cols),
        in_specs=[pl.BlockSpec((1, rows, cols), lambda l, i, j: (l, i, j))],
        out_specs=pl.BlockSpec((1, cols // div, rows), lambda l, i, j: (l, j, i)),
        out_shape=jax.ShapeDtypeStruct((n, c // div, r), jnp.uint32 if packed else BF16),
        compiler_params=_cparams(("arbitrary", "arbitrary", "arbitrary")),
        name="cast_bf16_transposed",
    )(x)


def _mod_kernel(c_ref, w_ref, b_ref, o_ref):
    c = c_ref[...]
    o_ref[0] = jnp.dot(_silu(c), w_ref[0], preferred_element_type=F32,
                       precision=lax.Precision.HIGHEST) + b_ref[0]


def _modulation(c_all, w_ada, b_ada):
    depth, _, n6 = w_ada.shape
    bc = c_all.shape[0]
    tn = 1536
    return pl.pallas_call(
        _mod_kernel,
        grid=(depth, n6 // tn),
        in_specs=[pl.BlockSpec((bc, D_MODEL), lambda l, j: (0, 0)),
                  pl.BlockSpec((1, D_MODEL, tn), lambda l, j: (l, 0, j)),
                  pl.BlockSpec((1, 1, tn), lambda l, j: (l, 0, j))],
        out_specs=pl.BlockSpec((1, bc, tn), lambda l, j: (l, 0, j)),
        out_shape=jax.ShapeDtypeStruct((depth, bc, n6), F32),
        compiler_params=_cparams(("arbitrary", "arbitrary")),
        name="adaln_mod",
    )(c_all, w_ada, b_ada.reshape(depth, 1, n6))


def _in_kernel(with_peer, *refs):
    if with_peer:
        x_ref, p_ref, g2_ref = refs[:3]
        refs = refs[3:]
    else:
        x_ref = refs[0]
        refs = refs[1:]
    (sh_ref, sc_ref, g_ref, w_ref, wst_ref, gq_ref, gk_ref, gm_ref) = refs[:8]
    outs = refs[8:]
    if with_peer:
        xo_ref = outs[0]
        outs = outs[1:]
    (z_ref, xbc_ref, q_ref, k_ref, kb_ref, v_ref, vb_ref, ug_ref, vn_ref, sm_ref, smt_ref) = outs

    x = x_ref[0]
    if with_peer:
        x = x + g2_ref[0] * p_ref[0]
        xo_ref[0] = x
    ms = jnp.mean(x * x, axis=-1, keepdims=True)
    h = x * lax.rsqrt(ms + EPS) * g_ref[...]
    h = h * (1.0 + sc_ref[0]) + sh_ref[0]
    hb = h.astype(BF16)
    proj = _dot(hb, w_ref[...])
    z_ref[0] = proj[:, _Z0:_XBC0]
    xbc_ref[0] = proj[:, _XBC0:_Q0]
    q = proj[:, _Q0:_K0]
    k = proj[:, _K0:_V0]
    v = proj[:, _V0:_U0]
    u = proj[:, _U0:_VM0]
    vm = proj[:, _VM0:_SM0]
    sm_ref[0] = proj[:, _SM0:PROJ_COLS]
    smt_ref[0] = _dot_nt(wst_ref[...], hb)
    gm = gm_ref[...]

    def gmean(y):
        hi, lo = _split2(y)
        return _dot(hi, gm) + _dot(lo, gm)

    qn = q * lax.rsqrt(gmean(q * q) + EPS) * gq_ref[...]
    kn = k * lax.rsqrt(gmean(k * k) + EPS) * gk_ref[...]
    q_ref[0] = qn.astype(BF16)
    k_ref[0] = kn
    kb_ref[0] = kn.astype(BF16)
    v_ref[0] = v
    vb_ref[0] = v.astype(BF16)
    ug_ref[0] = _gelu(u)
    gv = _gelu(vm)
    mu = gmean(gv)
    cen = gv - mu
    var = gmean(cen * cen)
    vn_ref[0] = cen * lax.rsqrt(var + EPS)


def _in_proj(x, peer, g2, sh, sc, g, w_r, ws_t, gq, gk, gm):
    b, t, _ = x.shape
    tm = min(t, 512)
    with_peer = peer is not None
    tok = lambda c: pl.BlockSpec((1, tm, c), lambda i, j: (i, j, 0))
    per_b = pl.BlockSpec((1, 1, D_MODEL), lambda i, j: (i, 0, 0))
    full = lambda a: pl.BlockSpec(a.shape, lambda i, j: (0,) * a.ndim)
    in_specs = [tok(D_MODEL)]
    args = [x]
    if with_peer:
        in_specs += [tok(D_MODEL), per_b]
        args += [peer, g2]
    in_specs += [per_b, per_b, full(g), full(w_r), full(ws_t), full(gq), full(gk), full(gm)]
    args += [sh, sc, g, w_r, ws_t, gq, gk, gm]
    out_cols = [(SSD_INNER, F32), (CONV_CH, F32), (FOX_INNER, BF16), (FOX_INNER, F32), (FOX_INNER, BF16),
                (FOX_INNER, F32), (FOX_INNER, BF16), (MLP_INNER, F32), (MLP_INNER, F32), (SMALL_COLS, F32)]
    out_specs = [tok(c) for c, _ in out_cols]
    out_shape = [jax.ShapeDtypeStruct((b, t, c), dt) for c, dt in out_cols]
    out_specs.append(pl.BlockSpec((1, SMALL_ROWS, tm), lambda i, j: (i, 0, j)))
    out_shape.append(jax.ShapeDtypeStruct((b, SMALL_ROWS, t), F32))
    if with_peer:
        out_specs = [tok(D_MODEL)] + out_specs
        out_shape = [jax.ShapeDtypeStruct((b, t, D_MODEL), F32)] + out_shape
    res = pl.pallas_call(
        functools.partial(_in_kernel, with_peer),
        grid=(b, t // tm),
        in_specs=in_specs, out_specs=out_specs, out_shape=out_shape,
        compiler_params=_cparams(("arbitrary", "arbitrary")),
        name="in_proj",
    )(*args)
    if with_peer:
        return res[0], res[1:]
    return x, res


def _ssd_kernel(L, xbc_ref, z_ref, sm_ref, smt_ref, hist_ref, s0_ref, cw_ref, cb_ref, dtbc_ref, dtbr_ref,
                ac_ref, ar_ref, dsk_ref, ng_ref, e_ref, y_ref, sfin_ref, cfin_ref, xpad, st, ybuf):
    c = pl.program_id(1)
    nc = pl.num_programs(1)

    @pl.when(c == 0)
    def _():
        xpad[0:8, :] = hist_ref[0]
        st[...] = s0_ref[0]

    xpad[8:8 + L, :] = xbc_ref[0]
    conv = cb_ref[...]
    for tap in range(CONV_W):
        conv = conv + xpad[5 + tap:5 + tap + L, :] * cw_ref[tap:tap + 1, :]
    tail = xpad[L:L + 8, :]
    xpad[0:8, :] = tail
    xc = _silu(conv)
    xs = xc[:, 0:SSD_INNER]

    r_i = _iota((L, L), 0)
    c_i = _iota((L, L), 1)
    causal = r_i >= c_i
    tri = jnp.where(causal, 1.0, 0.0).astype(BF16)
    triu = jnp.where(r_i <= c_i, 1.0, 0.0).astype(BF16)

    dtc = _softplus(sm_ref[0] + dtbc_ref[...])
    acum_c = _dot3_r(tri, dtc * ac_ref[...])
    e = e_ref[...]
    acum_x = _dot3_l(acum_c, e)
    dt_x = _dot3_l(dtc, e)
    dtr = _softplus(smt_ref[0][DT_ROW0:DT_ROW0 + 8, :] + dtbr_ref[...])
    acum_r = _dot3_l(dtr * ar_ref[...], triu)

    bmat = [xc[:, SSD_INNER + SSD_STATE * g:SSD_INNER + SSD_STATE * (g + 1)].astype(BF16) for g in range(SSD_GROUPS)]
    c0 = SSD_INNER + SSD_GROUPS * SSD_STATE
    cmat = [xc[:, c0 + SSD_STATE * g:c0 + SSD_STATE * (g + 1)].astype(BF16) for g in range(SSD_GROUPS)]
    cb = [_dot_nt(cmat[g], bmat[g]) for g in range(SSD_GROUPS)]
    gw = SSD_INNER // SSD_GROUPS
    stb = st[...].astype(BF16)
    y_off = jnp.concatenate([_dot(cmat[g], stb[:, gw * g:gw * (g + 1)]) for g in range(SSD_GROUPS)], axis=1)
    xsb = xs.astype(BF16)
    hpg = SSD_HEADS // SSD_GROUPS
    mixes = []
    for h in range(SSD_HEADS):
        seg = acum_c[:, h:h + 1] - acum_r[h:h + 1, :]
        dec = jnp.where(causal, jnp.exp(jnp.minimum(seg, 0.0)), 0.0)
        mixes.append((cb[h // hpg] * dec * dtr[h:h + 1, :]).astype(BF16))
    heads = [_dot(mixes[h], xsb[:, SSD_HEAD_DIM * h:SSD_HEAD_DIM * (h + 1)]) for h in range(SSD_HEADS)]
    for h in range(SSD_HEADS):
        ybuf[:, SSD_HEAD_DIM * h:SSD_HEAD_DIM * (h + 1)] = heads[h]

    y = ybuf[...] + y_off * jnp.exp(acum_x) + dsk_ref[...] * xs
    a_end = acum_x[L - 1:L, :]
    xw = (xs * dt_x * jnp.exp(a_end - acum_x)).astype(BF16)
    new_states = jnp.concatenate([_dot_tn(bmat[g], xw[:, gw * g:gw * (g + 1)]) for g in range(SSD_GROUPS)], axis=1)
    st[...] = st[...] * jnp.exp(a_end) + new_states

    yg = y * _silu(z_ref[0])
    ms = jnp.mean(yg * yg, axis=-1, keepdims=True)
    y_ref[0] = yg * lax.rsqrt(ms + EPS) * ng_ref[...]

    @pl.when(c == nc - 1)
    def _():
        sfin_ref[0] = st[...]
        cfin_ref[0] = tail


def _ssd(xbc, z, sm, smt, hist8, s0t, cw, cb, dtb_c, dtb_r, a_c, a_r, dsk_x, ng, e_mat, L):
    b, t, _ = xbc.shape
    tok = lambda c: pl.BlockSpec((1, L, c), lambda i, j: (i, j, 0))
    full = lambda a: pl.BlockSpec(a.shape, lambda i, j: (0,) * a.ndim)
    per_b = lambda a: pl.BlockSpec((1,) + a.shape[1:], lambda i, j: (i,) + (0,) * (a.ndim - 1))
    return pl.pallas_call(
        functools.partial(_ssd_kernel, L),
        grid=(b, t // L),
        in_specs=[tok(CONV_CH), tok(SSD_INNER), tok(SMALL_COLS),
                  pl.BlockSpec((1, SMALL_ROWS, L), lambda i, j: (i, 0, j)),
                  per_b(hist8), per_b(s0t), full(cw), full(cb), full(dtb_c), full(dtb_r), full(a_c), full(a_r),
                  full(dsk_x), full(ng), full(e_mat)],
        out_specs=[tok(SSD_INNER),
                   pl.BlockSpec((1, SSD_STATE, SSD_INNER), lambda i, j: (i, 0, 0)),
                   pl.BlockSpec((1, 8, CONV_CH), lambda i, j: (i, 0, 0))],
        out_shape=[jax.ShapeDtypeStruct((b, t, SSD_INNER), F32),
                   jax.ShapeDtypeStruct((b, SSD_STATE, SSD_INNER), F32),
                   jax.ShapeDtypeStruct((b, 8, CONV_CH), F32)],
        scratch_shapes=[pltpu.VMEM((L + 8, CONV_CH), F32), pltpu.VMEM((SSD_STATE, SSD_INNER), F32),
                        pltpu.VMEM((L, SSD_INNER), F32)],
        compiler_params=_cparams(("arbitrary", "arbitrary")),
        name="ssd_scan",
    )(xbc, z, sm, smt, hist8, s0t, cw, cb, dtb_c, dtb_r, a_c, a_r, dsk_x, ng, e_mat)


def _fcum_kernel(tf, activate, colsrc_ref, rowsrc_ref, fbc_ref, fbr_ref, initc_ref, initr_ref,
                 lfc_ref, fc_ref, fr_ref, endc_ref, endr_ref, carc, carr):
    j = pl.program_id(1)

    @pl.when(j == 0)
    def _():
        carc[...] = initc_ref[0]
        carr[...] = initr_ref[0]

    ts = min(tf, FCUM_SUB)
    r_i = _iota((ts, ts), 0)
    c_i = _iota((ts, ts), 1)
    tri = jnp.where(r_i >= c_i, 1.0, 0.0).astype(BF16)
    triu = jnp.where(r_i <= c_i, 1.0, 0.0).astype(BF16)
    local = []
    for k in range(tf // ts):
        xc = colsrc_ref[0, k * ts:(k + 1) * ts, :]
        xr = rowsrc_ref[0, :, k * ts:(k + 1) * ts]
        if activate:
            xc = -_softplus(-(xc + fbc_ref[...]))
            xr = -_softplus(-(xr + fbr_ref[...]))
        lfc_ref[0, k * ts:(k + 1) * ts, :] = xc
        local.append((_dot3_r(tri, xc), _dot3_l(xr, triu)))
    car_c = carc[0:1, :]
    car_r = carr[:, 0:1]
    for k, (pc, pr) in enumerate(local):
        fcol = car_c + pc
        frow = car_r + pr
        fc_ref[0, k * ts:(k + 1) * ts, :] = fcol
        fr_ref[0, :, k * ts:(k + 1) * ts] = frow
        car_c = fcol[ts - 1:ts, :]
        car_r = frow[:, ts - 1:ts]
    carc[...] = jnp.broadcast_to(car_c, carc.shape)
    carr[...] = jnp.broadcast_to(car_r, carr.shape)
    endc_ref[0] = carc[...]
    endr_ref[0] = carr[...]


def _fcum(colsrc, rowsrc, row_block, fb_c, fb_r, init_c, init_r, activate):
    b, t, _ = colsrc.shape
    tf = min(t, FCUM_TILE)
    full = lambda a: pl.BlockSpec(a.shape, lambda i, j: (0,) * a.ndim)
    per_b = lambda a: pl.BlockSpec((1,) + a.shape[1:], lambda i, j: (i,) + (0,) * (a.ndim - 1))
    return pl.pallas_call(
        functools.partial(_fcum_kernel, tf, activate),
        grid=(b, t // tf),
        in_specs=[pl.BlockSpec((1, tf, LANES), lambda i, j: (i, j, 0)),
                  pl.BlockSpec((1, 8, tf), lambda i, j: (i, row_block, j)),
                  full(fb_c), full(fb_r), per_b(init_c), per_b(init_r)],
        out_specs=[pl.BlockSpec((1, tf, LANES), lambda i, j: (i, j, 0)),
                   pl.BlockSpec((1, tf, LANES), lambda i, j: (i, j, 0)),
                   pl.BlockSpec((1, 8, tf), lambda i, j: (i, 0, j)),
                   pl.BlockSpec((1, 8, LANES), lambda i, j: (i, 0, 0)),
                   pl.BlockSpec((1, 8, LANES), lambda i, j: (i, 0, 0))],
        out_shape=[jax.ShapeDtypeStruct((b, t, LANES), F32), jax.ShapeDtypeStruct((b, t, LANES), F32),
                   jax.ShapeDtypeStruct((b, 8, t), F32), jax.ShapeDtypeStruct((b, 8, LANES), F32),
                   jax.ShapeDtypeStruct((b, 8, LANES), F32)],
        scratch_shapes=[pltpu.VMEM((8, LANES), F32), pltpu.VMEM((8, LANES), F32)],
        compiler_params=_cparams(("arbitrary", "arbitrary")),
        name="forget_cumsum",
    )(colsrc, rowsrc, fb_c, fb_r, init_c, init_r)


def _softmax_stats(s, carry):
    m, l, acc = carry
    tq, tk = s.shape
    m_new = jnp.maximum(m, jnp.max(s, axis=-1, keepdims=True))
    p = jnp.exp(s - m_new)
    alpha = jnp.exp(m - m_new)
    if tk % LANES == 0:
        psum = p[:, 0:LANES]
        for c in range(1, tk // LANES):
            psum = psum + p[:, c * LANES:(c + 1) * LANES]
    else:
        psum = jnp.where(_iota((tq, LANES), 1) == 0, jnp.sum(p, axis=-1, keepdims=True), 0.0)
    return m_new, alpha * l + psum, alpha * acc, p.astype(BF16)


def _softmax_init(tq):
    return tuple((jnp.full((tq, 1), NEG, F32), jnp.zeros((tq, LANES), F32), jnp.zeros((tq, FOX_HEAD_DIM), F32))
                 for _ in range(FOX_HEADS))


def _head_slice(h):
    return slice(FOX_HEAD_DIM * h, FOX_HEAD_DIM * (h + 1))


def _attend_block(qs, fqs, kb, vb, fk, mask, carries):
    heads = range(FOX_HEADS)
    scores = [_dot_nt(qs[h], kb[:, _head_slice(h)]) for h in heads]
    stats = []
    for h in heads:
        s = scores[h] + fqs[h] - fk[h:h + 1, :]
        if mask is not None:
            s = jnp.where(mask, s, NEG)
        stats.append(_softmax_stats(s, carries[h]))
    return tuple((m, l, acc + _dot(p, vb[:, _head_slice(h)])) for h, (m, l, acc, p) in zip(heads, stats))


def _attend_finish(carries, o_ref):
    for h in range(FOX_HEADS):
        _, l, acc = carries[h]
        o_ref[0, :, _head_slice(h)] = acc / jnp.sum(l, axis=-1, keepdims=True)


def _bf16_terms(x):
    hi, mid, lo = _split3(x)
    return hi.astype(F32), mid.astype(F32), lo.astype(F32)


def _fox_prep_kernel(q_ref, k_ref, v_ref, fc_ref, fr_ref, ka_ref, qa_ref, vt_ref):
    tm = q_ref.shape[1]
    qt = q_ref[0].astype(F32).T
    k = k_ref[0].astype(F32)
    vt_ref[0, 0] = v_ref[0].astype(F32).T.astype(BF16)
    lane = _iota((tm, LANES), 1)
    row = _iota((FOX_HEAD_DIM, tm), 0)
    d = FOX_HEAD_DIM
    for h in range(FOX_HEADS):
        pair, odd = divmod(h, 2)
        kc = k[:, pair * LANES:(pair + 1) * LANES]
        if odd:
            kc = pltpu.roll(kc, d, axis=1)
        ka = jnp.where(lane < d, kc, 0.0)
        for i, term in enumerate(_bf16_terms(-fc_ref[0][:, F_ROW0 + h:F_ROW0 + h + 1])):
            ka = jnp.where(lane == d + i, term, ka)
        ka = jnp.where(jnp.logical_and(lane >= d + 3, lane < d + 6), 1.0, ka)
        ka_ref[0, h] = ka.astype(BF16)
        tail = jnp.where(row < 3, 1.0, 0.0)
        for i, term in enumerate(_bf16_terms(fr_ref[0][h:h + 1, :])):
            tail = jnp.where(row == 3 + i, term, tail)
        qa_ref[0, h, 0:d, :] = qt[d * h:d * (h + 1), :].astype(BF16)
        qa_ref[0, h, d:2 * d, :] = tail.astype(BF16)


def _fox_prep(qb, kb, vb, fcol, frow, tm):
    b, t, _ = qb.shape
    tok = lambda c: pl.BlockSpec((1, tm, c), lambda i, j: (i, j, 0))
    return pl.pallas_call(
        _fox_prep_kernel,
        grid=(b, t // tm),
        in_specs=[tok(FOX_INNER), tok(FOX_INNER), tok(FOX_INNER), tok(LANES),
                  pl.BlockSpec((1, 8, tm), lambda i, j: (i, 0, j))],
        out_specs=[pl.BlockSpec((1, FOX_HEADS, tm, LANES), lambda i, j: (i, 0, j, 0)),
                   pl.BlockSpec((1, FOX_HEADS, LANES, tm), lambda i, j: (i, 0, 0, j)),
                   pl.BlockSpec((1, 1, FOX_INNER, tm), lambda i, j: (i, j, 0, 0))],
        out_shape=[jax.ShapeDtypeStruct((b, FOX_HEADS, t, LANES), BF16),
                   jax.ShapeDtypeStruct((b, FOX_HEADS, LANES, t), BF16),
                   jax.ShapeDtypeStruct((b, t // tm, FOX_INNER, tm), BF16)],
        compiler_params=_cparams(("arbitrary", "arbitrary")),
        name="fox_prep",
    )(qb, kb, vb, fcol, frow)


def _fox_prompt_t_kernel(tq, nk, qa_ref, ka_ref, vt_ref, fend_ref, ffirst_ref, thr_ref, o_ref):
    b = pl.program_id(0)
    qi = pl.program_id(1)
    half = tq // FOX_CHAINS
    d = FOX_HEAD_DIM
    diag = pl.multiple_of(qi * tq, tq)
    init = (jnp.full((1, half), NEG, F32), jnp.zeros((1, half), F32), jnp.zeros((d, half), F32))

    def update(blocks, qas, masks, carries):
        scores = [[_dot(ka, qa) for qa in qas] for ka, _ in blocks]
        for (_, vt), block_scores in zip(blocks, scores):
            stats = []
            for s, mask, (m, l, acc) in zip(block_scores, masks, carries):
                if mask is not None:
                    s = jnp.where(mask, s, NEG)
                m_new = jnp.maximum(m, jnp.max(s, axis=0, keepdims=True))
                p = jnp.exp(s - m_new)
                alpha = jnp.exp(m - m_new)
                stats.append((m_new, alpha * l + jnp.sum(p, axis=0, keepdims=True), alpha * acc, p.astype(BF16)))
            carries = tuple((m, l, acc + _dot(vt, p)) for m, l, acc, p in stats)
        return carries

    def key_block(h, j):
        return (ka_ref[0, h, pl.ds(pl.multiple_of(j * tq, tq), tq), :], vt_ref[0, j][d * h:d * (h + 1), :])

    for h in range(FOX_HEADS):
        base = (b * FOX_HEADS + h) * nk
        slack = thr_ref[0] + ffirst_ref[base + qi]
        n_live = lax.fori_loop(0, qi, lambda j, c: c + (slack - fend_ref[base + j] >= 0.0).astype(jnp.int32), 0)
        qas = [qa_ref[0, h, :, r2 * half:(r2 + 1) * half] for r2 in range(FOX_CHAINS)]
        ka_d = ka_ref[0, h, pl.ds(diag, tq), :]
        vt_d = vt_ref[0, qi][d * h:d * (h + 1), :]
        causal = [_iota((tq, half), 0) <= (r2 * half + _iota((tq, half), 1)) for r2 in range(FOX_CHAINS)]
        carries = update([(ka_d, vt_d)], qas, causal, [init] * FOX_CHAINS)
        no_mask = [None] * FOX_CHAINS

        def one(t, carries, qas=qas, h=h):
            return update([key_block(h, qi - 1 - t)], qas, no_mask, carries)

        def two(t, carries, qas=qas, h=h):
            j = qi - 1 - (n_live % 2) - 2 * t
            return update([key_block(h, j), key_block(h, j - 1)], qas, no_mask, carries)

        carries = lax.fori_loop(0, n_live % 2, one, carries)
        carries = lax.fori_loop(0, n_live // 2, two, carries)
        for r2 in range(FOX_CHAINS):
            _, l, acc = carries[r2]
            o_ref[0, r2 * half:(r2 + 1) * half, d * h:d * (h + 1)] = (acc / l).T


def _fox_prompt_t(qb, kb, vb, fcol, frow, score_bound):
    b, t, _ = qb.shape
    tq = min(t, FOX_PROMPT_TILE)
    nk = t // tq
    ka, qa, vt = _fox_prep(qb, kb, vb, fcol, frow, tq)
    f_heads = frow[:, :FOX_HEADS, :].reshape(b, FOX_HEADS, nk, tq)
    f_end = f_heads[:, :, :, tq - 1].reshape(-1)
    f_first = f_heads[:, :, :, 0].reshape(-1)
    thr = (2.0 * score_bound + EXP_UNDERFLOW).reshape(1).astype(F32)
    smem = pl.BlockSpec(memory_space=pltpu.SMEM)
    return pl.pallas_call(
        functools.partial(_fox_prompt_t_kernel, tq, nk),
        grid=(b, nk),
        in_specs=[pl.BlockSpec((1, FOX_HEADS, LANES, tq), lambda i, j: (i, 0, 0, j)),
                  pl.BlockSpec((1, FOX_HEADS, t, LANES), lambda i, j: (i, 0, 0, 0)),
                  pl.BlockSpec((1, nk, FOX_INNER, tq), lambda i, j: (i, 0, 0, 0)),
                  smem, smem, smem],
        out_specs=pl.BlockSpec((1, tq, FOX_INNER), lambda i, j: (i, j, 0)),
        out_shape=jax.ShapeDtypeStruct((b, t, FOX_INNER), F32),
        compiler_params=_cparams(("arbitrary", "arbitrary")),
        name="fox_prompt",
    )(qa, ka, vt, f_end, f_first, thr)


def _fox_sample_kernel(tq, tk, npast, q_ref, fq_ref, pk_ref, pv_ref, fpk_ref, k_ref, v_ref, fk_ref, o_ref):
    q_all = q_ref[0]
    fq_all = fq_ref[0]
    qs = [q_all[:, _head_slice(h)] for h in range(FOX_HEADS)]
    fqs = [fq_all[:, F_ROW0 + h:F_ROW0 + h + 1] for h in range(FOX_HEADS)]
    causal = _iota((tq, tq), 0) >= _iota((tq, tq), 1)

    def past_block(j, carries):
        start = pl.multiple_of(j * tk, tk)
        return _attend_block(qs, fqs, pk_ref[0, pl.ds(start, tk), :].astype(BF16),
                             pv_ref[0, pl.ds(start, tk), :].astype(BF16), fpk_ref[0, j], None, carries)

    carries = lax.fori_loop(0, npast, past_block, _softmax_init(tq))
    _attend_finish(_attend_block(qs, fqs, k_ref[0], v_ref[0], fk_ref[0], causal, carries), o_ref)


def _fox_sample(qb, fcol, past_k, past_v, layer, fpast_row, kb, vb, frow):
    b, t, _ = qb.shape
    p = past_k.shape[2]
    tk = min(p, 512)
    npast = p // tk
    fpk = fpast_row.reshape(b, 8, npast, tk).transpose(0, 2, 1, 3)
    bspec = lambda a: pl.BlockSpec((1,) + a.shape[1:], lambda i: (i,) + (0,) * (a.ndim - 1))
    cache = pl.BlockSpec((None, 1, p, FOX_INNER), lambda i: (layer, i, 0, 0))
    args = (qb, fcol, past_k, past_v, fpk, kb, vb, frow)
    return pl.pallas_call(
        functools.partial(_fox_sample_kernel, t, tk, npast),
        grid=(b,),
        in_specs=[bspec(qb), bspec(fcol), cache, cache, bspec(fpk), bspec(kb), bspec(vb), bspec(frow)],
        out_specs=pl.BlockSpec((1, t, FOX_INNER), lambda i: (i, 0, 0)),
        out_shape=jax.ShapeDtypeStruct((b, t, FOX_INNER), F32),
        compiler_params=_cparams(("arbitrary",)),
        name="fox_sample",
    )(*args)


def _mlp_kernel(lm, nchunk, ug_ref, vn_ref, ws_ref, bst_ref, y_ref):
    r_i = _iota((lm, lm), 0)
    c_i = _iota((lm, lm), 1)
    tril = r_i >= c_i
    vn = vn_ref[0].astype(BF16)
    ug = ug_ref[0]
    for g in range(MLP_GROUPS):
        lo, hi = MLP_GROUP_DIM * g, MLP_GROUP_DIM * (g + 1)
        w = jnp.where(tril, ws_ref[g], 0.0).astype(BF16)
        bias = bst_ref[:, g:g + 1]
        for c in range(nchunk):
            r0, r1 = c * lm, (c + 1) * lm
            sv = _dot(w, vn[r0:r1, lo:hi]) + bias
            y_ref[0, r0:r1, lo:hi] = ug[r0:r1, lo:hi] * sv


def _chunk_mlp(ug, vn, ws, bst):
    b, t, _ = ug.shape
    lm = ws.shape[1]
    tm = min(t, 4 * lm)
    tok = pl.BlockSpec((1, tm, MLP_INNER), lambda i, j: (i, j, 0))
    return pl.pallas_call(
        functools.partial(_mlp_kernel, lm, tm // lm),
        grid=(b, t // tm),
        in_specs=[tok, tok, pl.BlockSpec(ws.shape, lambda i, j: (0, 0, 0)),
                  pl.BlockSpec(bst.shape, lambda i, j: (0, 0))],
        out_specs=tok,
        out_shape=jax.ShapeDtypeStruct((b, t, MLP_INNER), F32),
        compiler_params=_cparams(("arbitrary", "arbitrary")),
        name="chunk_mlp",
    )(ug, vn, ws, bst)


def _out_kernel(x_ref, ys_ref, yf_ref, ym_ref, g1_ref, sh_ref, sc_ref, g_ref, wo_ref, x1_ref, h2_ref):
    mix = (_dot(ys_ref[0].astype(BF16), wo_ref[0:SSD_INNER, :])
           + _dot(yf_ref[0].astype(BF16), wo_ref[SSD_INNER:SSD_INNER + FOX_INNER, :])
           + _dot(ym_ref[0].astype(BF16), wo_ref[SSD_INNER + FOX_INNER:D_MODEL, :]))
    x1 = x_ref[0] + g1_ref[0] * mix
    x1_ref[0] = x1
    ms = jnp.mean(x1 * x1, axis=-1, keepdims=True)
    h = x1 * lax.rsqrt(ms + EPS) * g_ref[...]
    h2_ref[0] = _pack_rows((h * (1.0 + sc_ref[0]) + sh_ref[0]).astype(BF16))


def _out_proj(x, ys, yf, ym, g1, sh, sc, g, wo):
    b, t, _ = x.shape
    tm = min(t, 512)
    tok = lambda c: pl.BlockSpec((1, tm, c), lambda i, j: (i, j, 0))
    per_b = pl.BlockSpec((1, 1, D_MODEL), lambda i, j: (i, 0, 0))
    full = lambda a: pl.BlockSpec(a.shape, lambda i, j: (0,) * a.ndim)
    return pl.pallas_call(
        _out_kernel,
        grid=(b, t // tm),
        in_specs=[tok(D_MODEL), tok(SSD_INNER), tok(FOX_INNER), tok(MLP_INNER), per_b, per_b, per_b, full(g), full(wo)],
        out_specs=[tok(D_MODEL), pl.BlockSpec((1, tm // 2, D_MODEL), lambda i, j: (i, j, 0))],
        out_shape=[jax.ShapeDtypeStruct((b, t, D_MODEL), F32),
                   jax.ShapeDtypeStruct((b, t // 2, D_MODEL), jnp.uint32)],
        compiler_params=_cparams(("arbitrary", "arbitrary")),
        name="out_proj",
    )(x, ys, yf, ym, g1, sh, sc, g, wo)


def _top16(scores):
    nk, tn = scores[0].shape
    ridx = _iota((nk, tn), 0).astype(F32)
    r16 = _iota((PEER_TOPK, tn), 0)
    state = [(s, jnp.full((nk, tn), float(PEER_TOPK), F32), jnp.zeros((PEER_TOPK, tn), F32)) for s in scores]
    for it in range(PEER_TOPK):
        nxt_state = []
        for s, pos, tv in state:
            level = [(s[g:g + 8, :], ridx[g:g + 8, :]) for g in range(0, nk, 8)]
            while len(level) > 1:
                nxt = []
                for a in range(0, len(level), 2):
                    (va, ia), (vb, ib) = level[a], level[a + 1]
                    nxt.append((jnp.maximum(va, vb), jnp.where(va >= vb, ia, ib)))
                level = nxt
            v8, i8 = level[0]
            m = jnp.max(v8, axis=0, keepdims=True)
            first = jnp.min(jnp.where(v8 == m, i8, float(nk)), axis=0, keepdims=True)
            sel = ridx == first
            nxt_state.append((jnp.where(sel, -jnp.inf, s), jnp.where(sel, float(it), pos),
                              jnp.where(r16 == it, m, tv)))
        state = nxt_state
    return [(pos, tv) for _, pos, tv in state]


_CAND_GROUPS = [(0, 16)] + [(ka, 8) for ka in range(1, 8)]
_CAND_ROWS = 16 + 7 * 8 + 8


def _pair_select(ta, tb):
    tn = ta.shape[1]
    pieces, flats, valids = [], [], []
    for ka, rows in _CAND_GROUPS:
        pieces.append(ta[ka:ka + 1, :] + tb[0:rows, :])
        kb = _iota((rows, 1), 0)
        flats.append((ka * PEER_TOPK + kb).astype(F32))
        valids.append((ka + 1) * (kb + 1) <= PEER_TOPK)
    pieces.append(ta[8:16, :] + tb[0:1, :])
    flats.append(((8 + _iota((8, 1), 0)) * PEER_TOPK).astype(F32))
    valids.append(_iota((8, 1), 0) >= 0)
    cand0 = jnp.concatenate(pieces, axis=0)
    flat = jnp.concatenate(flats, axis=0)
    valid = jnp.concatenate(valids, axis=0)
    cand0 = jnp.where(valid, cand0, -jnp.inf)
    best = ta[0:1, :] + tb[0:1, :]

    cand = cand0
    selm = jnp.zeros((_CAND_ROWS, tn), F32)
    for _ in range(PEER_TOPK):
        m = jnp.max(cand, axis=0, keepdims=True)
        first = jnp.min(jnp.where(cand == m, flat, 4096.0), axis=0, keepdims=True)
        sel = flat == first
        cand = jnp.where(sel, -jnp.inf, cand)
        selm = jnp.where(sel, 1.0, selm)
    z = jnp.sum(jnp.where(selm > 0.0, jnp.exp(cand0 - best), 0.0), axis=0, keepdims=True)
    cnts = [jnp.sum(selm[0:16, :], axis=0, keepdims=True)]
    for i in range(1, 8):
        cnts.append(jnp.sum(selm[8 + 8 * i:16 + 8 * i, :], axis=0, keepdims=True))
    cnts.append(selm[_CAND_ROWS - 8:_CAND_ROWS, :])
    return jnp.concatenate(cnts, axis=0), z


def _peer_sel_kernel(tn, h_ref, wqt_ref, keys_ref, ea_ref, la_ref, eb_ref, pb_ref, qt_ref):
    qt_ref[...] = _dot_nt(_unpack_rows(wqt_ref[...]), _unpack_rows(h_ref[...])).astype(BF16)

    def head(h, carry):
        row = pl.multiple_of(h * (2 * PEER_HALF), 2 * PEER_HALF)
        sa_all = _dot(keys_ref[2 * h], qt_ref[pl.ds(row, PEER_HALF), :])
        sb_all = _dot(keys_ref[2 * h + 1], qt_ref[pl.ds(row + PEER_HALF, PEER_HALF), :])
        chunks = [slice(c * LANES, (c + 1) * LANES) for c in range(tn // LANES)]
        tops = _top16([s_all[:, cs] for cs in chunks for s_all in (sa_all, sb_all)])
        for c, cs in enumerate(chunks):
            sa, sb = sa_all[:, cs], sb_all[:, cs]
            (pos_a, ta), (pos_b, tb) = tops[2 * c], tops[2 * c + 1]
            cnt, z = _pair_select(ta, tb)
            pos_h = pos_a.astype(BF16)
            cnt_h = cnt.astype(BF16)
            la = jnp.zeros_like(pos_h)
            for ka in range(PEER_TOPK):
                la = jnp.where(pos_h == float(ka), jnp.broadcast_to(cnt_h[ka:ka + 1, :], pos_h.shape), la)
            ea = jnp.where(pos_a < float(PEER_TOPK), jnp.exp(sa - ta[0:1, :]), 0.0) / z
            ea_ref[h, :, cs] = _twin_bf16(ea.astype(BF16).astype(F32))
            la_ref[h, :, cs] = _twin_bf16(la.astype(F32))
            eb = jnp.where(pos_b < float(PEER_TOPK), jnp.exp(sb - tb[0:1, :]), 0.0)
            for sub in range(PEER_KEYS // PEER_SUB):
                src = slice(sub * PEER_SUB, (sub + 1) * PEER_SUB)
                dst = slice(sub * PEER_SUB // 2, (sub + 1) * PEER_SUB // 2)
                eb_ref[h, dst, cs] = pltpu.bitcast(eb[src].astype(BF16), jnp.uint32)
                pb_ref[h, dst, cs] = pltpu.bitcast(pos_b[src].astype(BF16), jnp.uint32)
        return carry

    lax.fori_loop(0, PEER_HEADS, head, 0)


def _peer_select(h2, wqt_all, layer, keys):
    n = 2 * h2.shape[0]
    tn = 256
    per_tok = pl.BlockSpec((PEER_HEADS, PEER_KEYS, tn), lambda i: (0, 0, i))
    per_tok_pk = pl.BlockSpec((PEER_HEADS, PEER_KEYS // 2, tn), lambda i: (0, 0, i))
    shp = lambda dt: jax.ShapeDtypeStruct((PEER_HEADS, PEER_KEYS, n), dt)
    shp_pk = jax.ShapeDtypeStruct((PEER_HEADS, PEER_KEYS // 2, n), jnp.uint32)
    return pl.pallas_call(
        functools.partial(_peer_sel_kernel, tn),
        grid=(n // tn,),
        in_specs=[pl.BlockSpec((tn // 2, D_MODEL), lambda i: (i, 0)),
                  pl.BlockSpec((None,) + wqt_all.shape[1:], lambda i: (layer, 0, 0)),
                  pl.BlockSpec(keys.shape, lambda i: (0, 0, 0))],
        out_specs=[per_tok, per_tok, per_tok_pk, per_tok_pk],
        out_shape=[shp(jnp.uint32), shp(jnp.uint32), shp_pk, shp_pk],
        scratch_shapes=[pltpu.VMEM((PEER_HEADS * 2 * PEER_HALF, tn), BF16)],
        compiler_params=_cparams(("arbitrary",)),
        name="peer_select",
    )(h2, wqt_all, keys)


def _peer_gate_stage(ec, ia0, rows, chunks, at_r, ga_w, ea_ref, la_ref, eb_ref, pb_ref):
    rows_per = ec // PEER_KEYS
    for r in rows:
        for c in chunks:
            cs = slice(c * LANES, (c + 1) * LANES)
            for sub in range(PEER_KEYS // PEER_SUB):
                pk = slice(sub * PEER_SUB // 2, (sub + 1) * PEER_SUB // 2)
                gate = None
                for h in range(PEER_HEADS):
                    la = la_ref[h, pl.ds(ia0, rows_per), cs][r:r + 1, :]
                    ea = ea_ref[h, pl.ds(ia0, rows_per), cs][r:r + 1, :]
                    la = pltpu.bitcast(jnp.broadcast_to(la, (PEER_SUB // 2, LANES)), BF16)
                    ea = pltpu.bitcast(jnp.broadcast_to(ea, (PEER_SUB // 2, LANES)), BF16)
                    pb = pltpu.bitcast(pb_ref[h, pk, cs], BF16)
                    eb = pltpu.bitcast(eb_ref[h, pk, cs], BF16)
                    term = jnp.where(pb < la, eb, 0.0) * ea
                    gate = term if gate is None else gate + term
                ex = slice(r * PEER_KEYS + sub * PEER_SUB, r * PEER_KEYS + (sub + 1) * PEER_SUB)
                ga_w[ex, cs] = gate * _gelu(at_r[ex, cs]).astype(BF16)


def _peer_dense_kernel(tn, ec, h_ref, u_ref, vt_ref, ea_ref, la_ref, eb_ref, pb_ref, o_ref, at, ga, acc):
    e = pl.program_id(1)
    ne = pl.num_programs(1)

    @pl.when(e == 0)
    def _():
        acc[...] = jnp.zeros_like(acc)

    rows_per = ec // PEER_KEYS
    ia0 = pl.multiple_of(e * rows_per, rows_per)
    parts = PEER_DENSE_PARTS
    per = ec // parts
    h_t = _unpack_rows(h_ref[...])
    for g in range(parts):
        rows = slice(g * per, (g + 1) * per)
        pk = slice(g * per // 2, (g + 1) * per // 2)
        at[rows, :] = _dot_nt(_unpack_rows(u_ref[pk, :]), h_t)
    for g in range(parts):
        rows = slice(g * per, (g + 1) * per)
        _peer_gate_stage(ec, ia0, range(g * rows_per // parts, (g + 1) * rows_per // parts), range(tn // LANES),
                         at, ga, ea_ref, la_ref, eb_ref, pb_ref)
        acc[...] += _dot(_unpack_rows(vt_ref[:, rows]), ga[rows, :])

    @pl.when(e == ne - 1)
    def _():
        o_ref[...] = acc[...].T


def _peer_dense(h2, u_all, vt_all, layer, ea, la, eb, pb):
    n = 2 * h2.shape[0]
    tn, ec = PEER_TOKEN_TILE, PEER_EXPERT_CHUNK
    per_tok = pl.BlockSpec((PEER_HEADS, PEER_KEYS, tn), lambda i, e: (0, 0, i))
    per_tok_pk = pl.BlockSpec((PEER_HEADS, PEER_KEYS // 2, tn), lambda i, e: (0, 0, i))
    return pl.pallas_call(
        functools.partial(_peer_dense_kernel, tn, ec),
        grid=(n // tn, PEER_EXPERTS // ec),
        in_specs=[pl.BlockSpec((tn // 2, D_MODEL), lambda i, e: (i, 0)),
                  pl.BlockSpec((None, ec // 2, D_MODEL), lambda i, e: (layer, e, 0)),
                  pl.BlockSpec((None, D_MODEL // 2, ec), lambda i, e: (layer, 0, e)),
                  per_tok, per_tok, per_tok_pk, per_tok_pk],
        out_specs=pl.BlockSpec((tn, D_MODEL), lambda i, e: (i, 0)),
        out_shape=jax.ShapeDtypeStruct((n, D_MODEL), F32),
        scratch_shapes=[pltpu.VMEM((ec, tn), F32), pltpu.VMEM((ec, tn), BF16), pltpu.VMEM((D_MODEL, tn), F32)],
        compiler_params=_cparams(("arbitrary", "arbitrary")),
        name="peer_dense",
    )(h2, u_all, vt_all, ea, la, eb, pb)


def _resid_kernel(x_ref, p_ref, g_ref, o_ref):
    o_ref[0] = x_ref[0] + g_ref[0] * p_ref[0]


def _residual(x, peer, g2):
    b, t, _ = x.shape
    tm = min(t, 512)
    tok = pl.BlockSpec((1, tm, D_MODEL), lambda i, j: (i, j, 0))
    return pl.pallas_call(
        _resid_kernel,
        grid=(b, t // tm),
        in_specs=[tok, tok, pl.BlockSpec((1, 1, D_MODEL), lambda i, j: (i, 0, 0))],
        out_specs=tok,
        out_shape=jax.ShapeDtypeStruct((b, t, D_MODEL), F32),
        compiler_params=_cparams(("arbitrary", "arbitrary")),
        name="peer_residual",
    )(x, peer, g2)


def _lane_pad(vec, offset, width=LANES):
    out = jnp.zeros((width,), F32)
    return out.at[offset:offset + vec.shape[0]].set(vec.astype(F32))


def _layer_params(l, norm1_g, norm2_g, w_in, conv_w, conv_b, dt_bias, a_log, d_skip, ssd_norm_g, q_norm_g,
                  k_norm_g, fgate_b, w_s, b_s, w_out, peer_keys):
    w = w_in[l]
    o = [0]
    for sz in (SSD_INNER, CONV_CH, SSD_HEADS, FOX_INNER, FOX_INNER, FOX_INNER, FOX_HEADS, MLP_INNER, MLP_INNER):
        o.append(o[-1] + sz)
    wz, wxbc, wdt, wq, wk, wv, wf, wu, wvm = [w[:, o[i]:o[i + 1]] for i in range(9)]
    w_small = jnp.zeros((D_MODEL, SMALL_COLS), F32)
    w_small = w_small.at[:, DT_ROW0:DT_ROW0 + SSD_HEADS].set(wdt).at[:, F_ROW0:F_ROW0 + FOX_HEADS].set(wf)
    w_r = jnp.concatenate([wz, wxbc, wq, wk, wv, wu, wvm, w_small], axis=1).astype(BF16)
    ws_t = w_small[:, :SMALL_ROWS].T.astype(BF16)
    a_neg = -jnp.exp(a_log[l].astype(F32))
    grp = jnp.arange(FOX_INNER) // FOX_HEAD_DIM
    gm = jnp.where(grp[:, None] == grp[None, :], 1.0 / FOX_HEAD_DIM, 0.0).astype(BF16)
    heads = jnp.arange(SSD_INNER) // SSD_HEAD_DIM
    e_mat = (jnp.arange(LANES)[:, None] == heads[None, :]).astype(BF16)
    return dict(
        norm1_g=norm1_g[l][None], norm2_g=norm2_g[l][None], w_r=w_r, ws_t=ws_t,
        gq=(jnp.tile(q_norm_g[l], FOX_HEADS) * (FOX_HEAD_DIM ** -0.5))[None],
        gk=jnp.tile(k_norm_g[l], FOX_HEADS)[None], gm=gm,
        fox_bound=1.02 * FOX_HEAD_DIM ** 0.5 * jnp.max(jnp.abs(q_norm_g[l])) * jnp.max(jnp.abs(k_norm_g[l])),
        conv_w=conv_w[l], conv_b=conv_b[l][None],
        dtb_c=_lane_pad(dt_bias[l], DT_ROW0)[None], dtb_r=dt_bias[l].astype(F32)[:, None],
        a_c=_lane_pad(a_neg, DT_ROW0)[None], a_r=a_neg[:, None],
        dsk_x=jnp.repeat(d_skip[l].astype(F32), SSD_HEAD_DIM)[None], ssd_g=ssd_norm_g[l][None], e_mat=e_mat,
        fb_c=_lane_pad(fgate_b[l], F_ROW0)[None], fb_r=_lane_pad(fgate_b[l], 0, 8)[:, None],
        w_s=w_s[l], bs_t=jnp.zeros((MLP_CHUNK, LANES), F32).at[:, :MLP_GROUPS].set(b_s[l].T),
        w_out=w_out[l].astype(BF16),
        keys=peer_keys[l].reshape(PEER_HEADS * 2, PEER_KEYS, PEER_HALF).astype(BF16),
    )


def _stream_mixers(x, peer, g2_prev, mod, p, hist8, s0t, past):
    sh1, sc1, g1, sh2, sc2, _ = mod
    b, t, _ = x.shape
    x, (z, xbc, qb, kn, kb, v, vb, ug, vn, sm, smt) = _in_proj(
        x, peer, g2_prev, sh1, sc1, p['norm1_g'], p['w_r'], p['ws_t'], p['gq'], p['gk'], p['gm'])
    L = SSD_CHUNK_PROMPT if t % SSD_CHUNK_PROMPT == 0 else t
    y_ssd, s_fin, c_fin = _ssd(xbc, z, sm, smt, hist8, s0t, p['conv_w'], p['conv_b'], p['dtb_c'], p['dtb_r'],
                               p['a_c'], p['a_r'], p['dsk_x'], p['ssd_g'], p['e_mat'], L)
    zeros8 = jnp.zeros((b, 8, LANES), F32)
    if past is None:
        logf_c, f_col, f_row, _, _ = _fcum(sm, smt, F_ROW0 // 8, p['fb_c'], p['fb_r'], zeros8, zeros8, True)
        y_fox = _fox_prompt_t(qb, kb, vb, f_col, f_row, p['fox_bound'])
    else:
        pk, pv, layer, plf_col, plf_row = past
        _, _, fp_row, end_c, end_r = _fcum(plf_col, plf_row, 0, p['fb_c'], p['fb_r'], zeros8, zeros8, False)
        logf_c, f_col, f_row, _, _ = _fcum(sm, smt, F_ROW0 // 8, p['fb_c'], p['fb_r'], end_c, end_r, True)
        y_fox = _fox_sample(qb, f_col, pk, pv, layer, fp_row, kb, vb, f_row)
    lm = MLP_CHUNK if t % MLP_CHUNK == 0 else t
    y_mlp = _chunk_mlp(ug, vn, p['w_s'][:, :lm, :lm], p['bs_t'][:lm])
    x1, h2 = _out_proj(x, y_ssd, y_fox, y_mlp, g1, sh2, sc2, p['norm2_g'], p['w_out'])
    logf = logf_c[:, :, F_ROW0:F_ROW0 + FOX_HEADS]
    new_ssm = s_fin.reshape(b, SSD_STATE, SSD_HEADS, SSD_HEAD_DIM).transpose(0, 2, 3, 1)
    new_conv = c_fin[:, 8 - (CONV_W - 1):, :]
    kc = kn.reshape(b, t, FOX_HEADS, FOX_HEAD_DIM)
    vc = v.reshape(b, t, FOX_HEADS, FOX_HEAD_DIM)
    return x1, h2, (kc, vc, logf, new_ssm, new_conv, vn)


def _peer(h2, p, tables, layer):
    b, t2, _ = h2.shape
    t = 2 * t2
    n_tok = b * t
    n_pad = -n_tok % PEER_TOKEN_TILE
    flat = h2.reshape(n_tok // 2, D_MODEL)
    if n_pad:
        flat = jnp.concatenate([flat, jnp.zeros((n_pad // 2, D_MODEL), jnp.uint32)], axis=0)
    wqt_all, u_all, vt_all = tables
    ea, la, eb, pb = _peer_select(flat, wqt_all, layer, p['keys'])
    out = _peer_dense(flat, u_all, vt_all, layer, ea, la, eb, pb)
    return out[:n_tok].reshape(b, t, D_MODEL)


def kernel(x_prompt, x_sample, c_prompt, c_sample, cache_fox_k, cache_fox_v, cache_fox_logf, state_ssm, state_conv, norm1_g, norm2_g, w_ada, b_ada, w_in, conv_w, conv_b, dt_bias, a_log, d_skip, ssd_norm_g, q_norm_g, k_norm_g, fgate_b, w_s, b_s, w_out, peer_wq, peer_keys, peer_u, peer_v):
    depth = w_ada.shape[0]
    bp, tp, _ = x_prompt.shape
    bs, ts, _ = x_sample.shape
    past_len = cache_fox_k.shape[2]
    mod_all = _modulation(jnp.concatenate([c_prompt, c_sample], axis=0).astype(F32), w_ada, b_ada)

    peer_tables = (_to_bf16_t(peer_wq, D_MODEL, 512, True), _to_bf16_packed(peer_u, 1024),
                   _to_bf16_t(peer_v, 512, D_MODEL, True))

    past_k_all = cache_fox_k.reshape(depth, bs, past_len, FOX_INNER)
    past_v_all = cache_fox_v.reshape(depth, bs, past_len, FOX_INNER)

    xp, xs = x_prompt, x_sample
    peer_p = peer_s = g2p = g2s = None
    outs = [[] for _ in range(11)]
    for l in range(depth):
        p = _layer_params(l, norm1_g, norm2_g, w_in, conv_w, conv_b, dt_bias, a_log, d_skip, ssd_norm_g, q_norm_g,
                          k_norm_g, fgate_b, w_s, b_s, w_out, peer_keys)
        mods = [m[:, None, :] for m in jnp.split(mod_all[l], 6, axis=-1)]
        mod_p = [m[:bp] for m in mods]
        mod_s = [m[bp:] for m in mods]
        hist_p = jnp.zeros((bp, 8, CONV_CH), F32)
        s0_p = jnp.zeros((bp, SSD_STATE, SSD_INNER), F32)
        hist_s = jnp.concatenate([jnp.zeros((bs, 8 - (CONV_W - 1), CONV_CH), F32), state_conv[l].astype(F32)], axis=1)
        s0_s = state_ssm[l].astype(F32).transpose(0, 3, 1, 2).reshape(bs, SSD_STATE, SSD_INNER)
        plf = cache_fox_logf[l].astype(F32)
        plf_col = jnp.pad(plf, ((0, 0), (0, 0), (F_ROW0, LANES - F_ROW0 - FOX_HEADS)))
        plf_row = jnp.pad(plf.transpose(0, 2, 1), ((0, 0), (0, 8 - FOX_HEADS), (0, 0)))
        past = (past_k_all, past_v_all, l, plf_col, plf_row)

        x1p, h2p, st_p = _stream_mixers(xp, peer_p, g2p, mod_p, p, hist_p, s0_p, None)
        x1s, h2s, st_s = _stream_mixers(xs, peer_s, g2s, mod_s, p, hist_s, s0_s, past)

        peer_p = _peer(h2p, p, peer_tables, l)
        peer_s = _peer(h2s, p, peer_tables, l)
        xp, xs, g2p, g2s = x1p, x1s, mod_p[5], mod_s[5]
        for i in range(5):
            outs[i].append(st_p[i])
        for i in range(6):
            outs[5 + i].append(st_s[i])

    yp = _residual(xp, peer_p, g2p)
    ys = _residual(xs, peer_s, g2s)
    return (yp, ys) + tuple(jnp.stack(o) for o in outs)
```

```python
import functools

import jax
import jax.numpy as jnp
from jax import lax
from jax.experimental import pallas as pl
from jax.experimental.pallas import tpu as pltpu

F32 = jnp.float32
BF16 = jnp.bfloat16
EPS = 1e-6

D_MODEL = 1024
SSD_HEADS = 8
SSD_HEAD_DIM = 64
SSD_INNER = SSD_HEADS * SSD_HEAD_DIM
SSD_GROUPS = 2
SSD_STATE = 64
CONV_W = 4
CONV_CH = SSD_INNER + 2 * SSD_GROUPS * SSD_STATE
FOX_HEADS = 4
FOX_HEAD_DIM = 64
FOX_INNER = FOX_HEADS * FOX_HEAD_DIM
MLP_GROUPS = 4
MLP_GROUP_DIM = 64
MLP_INNER = MLP_GROUPS * MLP_GROUP_DIM
MLP_CHUNK = 128
FOX_PROMPT_TILE = 1024
FOX_CHAINS = 4
FCUM_TILE = 2048
FCUM_SUB = 512
SSD_CHUNK_PROMPT = 128
PEER_HEADS = 8
PEER_KEYS = 128
PEER_EXPERTS = PEER_KEYS * PEER_KEYS
PEER_HALF = 128
PEER_TOPK = 16
PEER_TOKEN_TILE = 512
PEER_EXPERT_CHUNK = 2048
LANES = 128
MXU_DIM = 256
PEER_DENSE_PARTS = 4
PEER_SUB = 128
SMALL_COLS = LANES
SMALL_ROWS = 16
DT_ROW0, F_ROW0 = 0, 8
NEG = -1e30
EXP_UNDERFLOW = 104.0
VMEM_LIMIT = 56 * 1024 * 1024

_Z0, _XBC0, _Q0, _K0, _V0, _U0, _VM0, _SM0 = 0, 512, 1280, 1536, 1792, 2048, 2304, 2560
PROJ_COLS = _SM0 + SMALL_COLS


def _cparams(sem, flags=None):
    return pltpu.CompilerParams(dimension_semantics=sem, vmem_limit_bytes=VMEM_LIMIT, flags=flags)


def _dot(a, b):
    return jnp.dot(a, b, preferred_element_type=F32)


def _dot_nt(a, b):
    return lax.dot_general(a, b, (((1,), (1,)), ((), ())), preferred_element_type=F32)


def _dot_tn(a, b):
    return lax.dot_general(a, b, (((0,), (0,)), ((), ())), preferred_element_type=F32)


def _split2(x):
    hi = x.astype(BF16)
    lo = (x - hi.astype(F32)).astype(BF16)
    return hi, lo


def _split3(x):
    hi = x.astype(BF16)
    r = x - hi.astype(F32)
    mid = r.astype(BF16)
    lo = (r - mid.astype(F32)).astype(BF16)
    return hi, mid, lo


def _dot3_l(x, w):
    hi, mid, lo = _split3(x)
    return _dot(hi, w) + _dot(mid, w) + _dot(lo, w)


def _dot3_r(w, x):
    hi, mid, lo = _split3(x)
    return _dot(w, hi) + _dot(w, mid) + _dot(w, lo)


def _sigmoid(x):
    return 1.0 / (1.0 + jnp.exp(-x))


def _silu(x):
    return x * _sigmoid(x)


def _softplus(x):
    return jnp.maximum(x, 0.0) + jnp.log1p(jnp.exp(-jnp.abs(x)))


_GELU_A = 2.0 * 0.7978845608028654
_GELU_B = _GELU_A * 0.044715


def _gelu(x):
    z2 = x * (_GELU_A + _GELU_B * (x * x))
    return x / (1.0 + jnp.exp(-z2))


def _iota(shape, dim):
    return lax.broadcasted_iota(jnp.int32, shape, dim)


def _pack_rows(x):
    return pltpu.bitcast(x, jnp.uint32)


def _unpack_rows(x):
    return pltpu.bitcast(x, BF16)


def _cast_kernel(x_ref, o_ref):
    o_ref[0] = _pack_rows(x_ref[0].astype(BF16))


def _cast_t_kernel(packed, x_ref, o_ref):
    xt = x_ref[0].T.astype(BF16)
    o_ref[0] = _pack_rows(xt) if packed else xt


def _to_bf16_packed(x, rows):
    n, r, c = x.shape
    return pl.pallas_call(
        _cast_kernel,
        grid=(n, r // rows),
        in_specs=[pl.BlockSpec((1, rows, c), lambda l, i: (l, i, 0))],
        out_specs=pl.BlockSpec((1, rows // 2, c), lambda l, i: (l, i, 0)),
        out_shape=jax.ShapeDtypeStruct((n, r // 2, c), jnp.uint32),
        compiler_params=_cparams(("arbitrary", "arbitrary")),
        name="cast_bf16",
    )(x)


def _to_bf16_t(x, rows, cols, packed):
    n, r, c = x.shape
    div = 2 if packed else 1
    return pl.pallas_call(
        functools.partial(_cast_t_kernel, packed),
        grid=(n, r // rows, c // cols),
        in_specs=[pl.BlockSpec((1, rows, cols), lambda l, i, j: (l, i, j))],
        out_specs=pl.BlockSpec((1, cols // div, rows), lambda l, i, j: (l, j, i)),
        out_shape=jax.ShapeDtypeStruct((n, c // div, r), jnp.uint32 if packed else BF16),
        compiler_params=_cparams(("arbitrary", "arbitrary", "arbitrary")),
        name="cast_bf16_transposed",
    )(x)


def _mod_kernel(c_ref, w_ref, b_ref, o_ref):
    c = c_ref[...]
    o_ref[0] = jnp.dot(_silu(c), w_ref[0], preferred_element_type=F32,
                       precision=lax.Precision.HIGHEST) + b_ref[0]


def _modulation(c_all, w_ada, b_ada):
    depth, _, n6 = w_ada.shape
    bc = c_all.shape[0]
    tn = 1536
    return pl.pallas_call(
        _mod_kernel,
        grid=(depth, n6 // tn),
        in_specs=[pl.BlockSpec((bc, D_MODEL), lambda l, j: (0, 0)),
                  pl.BlockSpec((1, D_MODEL, tn), lambda l, j: (l, 0, j)),
                  pl.BlockSpec((1, 1, tn), lambda l, j: (l, 0, j))],
        out_specs=pl.BlockSpec((1, bc, tn), lambda l, j: (l, 0, j)),
        out_shape=jax.ShapeDtypeStruct((depth, bc, n6), F32),
        compiler_params=_cparams(("arbitrary", "arbitrary")),
        name="adaln_mod",
    )(c_all, w_ada, b_ada.reshape(depth, 1, n6))


def _in_kernel(with_peer, *refs):
    if with_peer:
        x_ref, p_ref, g2_ref = refs[:3]
        refs = refs[3:]
    else:
        x_ref = refs[0]
        refs = refs[1:]
    (sh_ref, sc_ref, g_ref, w_ref, wst_ref, gq_ref, gk_ref, gm_ref) = refs[:8]
    outs = refs[8:]
    if with_peer:
        xo_ref = outs[0]
        outs = outs[1:]
    (z_ref, xbc_ref, q_ref, k_ref, kb_ref, v_ref, vb_ref, ug_ref, vn_ref, sm_ref, smt_ref) = outs

    x = x_ref[0]
    if with_peer:
        x = x + g2_ref[0] * p_ref[0]
        xo_ref[0] = x
    ms = jnp.mean(x * x, axis=-1, keepdims=True)
    h = x * lax.rsqrt(ms + EPS) * g_ref[...]
    h = h * (1.0 + sc_ref[0]) + sh_ref[0]
    hb = h.astype(BF16)
    proj = _dot(hb, w_ref[...])
    z_ref[0] = proj[:, _Z0:_XBC0]
    xbc_ref[0] = proj[:, _XBC0:_Q0]
    q = proj[:, _Q0:_K0]
    k = proj[:, _K0:_V0]
    v = proj[:, _V0:_U0]
    u = proj[:, _U0:_VM0]
    vm = proj[:, _VM0:_SM0]
    sm_ref[0] = proj[:, _SM0:PROJ_COLS]
    smt_ref[0] = _dot_nt(wst_ref[...], hb)
    gm = gm_ref[...]

    def gmean(y):
        hi, lo = _split2(y)
        return _dot(hi, gm) + _dot(lo, gm)

    qn = q * lax.rsqrt(gmean(q * q) + EPS) * gq_ref[...]
    kn = k * lax.rsqrt(gmean(k * k) + EPS) * gk_ref[...]
    q_ref[0] = qn.astype(BF16)
    k_ref[0] = kn
    kb_ref[0] = kn.astype(BF16)
    v_ref[0] = v
    vb_ref[0] = v.astype(BF16)
    ug_ref[0] = _gelu(u)
    gv = _gelu(vm)
    mu = gmean(gv)
    cen = gv - mu
    var = gmean(cen * cen)
    vn_ref[0] = cen * lax.rsqrt(var + EPS)


def _in_proj(x, peer, g2, sh, sc, g, w_r, ws_t, gq, gk, gm):
    b, t, _ = x.shape
    tm = min(t, 512)
    with_peer = peer is not None
    tok = lambda c: pl.BlockSpec((1, tm, c), lambda i, j: (i, j, 0))
    per_b = pl.BlockSpec((1, 1, D_MODEL), lambda i, j: (i, 0, 0))
    full = lambda a: pl.BlockSpec(a.shape, lambda i, j: (0,) * a.ndim)
    in_specs = [tok(D_MODEL)]
    args = [x]
    if with_peer:
        in_specs += [tok(D_MODEL), per_b]
        args += [peer, g2]
    in_specs += [per_b, per_b, full(g), full(w_r), full(ws_t), full(gq), full(gk), full(gm)]
    args += [sh, sc, g, w_r, ws_t, gq, gk, gm]
    out_cols = [(SSD_INNER, F32), (CONV_CH, F32), (FOX_INNER, BF16), (FOX_INNER, F32), (FOX_INNER, BF16),
                (FOX_INNER, F32), (FOX_INNER, BF16), (MLP_INNER, F32), (MLP_INNER, F32), (SMALL_COLS, F32)]
    out_specs = [tok(c) for c, _ in out_cols]
    out_shape = [jax.ShapeDtypeStruct((b, t, c), dt) for c, dt in out_cols]
    out_specs.append(pl.BlockSpec((1, SMALL_ROWS, tm), lambda i, j: (i, 0, j)))
    out_shape.append(jax.ShapeDtypeStruct((b, SMALL_ROWS, t), F32))
    if with_peer:
        out_specs = [tok(D_MODEL)] + out_specs
        out_shape = [jax.ShapeDtypeStruct((b, t, D_MODEL), F32)] + out_shape
    res = pl.pallas_call(
        functools.partial(_in_kernel, with_peer),
        grid=(b, t // tm),
        in_specs=in_specs, out_specs=out_specs, out_shape=out_shape,
        compiler_params=_cparams(("arbitrary", "arbitrary")),
        name="in_proj",
    )(*args)
    if with_peer:
        return res[0], res[1:]
    return x, res


def _ssd_kernel(L, xbc_ref, z_ref, sm_ref, smt_ref, hist_ref, s0_ref, cw_ref, cb_ref, dtbc_ref, dtbr_ref,
                ac_ref, ar_ref, dsk_ref, ng_ref, e_ref, y_ref, sfin_ref, cfin_ref, xpad, st, ybuf):
    c = pl.program_id(1)
    nc = pl.num_programs(1)

    @pl.when(c == 0)
    def _():
        xpad[0:8, :] = hist_ref[0]
        st[...] = s0_ref[0]

    xpad[8:8 + L, :] = xbc_ref[0]
    conv = cb_ref[...]
    for tap in range(CONV_W):
        conv = conv + xpad[5 + tap:5 + tap + L, :] * cw_ref[tap:tap + 1, :]
    tail = xpad[L:L + 8, :]
    xpad[0:8, :] = tail
    xc = _silu(conv)
    xs = xc[:, 0:SSD_INNER]

    r_i = _iota((L, L), 0)
    c_i = _iota((L, L), 1)
    causal = r_i >= c_i
    tri = jnp.where(causal, 1.0, 0.0).astype(BF16)
    triu = jnp.where(r_i <= c_i, 1.0, 0.0).astype(BF16)

    dtc = _softplus(sm_ref[0] + dtbc_ref[...])
    acum_c = _dot3_r(tri, dtc * ac_ref[...])
    e = e_ref[...]
    acum_x = _dot3_l(acum_c, e)
    dt_x = _dot3_l(dtc, e)
    dtr = _softplus(smt_ref[0][DT_ROW0:DT_ROW0 + 8, :] + dtbr_ref[...])
    acum_r = _dot3_l(dtr * ar_ref[...], triu)

    bmat = [xc[:, SSD_INNER + SSD_STATE * g:SSD_INNER + SSD_STATE * (g + 1)].astype(BF16) for g in range(SSD_GROUPS)]
    c0 = SSD_INNER + SSD_GROUPS * SSD_STATE
    cmat = [xc[:, c0 + SSD_STATE * g:c0 + SSD_STATE * (g + 1)].astype(BF16) for g in range(SSD_GROUPS)]
    cb = [_dot_nt(cmat[g], bmat[g]) for g in range(SSD_GROUPS)]
    gw = SSD_INNER // SSD_GROUPS
    stb = st[...].astype(BF16)
    y_off = jnp.concatenate([_dot(cmat[g], stb[:, gw * g:gw * (g + 1)]) for g in range(SSD_GROUPS)], axis=1)
    xsb = xs.astype(BF16)
    hpg = SSD_HEADS // SSD_GROUPS
    mixes = []
    for h in range(SSD_HEADS):
        seg = acum_c[:, h:h + 1] - acum_r[h:h + 1, :]
        dec = jnp.where(causal, jnp.exp(jnp.minimum(seg, 0.0)), 0.0)
        mixes.append((cb[h // hpg] * dec * dtr[h:h + 1, :]).astype(BF16))
    heads = [_dot(mixes[h], xsb[:, SSD_HEAD_DIM * h:SSD_HEAD_DIM * (h + 1)]) for h in range(SSD_HEADS)]
    for h in range(SSD_HEADS):
        ybuf[:, SSD_HEAD_DIM * h:SSD_HEAD_DIM * (h + 1)] = heads[h]

    y = ybuf[...] + y_off * jnp.exp(acum_x) + dsk_ref[...] * xs
    a_end = acum_x[L - 1:L, :]
    xw = (xs * dt_x * jnp.exp(a_end - acum_x)).astype(BF16)
    new_states = jnp.concatenate([_dot_tn(bmat[g], xw[:, gw * g:gw * (g + 1)]) for g in range(SSD_GROUPS)], axis=1)
    st[...] = st[...] * jnp.exp(a_end) + new_states

    yg = y * _silu(z_ref[0])
    ms = jnp.mean(yg * yg, axis=-1, keepdims=True)
    y_ref[0] = yg * lax.rsqrt(ms + EPS) * ng_ref[...]

    @pl.when(c == nc - 1)
    def _():
        sfin_ref[0] = st[...]
        cfin_ref[0] = tail


def _ssd(xbc, z, sm, smt, hist8, s0t, cw, cb, dtb_c, dtb_r, a_c, a_r, dsk_x, ng, e_mat, L):
    b, t, _ = xbc.shape
    tok = lambda c: pl.BlockSpec((1, L, c), lambda i, j: (i, j, 0))
    full = lambda a: pl.BlockSpec(a.shape, lambda i, j: (0,) * a.ndim)
    per_b = lambda a: pl.BlockSpec((1,) + a.shape[1:], lambda i, j: (i,) + (0,) * (a.ndim - 1))
    return pl.pallas_call(
        functools.partial(_ssd_kernel, L),
        grid=(b, t // L),
        in_specs=[tok(CONV_CH), tok(SSD_INNER), tok(SMALL_COLS),
                  pl.BlockSpec((1, SMALL_ROWS, L), lambda i, j: (i, 0, j)),
                  per_b(hist8), per_b(s0t), full(cw), full(cb), full(dtb_c), full(dtb_r), full(a_c), full(a_r),
                  full(dsk_x), full(ng), full(e_mat)],
        out_specs=[tok(SSD_INNER),
                   pl.BlockSpec((1, SSD_STATE, SSD_INNER), lambda i, j: (i, 0, 0)),
                   pl.BlockSpec((1, 8, CONV_CH), lambda i, j: (i, 0, 0))],
        out_shape=[jax.ShapeDtypeStruct((b, t, SSD_INNER), F32),
                   jax.ShapeDtypeStruct((b, SSD_STATE, SSD_INNER), F32),
                   jax.ShapeDtypeStruct((b, 8, CONV_CH), F32)],
        scratch_shapes=[pltpu.VMEM((L + 8, CONV_CH), F32), pltpu.VMEM((SSD_STATE, SSD_INNER), F32),
                        pltpu.VMEM((L, SSD_INNER), F32)],
        compiler_params=_cparams(("arbitrary", "arbitrary")),
        name="ssd_scan",
    )(xbc, z, sm, smt, hist8, s0t, cw, cb, dtb_c, dtb_r, a_c, a_r, dsk_x, ng, e_mat)


def _fcum_kernel(tf, activate, colsrc_ref, rowsrc_ref, fbc_ref, fbr_ref, initc_ref, initr_ref,
                 lfc_ref, fc_ref, fr_ref, endc_ref, endr_ref, carc, carr):
    j = pl.program_id(1)

    @pl.when(j == 0)
    def _():
        carc[...] = initc_ref[0]
        carr[...] = initr_ref[0]

    ts = min(tf, FCUM_SUB)
    r_i = _iota((ts, ts), 0)
    c_i = _iota((ts, ts), 1)
    tri = jnp.where(r_i >= c_i, 1.0, 0.0).astype(BF16)
    triu = jnp.where(r_i <= c_i, 1.0, 0.0).astype(BF16)
    local = []
    for k in range(tf // ts):
        xc = colsrc_ref[0, k * ts:(k + 1) * ts, :]
        xr = rowsrc_ref[0, :, k * ts:(k + 1) * ts]
        if activate:
            xc = -_softplus(-(xc + fbc_ref[...]))
            xr = -_softplus(-(xr + fbr_ref[...]))
        lfc_ref[0, k * ts:(k + 1) * ts, :] = xc
        local.append((_dot3_r(tri, xc), _dot3_l(xr, triu)))
    car_c = carc[0:1, :]
    car_r = carr[:, 0:1]
    for k, (pc, pr) in enumerate(local):
        fcol = car_c + pc
        frow = car_r + pr
        fc_ref[0, k * ts:(k + 1) * ts, :] = fcol
        fr_ref[0, :, k * ts:(k + 1) * ts] = frow
        car_c = fcol[ts - 1:ts, :]
        car_r = frow[:, ts - 1:ts]
    carc[...] = jnp.broadcast_to(car_c, carc.shape)
    carr[...] = jnp.broadcast_to(car_r, carr.shape)
    endc_ref[0] = carc[...]
    endr_ref[0] = carr[...]


def _fcum(colsrc, rowsrc, row_block, fb_c, fb_r, init_c, init_r, activate):
    b, t, _ = colsrc.shape
    tf = min(t, FCUM_TILE)
    full = lambda a: pl.BlockSpec(a.shape, lambda i, j: (0,) * a.ndim)
    per_b = lambda a: pl.BlockSpec((1,) + a.shape[1:], lambda i, j: (i,) + (0,) * (a.ndim - 1))
    return pl.pallas_call(
        functools.partial(_fcum_kernel, tf, activate),
        grid=(b, t // tf),
        in_specs=[pl.BlockSpec((1, tf, LANES), lambda i, j: (i, j, 0)),
                  pl.BlockSpec((1, 8, tf), lambda i, j: (i, row_block, j)),
                  full(fb_c), full(fb_r), per_b(init_c), per_b(init_r)],
        out_specs=[pl.BlockSpec((1, tf, LANES), lambda i, j: (i, j, 0)),
                   pl.BlockSpec((1, tf, LANES), lambda i, j: (i, j, 0)),
                   pl.BlockSpec((1, 8, tf), lambda i, j: (i, 0, j)),
                   pl.BlockSpec((1, 8, LANES), lambda i, j: (i, 0, 0)),
                   pl.BlockSpec((1, 8, LANES), lambda i, j: (i, 0, 0))],
        out_shape=[jax.ShapeDtypeStruct((b, t, LANES), F32), jax.ShapeDtypeStruct((b, t, LANES), F32),
                   jax.ShapeDtypeStruct((b, 8, t), F32), jax.ShapeDtypeStruct((b, 8, LANES), F32),
                   jax.ShapeDtypeStruct((b, 8, LANES), F32)],
        scratch_shapes=[pltpu.VMEM((8, LANES), F32), pltpu.VMEM((8, LANES), F32)],
        compiler_params=_cparams(("arbitrary", "arbitrary")),
        name="forget_cumsum",
    )(colsrc, rowsrc, fb_c, fb_r, init_c, init_r)


def _softmax_stats(s, carry):
    m, l, acc = carry
    tq, tk = s.shape
    m_new = jnp.maximum(m, jnp.max(s, axis=-1, keepdims=True))
    p = jnp.exp(s - m_new)
    alpha = jnp.exp(m - m_new)
    if tk % LANES == 0:
        psum = p[:, 0:LANES]
        for c in range(1, tk // LANES):
            psum = psum + p[:, c * LANES:(c + 1) * LANES]
    else:
        psum = jnp.where(_iota((tq, LANES), 1) == 0, jnp.sum(p, axis=-1, keepdims=True), 0.0)
    return m_new, alpha * l + psum, alpha * acc, p.astype(BF16)


def _softmax_init(tq):
    return tuple((jnp.full((tq, 1), NEG, F32), jnp.zeros((tq, LANES), F32), jnp.zeros((tq, FOX_HEAD_DIM), F32))
                 for _ in range(FOX_HEADS))


def _head_slice(h):
    return slice(FOX_HEAD_DIM * h, FOX_HEAD_DIM * (h + 1))


def _attend_block(qs, fqs, kb, vb, fk, mask, carries):
    heads = range(FOX_HEADS)
    scores = [_dot_nt(qs[h], kb[:, _head_slice(h)]) for h in heads]
    stats = []
    for h in heads:
        s = scores[h] + fqs[h] - fk[h:h + 1, :]
        if mask is not None:
            s = jnp.where(mask, s, NEG)
        stats.append(_softmax_stats(s, carries[h]))
    return tuple((m, l, acc + _dot(p, vb[:, _head_slice(h)])) for h, (m, l, acc, p) in zip(heads, stats))


def _attend_finish(carries, o_ref):
    for h in range(FOX_HEADS):
        _, l, acc = carries[h]
        o_ref[0, :, _head_slice(h)] = acc / jnp.sum(l, axis=-1, keepdims=True)


def _bf16_terms(x):
    hi, mid, lo = _split3(x)
    return hi.astype(F32), mid.astype(F32), lo.astype(F32)


def _fox_prep_kernel(q_ref, k_ref, v_ref, fc_ref, fr_ref, ka_ref, qa_ref, vt_ref):
    tm = q_ref.shape[1]
    qt = q_ref[0].astype(F32).T
    k = k_ref[0].astype(F32)
    vt_ref[0, 0] = v_ref[0].astype(F32).T.astype(BF16)
    lane = _iota((tm, LANES), 1)
    row = _iota((FOX_HEAD_DIM, tm), 0)
    d = FOX_HEAD_DIM
    for h in range(FOX_HEADS):
        pair, odd = divmod(h, 2)
        kc = k[:, pair * LANES:(pair + 1) * LANES]
        if odd:
            kc = pltpu.roll(kc, d, axis=1)
        ka = jnp.where(lane < d, kc, 0.0)
        for i, term in enumerate(_bf16_terms(-fc_ref[0][:, F_ROW0 + h:F_ROW0 + h + 1])):
            ka = jnp.where(lane == d + i, term, ka)
        ka = jnp.where(jnp.logical_and(lane >= d + 3, lane < d + 6), 1.0, ka)
        ka_ref[0, h] = ka.astype(BF16)
        tail = jnp.where(row < 3, 1.0, 0.0)
        for i, term in enumerate(_bf16_terms(fr_ref[0][h:h + 1, :])):
            tail = jnp.where(row == 3 + i, term, tail)
        qa_ref[0, h, 0:d, :] = qt[d * h:d * (h + 1), :].astype(BF16)
        qa_ref[0, h, d:2 * d, :] = tail.astype(BF16)


def _fox_prep(qb, kb, vb, fcol, frow, tm):
    b, t, _ = qb.shape
    tok = lambda c: pl.BlockSpec((1, tm, c), lambda i, j: (i, j, 0))
    return pl.pallas_call(
        _fox_prep_kernel,
        grid=(b, t // tm),
        in_specs=[tok(FOX_INNER), tok(FOX_INNER), tok(FOX_INNER), tok(LANES),
                  pl.BlockSpec((1, 8, tm), lambda i, j: (i, 0, j))],
        out_specs=[pl.BlockSpec((1, FOX_HEADS, tm, LANES), lambda i, j: (i, 0, j, 0)),
                   pl.BlockSpec((1, FOX_HEADS, LANES, tm), lambda i, j: (i, 0, 0, j)),
                   pl.BlockSpec((1, 1, FOX_INNER, tm), lambda i, j: (i, j, 0, 0))],
        out_shape=[jax.ShapeDtypeStruct((b, FOX_HEADS, t, LANES), BF16),
                   jax.ShapeDtypeStruct((b, FOX_HEADS, LANES, t), BF16),
                   jax.ShapeDtypeStruct((b, t // tm, FOX_INNER, tm), BF16)],
        compiler_params=_cparams(("arbitrary", "arbitrary")),
        name="fox_prep",
    )(qb, kb, vb, fcol, frow)


def _fox_prompt_t_kernel(tq, nk, qa_ref, ka_ref, vt_ref, fend_ref, ffirst_ref, thr_ref, o_ref):
    b = pl.program_id(0)
    qi = pl.program_id(1)
    half = tq // FOX_CHAINS
    d = FOX_HEAD_DIM
    diag = pl.multiple_of(qi * tq, tq)
    init = (jnp.full((1, half), NEG, F32), jnp.zeros((1, half), F32), jnp.zeros((d, half), F32))

    def update(blocks, qas, masks, carries):
        scores = [[_dot(ka, qa) for qa in qas] for ka, _ in blocks]
        for (_, vt), block_scores in zip(blocks, scores):
            stats = []
            for s, mask, (m, l, acc) in zip(block_scores, masks, carries):
                if mask is not None:
                    s = jnp.where(mask, s, NEG)
                m_new = jnp.maximum(m, jnp.max(s, axis=0, keepdims=True))
                p = jnp.exp(s - m_new)
                alpha = jnp.exp(m - m_new)
                stats.append((m_new, alpha * l + jnp.sum(p, axis=0, keepdims=True), alpha * acc, p.astype(BF16)))
            carries = tuple((m, l, acc + _dot(vt, p)) for m, l, acc, p in stats)
        return carries

    def key_block(h, j):
        return (ka_ref[0, h, pl.ds(pl.multiple_of(j * tq, tq), tq), :], vt_ref[0, j][d * h:d * (h + 1), :])

    for h in range(FOX_HEADS):
        base = (b * FOX_HEADS + h) * nk
        slack = thr_ref[0] + ffirst_ref[base + qi]
        n_live = lax.fori_loop(0, qi, lambda j, c: c + (slack - fend_ref[base + j] >= 0.0).astype(jnp.int32), 0)
        qas = [qa_ref[0, h, :, r2 * half:(r2 + 1) * half] for r2 in range(FOX_CHAINS)]
        ka_d = ka_ref[0, h, pl.ds(diag, tq), :]
        vt_d = vt_ref[0, qi][d * h:d * (h + 1), :]
        causal = [_iota((tq, half), 0) <= (r2 * half + _iota((tq, half), 1)) for r2 in range(FOX_CHAINS)]
        carries = update([(ka_d, vt_d)], qas, causal, [init] * FOX_CHAINS)
        no_mask = [None] * FOX_CHAINS

        def one(t, carries, qas=qas, h=h):
            return update([key_block(h, qi - 1 - t)], qas, no_mask, carries)

        def two(t, carries, qas=qas, h=h):
            j = qi - 1 - (n_live % 2) - 2 * t
            return update([key_block(h, j), key_block(h, j - 1)], qas, no_mask, carries)

        carries = lax.fori_loop(0, n_live % 2, one, carries)
        carries = lax.fori_loop(0, n_live // 2, two, carries)
        for r2 in range(FOX_CHAINS):
            _, l, acc = carries[r2]
            o_ref[0, r2 * half:(r2 + 1) * half, d * h:d * (h + 1)] = (acc / l).T


def _fox_prompt_t(qb, kb, vb, fcol, frow, score_bound):
    b, t, _ = qb.shape
    tq = min(t, FOX_PROMPT_TILE)
    nk = t // tq
    ka, qa, vt = _fox_prep(qb, kb, vb, fcol, frow, tq)
    f_heads = frow[:, :FOX_HEADS, :].reshape(b, FOX_HEADS, nk, tq)
    f_end = f_heads[:, :, :, tq - 1].reshape(-1)
    f_first = f_heads[:, :, :, 0].reshape(-1)
    thr = (2.0 * score_bound + EXP_UNDERFLOW).reshape(1).astype(F32)
    smem = pl.BlockSpec(memory_space=pltpu.SMEM)
    return pl.pallas_call(
        functools.partial(_fox_prompt_t_kernel, tq, nk),
        grid=(b, nk),
        in_specs=[pl.BlockSpec((1, FOX_HEADS, LANES, tq), lambda i, j: (i, 0, 0, j)),
                  pl.BlockSpec((1, FOX_HEADS, t, LANES), lambda i, j: (i, 0, 0, 0)),
                  pl.BlockSpec((1, nk, FOX_INNER, tq), lambda i, j: (i, 0, 0, 0)),
                  smem, smem, smem],
        out_specs=pl.BlockSpec((1, tq, FOX_INNER), lambda i, j: (i, j, 0)),
        out_shape=jax.ShapeDtypeStruct((b, t, FOX_INNER), F32),
        compiler_params=_cparams(("arbitrary", "arbitrary")),
        name="fox_prompt",
    )(qa, ka, vt, f_end, f_first, thr)


def _fox_sample_kernel(tq, tk, npast, q_ref, fq_ref, pk_ref, pv_ref, fpk_ref, k_ref, v_ref, fk_ref, o_ref):
    q_all = q_ref[0]
    fq_all = fq_ref[0]
    qs = [q_all[:, _head_slice(h)] for h in range(FOX_HEADS)]
    fqs = [fq_all[:, F_ROW0 + h:F_ROW0 + h + 1] for h in range(FOX_HEADS)]
    causal = _iota((tq, tq), 0) >= _iota((tq, tq), 1)

    def past_block(j, carries):
        start = pl.multiple_of(j * tk, tk)
        return _attend_block(qs, fqs, pk_ref[0, pl.ds(start, tk), :].astype(BF16),
                             pv_ref[0, pl.ds(start, tk), :].astype(BF16), fpk_ref[0, j], None, carries)

    carries = lax.fori_loop(0, npast, past_block, _softmax_init(tq))
    _attend_finish(_attend_block(qs, fqs, k_ref[0], v_ref[0], fk_ref[0], causal, carries), o_ref)


def _fox_sample(qb, fcol, past_k, past_v, layer, fpast_row, kb, vb, frow):
    b, t, _ = qb.shape
    p = past_k.shape[2]
    tk = min(p, 512)
    npast = p // tk
    fpk = fpast_row.reshape(b, 8, npast, tk).transpose(0, 2, 1, 3)
    bspec = lambda a: pl.BlockSpec((1,) + a.shape[1:], lambda i: (i,) + (0,) * (a.ndim - 1))
    cache = pl.BlockSpec((None, 1, p, FOX_INNER), lambda i: (layer, i, 0, 0))
    args = (qb, fcol, past_k, past_v, fpk, kb, vb, frow)
    return pl.pallas_call(
        functools.partial(_fox_sample_kernel, t, tk, npast),
        grid=(b,),
        in_specs=[bspec(qb), bspec(fcol), cache, cache, bspec(fpk), bspec(kb), bspec(vb), bspec(frow)],
        out_specs=pl.BlockSpec((1, t, FOX_INNER), lambda i: (i, 0, 0)),
        out_shape=jax.ShapeDtypeStruct((b, t, FOX_INNER), F32),
        compiler_params=_cparams(("arbitrary",)),
        name="fox_sample",
    )(*args)


def _mlp_kernel(lm, nchunk, ug_ref, vn_ref, ws_ref, bst_ref, y_ref):
    r_i = _iota((lm, lm), 0)
    c_i = _iota((lm, lm), 1)
    tril = r_i >= c_i
    vn = vn_ref[0].astype(BF16)
    ug = ug_ref[0]
    for g in range(MLP_GROUPS):
        lo, hi = MLP_GROUP_DIM * g, MLP_GROUP_DIM * (g + 1)
        w = jnp.where(tril, ws_ref[g], 0.0).astype(BF16)
        bias = bst_ref[:, g:g + 1]
        for c in range(nchunk):
            r0, r1 = c * lm, (c + 1) * lm
            sv = _dot(w, vn[r0:r1, lo:hi]) + bias
            y_ref[0, r0:r1, lo:hi] = ug[r0:r1, lo:hi] * sv


def _chunk_mlp(ug, vn, ws, bst):
    b, t, _ = ug.shape
    lm = ws.shape[1]
    tm = min(t, 4 * lm)
    tok = pl.BlockSpec((1, tm, MLP_INNER), lambda i, j: (i, j, 0))
    return pl.pallas_call(
        functools.partial(_mlp_kernel, lm, tm // lm),
        grid=(b, t // tm),
        in_specs=[tok, tok, pl.BlockSpec(ws.shape, lambda i, j: (0, 0, 0)),
                  pl.BlockSpec(bst.shape, lambda i, j: (0, 0))],
        out_specs=tok,
        out_shape=jax.ShapeDtypeStruct((b, t, MLP_INNER), F32),
        compiler_params=_cparams(("arbitrary", "arbitrary")),
        name="chunk_mlp",
    )(ug, vn, ws, bst)


def _out_kernel(x_ref, ys_ref, yf_ref, ym_ref, g1_ref, sh_ref, sc_ref, g_ref, wo_ref, x1_ref, h2_ref):
    mix = (_dot(ys_ref[0].astype(BF16), wo_ref[0:SSD_INNER, :])
           + _dot(yf_ref[0].astype(BF16), wo_ref[SSD_INNER:SSD_INNER + FOX_INNER, :])
           + _dot(ym_ref[0].astype(BF16), wo_ref[SSD_INNER + FOX_INNER:D_MODEL, :]))
    x1 = x_ref[0] + g1_ref[0] * mix
    x1_ref[0] = x1
    ms = jnp.mean(x1 * x1, axis=-1, keepdims=True)
    h = x1 * lax.rsqrt(ms + EPS) * g_ref[...]
    h2_ref[0] = _pack_rows((h * (1.0 + sc_ref[0]) + sh_ref[0]).astype(BF16))


def _out_proj(x, ys, yf, ym, g1, sh, sc, g, wo):
    b, t, _ = x.shape
    tm = min(t, 512)
    tok = lambda c: pl.BlockSpec((1, tm, c), lambda i, j: (i, j, 0))
    per_b = pl.BlockSpec((1, 1, D_MODEL), lambda i, j: (i, 0, 0))
    full = lambda a: pl.BlockSpec(a.shape, lambda i, j: (0,) * a.ndim)
    return pl.pallas_call(
        _out_kernel,
        grid=(b, t // tm),
        in_specs=[tok(D_MODEL), tok(SSD_INNER), tok(FOX_INNER), tok(MLP_INNER), per_b, per_b, per_b, full(g), full(wo)],
        out_specs=[tok(D_MODEL), pl.BlockSpec((1, tm // 2, D_MODEL), lambda i, j: (i, j, 0))],
        out_shape=[jax.ShapeDtypeStruct((b, t, D_MODEL), F32),
                   jax.ShapeDtypeStruct((b, t // 2, D_MODEL), jnp.uint32)],
        compiler_params=_cparams(("arbitrary", "arbitrary")),
        name="out_proj",
    )(x, ys, yf, ym, g1, sh, sc, g, wo)


def _top16(scores):
    nk, tn = scores[0].shape
    ridx = _iota((nk, tn), 0).astype(F32)
    r16 = _iota((PEER_TOPK, tn), 0)
    state = [(s, jnp.full((nk, tn), float(PEER_TOPK), F32), jnp.zeros((PEER_TOPK, tn), F32)) for s in scores]
    for it in range(PEER_TOPK):
        nxt_state = []
        for s, pos, tv in state:
            level = [(s[g:g + 8, :], ridx[g:g + 8, :]) for g in range(0, nk, 8)]
            while len(level) > 1:
                nxt = []
                for a in range(0, len(level), 2):
                    (va, ia), (vb, ib) = level[a], level[a + 1]
                    nxt.append((jnp.maximum(va, vb), jnp.where(va >= vb, ia, ib)))
                level = nxt
            v8, i8 = level[0]
            m = jnp.max(v8, axis=0, keepdims=True)
            first = jnp.min(jnp.where(v8 == m, i8, float(nk)), axis=0, keepdims=True)
            sel = ridx == first
            nxt_state.append((jnp.where(sel, -jnp.inf, s), jnp.where(sel, float(it), pos),
                              jnp.where(r16 == it, m, tv)))
        state = nxt_state
    return [(pos, tv) for _, pos, tv in state]


_CAND_GROUPS = [(0, 16)] + [(ka, 8) for ka in range(1, 8)]
_CAND_ROWS = 16 + 7 * 8 + 8


def _pair_select(ta, tb):
    tn = ta.shape[1]
    pieces, flats, valids = [], [], []
    for ka, rows in _CAND_GROUPS:
        pieces.append(ta[ka:ka + 1, :] + tb[0:rows, :])
        kb = _iota((rows, 1), 0)
        flats.append((ka * PEER_TOPK + kb).astype(F32))
        valids.append((ka + 1) * (kb + 1) <= PEER_TOPK)
    pieces.append(ta[8:16, :] + tb[0:1, :])
    flats.append(((8 + _iota((8, 1), 0)) * PEER_TOPK).astype(F32))
    valids.append(_iota((8, 1), 0) >= 0)
    cand0 = jnp.concatenate(pieces, axis=0)
    flat = jnp.concatenate(flats, axis=0)
    valid = jnp.concatenate(valids, axis=0)
    cand0 = jnp.where(valid, cand0, -jnp.inf)
    best = ta[0:1, :] + tb[0:1, :]

    cand = cand0
    selm = jnp.zeros((_CAND_ROWS, tn), F32)
    for _ in range(PEER_TOPK):
        m = jnp.max(cand, axis=0, keepdims=True)
        first = jnp.min(jnp.where(cand == m, flat, 4096.0), axis=0, keepdims=True)
        sel = flat == first
        cand = jnp.where(sel, -jnp.inf, cand)
        selm = jnp.where(sel, 1.0, selm)
    z = jnp.sum(jnp.where(selm > 0.0, jnp.exp(cand0 - best), 0.0), axis=0, keepdims=True)
    cnts = [jnp.sum(selm[0:16, :], axis=0, keepdims=True)]
    for i in range(1, 8):
        cnts.append(jnp.sum(selm[8 + 8 * i:16 + 8 * i, :], axis=0, keepdims=True))
    cnts.append(selm[_CAND_ROWS - 8:_CAND_ROWS, :])
    return jnp.concatenate(cnts, axis=0), z


def _peer_sel_kernel(tn, h_ref, wqt_ref, keys_ref, ea_ref, la_ref, eb_ref, pb_ref, qt_ref):
    qt_ref[...] = _dot_nt(wqt_ref[...], _unpack_rows(h_ref[...])).astype(BF16)

    def head(h, carry):
        row = pl.multiple_of(h * (2 * PEER_HALF), 2 * PEER_HALF)
        sa_all = _dot(keys_ref[2 * h], qt_ref[pl.ds(row, PEER_HALF), :])
        sb_all = _dot(keys_ref[2 * h + 1], qt_ref[pl.ds(row + PEER_HALF, PEER_HALF), :])
        chunks = [slice(c * LANES, (c + 1) * LANES) for c in range(tn // LANES)]
        tops = _top16([s_all[:, cs] for cs in chunks for s_all in (sa_all, sb_all)])
        for c, cs in enumerate(chunks):
            sa, sb = sa_all[:, cs], sb_all[:, cs]
            (pos_a, ta), (pos_b, tb) = tops[2 * c], tops[2 * c + 1]
            cnt, z = _pair_select(ta, tb)
            pos_h = pos_a.astype(BF16)
            cnt_h = cnt.astype(BF16)
            la = jnp.zeros_like(pos_h)
            for ka in range(PEER_TOPK):
                la = jnp.where(pos_h == float(ka), jnp.broadcast_to(cnt_h[ka:ka + 1, :], pos_h.shape), la)
            ea_ref[h, :, cs] = jnp.where(pos_a < float(PEER_TOPK), jnp.exp(sa - ta[0:1, :]), 0.0) / z
            la_ref[h, :, cs] = la.astype(F32)
            eb = jnp.where(pos_b < float(PEER_TOPK), jnp.exp(sb - tb[0:1, :]), 0.0)
            for sub in range(PEER_KEYS // PEER_SUB):
                src = slice(sub * PEER_SUB, (sub + 1) * PEER_SUB)
                dst = slice(sub * PEER_SUB // 2, (sub + 1) * PEER_SUB // 2)
                eb_ref[h, dst, cs] = pltpu.bitcast(eb[src].astype(BF16), jnp.uint32)
                pb_ref[h, dst, cs] = pltpu.bitcast(pos_b[src].astype(BF16), jnp.uint32)
        return carry

    lax.fori_loop(0, PEER_HEADS, head, 0)


def _peer_select(h2, wqt_all, layer, keys):
    n = 2 * h2.shape[0]
    tn = 256
    per_tok = pl.BlockSpec((PEER_HEADS, PEER_KEYS, tn), lambda i: (0, 0, i))
    per_tok_pk = pl.BlockSpec((PEER_HEADS, PEER_KEYS // 2, tn), lambda i: (0, 0, i))
    shp = lambda dt: jax.ShapeDtypeStruct((PEER_HEADS, PEER_KEYS, n), dt)
    shp_pk = jax.ShapeDtypeStruct((PEER_HEADS, PEER_KEYS // 2, n), jnp.uint32)
    return pl.pallas_call(
        functools.partial(_peer_sel_kernel, tn),
        grid=(n // tn,),
        in_specs=[pl.BlockSpec((tn // 2, D_MODEL), lambda i: (i, 0)),
                  pl.BlockSpec((None,) + wqt_all.shape[1:], lambda i: (layer, 0, 0)),
                  pl.BlockSpec(keys.shape, lambda i: (0, 0, 0))],
        out_specs=[per_tok, per_tok, per_tok_pk, per_tok_pk],
        out_shape=[shp(F32), shp(F32), shp_pk, shp_pk],
        scratch_shapes=[pltpu.VMEM((PEER_HEADS * 2 * PEER_HALF, tn), BF16)],
        compiler_params=_cparams(("arbitrary",)),
        name="peer_select",
    )(h2, wqt_all, keys)


def _peer_gate_stage(ec, ia0, rows, chunks, at_r, ga_w, ea_ref, la_ref, eb_ref, pb_ref):
    rows_per = ec // PEER_KEYS
    for r in rows:
        for c in chunks:
            cs = slice(c * LANES, (c + 1) * LANES)
            for sub in range(PEER_KEYS // PEER_SUB):
                pk = slice(sub * PEER_SUB // 2, (sub + 1) * PEER_SUB // 2)
                gate = None
                for h in range(PEER_HEADS):
                    la = la_ref[h, pl.ds(ia0, rows_per), cs][r:r + 1, :]
                    ea = ea_ref[h, pl.ds(ia0, rows_per), cs][r:r + 1, :]
                    la = jnp.broadcast_to(la, (PEER_SUB, LANES)).astype(BF16)
                    ea = jnp.broadcast_to(ea, (PEER_SUB, LANES)).astype(BF16)
                    pb = pltpu.bitcast(pb_ref[h, pk, cs], BF16)
                    eb = pltpu.bitcast(eb_ref[h, pk, cs], BF16)
                    term = jnp.where(pb < la, eb, 0.0) * ea
                    gate = term if gate is None else gate + term
                ex = slice(r * PEER_KEYS + sub * PEER_SUB, r * PEER_KEYS + (sub + 1) * PEER_SUB)
                ga_w[ex, cs] = gate * _gelu(at_r[ex, cs]).astype(BF16)


def _peer_dense_kernel(tn, ec, h_ref, u_ref, vt_ref, ea_ref, la_ref, eb_ref, pb_ref, o_ref, at, ga, acc):
    e = pl.program_id(1)
    ne = pl.num_programs(1)

    @pl.when(e == 0)
    def _():
        acc[...] = jnp.zeros_like(acc)

    rows_per = ec // PEER_KEYS
    ia0 = pl.multiple_of(e * rows_per, rows_per)
    parts = PEER_DENSE_PARTS
    per = ec // parts
    h_t = _unpack_rows(h_ref[...])
    for g in range(parts):
        rows = slice(g * per, (g + 1) * per)
        pk = slice(g * per // 2, (g + 1) * per // 2)
        at[rows, :] = _dot_nt(_unpack_rows(u_ref[pk, :]), h_t)
    for g in range(parts):
        rows = slice(g * per, (g + 1) * per)
        _peer_gate_stage(ec, ia0, range(g * rows_per // parts, (g + 1) * rows_per // parts), range(tn // LANES),
                         at, ga, ea_ref, la_ref, eb_ref, pb_ref)
        acc[...] += _dot(_unpack_rows(vt_ref[:, rows]), ga[rows, :])

    @pl.when(e == ne - 1)
    def _():
        o_ref[...] = acc[...].T


def _peer_dense(h2, u_all, vt_all, layer, ea, la, eb, pb):
    n = 2 * h2.shape[0]
    tn, ec = PEER_TOKEN_TILE, PEER_EXPERT_CHUNK
    per_tok = pl.BlockSpec((PEER_HEADS, PEER_KEYS, tn), lambda i, e: (0, 0, i))
    per_tok_pk = pl.BlockSpec((PEER_HEADS, PEER_KEYS // 2, tn), lambda i, e: (0, 0, i))
    return pl.pallas_call(
        functools.partial(_peer_dense_kernel, tn, ec),
        grid=(n // tn, PEER_EXPERTS // ec),
        in_specs=[pl.BlockSpec((tn // 2, D_MODEL), lambda i, e: (i, 0)),
                  pl.BlockSpec((None, ec // 2, D_MODEL), lambda i, e: (layer, e, 0)),
                  pl.BlockSpec((None, D_MODEL // 2, ec), lambda i, e: (layer, 0, e)),
                  per_tok, per_tok, per_tok_pk, per_tok_pk],
        out_specs=pl.BlockSpec((tn, D_MODEL), lambda i, e: (i, 0)),
        out_shape=jax.ShapeDtypeStruct((n, D_MODEL), F32),
        scratch_shapes=[pltpu.VMEM((ec, tn), F32), pltpu.VMEM((ec, tn), BF16), pltpu.VMEM((D_MODEL, tn), F32)],
        compiler_params=_cparams(("arbitrary", "arbitrary")),
        name="peer_dense",
    )(h2, u_all, vt_all, ea, la, eb, pb)


def _resid_kernel(x_ref, p_ref, g_ref, o_ref):
    o_ref[0] = x_ref[0] + g_ref[0] * p_ref[0]


def _residual(x, peer, g2):
    b, t, _ = x.shape
    tm = min(t, 512)
    tok = pl.BlockSpec((1, tm, D_MODEL), lambda i, j: (i, j, 0))
    return pl.pallas_call(
        _resid_kernel,
        grid=(b, t // tm),
        in_specs=[tok, tok, pl.BlockSpec((1, 1, D_MODEL), lambda i, j: (i, 0, 0))],
        out_specs=tok,
        out_shape=jax.ShapeDtypeStruct((b, t, D_MODEL), F32),
        compiler_params=_cparams(("arbitrary", "arbitrary")),
        name="peer_residual",
    )(x, peer, g2)


def _lane_pad(vec, offset, width=LANES):
    out = jnp.zeros((width,), F32)
    return out.at[offset:offset + vec.shape[0]].set(vec.astype(F32))


def _layer_params(l, norm1_g, norm2_g, w_in, conv_w, conv_b, dt_bias, a_log, d_skip, ssd_norm_g, q_norm_g,
                  k_norm_g, fgate_b, w_s, b_s, w_out, peer_keys):
    w = w_in[l]
    o = [0]
    for sz in (SSD_INNER, CONV_CH, SSD_HEADS, FOX_INNER, FOX_INNER, FOX_INNER, FOX_HEADS, MLP_INNER, MLP_INNER):
        o.append(o[-1] + sz)
    wz, wxbc, wdt, wq, wk, wv, wf, wu, wvm = [w[:, o[i]:o[i + 1]] for i in range(9)]
    w_small = jnp.zeros((D_MODEL, SMALL_COLS), F32)
    w_small = w_small.at[:, DT_ROW0:DT_ROW0 + SSD_HEADS].set(wdt).at[:, F_ROW0:F_ROW0 + FOX_HEADS].set(wf)
    w_r = jnp.concatenate([wz, wxbc, wq, wk, wv, wu, wvm, w_small], axis=1).astype(BF16)
    ws_t = w_small[:, :SMALL_ROWS].T.astype(BF16)
    a_neg = -jnp.exp(a_log[l].astype(F32))
    grp = jnp.arange(FOX_INNER) // FOX_HEAD_DIM
    gm = jnp.where(grp[:, None] == grp[None, :], 1.0 / FOX_HEAD_DIM, 0.0).astype(BF16)
    heads = jnp.arange(SSD_INNER) // SSD_HEAD_DIM
    e_mat = (jnp.arange(LANES)[:, None] == heads[None, :]).astype(BF16)
    return dict(
        norm1_g=norm1_g[l][None], norm2_g=norm2_g[l][None], w_r=w_r, ws_t=ws_t,
        gq=(jnp.tile(q_norm_g[l], FOX_HEADS) * (FOX_HEAD_DIM ** -0.5))[None],
        gk=jnp.tile(k_norm_g[l], FOX_HEADS)[None], gm=gm,
        fox_bound=1.02 * FOX_HEAD_DIM ** 0.5 * jnp.max(jnp.abs(q_norm_g[l])) * jnp.max(jnp.abs(k_norm_g[l])),
        conv_w=conv_w[l], conv_b=conv_b[l][None],
        dtb_c=_lane_pad(dt_bias[l], DT_ROW0)[None], dtb_r=dt_bias[l].astype(F32)[:, None],
        a_c=_lane_pad(a_neg, DT_ROW0)[None], a_r=a_neg[:, None],
        dsk_x=jnp.repeat(d_skip[l].astype(F32), SSD_HEAD_DIM)[None], ssd_g=ssd_norm_g[l][None], e_mat=e_mat,
        fb_c=_lane_pad(fgate_b[l], F_ROW0)[None], fb_r=_lane_pad(fgate_b[l], 0, 8)[:, None],
        w_s=w_s[l], bs_t=jnp.zeros((MLP_CHUNK, LANES), F32).at[:, :MLP_GROUPS].set(b_s[l].T),
        w_out=w_out[l].astype(BF16),
        keys=peer_keys[l].reshape(PEER_HEADS * 2, PEER_KEYS, PEER_HALF).astype(BF16),
    )


def _stream_mixers(x, peer, g2_prev, mod, p, hist8, s0t, past):
    sh1, sc1, g1, sh2, sc2, _ = mod
    b, t, _ = x.shape
    x, (z, xbc, qb, kn, kb, v, vb, ug, vn, sm, smt) = _in_proj(
        x, peer, g2_prev, sh1, sc1, p['norm1_g'], p['w_r'], p['ws_t'], p['gq'], p['gk'], p['gm'])
    L = SSD_CHUNK_PROMPT if t % SSD_CHUNK_PROMPT == 0 else t
    y_ssd, s_fin, c_fin = _ssd(xbc, z, sm, smt, hist8, s0t, p['conv_w'], p['conv_b'], p['dtb_c'], p['dtb_r'],
                               p['a_c'], p['a_r'], p['dsk_x'], p['ssd_g'], p['e_mat'], L)
    zeros8 = jnp.zeros((b, 8, LANES), F32)
    if past is None:
        logf_c, f_col, f_row, _, _ = _fcum(sm, smt, F_ROW0 // 8, p['fb_c'], p['fb_r'], zeros8, zeros8, True)
        y_fox = _fox_prompt_t(qb, kb, vb, f_col, f_row, p['fox_bound'])
    else:
        pk, pv, layer, plf_col, plf_row = past
        _, _, fp_row, end_c, end_r = _fcum(plf_col, plf_row, 0, p['fb_c'], p['fb_r'], zeros8, zeros8, False)
        logf_c, f_col, f_row, _, _ = _fcum(sm, smt, F_ROW0 // 8, p['fb_c'], p['fb_r'], end_c, end_r, True)
        y_fox = _fox_sample(qb, f_col, pk, pv, layer, fp_row, kb, vb, f_row)
    lm = MLP_CHUNK if t % MLP_CHUNK == 0 else t
    y_mlp = _chunk_mlp(ug, vn, p['w_s'][:, :lm, :lm], p['bs_t'][:lm])
    x1, h2 = _out_proj(x, y_ssd, y_fox, y_mlp, g1, sh2, sc2, p['norm2_g'], p['w_out'])
    logf = logf_c[:, :, F_ROW0:F_ROW0 + FOX_HEADS]
    new_ssm = s_fin.reshape(b, SSD_STATE, SSD_HEADS, SSD_HEAD_DIM).transpose(0, 2, 3, 1)
    new_conv = c_fin[:, 8 - (CONV_W - 1):, :]
    kc = kn.reshape(b, t, FOX_HEADS, FOX_HEAD_DIM)
    vc = v.reshape(b, t, FOX_HEADS, FOX_HEAD_DIM)
    return x1, h2, (kc, vc, logf, new_ssm, new_conv, vn)


def _peer(h2, p, tables, layer):
    b, t2, _ = h2.shape
    t = 2 * t2
    n_tok = b * t
    n_pad = -n_tok % PEER_TOKEN_TILE
    flat = h2.reshape(n_tok // 2, D_MODEL)
    if n_pad:
        flat = jnp.concatenate([flat, jnp.zeros((n_pad // 2, D_MODEL), jnp.uint32)], axis=0)
    wqt_all, u_all, vt_all = tables
    ea, la, eb, pb = _peer_select(flat, wqt_all, layer, p['keys'])
    out = _peer_dense(flat, u_all, vt_all, layer, ea, la, eb, pb)
    return out[:n_tok].reshape(b, t, D_MODEL)


def kernel(x_prompt, x_sample, c_prompt, c_sample, cache_fox_k, cache_fox_v, cache_fox_logf, state_ssm, state_conv, norm1_g, norm2_g, w_ada, b_ada, w_in, conv_w, conv_b, dt_bias, a_log, d_skip, ssd_norm_g, q_norm_g, k_norm_g, fgate_b, w_s, b_s, w_out, peer_wq, peer_keys, peer_u, peer_v):
    depth = w_ada.shape[0]
    bp, tp, _ = x_prompt.shape
    bs, ts, _ = x_sample.shape
    past_len = cache_fox_k.shape[2]
    mod_all = _modulation(jnp.concatenate([c_prompt, c_sample], axis=0).astype(F32), w_ada, b_ada)

    peer_tables = (_to_bf16_t(peer_wq, D_MODEL, 512, False), _to_bf16_packed(peer_u, 1024),
                   _to_bf16_t(peer_v, 512, D_MODEL, True))

    past_k_all = cache_fox_k.reshape(depth, bs, past_len, FOX_INNER)
    past_v_all = cache_fox_v.reshape(depth, bs, past_len, FOX_INNER)

    xp, xs = x_prompt, x_sample
    peer_p = peer_s = g2p = g2s = None
    outs = [[] for _ in range(11)]
    for l in range(depth):
        p = _layer_params(l, norm1_g, norm2_g, w_in, conv_w, conv_b, dt_bias, a_log, d_skip, ssd_norm_g, q_norm_g,
                          k_norm_g, fgate_b, w_s, b_s, w_out, peer_keys)
        mods = [m[:, None, :] for m in jnp.split(mod_all[l], 6, axis=-1)]
        mod_p = [m[:bp] for m in mods]
        mod_s = [m[bp:] for m in mods]
        hist_p = jnp.zeros((bp, 8, CONV_CH), F32)
        s0_p = jnp.zeros((bp, SSD_STATE, SSD_INNER), F32)
        hist_s = jnp.concatenate([jnp.zeros((bs, 8 - (CONV_W - 1), CONV_CH), F32), state_conv[l].astype(F32)], axis=1)
        s0_s = state_ssm[l].astype(F32).transpose(0, 3, 1, 2).reshape(bs, SSD_STATE, SSD_INNER)
        plf = cache_fox_logf[l].astype(F32)
        plf_col = jnp.pad(plf, ((0, 0), (0, 0), (F_ROW0, LANES - F_ROW0 - FOX_HEADS)))
        plf_row = jnp.pad(plf.transpose(0, 2, 1), ((0, 0), (0, 8 - FOX_HEADS), (0, 0)))
        past = (past_k_all, past_v_all, l, plf_col, plf_row)

        x1p, h2p, st_p = _stream_mixers(xp, peer_p, g2p, mod_p, p, hist_p, s0_p, None)
        x1s, h2s, st_s = _stream_mixers(xs, peer_s, g2s, mod_s, p, hist_s, s0_s, past)

        peer_p = _peer(h2p, p, peer_tables, l)
        peer_s = _peer(h2s, p, peer_tables, l)
        xp, xs, g2p, g2s = x1p, x1s, mod_p[5], mod_s[5]
        for i in range(5):
            outs[i].append(st_p[i])
        for i in range(6):
            outs[5 + i].append(st_s[i])

    yp = _residual(xp, peer_p, g2p)
    ys = _residual(xs, peer_s, g2s)
    return (yp, ys) + tuple(jnp.stack(o) for o in outs)
```

```python
import functools

import jax
import jax.numpy as jnp
from jax import lax
from jax.experimental import pallas as pl
from jax.experimental.pallas import tpu as pltpu

F32 = jnp.float32
BF16 = jnp.bfloat16
EPS = 1e-6

D_MODEL = 1024
SSD_HEADS = 8
SSD_HEAD_DIM = 64
SSD_INNER = SSD_HEADS * SSD_HEAD_DIM
SSD_GROUPS = 2
SSD_STATE = 64
CONV_W = 4
CONV_CH = SSD_INNER + 2 * SSD_GROUPS * SSD_STATE
FOX_HEADS = 4
FOX_HEAD_DIM = 64
FOX_INNER = FOX_HEADS * FOX_HEAD_DIM
MLP_GROUPS = 4
MLP_GROUP_DIM = 64
MLP_INNER = MLP_GROUPS * MLP_GROUP_DIM
MLP_CHUNK = 128
FOX_PROMPT_TILE = 1024
FOX_CHAINS = 4
FCUM_TILE = 2048
FCUM_SUB = 512
SSD_CHUNK_PROMPT = 128
PEER_HEADS = 8
PEER_KEYS = 128
PEER_EXPERTS = PEER_KEYS * PEER_KEYS
PEER_HALF = 128
PEER_TOPK = 16
PEER_TOKEN_TILE = 512
PEER_EXPERT_CHUNK = 2048
LANES = 128
MXU_DIM = 256
PEER_DENSE_PARTS = 4
PEER_SUB = 128
SMALL_COLS = LANES
SMALL_ROWS = 16
DT_ROW0, F_ROW0 = 0, 8
NEG = -1e30
EXP_UNDERFLOW = 104.0
VMEM_LIMIT = 56 * 1024 * 1024

_Z0, _XBC0, _Q0, _K0, _V0, _U0, _VM0, _SM0 = 0, 512, 1280, 1536, 1792, 2048, 2304, 2560
PROJ_COLS = _SM0 + SMALL_COLS


def _cparams(sem, flags=None):
    return pltpu.CompilerParams(dimension_semantics=sem, vmem_limit_bytes=VMEM_LIMIT, flags=flags)


def _dot(a, b):
    return jnp.dot(a, b, preferred_element_type=F32)


def _dot_nt(a, b):
    return lax.dot_general(a, b, (((1,), (1,)), ((), ())), preferred_element_type=F32)


def _dot_tn(a, b):
    return lax.dot_general(a, b, (((0,), (0,)), ((), ())), preferred_element_type=F32)


def _split2(x):
    hi = x.astype(BF16)
    lo = (x - hi.astype(F32)).astype(BF16)
    return hi, lo


def _split3(x):
    hi = x.astype(BF16)
    r = x - hi.astype(F32)
    mid = r.astype(BF16)
    lo = (r - mid.astype(F32)).astype(BF16)
    return hi, mid, lo


def _dot3_l(x, w):
    hi, mid, lo = _split3(x)
    return _dot(hi, w) + _dot(mid, w) + _dot(lo, w)


def _dot3_r(w, x):
    hi, mid, lo = _split3(x)
    return _dot(w, hi) + _dot(w, mid) + _dot(w, lo)


def _sigmoid(x):
    return 1.0 / (1.0 + jnp.exp(-x))


def _silu(x):
    return x * _sigmoid(x)


def _softplus(x):
    return jnp.maximum(x, 0.0) + jnp.log1p(jnp.exp(-jnp.abs(x)))


_GELU_A = 2.0 * 0.7978845608028654
_GELU_B = _GELU_A * 0.044715


def _gelu(x):
    z2 = x * (_GELU_A + _GELU_B * (x * x))
    return x / (1.0 + jnp.exp(-z2))


def _iota(shape, dim):
    return lax.broadcasted_iota(jnp.int32, shape, dim)


def _pack_rows(x):
    return pltpu.bitcast(x, jnp.uint32)


def _unpack_rows(x):
    return pltpu.bitcast(x, BF16)


def _cast_kernel(x_ref, o_ref):
    o_ref[0] = _pack_rows(x_ref[0].astype(BF16))


def _cast_t_kernel(packed, x_ref, o_ref):
    xt = x_ref[0].T.astype(BF16)
    o_ref[0] = _pack_rows(xt) if packed else xt


def _to_bf16_packed(x, rows):
    n, r, c = x.shape
    return pl.pallas_call(
        _cast_kernel,
        grid=(n, r // rows),
        in_specs=[pl.BlockSpec((1, rows, c), lambda l, i: (l, i, 0))],
        out_specs=pl.BlockSpec((1, rows // 2, c), lambda l, i: (l, i, 0)),
        out_shape=jax.ShapeDtypeStruct((n, r // 2, c), jnp.uint32),
        compiler_params=_cparams(("arbitrary", "arbitrary")),
        name="cast_bf16",
    )(x)


def _to_bf16_t(x, rows, cols, packed):
    n, r, c = x.shape
    div = 2 if packed else 1
    return pl.pallas_call(
        functools.partial(_cast_t_kernel, packed),
        grid=(n, r // rows, c // cols),
        in_specs=[pl.BlockSpec((1, rows, cols), lambda l, i, j: (l, i, j))],
        out_specs=pl.BlockSpec((1, cols // div, rows), lambda l, i, j: (l, j, i)),
        out_shape=jax.ShapeDtypeStruct((n, c // div, r), jnp.uint32 if packed else BF16),
        compiler_params=_cparams(("arbitrary", "arbitrary", "arbitrary")),
        name="cast_bf16_transposed",
    )(x)


def _mod_kernel(c_ref, w_ref, b_ref, o_ref):
    c = c_ref[...]
    o_ref[0] = jnp.dot(_silu(c), w_ref[0], preferred_element_type=F32,
                       precision=lax.Precision.HIGHEST) + b_ref[0]


def _modulation(c_all, w_ada, b_ada):
    depth, _, n6 = w_ada.shape
    bc = c_all.shape[0]
    tn = 1536
    return pl.pallas_call(
        _mod_kernel,
        grid=(depth, n6 // tn),
        in_specs=[pl.BlockSpec((bc, D_MODEL), lambda l, j: (0, 0)),
                  pl.BlockSpec((1, D_MODEL, tn), lambda l, j: (l, 0, j)),
                  pl.BlockSpec((1, 1, tn), lambda l, j: (l, 0, j))],
        out_specs=pl.BlockSpec((1, bc, tn), lambda l, j: (l, 0, j)),
        out_shape=jax.ShapeDtypeStruct((depth, bc, n6), F32),
        compiler_params=_cparams(("arbitrary", "arbitrary")),
        name="adaln_mod",
    )(c_all, w_ada, b_ada.reshape(depth, 1, n6))


def _in_kernel(with_peer, *refs):
    if with_peer:
        x_ref, p_ref, g2_ref = refs[:3]
        refs = refs[3:]
    else:
        x_ref = refs[0]
        refs = refs[1:]
    (sh_ref, sc_ref, g_ref, w_ref, wst_ref, gq_ref, gk_ref, gm_ref) = refs[:8]
    outs = refs[8:]
    if with_peer:
        xo_ref = outs[0]
        outs = outs[1:]
    (z_ref, xbc_ref, q_ref, k_ref, kb_ref, v_ref, vb_ref, ug_ref, vn_ref, sm_ref, smt_ref) = outs

    x = x_ref[0]
    if with_peer:
        x = x + g2_ref[0] * p_ref[0]
        xo_ref[0] = x
    ms = jnp.mean(x * x, axis=-1, keepdims=True)
    h = x * lax.rsqrt(ms + EPS) * g_ref[...]
    h = h * (1.0 + sc_ref[0]) + sh_ref[0]
    hb = h.astype(BF16)
    proj = _dot(hb, w_ref[...])
    z_ref[0] = proj[:, _Z0:_XBC0]
    xbc_ref[0] = proj[:, _XBC0:_Q0]
    q = proj[:, _Q0:_K0]
    k = proj[:, _K0:_V0]
    v = proj[:, _V0:_U0]
    u = proj[:, _U0:_VM0]
    vm = proj[:, _VM0:_SM0]
    sm_ref[0] = proj[:, _SM0:PROJ_COLS]
    smt_ref[0] = _dot_nt(wst_ref[...], hb)
    gm = gm_ref[...]

    def gmean(y):
        hi, lo = _split2(y)
        return _dot(hi, gm) + _dot(lo, gm)

    qn = q * lax.rsqrt(gmean(q * q) + EPS) * gq_ref[...]
    kn = k * lax.rsqrt(gmean(k * k) + EPS) * gk_ref[...]
    q_ref[0] = qn.astype(BF16)
    k_ref[0] = kn
    kb_ref[0] = kn.astype(BF16)
    v_ref[0] = v
    vb_ref[0] = v.astype(BF16)
    ug_ref[0] = _gelu(u)
    gv = _gelu(vm)
    mu = gmean(gv)
    cen = gv - mu
    var = gmean(cen * cen)
    vn_ref[0] = cen * lax.rsqrt(var + EPS)


def _in_proj(x, peer, g2, sh, sc, g, w_r, ws_t, gq, gk, gm):
    b, t, _ = x.shape
    tm = min(t, 512)
    with_peer = peer is not None
    tok = lambda c: pl.BlockSpec((1, tm, c), lambda i, j: (i, j, 0))
    per_b = pl.BlockSpec((1, 1, D_MODEL), lambda i, j: (i, 0, 0))
    full = lambda a: pl.BlockSpec(a.shape, lambda i, j: (0,) * a.ndim)
    in_specs = [tok(D_MODEL)]
    args = [x]
    if with_peer:
        in_specs += [tok(D_MODEL), per_b]
        args += [peer, g2]
    in_specs += [per_b, per_b, full(g), full(w_r), full(ws_t), full(gq), full(gk), full(gm)]
    args += [sh, sc, g, w_r, ws_t, gq, gk, gm]
    out_cols = [(SSD_INNER, F32), (CONV_CH, F32), (FOX_INNER, BF16), (FOX_INNER, F32), (FOX_INNER, BF16),
                (FOX_INNER, F32), (FOX_INNER, BF16), (MLP_INNER, F32), (MLP_INNER, F32), (SMALL_COLS, F32)]
    out_specs = [tok(c) for c, _ in out_cols]
    out_shape = [jax.ShapeDtypeStruct((b, t, c), dt) for c, dt in out_cols]
    out_specs.append(pl.BlockSpec((1, SMALL_ROWS, tm), lambda i, j: (i, 0, j)))
    out_shape.append(jax.ShapeDtypeStruct((b, SMALL_ROWS, t), F32))
    if with_peer:
        out_specs = [tok(D_MODEL)] + out_specs
        out_shape = [jax.ShapeDtypeStruct((b, t, D_MODEL), F32)] + out_shape
    res = pl.pallas_call(
        functools.partial(_in_kernel, with_peer),
        grid=(b, t // tm),
        in_specs=in_specs, out_specs=out_specs, out_shape=out_shape,
        compiler_params=_cparams(("arbitrary", "arbitrary")),
        name="in_proj",
    )(*args)
    if with_peer:
        return res[0], res[1:]
    return x, res


def _ssd_kernel(L, xbc_ref, z_ref, sm_ref, smt_ref, hist_ref, s0_ref, cw_ref, cb_ref, dtbc_ref, dtbr_ref,
                ac_ref, ar_ref, dsk_ref, ng_ref, e_ref, y_ref, sfin_ref, cfin_ref, xpad, st, ybuf):
    c = pl.program_id(1)
    nc = pl.num_programs(1)

    @pl.when(c == 0)
    def _():
        xpad[0:8, :] = hist_ref[0]
        st[...] = s0_ref[0]

    xpad[8:8 + L, :] = xbc_ref[0]
    conv = cb_ref[...]
    for tap in range(CONV_W):
        conv = conv + xpad[5 + tap:5 + tap + L, :] * cw_ref[tap:tap + 1, :]
    tail = xpad[L:L + 8, :]
    xpad[0:8, :] = tail
    xc = _silu(conv)
    xs = xc[:, 0:SSD_INNER]

    r_i = _iota((L, L), 0)
    c_i = _iota((L, L), 1)
    causal = r_i >= c_i
    tri = jnp.where(causal, 1.0, 0.0).astype(BF16)
    triu = jnp.where(r_i <= c_i, 1.0, 0.0).astype(BF16)

    dtc = _softplus(sm_ref[0] + dtbc_ref[...])
    acum_c = _dot3_r(tri, dtc * ac_ref[...])
    e = e_ref[...]
    acum_x = _dot3_l(acum_c, e)
    dt_x = _dot3_l(dtc, e)
    dtr = _softplus(smt_ref[0][DT_ROW0:DT_ROW0 + 8, :] + dtbr_ref[...])
    acum_r = _dot3_l(dtr * ar_ref[...], triu)

    bmat = [xc[:, SSD_INNER + SSD_STATE * g:SSD_INNER + SSD_STATE * (g + 1)].astype(BF16) for g in range(SSD_GROUPS)]
    c0 = SSD_INNER + SSD_GROUPS * SSD_STATE
    cmat = [xc[:, c0 + SSD_STATE * g:c0 + SSD_STATE * (g + 1)].astype(BF16) for g in range(SSD_GROUPS)]
    cb = [_dot_nt(cmat[g], bmat[g]) for g in range(SSD_GROUPS)]
    gw = SSD_INNER // SSD_GROUPS
    stb = st[...].astype(BF16)
    y_off = jnp.concatenate([_dot(cmat[g], stb[:, gw * g:gw * (g + 1)]) for g in range(SSD_GROUPS)], axis=1)
    xsb = xs.astype(BF16)
    hpg = SSD_HEADS // SSD_GROUPS
    mixes = []
    for h in range(SSD_HEADS):
        seg = acum_c[:, h:h + 1] - acum_r[h:h + 1, :]
        dec = jnp.where(causal, jnp.exp(jnp.minimum(seg, 0.0)), 0.0)
        mixes.append((cb[h // hpg] * dec * dtr[h:h + 1, :]).astype(BF16))
    heads = [_dot(mixes[h], xsb[:, SSD_HEAD_DIM * h:SSD_HEAD_DIM * (h + 1)]) for h in range(SSD_HEADS)]
    for h in range(SSD_HEADS):
        ybuf[:, SSD_HEAD_DIM * h:SSD_HEAD_DIM * (h + 1)] = heads[h]

    y = ybuf[...] + y_off * jnp.exp(acum_x) + dsk_ref[...] * xs
    a_end = acum_x[L - 1:L, :]
    xw = (xs * dt_x * jnp.exp(a_end - acum_x)).astype(BF16)
    new_states = jnp.concatenate([_dot_tn(bmat[g], xw[:, gw * g:gw * (g + 1)]) for g in range(SSD_GROUPS)], axis=1)
    st[...] = st[...] * jnp.exp(a_end) + new_states

    yg = y * _silu(z_ref[0])
    ms = jnp.mean(yg * yg, axis=-1, keepdims=True)
    y_ref[0] = yg * lax.rsqrt(ms + EPS) * ng_ref[...]

    @pl.when(c == nc - 1)
    def _():
        sfin_ref[0] = st[...]
        cfin_ref[0] = tail


def _ssd(xbc, z, sm, smt, hist8, s0t, cw, cb, dtb_c, dtb_r, a_c, a_r, dsk_x, ng, e_mat, L):
    b, t, _ = xbc.shape
    tok = lambda c: pl.BlockSpec((1, L, c), lambda i, j: (i, j, 0))
    full = lambda a: pl.BlockSpec(a.shape, lambda i, j: (0,) * a.ndim)
    per_b = lambda a: pl.BlockSpec((1,) + a.shape[1:], lambda i, j: (i,) + (0,) * (a.ndim - 1))
    return pl.pallas_call(
        functools.partial(_ssd_kernel, L),
        grid=(b, t // L),
        in_specs=[tok(CONV_CH), tok(SSD_INNER), tok(SMALL_COLS),
                  pl.BlockSpec((1, SMALL_ROWS, L), lambda i, j: (i, 0, j)),
                  per_b(hist8), per_b(s0t), full(cw), full(cb), full(dtb_c), full(dtb_r), full(a_c), full(a_r),
                  full(dsk_x), full(ng), full(e_mat)],
        out_specs=[tok(SSD_INNER),
                   pl.BlockSpec((1, SSD_STATE, SSD_INNER), lambda i, j: (i, 0, 0)),
                   pl.BlockSpec((1, 8, CONV_CH), lambda i, j: (i, 0, 0))],
        out_shape=[jax.ShapeDtypeStruct((b, t, SSD_INNER), F32),
                   jax.ShapeDtypeStruct((b, SSD_STATE, SSD_INNER), F32),
                   jax.ShapeDtypeStruct((b, 8, CONV_CH), F32)],
        scratch_shapes=[pltpu.VMEM((L + 8, CONV_CH), F32), pltpu.VMEM((SSD_STATE, SSD_INNER), F32),
                        pltpu.VMEM((L, SSD_INNER), F32)],
        compiler_params=_cparams(("arbitrary", "arbitrary")),
        name="ssd_scan",
    )(xbc, z, sm, smt, hist8, s0t, cw, cb, dtb_c, dtb_r, a_c, a_r, dsk_x, ng, e_mat)


def _fcum_kernel(tf, activate, colsrc_ref, rowsrc_ref, fbc_ref, fbr_ref, initc_ref, initr_ref,
                 lfc_ref, fc_ref, fr_ref, endc_ref, endr_ref, carc, carr):
    j = pl.program_id(1)

    @pl.when(j == 0)
    def _():
        carc[...] = initc_ref[0]
        carr[...] = initr_ref[0]

    ts = min(tf, FCUM_SUB)
    r_i = _iota((ts, ts), 0)
    c_i = _iota((ts, ts), 1)
    tri = jnp.where(r_i >= c_i, 1.0, 0.0).astype(BF16)
    triu = jnp.where(r_i <= c_i, 1.0, 0.0).astype(BF16)
    local = []
    for k in range(tf // ts):
        xc = colsrc_ref[0, k * ts:(k + 1) * ts, :]
        xr = rowsrc_ref[0, :, k * ts:(k + 1) * ts]
        if activate:
            xc = -_softplus(-(xc + fbc_ref[...]))
            xr = -_softplus(-(xr + fbr_ref[...]))
        lfc_ref[0, k * ts:(k + 1) * ts, :] = xc
        local.append((_dot3_r(tri, xc), _dot3_l(xr, triu)))
    car_c = carc[0:1, :]
    car_r = carr[:, 0:1]
    for k, (pc, pr) in enumerate(local):
        fcol = car_c + pc
        frow = car_r + pr
        fc_ref[0, k * ts:(k + 1) * ts, :] = fcol
        fr_ref[0, :, k * ts:(k + 1) * ts] = frow
        car_c = fcol[ts - 1:ts, :]
        car_r = frow[:, ts - 1:ts]
    carc[...] = jnp.broadcast_to(car_c, carc.shape)
    carr[...] = jnp.broadcast_to(car_r, carr.shape)
    endc_ref[0] = carc[...]
    endr_ref[0] = carr[...]


def _fcum(colsrc, rowsrc, row_block, fb_c, fb_r, init_c, init_r, activate):
    b, t, _ = colsrc.shape
    tf = min(t, FCUM_TILE)
    full = lambda a: pl.BlockSpec(a.shape, lambda i, j: (0,) * a.ndim)
    per_b = lambda a: pl.BlockSpec((1,) + a.shape[1:], lambda i, j: (i,) + (0,) * (a.ndim - 1))
    return pl.pallas_call(
        functools.partial(_fcum_kernel, tf, activate),
        grid=(b, t // tf),
        in_specs=[pl.BlockSpec((1, tf, LANES), lambda i, j: (i, j, 0)),
                  pl.BlockSpec((1, 8, tf), lambda i, j: (i, row_block, j)),
                  full(fb_c), full(fb_r), per_b(init_c), per_b(init_r)],
        out_specs=[pl.BlockSpec((1, tf, LANES), lambda i, j: (i, j, 0)),
                   pl.BlockSpec((1, tf, LANES), lambda i, j: (i, j, 0)),
                   pl.BlockSpec((1, 8, tf), lambda i, j: (i, 0, j)),
                   pl.BlockSpec((1, 8, LANES), lambda i, j: (i, 0, 0)),
                   pl.BlockSpec((1, 8, LANES), lambda i, j: (i, 0, 0))],
        out_shape=[jax.ShapeDtypeStruct((b, t, LANES), F32), jax.ShapeDtypeStruct((b, t, LANES), F32),
                   jax.ShapeDtypeStruct((b, 8, t), F32), jax.ShapeDtypeStruct((b, 8, LANES), F32),
                   jax.ShapeDtypeStruct((b, 8, LANES), F32)],
        scratch_shapes=[pltpu.VMEM((8, LANES), F32), pltpu.VMEM((8, LANES), F32)],
        compiler_params=_cparams(("arbitrary", "arbitrary")),
        name="forget_cumsum",
    )(colsrc, rowsrc, fb_c, fb_r, init_c, init_r)


def _softmax_stats(s, carry):
    m, l, acc = carry
    tq, tk = s.shape
    m_new = jnp.maximum(m, jnp.max(s, axis=-1, keepdims=True))
    p = jnp.exp(s - m_new)
    alpha = jnp.exp(m - m_new)
    if tk % LANES == 0:
        psum = p[:, 0:LANES]
        for c in range(1, tk // LANES):
            psum = psum + p[:, c * LANES:(c + 1) * LANES]
    else:
        psum = jnp.where(_iota((tq, LANES), 1) == 0, jnp.sum(p, axis=-1, keepdims=True), 0.0)
    return m_new, alpha * l + psum, alpha * acc, p.astype(BF16)


def _softmax_init(tq):
    return tuple((jnp.full((tq, 1), NEG, F32), jnp.zeros((tq, LANES), F32), jnp.zeros((tq, FOX_HEAD_DIM), F32))
                 for _ in range(FOX_HEADS))


def _head_slice(h):
    return slice(FOX_HEAD_DIM * h, FOX_HEAD_DIM * (h + 1))


def _attend_block(qs, fqs, kb, vb, fk, mask, carries):
    heads = range(FOX_HEADS)
    scores = [_dot_nt(qs[h], kb[:, _head_slice(h)]) for h in heads]
    stats = []
    for h in heads:
        s = scores[h] + fqs[h] - fk[h:h + 1, :]
        if mask is not None:
            s = jnp.where(mask, s, NEG)
        stats.append(_softmax_stats(s, carries[h]))
    return tuple((m, l, acc + _dot(p, vb[:, _head_slice(h)])) for h, (m, l, acc, p) in zip(heads, stats))


def _attend_finish(carries, o_ref):
    for h in range(FOX_HEADS):
        _, l, acc = carries[h]
        o_ref[0, :, _head_slice(h)] = acc / jnp.sum(l, axis=-1, keepdims=True)


def _bf16_terms(x):
    hi, mid, lo = _split3(x)
    return hi.astype(F32), mid.astype(F32), lo.astype(F32)


def _fox_prep_kernel(q_ref, k_ref, v_ref, fc_ref, fr_ref, ka_ref, qa_ref, vt_ref):
    tm = q_ref.shape[1]
    qt = q_ref[0].astype(F32).T
    k = k_ref[0].astype(F32)
    vt_ref[0, 0] = v_ref[0].astype(F32).T.astype(BF16)
    lane = _iota((tm, LANES), 1)
    row = _iota((FOX_HEAD_DIM, tm), 0)
    d = FOX_HEAD_DIM
    for h in range(FOX_HEADS):
        pair, odd = divmod(h, 2)
        kc = k[:, pair * LANES:(pair + 1) * LANES]
        if odd:
            kc = pltpu.roll(kc, d, axis=1)
        ka = jnp.where(lane < d, kc, 0.0)
        for i, term in enumerate(_bf16_terms(-fc_ref[0][:, F_ROW0 + h:F_ROW0 + h + 1])):
            ka = jnp.where(lane == d + i, term, ka)
        ka = jnp.where(jnp.logical_and(lane >= d + 3, lane < d + 6), 1.0, ka)
        ka_ref[0, h] = ka.astype(BF16)
        tail = jnp.where(row < 3, 1.0, 0.0)
        for i, term in enumerate(_bf16_terms(fr_ref[0][h:h + 1, :])):
            tail = jnp.where(row == 3 + i, term, tail)
        qa_ref[0, h, 0:d, :] = qt[d * h:d * (h + 1), :].astype(BF16)
        qa_ref[0, h, d:2 * d, :] = tail.astype(BF16)


def _fox_prep(qb, kb, vb, fcol, frow, tm):
    b, t, _ = qb.shape
    tok = lambda c: pl.BlockSpec((1, tm, c), lambda i, j: (i, j, 0))
    return pl.pallas_call(
        _fox_prep_kernel,
        grid=(b, t // tm),
        in_specs=[tok(FOX_INNER), tok(FOX_INNER), tok(FOX_INNER), tok(LANES),
                  pl.BlockSpec((1, 8, tm), lambda i, j: (i, 0, j))],
        out_specs=[pl.BlockSpec((1, FOX_HEADS, tm, LANES), lambda i, j: (i, 0, j, 0)),
                   pl.BlockSpec((1, FOX_HEADS, LANES, tm), lambda i, j: (i, 0, 0, j)),
                   pl.BlockSpec((1, 1, FOX_INNER, tm), lambda i, j: (i, j, 0, 0))],
        out_shape=[jax.ShapeDtypeStruct((b, FOX_HEADS, t, LANES), BF16),
                   jax.ShapeDtypeStruct((b, FOX_HEADS, LANES, t), BF16),
                   jax.ShapeDtypeStruct((b, t // tm, FOX_INNER, tm), BF16)],
        compiler_params=_cparams(("arbitrary", "arbitrary")),
        name="fox_prep",
    )(qb, kb, vb, fcol, frow)


def _fox_prompt_t_kernel(tq, nk, qa_ref, ka_ref, vt_ref, fend_ref, ffirst_ref, thr_ref, o_ref):
    b = pl.program_id(0)
    qi = pl.program_id(1)
    half = tq // FOX_CHAINS
    d = FOX_HEAD_DIM
    diag = pl.multiple_of(qi * tq, tq)
    init = (jnp.full((1, half), NEG, F32), jnp.zeros((1, half), F32), jnp.zeros((d, half), F32))

    def update(blocks, qas, masks, carries):
        scores = [[_dot(ka, qa) for qa in qas] for ka, _ in blocks]
        for (_, vt), block_scores in zip(blocks, scores):
            stats = []
            for s, mask, (m, l, acc) in zip(block_scores, masks, carries):
                if mask is not None:
                    s = jnp.where(mask, s, NEG)
                m_new = jnp.maximum(m, jnp.max(s, axis=0, keepdims=True))
                p = jnp.exp(s - m_new)
                alpha = jnp.exp(m - m_new)
                stats.append((m_new, alpha * l + jnp.sum(p, axis=0, keepdims=True), alpha * acc, p.astype(BF16)))
            carries = tuple((m, l, acc + _dot(vt, p)) for m, l, acc, p in stats)
        return carries

    def key_block(h, j):
        return (ka_ref[0, h, pl.ds(pl.multiple_of(j * tq, tq), tq), :], vt_ref[0, j][d * h:d * (h + 1), :])

    for h in range(FOX_HEADS):
        base = (b * FOX_HEADS + h) * nk
        slack = thr_ref[0] + ffirst_ref[base + qi]
        n_live = lax.fori_loop(0, qi, lambda j, c: c + (slack - fend_ref[base + j] >= 0.0).astype(jnp.int32), 0)
        qas = [qa_ref[0, h, :, r2 * half:(r2 + 1) * half] for r2 in range(FOX_CHAINS)]
        ka_d = ka_ref[0, h, pl.ds(diag, tq), :]
        vt_d = vt_ref[0, qi][d * h:d * (h + 1), :]
        causal = [_iota((tq, half), 0) <= (r2 * half + _iota((tq, half), 1)) for r2 in range(FOX_CHAINS)]
        carries = update([(ka_d, vt_d)], qas, causal, [init] * FOX_CHAINS)
        no_mask = [None] * FOX_CHAINS

        def one(t, carries, qas=qas, h=h):
            return update([key_block(h, qi - 1 - t)], qas, no_mask, carries)

        def two(t, carries, qas=qas, h=h):
            j = qi - 1 - (n_live % 2) - 2 * t
            return update([key_block(h, j), key_block(h, j - 1)], qas, no_mask, carries)

        carries = lax.fori_loop(0, n_live % 2, one, carries)
        carries = lax.fori_loop(0, n_live // 2, two, carries)
        for r2 in range(FOX_CHAINS):
            _, l, acc = carries[r2]
            o_ref[0, r2 * half:(r2 + 1) * half, d * h:d * (h + 1)] = (acc / l).T


def _fox_prompt_t(qb, kb, vb, fcol, frow, score_bound):
    b, t, _ = qb.shape
    tq = min(t, FOX_PROMPT_TILE)
    nk = t // tq
    ka, qa, vt = _fox_prep(qb, kb, vb, fcol, frow, tq)
    f_heads = frow[:, :FOX_HEADS, :].reshape(b, FOX_HEADS, nk, tq)
    f_end = f_heads[:, :, :, tq - 1].reshape(-1)
    f_first = f_heads[:, :, :, 0].reshape(-1)
    thr = (2.0 * score_bound + EXP_UNDERFLOW).reshape(1).astype(F32)
    smem = pl.BlockSpec(memory_space=pltpu.SMEM)
    return pl.pallas_call(
        functools.partial(_fox_prompt_t_kernel, tq, nk),
        grid=(b, nk),
        in_specs=[pl.BlockSpec((1, FOX_HEADS, LANES, tq), lambda i, j: (i, 0, 0, j)),
                  pl.BlockSpec((1, FOX_HEADS, t, LANES), lambda i, j: (i, 0, 0, 0)),
                  pl.BlockSpec((1, nk, FOX_INNER, tq), lambda i, j: (i, 0, 0, 0)),
                  smem, smem, smem],
        out_specs=pl.BlockSpec((1, tq, FOX_INNER), lambda i, j: (i, j, 0)),
        out_shape=jax.ShapeDtypeStruct((b, t, FOX_INNER), F32),
        compiler_params=_cparams(("arbitrary", "arbitrary")),
        name="fox_prompt",
    )(qa, ka, vt, f_end, f_first, thr)


def _fox_sample_kernel(tq, tk, npast, q_ref, fq_ref, pk_ref, pv_ref, fpk_ref, k_ref, v_ref, fk_ref, o_ref):
    q_all = q_ref[0]
    fq_all = fq_ref[0]
    qs = [q_all[:, _head_slice(h)] for h in range(FOX_HEADS)]
    fqs = [fq_all[:, F_ROW0 + h:F_ROW0 + h + 1] for h in range(FOX_HEADS)]
    causal = _iota((tq, tq), 0) >= _iota((tq, tq), 1)

    def past_block(j, carries):
        start = pl.multiple_of(j * tk, tk)
        return _attend_block(qs, fqs, pk_ref[0, pl.ds(start, tk), :].astype(BF16),
                             pv_ref[0, pl.ds(start, tk), :].astype(BF16), fpk_ref[0, j], None, carries)

    carries = lax.fori_loop(0, npast, past_block, _softmax_init(tq))
    _attend_finish(_attend_block(qs, fqs, k_ref[0], v_ref[0], fk_ref[0], causal, carries), o_ref)


def _fox_sample(qb, fcol, past_k, past_v, layer, fpast_row, kb, vb, frow):
    b, t, _ = qb.shape
    p = past_k.shape[2]
    tk = min(p, 512)
    npast = p // tk
    fpk = fpast_row.reshape(b, 8, npast, tk).transpose(0, 2, 1, 3)
    bspec = lambda a: pl.BlockSpec((1,) + a.shape[1:], lambda i: (i,) + (0,) * (a.ndim - 1))
    cache = pl.BlockSpec((None, 1, p, FOX_INNER), lambda i: (layer, i, 0, 0))
    args = (qb, fcol, past_k, past_v, fpk, kb, vb, frow)
    return pl.pallas_call(
        functools.partial(_fox_sample_kernel, t, tk, npast),
        grid=(b,),
        in_specs=[bspec(qb), bspec(fcol), cache, cache, bspec(fpk), bspec(kb), bspec(vb), bspec(frow)],
        out_specs=pl.BlockSpec((1, t, FOX_INNER), lambda i: (i, 0, 0)),
        out_shape=jax.ShapeDtypeStruct((b, t, FOX_INNER), F32),
        compiler_params=_cparams(("arbitrary",)),
        name="fox_sample",
    )(*args)


def _mlp_kernel(lm, nchunk, ug_ref, vn_ref, ws_ref, bst_ref, y_ref):
    r_i = _iota((lm, lm), 0)
    c_i = _iota((lm, lm), 1)
    tril = r_i >= c_i
    vn = vn_ref[0].astype(BF16)
    ug = ug_ref[0]
    for g in range(MLP_GROUPS):
        lo, hi = MLP_GROUP_DIM * g, MLP_GROUP_DIM * (g + 1)
        w = jnp.where(tril, ws_ref[g], 0.0).astype(BF16)
        bias = bst_ref[:, g:g + 1]
        for c in range(nchunk):
            r0, r1 = c * lm, (c + 1) * lm
            sv = _dot(w, vn[r0:r1, lo:hi]) + bias
            y_ref[0, r0:r1, lo:hi] = ug[r0:r1, lo:hi] * sv


def _chunk_mlp(ug, vn, ws, bst):
    b, t, _ = ug.shape
    lm = ws.shape[1]
    tm = min(t, 4 * lm)
    tok = pl.BlockSpec((1, tm, MLP_INNER), lambda i, j: (i, j, 0))
    return pl.pallas_call(
        functools.partial(_mlp_kernel, lm, tm // lm),
        grid=(b, t // tm),
        in_specs=[tok, tok, pl.BlockSpec(ws.shape, lambda i, j: (0, 0, 0)),
                  pl.BlockSpec(bst.shape, lambda i, j: (0, 0))],
        out_specs=tok,
        out_shape=jax.ShapeDtypeStruct((b, t, MLP_INNER), F32),
        compiler_params=_cparams(("arbitrary", "arbitrary")),
        name="chunk_mlp",
    )(ug, vn, ws, bst)


def _out_kernel(x_ref, ys_ref, yf_ref, ym_ref, g1_ref, sh_ref, sc_ref, g_ref, wo_ref, x1_ref, h2_ref):
    mix = (_dot(ys_ref[0].astype(BF16), wo_ref[0:SSD_INNER, :])
           + _dot(yf_ref[0].astype(BF16), wo_ref[SSD_INNER:SSD_INNER + FOX_INNER, :])
           + _dot(ym_ref[0].astype(BF16), wo_ref[SSD_INNER + FOX_INNER:D_MODEL, :]))
    x1 = x_ref[0] + g1_ref[0] * mix
    x1_ref[0] = x1
    ms = jnp.mean(x1 * x1, axis=-1, keepdims=True)
    h = x1 * lax.rsqrt(ms + EPS) * g_ref[...]
    h2_ref[0] = _pack_rows((h * (1.0 + sc_ref[0]) + sh_ref[0]).astype(BF16))


def _out_proj(x, ys, yf, ym, g1, sh, sc, g, wo):
    b, t, _ = x.shape
    tm = min(t, 512)
    tok = lambda c: pl.BlockSpec((1, tm, c), lambda i, j: (i, j, 0))
    per_b = pl.BlockSpec((1, 1, D_MODEL), lambda i, j: (i, 0, 0))
    full = lambda a: pl.BlockSpec(a.shape, lambda i, j: (0,) * a.ndim)
    return pl.pallas_call(
        _out_kernel,
        grid=(b, t // tm),
        in_specs=[tok(D_MODEL), tok(SSD_INNER), tok(FOX_INNER), tok(MLP_INNER), per_b, per_b, per_b, full(g), full(wo)],
        out_specs=[tok(D_MODEL), pl.BlockSpec((1, tm // 2, D_MODEL), lambda i, j: (i, j, 0))],
        out_shape=[jax.ShapeDtypeStruct((b, t, D_MODEL), F32),
                   jax.ShapeDtypeStruct((b, t // 2, D_MODEL), jnp.uint32)],
        compiler_params=_cparams(("arbitrary", "arbitrary")),
        name="out_proj",
    )(x, ys, yf, ym, g1, sh, sc, g, wo)


def _top16(scores):
    nk, tn = scores[0].shape
    ridx = _iota((nk, tn), 0).astype(F32)
    r16 = _iota((PEER_TOPK, tn), 0)
    state = [(s, jnp.full((nk, tn), float(PEER_TOPK), F32), jnp.zeros((PEER_TOPK, tn), F32)) for s in scores]
    for it in range(PEER_TOPK):
        nxt_state = []
        for s, pos, tv in state:
            level = [(s[g:g + 8, :], ridx[g:g + 8, :]) for g in range(0, nk, 8)]
            while len(level) > 1:
                nxt = []
                for a in range(0, len(level), 2):
                    (va, ia), (vb, ib) = level[a], level[a + 1]
                    nxt.append((jnp.maximum(va, vb), jnp.where(va >= vb, ia, ib)))
                level = nxt
            v8, i8 = level[0]
            m = jnp.max(v8, axis=0, keepdims=True)
            first = jnp.min(jnp.where(v8 == m, i8, float(nk)), axis=0, keepdims=True)
            sel = ridx == first
            nxt_state.append((jnp.where(sel, -jnp.inf, s), jnp.where(sel, float(it), pos),
                              jnp.where(r16 == it, m, tv)))
        state = nxt_state
    return [(pos, tv) for _, pos, tv in state]


_CAND_GROUPS = [(0, 16)] + [(ka, 8) for ka in range(1, 8)]
_CAND_ROWS = 16 + 7 * 8 + 8


def _pair_select(ta, tb):
    tn = ta.shape[1]
    pieces, flats, valids = [], [], []
    for ka, rows in _CAND_GROUPS:
        pieces.append(ta[ka:ka + 1, :] + tb[0:rows, :])
        kb = _iota((rows, 1), 0)
        flats.append((ka * PEER_TOPK + kb).astype(F32))
        valids.append((ka + 1) * (kb + 1) <= PEER_TOPK)
    pieces.append(ta[8:16, :] + tb[0:1, :])
    flats.append(((8 + _iota((8, 1), 0)) * PEER_TOPK).astype(F32))
    valids.append(_iota((8, 1), 0) >= 0)
    cand0 = jnp.concatenate(pieces, axis=0)
    flat = jnp.concatenate(flats, axis=0)
    valid = jnp.concatenate(valids, axis=0)
    cand0 = jnp.where(valid, cand0, -jnp.inf)
    best = ta[0:1, :] + tb[0:1, :]

    cand = cand0
    selm = jnp.zeros((_CAND_ROWS, tn), F32)
    for _ in range(PEER_TOPK):
        m = jnp.max(cand, axis=0, keepdims=True)
        first = jnp.min(jnp.where(cand == m, flat, 4096.0), axis=0, keepdims=True)
        sel = flat == first
        cand = jnp.where(sel, -jnp.inf, cand)
        selm = jnp.where(sel, 1.0, selm)
    z = jnp.sum(jnp.where(selm > 0.0, jnp.exp(cand0 - best), 0.0), axis=0, keepdims=True)
    cnts = [jnp.sum(selm[0:16, :], axis=0, keepdims=True)]
    for i in range(1, 8):
        cnts.append(jnp.sum(selm[8 + 8 * i:16 + 8 * i, :], axis=0, keepdims=True))
    cnts.append(selm[_CAND_ROWS - 8:_CAND_ROWS, :])
    return jnp.concatenate(cnts, axis=0), z


def _peer_sel_kernel(tn, h_ref, wqt_ref, keys_ref, ea_ref, la_ref, eb_ref, pb_ref, qt_ref):
    qt_ref[...] = _dot_nt(wqt_ref[...], _unpack_rows(h_ref[...])).astype(BF16)

    def head(h, carry):
        row = pl.multiple_of(h * (2 * PEER_HALF), 2 * PEER_HALF)
        sa_all = _dot(keys_ref[2 * h], qt_ref[pl.ds(row, PEER_HALF), :])
        sb_all = _dot(keys_ref[2 * h + 1], qt_ref[pl.ds(row + PEER_HALF, PEER_HALF), :])
        chunks = [slice(c * LANES, (c + 1) * LANES) for c in range(tn // LANES)]
        tops = _top16([s_all[:, cs] for cs in chunks for s_all in (sa_all, sb_all)])
        for c, cs in enumerate(chunks):
            sa, sb = sa_all[:, cs], sb_all[:, cs]
            (pos_a, ta), (pos_b, tb) = tops[2 * c], tops[2 * c + 1]
            cnt, z = _pair_select(ta, tb)
            pos_h = pos_a.astype(BF16)
            cnt_h = cnt.astype(BF16)
            la = jnp.zeros_like(pos_h)
            for ka in range(PEER_TOPK):
                la = jnp.where(pos_h == float(ka), jnp.broadcast_to(cnt_h[ka:ka + 1, :], pos_h.shape), la)
            ea_ref[h, :, cs] = jnp.where(pos_a < float(PEER_TOPK), jnp.exp(sa - ta[0:1, :]), 0.0) / z
            la_ref[h, :, cs] = la.astype(F32)
            eb = jnp.where(pos_b < float(PEER_TOPK), jnp.exp(sb - tb[0:1, :]), 0.0)
            for sub in range(PEER_KEYS // PEER_SUB):
                src = slice(sub * PEER_SUB, (sub + 1) * PEER_SUB)
                dst = slice(sub * PEER_SUB // 2, (sub + 1) * PEER_SUB // 2)
                eb_ref[h, dst, cs] = pltpu.bitcast(eb[src].astype(BF16), jnp.uint32)
                pb_ref[h, dst, cs] = pltpu.bitcast(pos_b[src].astype(BF16), jnp.uint32)
        return carry

    lax.fori_loop(0, PEER_HEADS, head, 0)


def _peer_select(h2, wqt_all, layer, keys):
    n = 2 * h2.shape[0]
    tn = 512
    per_tok = pl.BlockSpec((PEER_HEADS, PEER_KEYS, tn), lambda i: (0, 0, i))
    per_tok_pk = pl.BlockSpec((PEER_HEADS, PEER_KEYS // 2, tn), lambda i: (0, 0, i))
    shp = lambda dt: jax.ShapeDtypeStruct((PEER_HEADS, PEER_KEYS, n), dt)
    shp_pk = jax.ShapeDtypeStruct((PEER_HEADS, PEER_KEYS // 2, n), jnp.uint32)
    return pl.pallas_call(
        functools.partial(_peer_sel_kernel, tn),
        grid=(n // tn,),
        in_specs=[pl.BlockSpec((tn // 2, D_MODEL), lambda i: (i, 0)),
                  pl.BlockSpec((None,) + wqt_all.shape[1:], lambda i: (layer, 0, 0)),
                  pl.BlockSpec(keys.shape, lambda i: (0, 0, 0))],
        out_specs=[per_tok, per_tok, per_tok_pk, per_tok_pk],
        out_shape=[shp(F32), shp(F32), shp_pk, shp_pk],
        scratch_shapes=[pltpu.VMEM((PEER_HEADS * 2 * PEER_HALF, tn), BF16)],
        compiler_params=_cparams(("arbitrary",)),
        name="peer_select",
    )(h2, wqt_all, keys)


def _peer_gate_stage(ec, ia0, rows, chunks, at_r, ga_w, ea_ref, la_ref, eb_ref, pb_ref):
    rows_per = ec // PEER_KEYS
    for r in rows:
        for c in chunks:
            cs = slice(c * LANES, (c + 1) * LANES)
            for sub in range(PEER_KEYS // PEER_SUB):
                pk = slice(sub * PEER_SUB // 2, (sub + 1) * PEER_SUB // 2)
                gate = None
                for h in range(PEER_HEADS):
                    la = la_ref[h, pl.ds(ia0, rows_per), cs][r:r + 1, :]
                    ea = ea_ref[h, pl.ds(ia0, rows_per), cs][r:r + 1, :]
                    la = jnp.broadcast_to(la, (PEER_SUB, LANES)).astype(BF16)
                    ea = jnp.broadcast_to(ea, (PEER_SUB, LANES)).astype(BF16)
                    pb = pltpu.bitcast(pb_ref[h, pk, cs], BF16)
                    eb = pltpu.bitcast(eb_ref[h, pk, cs], BF16)
                    term = jnp.where(pb < la, eb, 0.0) * ea
                    gate = term if gate is None else gate + term
                ex = slice(r * PEER_KEYS + sub * PEER_SUB, r * PEER_KEYS + (sub + 1) * PEER_SUB)
                ga_w[ex, cs] = gate * _gelu(at_r[ex, cs]).astype(BF16)


def _peer_dense_kernel(tn, ec, h_ref, u_ref, vt_ref, ea_ref, la_ref, eb_ref, pb_ref, o_ref, at, ga, acc):
    e = pl.program_id(1)
    ne = pl.num_programs(1)

    @pl.when(e == 0)
    def _():
        acc[...] = jnp.zeros_like(acc)

    rows_per = ec // PEER_KEYS
    ia0 = pl.multiple_of(e * rows_per, rows_per)
    parts = PEER_DENSE_PARTS
    per = ec // parts
    h_t = _unpack_rows(h_ref[...])
    for g in range(parts):
        rows = slice(g * per, (g + 1) * per)
        pk = slice(g * per // 2, (g + 1) * per // 2)
        at[rows, :] = _dot_nt(_unpack_rows(u_ref[pk, :]), h_t)
    for g in range(parts):
        rows = slice(g * per, (g + 1) * per)
        _peer_gate_stage(ec, ia0, range(g * rows_per // parts, (g + 1) * rows_per // parts), range(tn // LANES),
                         at, ga, ea_ref, la_ref, eb_ref, pb_ref)
        acc[...] += _dot(_unpack_rows(vt_ref[:, rows]), ga[rows, :])

    @pl.when(e == ne - 1)
    def _():
        o_ref[...] = acc[...].T


def _peer_dense(h2, u_all, vt_all, layer, ea, la, eb, pb):
    n = 2 * h2.shape[0]
    tn, ec = PEER_TOKEN_TILE, PEER_EXPERT_CHUNK
    per_tok = pl.BlockSpec((PEER_HEADS, PEER_KEYS, tn), lambda i, e: (0, 0, i))
    per_tok_pk = pl.BlockSpec((PEER_HEADS, PEER_KEYS // 2, tn), lambda i, e: (0, 0, i))
    return pl.pallas_call(
        functools.partial(_peer_dense_kernel, tn, ec),
        grid=(n // tn, PEER_EXPERTS // ec),
        in_specs=[pl.BlockSpec((tn // 2, D_MODEL), lambda i, e: (i, 0)),
                  pl.BlockSpec((None, ec // 2, D_MODEL), lambda i, e: (layer, e, 0)),
                  pl.BlockSpec((None, D_MODEL // 2, ec), lambda i, e: (layer, 0, e)),
                  per_tok, per_tok, per_tok_pk, per_tok_pk],
        out_specs=pl.BlockSpec((tn, D_MODEL), lambda i, e: (i, 0)),
        out_shape=jax.ShapeDtypeStruct((n, D_MODEL), F32),
        scratch_shapes=[pltpu.VMEM((ec, tn), F32), pltpu.VMEM((ec, tn), BF16), pltpu.VMEM((D_MODEL, tn), F32)],
        compiler_params=_cparams(("arbitrary", "arbitrary")),
        name="peer_dense",
    )(h2, u_all, vt_all, ea, la, eb, pb)


def _resid_kernel(x_ref, p_ref, g_ref, o_ref):
    o_ref[0] = x_ref[0] + g_ref[0] * p_ref[0]


def _residual(x, peer, g2):
    b, t, _ = x.shape
    tm = min(t, 512)
    tok = pl.BlockSpec((1, tm, D_MODEL), lambda i, j: (i, j, 0))
    return pl.pallas_call(
        _resid_kernel,
        grid=(b, t // tm),
        in_specs=[tok, tok, pl.BlockSpec((1, 1, D_MODEL), lambda i, j: (i, 0, 0))],
        out_specs=tok,
        out_shape=jax.ShapeDtypeStruct((b, t, D_MODEL), F32),
        compiler_params=_cparams(("arbitrary", "arbitrary")),
        name="peer_residual",
    )(x, peer, g2)


def _lane_pad(vec, offset, width=LANES):
    out = jnp.zeros((width,), F32)
    return out.at[offset:offset + vec.shape[0]].set(vec.astype(F32))


def _layer_params(l, norm1_g, norm2_g, w_in, conv_w, conv_b, dt_bias, a_log, d_skip, ssd_norm_g, q_norm_g,
                  k_norm_g, fgate_b, w_s, b_s, w_out, peer_keys):
    w = w_in[l]
    o = [0]
    for sz in (SSD_INNER, CONV_CH, SSD_HEADS, FOX_INNER, FOX_INNER, FOX_INNER, FOX_HEADS, MLP_INNER, MLP_INNER):
        o.append(o[-1] + sz)
    wz, wxbc, wdt, wq, wk, wv, wf, wu, wvm = [w[:, o[i]:o[i + 1]] for i in range(9)]
    w_small = jnp.zeros((D_MODEL, SMALL_COLS), F32)
    w_small = w_small.at[:, DT_ROW0:DT_ROW0 + SSD_HEADS].set(wdt).at[:, F_ROW0:F_ROW0 + FOX_HEADS].set(wf)
    w_r = jnp.concatenate([wz, wxbc, wq, wk, wv, wu, wvm, w_small], axis=1).astype(BF16)
    ws_t = w_small[:, :SMALL_ROWS].T.astype(BF16)
    a_neg = -jnp.exp(a_log[l].astype(F32))
    grp = jnp.arange(FOX_INNER) // FOX_HEAD_DIM
    gm = jnp.where(grp[:, None] == grp[None, :], 1.0 / FOX_HEAD_DIM, 0.0).astype(BF16)
    heads = jnp.arange(SSD_INNER) // SSD_HEAD_DIM
    e_mat = (jnp.arange(LANES)[:, None] == heads[None, :]).astype(BF16)
    return dict(
        norm1_g=norm1_g[l][None], norm2_g=norm2_g[l][None], w_r=w_r, ws_t=ws_t,
        gq=(jnp.tile(q_norm_g[l], FOX_HEADS) * (FOX_HEAD_DIM ** -0.5))[None],
        gk=jnp.tile(k_norm_g[l], FOX_HEADS)[None], gm=gm,
        fox_bound=1.02 * FOX_HEAD_DIM ** 0.5 * jnp.max(jnp.abs(q_norm_g[l])) * jnp.max(jnp.abs(k_norm_g[l])),
        conv_w=conv_w[l], conv_b=conv_b[l][None],
        dtb_c=_lane_pad(dt_bias[l], DT_ROW0)[None], dtb_r=dt_bias[l].astype(F32)[:, None],
        a_c=_lane_pad(a_neg, DT_ROW0)[None], a_r=a_neg[:, None],
        dsk_x=jnp.repeat(d_skip[l].astype(F32), SSD_HEAD_DIM)[None], ssd_g=ssd_norm_g[l][None], e_mat=e_mat,
        fb_c=_lane_pad(fgate_b[l], F_ROW0)[None], fb_r=_lane_pad(fgate_b[l], 0, 8)[:, None],
        w_s=w_s[l], bs_t=jnp.zeros((MLP_CHUNK, LANES), F32).at[:, :MLP_GROUPS].set(b_s[l].T),
        w_out=w_out[l].astype(BF16),
        keys=peer_keys[l].reshape(PEER_HEADS * 2, PEER_KEYS, PEER_HALF).astype(BF16),
    )


def _stream_mixers(x, peer, g2_prev, mod, p, hist8, s0t, past):
    sh1, sc1, g1, sh2, sc2, _ = mod
    b, t, _ = x.shape
    x, (z, xbc, qb, kn, kb, v, vb, ug, vn, sm, smt) = _in_proj(
        x, peer, g2_prev, sh1, sc1, p['norm1_g'], p['w_r'], p['ws_t'], p['gq'], p['gk'], p['gm'])
    L = SSD_CHUNK_PROMPT if t % SSD_CHUNK_PROMPT == 0 else t
    y_ssd, s_fin, c_fin = _ssd(xbc, z, sm, smt, hist8, s0t, p['conv_w'], p['conv_b'], p['dtb_c'], p['dtb_r'],
                               p['a_c'], p['a_r'], p['dsk_x'], p['ssd_g'], p['e_mat'], L)
    zeros8 = jnp.zeros((b, 8, LANES), F32)
    if past is None:
        logf_c, f_col, f_row, _, _ = _fcum(sm, smt, F_ROW0 // 8, p['fb_c'], p['fb_r'], zeros8, zeros8, True)
        y_fox = _fox_prompt_t(qb, kb, vb, f_col, f_row, p['fox_bound'])
    else:
        pk, pv, layer, plf_col, plf_row = past
        _, _, fp_row, end_c, end_r = _fcum(plf_col, plf_row, 0, p['fb_c'], p['fb_r'], zeros8, zeros8, False)
        logf_c, f_col, f_row, _, _ = _fcum(sm, smt, F_ROW0 // 8, p['fb_c'], p['fb_r'], end_c, end_r, True)
        y_fox = _fox_sample(qb, f_col, pk, pv, layer, fp_row, kb, vb, f_row)
    lm = MLP_CHUNK if t % MLP_CHUNK == 0 else t
    y_mlp = _chunk_mlp(ug, vn, p['w_s'][:, :lm, :lm], p['bs_t'][:lm])
    x1, h2 = _out_proj(x, y_ssd, y_fox, y_mlp, g1, sh2, sc2, p['norm2_g'], p['w_out'])
    logf = logf_c[:, :, F_ROW0:F_ROW0 + FOX_HEADS]
    new_ssm = s_fin.reshape(b, SSD_STATE, SSD_HEADS, SSD_HEAD_DIM).transpose(0, 2, 3, 1)
    new_conv = c_fin[:, 8 - (CONV_W - 1):, :]
    kc = kn.reshape(b, t, FOX_HEADS, FOX_HEAD_DIM)
    vc = v.reshape(b, t, FOX_HEADS, FOX_HEAD_DIM)
    return x1, h2, (kc, vc, logf, new_ssm, new_conv, vn)


def _peer(h2, p, tables, layer):
    b, t2, _ = h2.shape
    t = 2 * t2
    n_tok = b * t
    n_pad = -n_tok % PEER_TOKEN_TILE
    flat = h2.reshape(n_tok // 2, D_MODEL)
    if n_pad:
        flat = jnp.concatenate([flat, jnp.zeros((n_pad // 2, D_MODEL), jnp.uint32)], axis=0)
    wqt_all, u_all, vt_all = tables
    ea, la, eb, pb = _peer_select(flat, wqt_all, layer, p['keys'])
    out = _peer_dense(flat, u_all, vt_all, layer, ea, la, eb, pb)
    return out[:n_tok].reshape(b, t, D_MODEL)


def kernel(x_prompt, x_sample, c_prompt, c_sample, cache_fox_k, cache_fox_v, cache_fox_logf, state_ssm, state_conv, norm1_g, norm2_g, w_ada, b_ada, w_in, conv_w, conv_b, dt_bias, a_log, d_skip, ssd_norm_g, q_norm_g, k_norm_g, fgate_b, w_s, b_s, w_out, peer_wq, peer_keys, peer_u, peer_v):
    depth = w_ada.shape[0]
    bp, tp, _ = x_prompt.shape
    bs, ts, _ = x_sample.shape
    past_len = cache_fox_k.shape[2]
    mod_all = _modulation(jnp.concatenate([c_prompt, c_sample], axis=0).astype(F32), w_ada, b_ada)

    peer_tables = (_to_bf16_t(peer_wq, D_MODEL, 512, False), _to_bf16_packed(peer_u, 1024),
                   _to_bf16_t(peer_v, 512, D_MODEL, True))

    past_k_all = cache_fox_k.reshape(depth, bs, past_len, FOX_INNER)
    past_v_all = cache_fox_v.reshape(depth, bs, past_len, FOX_INNER)

    xp, xs = x_prompt, x_sample
    peer_p = peer_s = g2p = g2s = None
    outs = [[] for _ in range(11)]
    for l in range(depth):
        p = _layer_params(l, norm1_g, norm2_g, w_in, conv_w, conv_b, dt_bias, a_log, d_skip, ssd_norm_g, q_norm_g,
                          k_norm_g, fgate_b, w_s, b_s, w_out, peer_keys)
        mods = [m[:, None, :] for m in jnp.split(mod_all[l], 6, axis=-1)]
        mod_p = [m[:bp] for m in mods]
        mod_s = [m[bp:] for m in mods]
        hist_p = jnp.zeros((bp, 8, CONV_CH), F32)
        s0_p = jnp.zeros((bp, SSD_STATE, SSD_INNER), F32)
        hist_s = jnp.concatenate([jnp.zeros((bs, 8 - (CONV_W - 1), CONV_CH), F32), state_conv[l].astype(F32)], axis=1)
        s0_s = state_ssm[l].astype(F32).transpose(0, 3, 1, 2).reshape(bs, SSD_STATE, SSD_INNER)
        plf = cache_fox_logf[l].astype(F32)
        plf_col = jnp.pad(plf, ((0, 0), (0, 0), (F_ROW0, LANES - F_ROW0 - FOX_HEADS)))
        plf_row = jnp.pad(plf.transpose(0, 2, 1), ((0, 0), (0, 8 - FOX_HEADS), (0, 0)))
        past = (past_k_all, past_v_all, l, plf_col, plf_row)

        x1p, h2p, st_p = _stream_mixers(xp, peer_p, g2p, mod_p, p, hist_p, s0_p, None)
        x1s, h2s, st_s = _stream_mixers(xs, peer_s, g2s, mod_s, p, hist_s, s0_s, past)

        peer_p = _peer(h2p, p, peer_tables, l)
        peer_s = _peer(h2s, p, peer_tables, l)
        xp, xs, g2p, g2s = x1p, x1s, mod_p[5], mod_s[5]
        for i in range(5):
            outs[i].append(st_p[i])
        for i in range(6):
            outs[5 + i].append(st_s[i])

    yp = _residual(xp, peer_p, g2p)
    ys = _residual(xs, peer_s, g2s)
    return (yp, ys) + tuple(jnp.stack(o) for o in outs)
```
